```python
import math
import jax, jax.numpy as jnp
from jax import lax
import numpy as np

D_MODEL = 1024
BATCH = 8
SEQ = 2048
DEPTH = 2
DEC_BATCH = 32
DEC_SEQ = 4
PAST_LEN = 8192
PAGE_SIZE = 128

N_MIXERS = 2
N_NSA = (DEPTH + 1) // 2
N_GDN = DEPTH // 2
MEM_LEN = 256
HEAD_DIM = 64
NSA_HEADS = 12
NSA_KV_HEADS = 2
NSA_GROUP = NSA_HEADS // NSA_KV_HEADS
NSA_WIDTH = NSA_HEADS * HEAD_DIM
NSA_KV_COLS = NSA_KV_HEADS * HEAD_DIM
CMP_STRIDE = 16
CMP_BLOCK = 2 * CMP_STRIDE
CMP_HIDDEN = 128
SLC_BLOCK = 64
SLC_TOPN = 16
WINDOW = 512
SLC_QBLOCK = 64
WIN_QBLOCK = 128
GDN_HEADS = 6
GDN_DK = 128
GDN_DV = 128
GDN_WIDTH = GDN_HEADS * GDN_DV
GDN_CONV = 4
GDN_CONV_CH = 2 * GDN_HEADS * GDN_DK + GDN_HEADS * GDN_DV
GDN_CHUNK = 64
XA_HEADS = 4
XA_DIM = 64
XA_WIDTH = XA_HEADS * XA_DIM
MIX_WIDTH = NSA_WIDTH + XA_WIDTH
NSA_MIX_IN = NSA_WIDTH + 6 * NSA_KV_COLS + 3 * NSA_HEADS
NSA_IN = NSA_MIX_IN + XA_WIDTH
GDN_MIX_IN = GDN_CONV_CH + GDN_WIDTH + 2 * GDN_HEADS
GDN_IN = GDN_MIX_IN + XA_WIDTH
D_FF = 4 * D_MODEL
NORM_EPS = 1e-6
L2_EPS = 1e-6
NEG_INF = -1e30
FORCED = 1e9

kernel_name = 'nsa_gdn_hybrid_decode_step'


def rms_norm(x, g):
    xf = x.astype(jnp.float32)
    y = xf * lax.rsqrt(jnp.mean(xf * xf, axis=-1, keepdims=True) + NORM_EPS)
    return (y * g.astype(jnp.float32)).astype(x.dtype)


def alibi_slopes():
    h = jnp.arange(1, NSA_HEADS + 1, dtype=jnp.float32)
    return jnp.exp2(-8.0 * h / NSA_HEADS).reshape(NSA_KV_HEADS, NSA_GROUP)


def masked_softmax(s, mask):
    return jax.nn.softmax(jnp.where(mask, s.astype(jnp.float32), NEG_INF), axis=-1)


def map_query_blocks(fn, T, qb, items):
    nb = -(-T // qb)
    pad = nb * qb - T
    split = []
    for a, ax in items:
        if pad:
            widths = [(0, 0)] * a.ndim
            widths[ax] = (0, pad)
            a = jnp.pad(a, widths, mode='edge')
        shp = a.shape
        a = a.reshape(shp[:ax] + (nb, qb) + shp[ax + 1:])
        split.append(jnp.moveaxis(a, ax, 0))
    out = jnp.moveaxis(lax.map(fn, tuple(split)), 0, 1)
    out = out.reshape((out.shape[0], nb * qb) + out.shape[3:])
    return out[:, :T]


def compress_blocks(kv, pe, w1, w2):
    B, L, KH, D = kv.shape
    c = kv.reshape(B, L // CMP_STRIDE, CMP_STRIDE, KH, D)
    blocks = jnp.concatenate([c[:, :-1], c[:, 1:]], axis=2) + pe[None, None, :, None, :]
    hid = jax.nn.silu(jnp.einsum('bnlkd,ldf->bnkf', blocks, w1))
    return jnp.einsum('bnkf,fd->bnkd', hid, w2)


def cmp_to_slc(nc, ns):
    start = jnp.arange(nc)[:, None] * CMP_STRIDE
    j = jnp.arange(ns)[None, :]
    return ((start < (j + 1) * SLC_BLOCK) & (start + CMP_BLOCK > j * SLC_BLOCK)).astype(jnp.float32)


def nsa_mix(q, pos, gates, kv_full, win_kv, win_pos0, pe_k, w1_k, w2_k, pe_v, w1_v, w2_v):
    B, T, H, D = q.shape
    L = kv_full.shape[1]
    dt = q.dtype
    f32 = jnp.float32
    slopes = alibi_slopes()
    qg = (q * D ** -0.5).reshape(B, T, NSA_KV_HEADS, NSA_GROUP, D)

    kc = compress_blocks(kv_full[:, :, 0], pe_k, w1_k, w2_k)
    vc = compress_blocks(kv_full[:, :, 1], pe_v, w1_v, w2_v)
    nc = kc.shape[1]
    cdist = pos[:, None] - (jnp.arange(nc) * CMP_STRIDE + CMP_BLOCK - 1)[None, :]
    cvis = cdist >= 0
    s = jnp.einsum('btkgd,bnkd->bkgtn', qg, kc).astype(f32) - slopes[:, :, None, None] * cdist.astype(f32)
    p_cmp = masked_softmax(s, cvis) * jnp.any(cvis, axis=-1, keepdims=True)
    o_cmp = jnp.einsum('bkgtn,bnkd->btkgd', p_cmp.astype(dt), vc)

    ns = L // SLC_BLOCK
    imp = jnp.einsum('bkgtn,nj->bktj', p_cmp, cmp_to_slc(nc, ns))
    blk = jnp.arange(ns)[None, :]
    cur = (pos // SLC_BLOCK)[:, None]
    forced = (blk == 0) | (blk == cur) | (blk == cur - 1)
    imp = jnp.where(forced, FORCED, jnp.where(blk <= cur, imp, NEG_INF))
    top_s, top_i = lax.top_k(imp, min(SLC_TOPN, ns))
    top_ok = top_s > 0.5 * NEG_INF

    kb = kv_full[:, :, 2].reshape(B, ns, SLC_BLOCK, NSA_KV_HEADS, D).transpose(0, 3, 1, 2, 4)
    vb = kv_full[:, :, 3].reshape(B, ns, SLC_BLOCK, NSA_KV_HEADS, D).transpose(0, 3, 1, 2, 4)
    gather = jax.vmap(jax.vmap(lambda blocks, idx: blocks[idx]))

    def slc_block(args):
        qb, pb, ib, okb = args
        ks = gather(kb, ib)
        vs = gather(vb, ib)
        kpos = ib[..., None] * SLC_BLOCK + jnp.arange(SLC_BLOCK)
        dist = pb[:, None, None] - kpos
        mask = ((dist >= 0) & okb[..., None])[:, :, None]
        sc = jnp.einsum('btkgd,bktnsd->bkgtns', qb, ks).astype(f32) - slopes[:, :, None, None, None] * dist[:, :, None].astype(f32)
        sh = sc.shape
        pr = masked_softmax(sc.reshape(sh[:4] + (-1,)), mask.reshape(mask.shape[:4] + (-1,)))
        return jnp.einsum('bkgtns,bktnsd->btkgd', pr.reshape(sh).astype(dt), vs)

    o_slc = map_query_blocks(slc_block, T, min(SLC_QBLOCK, T), [(qg, 1), (pos, 0), (top_i, 2), (top_ok, 2)])

    qw = min(WIN_QBLOCK, T)
    kvw = jnp.pad(win_kv, ((0, 0), (WINDOW, qw), (0, 0), (0, 0), (0, 0)))

    def win_block(args):
        qb, pb = args
        kv = lax.dynamic_slice_in_dim(kvw, pb[0] - win_pos0, WINDOW + qw, axis=1)
        kpos = pb[0] - WINDOW + jnp.arange(WINDOW + qw)
        dist = pb[:, None] - kpos[None, :]
        mask = (dist >= 0) & (dist < WINDOW) & (kpos >= win_pos0)[None, :]
        sc = jnp.einsum('btkgd,bskd->bkgts', qb, kv[:, :, 0]).astype(f32) - slopes[:, :, None, None] * dist.astype(f32)
        pr = masked_softmax(sc, mask)
        return jnp.einsum('bkgts,bskd->btkgd', pr.astype(dt), kv[:, :, 1])

    o_win = map_query_blocks(win_block, T, qw, [(qg, 1), (pos, 0)])

    g = gates.reshape(B, T, NSA_KV_HEADS, NSA_GROUP, 3)
    o = g[..., 0:1] * o_cmp + g[..., 1:2] * o_slc + g[..., 2:3] * o_win
    return o.reshape(B, T, H * D)


def nsa_layer_mix(proj, pos0, past, pe_k, w1_k, w2_k, pe_v, w1_v, w2_v):
    B, T, _ = proj.shape
    q = proj[..., :NSA_WIDTH].reshape(B, T, NSA_HEADS, HEAD_DIM)
    kv = proj[..., NSA_WIDTH:NSA_WIDTH + 6 * NSA_KV_COLS].reshape(B, T, 6, NSA_KV_HEADS, HEAD_DIM)
    gates = jax.nn.sigmoid(proj[..., NSA_WIDTH + 6 * NSA_KV_COLS:NSA_MIX_IN]).reshape(B, T, NSA_HEADS, 3)
    rows, win_new = kv[:, :, :4], kv[:, :, 4:]
    if past is None:
        full, win, win_pos0 = rows, win_new, pos0
    else:
        full = jnp.concatenate([past[0], rows], axis=1)
        win = jnp.concatenate([past[1], win_new], axis=1)
        win_pos0 = pos0 - past[1].shape[1]
    L = full.shape[1]
    Lp = -(-L // SLC_BLOCK) * SLC_BLOCK
    full = jnp.pad(full, ((0, 0), (0, Lp - L), (0, 0), (0, 0), (0, 0)))
    pos = pos0 + jnp.arange(T, dtype=jnp.int32)
    o = nsa_mix(q, pos, gates, full, win, win_pos0, pe_k, w1_k, w2_k, pe_v, w1_v, w2_v)
    return o, rows, win[:, -min(WINDOW, win.shape[1]):]


def chunked_gated_delta(q, k, v, g, beta, S0):
    B, T, H, DK = q.shape
    DV = v.shape[-1]
    f32 = jnp.float32
    C = min(GDN_CHUNK, T)
    nc = -(-T // C)
    pad = nc * C - T
    q = q.astype(f32)
    k = k.astype(f32)
    q = q * lax.rsqrt(jnp.sum(q * q, -1, keepdims=True) + L2_EPS) * DK ** -0.5
    k = k * lax.rsqrt(jnp.sum(k * k, -1, keepdims=True) + L2_EPS)

    def prep(a):
        a = a.astype(f32)
        a = jnp.pad(a, [(0, 0), (0, pad)] + [(0, 0)] * (a.ndim - 2))
        a = a.reshape((B, nc, C) + a.shape[2:])
        return jnp.moveaxis(jnp.moveaxis(a, 3, 2), 1, 0)

    q, k, v, g, beta = prep(q), prep(k), prep(v), prep(g), prep(beta)
    gc = jnp.cumsum(g, axis=-1)
    incl = jnp.tril(jnp.ones((C, C), bool))
    strict = jnp.tril(jnp.ones((C, C), bool), -1)
    decay = jnp.exp(jnp.where(incl, gc[..., :, None] - gc[..., None, :], NEG_INF))
    kb = k * beta[..., None]
    A = jnp.where(strict, jnp.einsum('nbhid,nbhjd->nbhij', kb, k) * decay, 0.0)
    rhs = jnp.concatenate([v * beta[..., None], kb * jnp.exp(gc)[..., None]], axis=-1)
    sol = lax.linalg.triangular_solve(A, rhs, left_side=True, lower=True, unit_diagonal=True)
    u, w = sol[..., :DV], sol[..., DV:]
    qk = jnp.where(incl, jnp.einsum('nbhid,nbhjd->nbhij', q, k) * decay, 0.0)

    def step(S, xs):
        qc, kc, uc, wc, gcc, qkc = xs
        v_new = uc - jnp.einsum('bhck,bhkv->bhcv', wc, S)
        o = jnp.einsum('bhck,bhkv->bhcv', qc * jnp.exp(gcc)[..., None], S) + jnp.einsum('bhij,bhjv->bhiv', qkc, v_new)
        gl = gcc[..., -1:]
        S = S * jnp.exp(gl)[..., None] + jnp.einsum('bhck,bhcv->bhkv', kc * jnp.exp(gl - gcc)[..., None], v_new)
        return S, o

    S, o = lax.scan(step, S0.astype(f32), (q, k, u, w, gc, qk))
    o = jnp.swapaxes(jnp.moveaxis(o, 0, 1), 2, 3).reshape(B, nc * C, H, DV)[:, :T]
    return o, S


def gdn_mix(proj, conv_buf, S0, conv_w, a_log, dt_bias, norm_g):
    B, T, _ = proj.shape
    f32 = jnp.float32
    qkv = proj[..., :GDN_CONV_CH]
    z = proj[..., GDN_CONV_CH:GDN_CONV_CH + GDN_WIDTH].reshape(B, T, GDN_HEADS, GDN_DV)
    b_raw = proj[..., GDN_CONV_CH + GDN_WIDTH:GDN_CONV_CH + GDN_WIDTH + GDN_HEADS]
    a_raw = proj[..., GDN_CONV_CH + GDN_WIDTH + GDN_HEADS:GDN_MIX_IN]
    xc = jnp.concatenate([conv_buf.astype(qkv.dtype), qkv], axis=1)
    conv = xc[:, 0:T] * conv_w[0]
    for i in range(1, GDN_CONV):
        conv = conv + xc[:, i:i + T] * conv_w[i]
    act = jax.nn.silu(conv)
    new_buf = xc[:, -(GDN_CONV - 1):]
    hk = GDN_HEADS * GDN_DK
    q = act[..., :hk].reshape(B, T, GDN_HEADS, GDN_DK)
    k = act[..., hk:2 * hk].reshape(B, T, GDN_HEADS, GDN_DK)
    v = act[..., 2 * hk:].reshape(B, T, GDN_HEADS, GDN_DV)
    beta = jax.nn.sigmoid(b_raw.astype(f32))
    g = -jnp.exp(a_log.astype(f32)) * jax.nn.softplus(a_raw.astype(f32) + dt_bias.astype(f32))
    o, S = chunked_gated_delta(q, k, v, g, beta, S0)
    o = rms_norm(o.astype(proj.dtype), norm_g) * jax.nn.silu(z)
    return o.reshape(B, T, GDN_WIDTH), S.astype(S0.dtype), new_buf


def cross_attend(xq, mkv):
    B, T, _ = xq.shape
    q = xq.reshape(B, T, XA_HEADS, XA_DIM) * XA_DIM ** -0.5
    s = jnp.einsum('bthd,bmhd->bhtm', q, mkv[:, :, 0]).astype(jnp.float32)
    p = jax.nn.softmax(s, axis=-1).astype(xq.dtype)
    return jnp.einsum('bhtm,bmhd->bthd', p, mkv[:, :, 1]).reshape(B, T, XA_WIDTH)


def run_trunk(x, pos0, mem_kv, nsa_past, gdn_state, p):
    nsa_rows, nsa_win, gdn_s, gdn_conv = [], [], [], []
    for i in range(DEPTH):
        li = i // N_MIXERS
        h = rms_norm(x, p['norm_mix_g'][i])
        if i % N_MIXERS == 0:
            proj = h @ p['nsa_w_in'][li]
            o_mix, rows, win = nsa_layer_mix(proj[..., :NSA_MIX_IN], pos0, nsa_past[li],
                                             p['cmp_pe_k'][li], p['cmp_w1_k'][li], p['cmp_w2_k'][li],
                                             p['cmp_pe_v'][li], p['cmp_w1_v'][li], p['cmp_w2_v'][li])
            nsa_rows.append(rows)
            nsa_win.append(win)
        else:
            proj = h @ p['gdn_w_in'][li]
            S0, buf0 = gdn_state[li]
            o_mix, S, buf = gdn_mix(proj[..., :GDN_MIX_IN], buf0, S0, p['gdn_conv_w'][li], p['gdn_a_log'][li],
                                    p['gdn_dt_bias'][li], p['gdn_norm_g'][li])
            gdn_s.append(S)
            gdn_conv.append(buf)
        o_mem = cross_attend(proj[..., -XA_WIDTH:], mem_kv[i])
        x = x + jnp.concatenate([o_mix, o_mem], axis=-1) @ p['w_out'][i]
        h = rms_norm(x, p['norm_ffn_g'][i])
        x = x + jnp.square(jax.nn.relu(h @ p['w_up'][i])) @ p['w_down'][i]
    y = rms_norm(x, p['final_norm_g'])
    return y, jnp.stack(nsa_rows, 2), jnp.stack(nsa_win, 1), jnp.stack(gdn_s, 1), jnp.stack(gdn_conv, 1)


def setup_inputs(seed: int = 0) -> dict:
    key = jax.random.key(seed)
    ks = iter(jax.random.split(key, 40))
    nrm = lambda shape, scale=1.0: jax.random.normal(next(ks), shape, jnp.float32) * scale
    gain = lambda shape: 1.0 + nrm(shape, 0.05)
    n_pages = PAST_LEN // PAGE_SIZE
    n_phys = (DEC_BATCH * n_pages * 5 + 3) // 4
    win_buf = min(WINDOW, PAST_LEN)
    page_table = jax.random.permutation(next(ks), n_phys)[:DEC_BATCH * n_pages].reshape(DEC_BATCH, n_pages).astype(jnp.int32)
    dt = jnp.exp(jax.random.uniform(next(ks), (N_GDN, GDN_HEADS), jnp.float32, math.log(1e-3), math.log(1e-1)))
    dt_bias = dt + jnp.log(-jnp.expm1(-dt))
    a_log = jnp.log(jax.random.uniform(next(ks), (N_GDN, GDN_HEADS), jnp.float32, 1.0, 16.0))
    return {
        'x_prompt': nrm((BATCH, SEQ, D_MODEL)),
        'x_sample': nrm((DEC_BATCH, DEC_SEQ, D_MODEL)),
        'mem_prompt': nrm((BATCH, MEM_LEN, D_MODEL)),
        'cache_nsa_kv': nrm((n_phys, PAGE_SIZE, N_NSA, 4, NSA_KV_HEADS, HEAD_DIM)),
        'cache_nsa_win': nrm((DEC_BATCH, N_NSA, win_buf, 2, NSA_KV_HEADS, HEAD_DIM)),
        'state_gdn_s': nrm((DEC_BATCH, N_GDN, GDN_HEADS, GDN_DK, GDN_DV), 0.1),
        'state_gdn_conv': nrm((DEC_BATCH, N_GDN, GDN_CONV - 1, GDN_CONV_CH)),
        'cache_mem_kv': nrm((DEC_BATCH, DEPTH, MEM_LEN, 2, XA_HEADS, XA_DIM)),
        'page_table': page_table,
        'norm_mix_g': gain((DEPTH, D_MODEL)),
        'norm_mem_g': gain((DEPTH, D_MODEL)),
        'w_mem_kv': nrm((DEPTH, D_MODEL, 2 * XA_WIDTH), D_MODEL ** -0.5),
        'nsa_w_in': nrm((N_NSA, D_MODEL, NSA_IN), D_MODEL ** -0.5),
        'cmp_pe_k': nrm((N_NSA, CMP_BLOCK, HEAD_DIM), 0.5),
        'cmp_w1_k': nrm((N_NSA, CMP_BLOCK, HEAD_DIM, CMP_HIDDEN), (CMP_BLOCK * HEAD_DIM) ** -0.5),
        'cmp_w2_k': nrm((N_NSA, CMP_HIDDEN, HEAD_DIM), CMP_HIDDEN ** -0.5),
        'cmp_pe_v': nrm((N_NSA, CMP_BLOCK, HEAD_DIM), 0.5),
        'cmp_w1_v': nrm((N_NSA, CMP_BLOCK, HEAD_DIM, CMP_HIDDEN), (CMP_BLOCK * HEAD_DIM) ** -0.5),
        'cmp_w2_v': nrm((N_NSA, CMP_HIDDEN, HEAD_DIM), CMP_HIDDEN ** -0.5),
        'gdn_w_in': nrm((N_GDN, D_MODEL, GDN_IN), D_MODEL ** -0.5),
        'gdn_conv_w': nrm((N_GDN, GDN_CONV, GDN_CONV_CH), GDN_CONV ** -0.5),
        'gdn_a_log': a_log,
        'gdn_dt_bias': dt_bias,
        'gdn_norm_g': gain((N_GDN, GDN_DV)),
        'w_out': nrm((DEPTH, MIX_WIDTH, D_MODEL), MIX_WIDTH ** -0.5),
        'norm_ffn_g': gain((DEPTH, D_MODEL)),
        'w_up': nrm((DEPTH, D_MODEL, D_FF), D_MODEL ** -0.5),
        'w_down': nrm((DEPTH, D_FF, D_MODEL), D_FF ** -0.5),
        'final_norm_g': gain((D_MODEL,)),
    }


def reference(x_prompt, x_sample, mem_prompt, cache_nsa_kv, cache_nsa_win, state_gdn_s, state_gdn_conv,
              cache_mem_kv, page_table, norm_mix_g, norm_mem_g, w_mem_kv, nsa_w_in, cmp_pe_k, cmp_w1_k,
              cmp_w2_k, cmp_pe_v, cmp_w1_v, cmp_w2_v, gdn_w_in, gdn_conv_w, gdn_a_log, gdn_dt_bias,
              gdn_norm_g, w_out, norm_ffn_g, w_up, w_down, final_norm_g):
    p = dict(norm_mix_g=norm_mix_g, nsa_w_in=nsa_w_in, cmp_pe_k=cmp_pe_k, cmp_w1_k=cmp_w1_k,
             cmp_w2_k=cmp_w2_k, cmp_pe_v=cmp_pe_v, cmp_w1_v=cmp_w1_v, cmp_w2_v=cmp_w2_v,
             gdn_w_in=gdn_w_in, gdn_conv_w=gdn_conv_w, gdn_a_log=gdn_a_log, gdn_dt_bias=gdn_dt_bias,
             gdn_norm_g=gdn_norm_g, w_out=w_out, norm_ffn_g=norm_ffn_g, w_up=w_up, w_down=w_down,
             final_norm_g=final_norm_g)

    B, M = mem_prompt.shape[0], mem_prompt.shape[1]
    mem_kv_p = [(rms_norm(mem_prompt, norm_mem_g[i]) @ w_mem_kv[i]).reshape(B, M, 2, XA_HEADS, XA_DIM)
                for i in range(DEPTH)]
    gdn0 = [(jnp.zeros((B, GDN_HEADS, GDN_DK, GDN_DV), x_prompt.dtype),
             jnp.zeros((B, GDN_CONV - 1, GDN_CONV_CH), x_prompt.dtype))] * N_GDN
    y_prompt, nsa_kv_prompt, nsa_win_prompt, gdn_s_prompt, gdn_conv_prompt = run_trunk(
        x_prompt, 0, mem_kv_p, [None] * N_NSA, gdn0, p)
    mem_kv_prompt = jnp.stack(mem_kv_p, axis=1)

    DB = x_sample.shape[0]
    past_len = page_table.shape[1] * PAGE_SIZE
    past = cache_nsa_kv[page_table].reshape((DB, past_len) + cache_nsa_kv.shape[2:])
    nsa_past = [(past[:, :, li], cache_nsa_win[:, li]) for li in range(N_NSA)]
    gdn_past = [(state_gdn_s[:, li], state_gdn_conv[:, li]) for li in range(N_GDN)]
    mem_kv_s = [cache_mem_kv[:, i] for i in range(DEPTH)]
    y_sample, nsa_kv_sample, nsa_win_sample, gdn_s_sample, gdn_conv_sample = run_trunk(
        x_sample, past_len, mem_kv_s, nsa_past, gdn_past, p)

    return (y_prompt, y_sample, nsa_kv_prompt, nsa_kv_sample, nsa_win_prompt, nsa_win_sample,
            gdn_s_prompt, gdn_s_sample, gdn_conv_prompt, gdn_conv_sample, mem_kv_prompt)
```

```python
import functools

import jax
import jax.numpy as jnp
from jax import lax
from jax.experimental import pallas as pl
from jax.experimental.pallas import tpu as pltpu

f32 = jnp.float32
bf16 = jnp.bfloat16

D_MODEL = 1024
DEPTH = 2
PAGE_SIZE = 128
HEAD_DIM = 64
NSA_HEADS = 12
NSA_KV_HEADS = 2
NSA_GROUP = NSA_HEADS // NSA_KV_HEADS
NSA_WIDTH = NSA_HEADS * HEAD_DIM
NSA_KV_COLS = NSA_KV_HEADS * HEAD_DIM
CMP_STRIDE = 16
CMP_BLOCK = 2 * CMP_STRIDE
CMP_HIDDEN = 128
SLC_BLOCK = 64
SLC_TOPN = 16
WINDOW = 512
GDN_HEADS = 6
GDN_DK = 128
GDN_DV = 128
GDN_WIDTH = GDN_HEADS * GDN_DV
GDN_CONV = 4
GDN_CONV_CH = 2 * GDN_HEADS * GDN_DK + GDN_HEADS * GDN_DV
XA_HEADS = 4
XA_DIM = 64
XA_WIDTH = XA_HEADS * XA_DIM
D_FF = 4 * D_MODEL
NORM_EPS = 1e-6
L2_EPS = 1e-6
NEG_INF = -1e30
FORCED = 1e9

VMEM_LIMIT = 52 * 1024 * 1024
LANES = 128

SLOPES = [[2.0 ** (-8.0 * (k * NSA_GROUP + g + 1) / NSA_HEADS) for g in range(NSA_GROUP)]
          for k in range(NSA_KV_HEADS)]

NT = (((1,), (1,)), ((), ()))
TN = (((0,), (0,)), ((), ()))


def _cparams(*sem):
    return pltpu.CompilerParams(dimension_semantics=sem, vmem_limit_bytes=VMEM_LIMIT)


def _rms(x, g):
    r = lax.rsqrt(jnp.mean(x * x, axis=-1, keepdims=True) + NORM_EPS)
    return (x * r) * g


def _sigmoid(x):
    return 1.0 / (1.0 + jnp.exp(-x))


def _silu(x):
    return x * _sigmoid(x)


def _split3(a):
    hi = a.astype(bf16)
    r1 = a - hi.astype(f32)
    mid = r1.astype(bf16)
    lo = (r1 - mid.astype(f32)).astype(bf16)
    return hi, mid, lo


def _dot_exact_rhs(a, b_bf16):
    hi, mid, lo = _split3(a)
    d = lambda x: jnp.dot(x, b_bf16, preferred_element_type=f32)
    return d(hi) + d(mid) + d(lo)


def _dot_hp(a, b, dims=None):
    a_hi = a.astype(bf16)
    a_lo = (a - a_hi.astype(f32)).astype(bf16)
    b_hi = b.astype(bf16)
    b_lo = (b - b_hi.astype(f32)).astype(bf16)
    if dims is None:
        d = lambda x, y: jnp.dot(x, y, preferred_element_type=f32)
    else:
        d = lambda x, y: lax.dot_general(x, y, dims, preferred_element_type=f32)
    return d(a_hi, b_hi) + (d(a_hi, b_lo) + d(a_lo, b_hi))


def _norm_proj_kernel(x_ref, g_ref, *refs, scales):
    n = len(scales)
    w_refs, o_refs = refs[:n], refs[n:]
    h = _rms(x_ref[...], g_ref[...]).astype(bf16)
    for scale, w_ref, o_ref in zip(scales, w_refs, o_refs):
        y = jnp.dot(h, w_ref[...], preferred_element_type=f32)
        if scale != 1.0:
            y = y * scale
        o_ref[...] = y.astype(o_ref.dtype)


def norm_proj(x, g, groups, tm=256):
    N, D = x.shape
    tm = min(tm, N)
    assert N % tm == 0
    in_specs = [pl.BlockSpec((tm, D), lambda i: (i, 0)), pl.BlockSpec((1, D), lambda i: (0, 0))]
    out_specs, out_shape = [], []
    for w, dt, _ in groups:
        n = w.shape[1]
        in_specs.append(pl.BlockSpec((D, n), lambda i: (0, 0)))
        out_specs.append(pl.BlockSpec((tm, n), lambda i: (i, 0)))
        out_shape.append(jax.ShapeDtypeStruct((N, n), dt))
    return pl.pallas_call(
        functools.partial(_norm_proj_kernel, scales=tuple(s for _, _, s in groups)),
        grid=(N // tm,),
        in_specs=in_specs,
        out_specs=out_specs,
        out_shape=out_shape,
        compiler_params=_cparams("parallel"),
        name="norm_proj",
    )(x, g.reshape(1, D), *[w for w, _, _ in groups])


def _xattn_kernel(q_ref, kv_ref, o_ref):
    q = q_ref[0]
    kv = kv_ref[0, 0]
    outs = []
    for h in range(XA_HEADS):
        qh = q[:, h * XA_DIM:(h + 1) * XA_DIM]
        kh = kv[:, h * XA_DIM:(h + 1) * XA_DIM].astype(bf16)
        vh = kv[:, XA_WIDTH + h * XA_DIM:XA_WIDTH + (h + 1) * XA_DIM].astype(bf16)
        s = lax.dot_general(qh, kh, NT, preferred_element_type=f32)
        m = jnp.max(s, axis=-1, keepdims=True)
        e = jnp.exp(s - m)
        p = e / jnp.sum(e, axis=-1, keepdims=True)
        outs.append(jnp.dot(p.astype(bf16), vh, preferred_element_type=f32))
    o_ref[0] = jnp.concatenate(outs, axis=1).astype(o_ref.dtype)


def cross_attend(xq, mkv, layer, tq=256):
    B, T, _ = xq.shape
    M = mkv.shape[2]
    tq = min(tq, T)
    return pl.pallas_call(
        _xattn_kernel,
        grid=(B, T // tq),
        in_specs=[pl.BlockSpec((1, tq, XA_WIDTH), lambda b, i: (b, i, 0)),
                  pl.BlockSpec((1, 1, M, 2 * XA_WIDTH), lambda b, i: (b, layer, 0, 0))],
        out_specs=pl.BlockSpec((1, tq, XA_WIDTH), lambda b, i: (b, i, 0)),
        out_shape=jax.ShapeDtypeStruct((B, T, XA_WIDTH), bf16),
        compiler_params=_cparams("parallel", "parallel"),
        name="cross_attend",
    )(xq, mkv)


def _post_kernel(x_ref, om_ref, oc_ref, wo_ref, gf_ref, wu_ref, wd_ref, gl_ref, o_ref,
                 x1_s, h_s, acc_s, *, mix_w, final):
    j = pl.program_id(1)

    @pl.when(j == 0)
    def _():
        x1 = x_ref[...] + (jnp.dot(om_ref[...], wo_ref[0:mix_w, :], preferred_element_type=f32)
                           + jnp.dot(oc_ref[...], wo_ref[mix_w:, :], preferred_element_type=f32))
        x1_s[...] = x1
        h_s[...] = _rms(x1, gf_ref[...]).astype(bf16)
        acc_s[...] = jnp.zeros_like(acc_s)

    u = jnp.dot(h_s[...], wu_ref[...], preferred_element_type=f32)
    u = jnp.square(jnp.maximum(u, 0.0)).astype(bf16)
    acc_s[...] += jnp.dot(u, wd_ref[...], preferred_element_type=f32)

    @pl.when(j == pl.num_programs(1) - 1)
    def _():
        x2 = x1_s[...] + acc_s[...]
        if final:
            x2 = _rms(x2, gl_ref[...])
        o_ref[...] = x2


def post_block(x, o_mix, o_mem, w_out, g_ffn, w_up, w_down, g_final, final, tm=1024, tf=512):
    N, D = x.shape
    mix_w = o_mix.shape[1]
    F = w_up.shape[1]
    tm = min(tm, N)
    return pl.pallas_call(
        functools.partial(_post_kernel, mix_w=mix_w, final=final),
        grid=(N // tm, F // tf),
        in_specs=[pl.BlockSpec((tm, D), lambda i, j: (i, 0)),
                  pl.BlockSpec((tm, mix_w), lambda i, j: (i, 0)),
                  pl.BlockSpec((tm, XA_WIDTH), lambda i, j: (i, 0)),
                  pl.BlockSpec((mix_w + XA_WIDTH, D), lambda i, j: (0, 0)),
                  pl.BlockSpec((1, D), lambda i, j: (0, 0)),
                  pl.BlockSpec((D, tf), lambda i, j: (0, j)),
                  pl.BlockSpec((tf, D), lambda i, j: (j, 0)),
                  pl.BlockSpec((1, D), lambda i, j: (0, 0))],
        out_specs=pl.BlockSpec((tm, D), lambda i, j: (i, 0)),
        out_shape=jax.ShapeDtypeStruct((N, D), f32),
        scratch_shapes=[pltpu.VMEM((tm, D), f32), pltpu.VMEM((tm, D), bf16), pltpu.VMEM((tm, D), f32)],
        compiler_params=_cparams("parallel", "arbitrary"),
        name="post_block",
    )(x, o_mix, o_mem, w_out, g_ffn.reshape(1, D), w_up, w_down, g_final.reshape(1, D))


def _compress(load_chunk_rows, nch, pe_ref, w_ref, w2_ref):
    acc_p = jnp.zeros((nch, 2 * CMP_HIDDEN), f32)
    acc_q = jnp.zeros((nch, 2 * CMP_HIDDEN), f32)
    for l in range(CMP_STRIDE):
        x = load_chunk_rows(l)
        xp = (x + pe_ref[l:l + 1, :]).astype(bf16)
        xq = (x + pe_ref[CMP_STRIDE + l:CMP_STRIDE + l + 1, :]).astype(bf16)
        acc_p = acc_p + jnp.dot(xp, w_ref[l, :, 0:2 * CMP_HIDDEN], preferred_element_type=f32)
        acc_q = acc_q + jnp.dot(xq, w_ref[l, :, 2 * CMP_HIDDEN:], preferred_element_type=f32)
    hid = _silu(acc_p + pltpu.roll(acc_q, nch - 1, 0)).astype(bf16)
    outs = [jnp.dot(hid[:, h * CMP_HIDDEN:(h + 1) * CMP_HIDDEN], w2_ref[...], preferred_element_type=f32)
            for h in range(NSA_KV_HEADS)]
    return jnp.concatenate(outs, axis=1)


def _cmp_prompt_kernel(xk_ref, xv_ref, pek_ref, pev_ref, wk_ref, wv_ref, w2k_ref, w2v_ref, kc_ref, vc_ref, *, nch):
    kc_ref[0] = _compress(lambda l: xk_ref[0, pl.ds(l, nch, stride=CMP_STRIDE), :], nch, pek_ref, wk_ref, w2k_ref)
    vc_ref[0] = _compress(lambda l: xv_ref[0, pl.ds(l, nch, stride=CMP_STRIDE), :], nch, pev_ref, wv_ref, w2v_ref)


def _cmp_weights(pe, w1, w2):
    pe2 = jnp.tile(pe, (1, NSA_KV_HEADS))
    z = jnp.zeros((CMP_STRIDE, HEAD_DIM, CMP_HIDDEN), w1.dtype)
    a, b = w1[:CMP_STRIDE], w1[CMP_STRIDE:]
    top = jnp.concatenate([a, z, b, z], axis=2)
    bot = jnp.concatenate([z, a, z, b], axis=2)
    wbd = jnp.concatenate([top, bot], axis=1).astype(bf16)
    return pe2, wbd, w2.astype(bf16)


def cmp_prompt(rows, cw):
    B, T, _ = rows.shape
    nch = T // CMP_STRIDE
    pek, wk, w2k, pev, wv, w2v = cw
    full = lambda a: pl.BlockSpec(a.shape, lambda b: (0,) * a.ndim)
    return pl.pallas_call(
        functools.partial(_cmp_prompt_kernel, nch=nch),
        grid=(B,),
        in_specs=[pl.BlockSpec((1, T, LANES), lambda b: (b, 0, 0)),
                  pl.BlockSpec((1, T, LANES), lambda b: (b, 0, 1)),
                  full(pek), full(pev), full(wk), full(wv), full(w2k), full(w2v)],
        out_specs=[pl.BlockSpec((1, nch, LANES), lambda b: (b, 0, 0))] * 2,
        out_shape=[jax.ShapeDtypeStruct((B, nch, LANES), f32)] * 2,
        compiler_params=_cparams("parallel"),
        name="cmp_prompt",
    )(rows, rows, pek, pev, wk, wv, w2k, w2v)


def _padded_queries(q_all, kvh, tq):
    del tq
    lo = kvh * NSA_GROUP
    return jnp.concatenate([q_all[:, (lo + g) * LANES:(lo + g + 1) * LANES] for g in range(NSA_GROUP)], axis=0)


def _cmp_branch(qm, kc, vc, pos_col, kvh, tq):
    ncp = kc.shape[0]
    s = lax.dot_general(qm, kc.astype(bf16), NT, preferred_element_type=f32)
    n_idx = lax.broadcasted_iota(jnp.int32, (tq, ncp), 1)
    cdist = pos_col - (n_idx * CMP_STRIDE + (CMP_BLOCK - 1))
    cvis = cdist >= 0
    any_vis = pos_col >= (CMP_BLOCK - 1)
    cdf = cdist.astype(f32)
    p_sum = jnp.zeros((tq, ncp), f32)
    ps = []
    for g in range(NSA_GROUP):
        sg = jnp.where(cvis, s[g * tq:(g + 1) * tq] - SLOPES[kvh][g] * cdf, NEG_INF)
        m = jnp.max(sg, axis=-1, keepdims=True)
        e = jnp.exp(sg - m)
        p = e / jnp.sum(e, axis=-1, keepdims=True)
        p = jnp.where(any_vis, p, 0.0)
        p_sum = p_sum + p
        ps.append(p)
    p_all = jnp.concatenate(ps, axis=0).astype(bf16)
    o = jnp.dot(p_all, vc.astype(bf16), preferred_element_type=f32)
    return o, p_sum


def _select_blocks(p_sum, pos_col, ns, nsp, tq):
    ncp = p_sum.shape[1]
    ci = lax.broadcasted_iota(jnp.int32, (ncp, nsp), 0) * CMP_STRIDE
    cj = lax.broadcasted_iota(jnp.int32, (ncp, nsp), 1)
    c2s = ((ci < (cj + 1) * SLC_BLOCK) & (ci + CMP_BLOCK > cj * SLC_BLOCK)).astype(f32).astype(bf16)
    imp = _dot_exact_rhs(p_sum, c2s)
    blk = lax.broadcasted_iota(jnp.int32, (tq, nsp), 1)
    cur = pos_col // SLC_BLOCK
    forced = (blk == 0) | (blk == cur) | (blk == cur - 1)
    imp = jnp.where(forced, FORCED, jnp.where(blk <= cur, imp, NEG_INF))
    rank = jnp.zeros((tq, nsp), jnp.int32)
    for jp in range(ns):
        col = imp[:, jp:jp + 1]
        beats = (col > imp) | ((col == imp) & (blk > jp))
        rank = rank + beats.astype(jnp.int32)
    return ((rank < SLC_TOPN) & (blk <= cur)).astype(f32)


def _softmax_tile_update(s, valid, distf, kvh, tq, v_tile, m_s, l_s, acc_s):
    for g in range(NSA_GROUP):
        r = slice(g * tq, (g + 1) * tq)
        sg = jnp.where(valid, s[r] - SLOPES[kvh][g] * distf, NEG_INF)
        m_old = m_s[r]
        m_new = jnp.maximum(m_old, jnp.max(sg, axis=-1, keepdims=True))
        alpha = jnp.exp(m_old - m_new)
        p = jnp.where(valid, jnp.exp(sg - m_new), 0.0)
        l_s[r] = alpha * l_s[r] + jnp.sum(p, axis=-1, keepdims=True)
        acc_s[r] = alpha * acc_s[r] + jnp.dot(p.astype(bf16), v_tile, preferred_element_type=f32)
        m_s[r] = m_new


def _reset_softmax_state(m_s, l_s, acc_s):
    m_s[...] = jnp.full(m_s.shape, NEG_INF, f32)
    l_s[...] = jnp.zeros(l_s.shape, f32)
    acc_s[...] = jnp.zeros(acc_s.shape, f32)


def _combine(gates, kvh, tq, o_cmp, o_slc, o_win):
    lo = kvh * HEAD_DIM
    outs = []
    for g in range(NSA_GROUP):
        h = kvh * NSA_GROUP + g
        r = slice(g * tq, (g + 1) * tq)
        o = (gates[:, 3 * h:3 * h + 1] * o_cmp[r] + gates[:, 3 * h + 1:3 * h + 2] * o_slc[r]
             + gates[:, 3 * h + 2:3 * h + 3] * o_win[r])
        outs.append(o[:, lo:lo + HEAD_DIM])
    return outs


def _nsa_prompt_kernel(q_ref, g_ref, sk_ref, sv_ref, wk_ref, wv_ref, kc_ref, vc_ref, o_ref,
                       m_s, l_s, acc_s, *, tq, kt, ns):
    qi = pl.program_id(1)
    p0 = qi * tq
    pos_col = p0 + lax.broadcasted_iota(jnp.int32, (tq, 1), 0)
    gates = _sigmoid(g_ref[0])
    q_all = q_ref[0]
    kc = kc_ref[0]
    vc = vc_ref[0]
    outs = []
    for kvh in range(NSA_KV_HEADS):
        qm = _padded_queries(q_all, kvh, tq)
        o_cmp, p_sum = _cmp_branch(qm, kc, vc, pos_col, kvh, tq)
        sel = _select_blocks(p_sum, pos_col, ns, LANES, tq).astype(bf16)

        _reset_softmax_state(m_s, l_s, acc_s)

        def slc_body(c, _):
            k0 = pl.multiple_of(c * kt, kt)
            k_tile = sk_ref[0, pl.ds(k0, kt), :].astype(bf16)
            v_tile = sv_ref[0, pl.ds(k0, kt), :].astype(bf16)
            s = lax.dot_general(qm, k_tile, NT, preferred_element_type=f32)
            ej = lax.broadcasted_iota(jnp.int32, (LANES, kt), 0)
            ek = (k0 + lax.broadcasted_iota(jnp.int32, (LANES, kt), 1)) // SLC_BLOCK
            expand = (ej == ek).astype(f32).astype(bf16)
            sel_k = jnp.dot(sel, expand, preferred_element_type=f32)
            dist = pos_col - (k0 + lax.broadcasted_iota(jnp.int32, (tq, kt), 1))
            valid = (sel_k > 0.5) & (dist >= 0)
            _softmax_tile_update(s, valid, dist.astype(f32), kvh, tq, v_tile, m_s, l_s, acc_s)
            return 0

        lax.fori_loop(0, (p0 + tq + kt - 1) // kt, slc_body, 0)
        o_slc = acc_s[...] / l_s[...]

        _reset_softmax_state(m_s, l_s, acc_s)
        n_back = WINDOW // tq

        def win_body(r, _):
            k0 = pl.multiple_of((qi - n_back + r) * tq, tq)
            k_tile = wk_ref[0, pl.ds(k0, tq), :].astype(bf16)
            v_tile = wv_ref[0, pl.ds(k0, tq), :].astype(bf16)
            s = lax.dot_general(qm, k_tile, NT, preferred_element_type=f32)
            dist = pos_col - (k0 + lax.broadcasted_iota(jnp.int32, (tq, tq), 1))
            valid = (dist >= 0) & (dist < WINDOW)
            _softmax_tile_update(s, valid, dist.astype(f32), kvh, tq, v_tile, m_s, l_s, acc_s)
            return 0

        lax.fori_loop(jnp.maximum(n_back - qi, 0), n_back + 1, win_body, 0)
        o_win = acc_s[...] / l_s[...]

        outs.extend(_combine(gates, kvh, tq, o_cmp, o_slc, o_win))
    o_ref[0] = jnp.concatenate(outs, axis=1).astype(o_ref.dtype)


def nsa_prompt(q, gates, rows, win, kc, vc, tq=128, kt=256):
    B, T, _ = q.shape
    ns = T // SLC_BLOCK
    assert T % kt == 0 and T % tq == 0 and WINDOW % tq == 0 and ns <= LANES
    ncp = kc.shape[1]
    return pl.pallas_call(
        functools.partial(_nsa_prompt_kernel, tq=tq, kt=kt, ns=ns),
        grid=(B, T // tq),
        in_specs=[pl.BlockSpec((1, tq, NSA_HEADS * LANES), lambda b, i: (b, i, 0)),
                  pl.BlockSpec((1, tq, 3 * NSA_HEADS), lambda b, i: (b, i, 0)),
                  pl.BlockSpec((1, T, LANES), lambda b, i: (b, 0, 2)),
                  pl.BlockSpec((1, T, LANES), lambda b, i: (b, 0, 3)),
                  pl.BlockSpec((1, T, LANES), lambda b, i: (b, 0, 0)),
                  pl.BlockSpec((1, T, LANES), lambda b, i: (b, 0, 1)),
                  pl.BlockSpec((1, ncp, LANES), lambda b, i: (b, 0, 0)),
                  pl.BlockSpec((1, ncp, LANES), lambda b, i: (b, 0, 0))],
        out_specs=pl.BlockSpec((1, tq, NSA_WIDTH), lambda b, i: (b, i, 0)),
        out_shape=jax.ShapeDtypeStruct((B, T, NSA_WIDTH), bf16),
        scratch_shapes=[pltpu.VMEM((NSA_GROUP * tq, 1), f32), pltpu.VMEM((NSA_GROUP * tq, 1), f32),
                        pltpu.VMEM((NSA_GROUP * tq, LANES), f32)],
        compiler_params=_cparams("parallel", "arbitrary"),
        name="nsa_prompt",
    )(q, gates, rows, rows, win, win, kc, vc)


def _nsa_sample_kernel(pt_ref, *refs, pps, n_steps, tq, past_len, ns, nsp):
    page_refs = refs[:pps]
    (q_ref, g_ref, rn_ref, wn_ref, cw_ref, pek_ref, pev_ref, wk_ref, wv_ref, w2k_ref, w2v_ref,
     o_ref, xk_s, xv_s, ks_s, vs_s, win_s) = refs[pps:]
    del pt_ref
    step = pl.program_id(1)
    for i in range(pps):
        page = page_refs[i][0]
        r0 = pl.multiple_of((step * pps + i) * PAGE_SIZE, PAGE_SIZE)
        xk_s[pl.ds(r0, PAGE_SIZE), :] = page[:, 0:LANES]
        xv_s[pl.ds(r0, PAGE_SIZE), :] = page[:, LANES:2 * LANES]
        ks_s[pl.ds(r0, PAGE_SIZE), :] = page[:, 2 * LANES:3 * LANES].astype(bf16)
        vs_s[pl.ds(r0, PAGE_SIZE), :] = page[:, 3 * LANES:4 * LANES].astype(bf16)

    @pl.when(step == n_steps - 1)
    def _():
        nk = ks_s.shape[0]
        tail = nk - past_len
        rn = rn_ref[0]
        ztail = jnp.zeros((tail - tq, LANES), f32)
        ks_s[past_len:, :] = jnp.concatenate([rn[:, 2 * LANES:3 * LANES], ztail], axis=0).astype(bf16)
        vs_s[past_len:, :] = jnp.concatenate([rn[:, 3 * LANES:4 * LANES], ztail], axis=0).astype(bf16)
        nwin = cw_ref.shape[1]
        win_s[0:nwin, :] = cw_ref[0]
        win_s[nwin:, :] = jnp.concatenate(
            [wn_ref[0], jnp.zeros((win_s.shape[0] - nwin - tq, 2 * LANES), f32)], axis=0)

        nch = past_len // CMP_STRIDE
        kc = _compress(lambda l: xk_s[pl.ds(l, nch, stride=CMP_STRIDE), :], nch, pek_ref, wk_ref, w2k_ref)
        vc = _compress(lambda l: xv_s[pl.ds(l, nch, stride=CMP_STRIDE), :], nch, pev_ref, wv_ref, w2v_ref)

        pos_col = past_len + lax.broadcasted_iota(jnp.int32, (tq, 1), 0)
        gates = _sigmoid(g_ref[0])
        q_all = q_ref[0]
        ej = lax.broadcasted_iota(jnp.int32, (nsp, nk), 0)
        ek = lax.broadcasted_iota(jnp.int32, (nsp, nk), 1) // SLC_BLOCK
        expand = (ej == ek).astype(f32).astype(bf16)
        kpos = lax.broadcasted_iota(jnp.int32, (tq, nk), 1)
        dist = pos_col - kpos
        distf = dist.astype(f32)
        wlen = win_s.shape[0]
        wpos = (past_len - nwin) + lax.broadcasted_iota(jnp.int32, (tq, wlen), 1)
        wdist = pos_col - wpos
        wvalid = (wdist >= 0) & (wdist < WINDOW)
        wdistf = wdist.astype(f32)
        wk = win_s[:, 0:LANES].astype(bf16)
        wv = win_s[:, LANES:].astype(bf16)
        outs = []
        for kvh in range(NSA_KV_HEADS):
            qm = _padded_queries(q_all, kvh, tq)
            o_cmp, p_sum = _cmp_branch(qm, kc, vc, pos_col, kvh, tq)
            sel = _select_blocks(p_sum, pos_col, ns, nsp, tq).astype(bf16)
            sel_k = jnp.dot(sel, expand, preferred_element_type=f32)
            valid = (sel_k > 0.5) & (dist >= 0)
            s = lax.dot_general(qm, ks_s[...], NT, preferred_element_type=f32)
            sw = lax.dot_general(qm, wk, NT, preferred_element_type=f32)
            ps, pw = [], []
            for g in range(NSA_GROUP):
                r = slice(g * tq, (g + 1) * tq)
                sg = jnp.where(valid, s[r] - SLOPES[kvh][g] * distf, NEG_INF)
                e = jnp.where(valid, jnp.exp(sg - jnp.max(sg, axis=-1, keepdims=True)), 0.0)
                ps.append(e / jnp.sum(e, axis=-1, keepdims=True))
                sg = jnp.where(wvalid, sw[r] - SLOPES[kvh][g] * wdistf, NEG_INF)
                e = jnp.where(wvalid, jnp.exp(sg - jnp.max(sg, axis=-1, keepdims=True)), 0.0)
                pw.append(e / jnp.sum(e, axis=-1, keepdims=True))
            o_slc = jnp.dot(jnp.concatenate(ps, axis=0).astype(bf16), vs_s[...], preferred_element_type=f32)
            o_win = jnp.dot(jnp.concatenate(pw, axis=0).astype(bf16), wv, preferred_element_type=f32)
            outs.extend(_combine(gates, kvh, tq, o_cmp, o_slc, o_win))
        o_ref[0] = jnp.concatenate(outs, axis=1).astype(o_ref.dtype)


def nsa_sample(page_table, cache, q, gates, rows_new, win_new, cache_win, cw, pps=8):
    B, tq, _ = q.shape
    n_pages = page_table.shape[1]
    past_len = n_pages * PAGE_SIZE
    assert n_pages % pps == 0 and past_len % SLC_BLOCK == 0
    n_steps = n_pages // pps
    nk = past_len + 512
    ns = past_len // SLC_BLOCK + 1
    nsp = -(-ns // LANES) * LANES
    nwin = cache_win.shape[1]
    wlen = nwin + LANES
    pek, wk, w2k, pev, wv, w2v = cw
    per_b = lambda n: pl.BlockSpec((1, tq, n), lambda b, s, pt: (b, 0, 0))
    full = lambda a: pl.BlockSpec(a.shape, lambda b, s, pt: (0,) * a.ndim)
    page_specs = [pl.BlockSpec((1, PAGE_SIZE, 4 * LANES),
                               functools.partial(lambda b, s, pt, i: (pt[b, s * pps + i], 0, 0), i=i))
                  for i in range(pps)]
    grid_spec = pltpu.PrefetchScalarGridSpec(
        num_scalar_prefetch=1,
        grid=(B, n_steps),
        in_specs=page_specs + [per_b(NSA_HEADS * LANES), per_b(3 * NSA_HEADS), per_b(4 * LANES), per_b(2 * LANES),
                               pl.BlockSpec((1, nwin, 2 * LANES), lambda b, s, pt: (b, 0, 0)),
                               full(pek), full(pev), full(wk), full(wv), full(w2k), full(w2v)],
        out_specs=per_b(NSA_WIDTH),
        scratch_shapes=[pltpu.VMEM((past_len, LANES), f32), pltpu.VMEM((past_len, LANES), f32),
                        pltpu.VMEM((nk, LANES), bf16), pltpu.VMEM((nk, LANES), bf16),
                        pltpu.VMEM((wlen, 2 * LANES), f32)])
    return pl.pallas_call(
        functools.partial(_nsa_sample_kernel, pps=pps, n_steps=n_steps, tq=tq, past_len=past_len, ns=ns, nsp=nsp),
        grid_spec=grid_spec,
        out_shape=jax.ShapeDtypeStruct((B, tq, NSA_WIDTH), bf16),
        compiler_params=_cparams("parallel", "arbitrary"),
        name="nsa_sample",
    )(page_table, *([cache] * pps), q, gates, rows_new, win_new, cache_win, pek, pev, wk, wv, w2k, w2v)


def _gdn_prep_kernel(x_ref, halo_ref, cb_ref, cw_ref, ba_ref, al_ref, dtb_ref,
                     u_ref, w_ref, qg_ref, kg_ref, qk_ref, eg_ref, xs_s, *, tb, C, t_valid):
    i = pl.program_id(1)
    HK = GDN_HEADS * GDN_DK

    @pl.when(i == 0)
    def _():
        xs_s[0:8, :] = cb_ref[0]

    @pl.when(i > 0)
    def _():
        xs_s[0:8, :] = halo_ref[0]

    xs_s[8:, :] = x_ref[0]
    ba = ba_ref[0]
    ri = lax.broadcasted_iota(jnp.int32, (C, C), 0)
    ci = lax.broadcasted_iota(jnp.int32, (C, C), 1)
    eye = (ri == ci).astype(f32)
    row_t = i * tb + lax.broadcasted_iota(jnp.int32, (tb, 1), 0)
    live = row_t < t_valid

    def conv_act(c0):
        acc = cw_ref[GDN_CONV - 1:GDN_CONV, c0:c0 + LANES] * xs_s[pl.ds(8, tb), c0:c0 + LANES]
        for k in range(1, GDN_CONV):
            acc = acc + cw_ref[GDN_CONV - 1 - k:GDN_CONV - k, c0:c0 + LANES] * xs_s[pl.ds(8 - k, tb), c0:c0 + LANES]
        return _silu(acc)

    for h in range(GDN_HEADS):
        q = conv_act(h * GDN_DK)
        k = conv_act(HK + h * GDN_DK)
        v = conv_act(2 * HK + h * GDN_DV)
        q = q * lax.rsqrt(jnp.sum(q * q, axis=-1, keepdims=True) + L2_EPS) * (GDN_DK ** -0.5)
        k = k * lax.rsqrt(jnp.sum(k * k, axis=-1, keepdims=True) + L2_EPS)
        beta = _sigmoid(ba[:, h:h + 1])
        ar = ba[:, GDN_HEADS + h:GDN_HEADS + h + 1] + dtb_ref[:, h:h + 1]
        softplus = jnp.maximum(ar, 0.0) + jnp.log(1.0 + jnp.exp(-jnp.abs(ar)))
        gcol = -jnp.exp(al_ref[:, h:h + 1]) * softplus
        if t_valid < 10 ** 9:
            beta = jnp.where(live, beta, 0.0)
            gcol = jnp.where(live, gcol, 0.0)
            k = jnp.where(live, k, 0.0)
            v = jnp.where(live, v, 0.0)
        for c in range(tb // C):
            r = slice(c * C, (c + 1) * C)
            qc, kc, vc, bc = q[r], k[r], v[r], beta[r]
            gc = jnp.broadcast_to(gcol[r], (C, LANES))
            sh = 1
            while sh < C:
                rolled = pltpu.roll(gc, sh, 0)
                gc = gc + jnp.where(lax.broadcasted_iota(jnp.int32, (C, LANES), 0) >= sh, rolled, 0.0)
                sh *= 2
            gct = gc.T
            diff = gc[:, 0:C] - gct[0:C, :]
            decay = jnp.exp(jnp.where(ri >= ci, diff, NEG_INF))
            kb = kc * bc
            a = jnp.where(ri > ci, _dot_hp(kb, kc, NT) * decay, 0.0)
            n = -a
            t = eye + n
            sh = 2
            while sh < C:
                n = _dot_hp(n, n)
                t = _dot_hp(t, eye + n)
                sh *= 2
            rhs = jnp.concatenate([vc * bc, kb * jnp.exp(gc)], axis=1)
            sol = _dot_hp(t, rhs)
            qk = jnp.where(ri >= ci, lax.dot_general(qc.astype(bf16), kc.astype(bf16), NT,
                                                     preferred_element_type=f32) * decay, 0.0)
            gl = gc[C - 1:C, :]
            cs = slice(h * GDN_DV, (h + 1) * GDN_DV)
            u_ref[0, r, cs] = sol[:, 0:GDN_DV]
            w_ref[0, r, cs] = sol[:, GDN_DV:].astype(w_ref.dtype)
            qg_ref[0, r, cs] = (qc * jnp.exp(gc)).astype(qg_ref.dtype)
            kg_ref[0, r, cs] = (kc * jnp.exp(gl - gc)).astype(kg_ref.dtype)
            qk_ref[0, r, h * C:(h + 1) * C] = qk.astype(qk_ref.dtype)
            eg_ref[0, c, h:h + 1, :] = jnp.exp(gl)


def gdn_prep(qkv, conv_buf8, conv_w, ba, a_log, dt_bias, C, tb, t_valid):
    B, T, CH = qkv.shape
    nb = T // tb
    hb = tb // 8
    per = lambda n, dt: (pl.BlockSpec((1, tb, n), lambda b, i: (b, i, 0)), jax.ShapeDtypeStruct((B, T, n), dt))
    outs = [per(GDN_WIDTH, f32), per(GDN_WIDTH, bf16), per(GDN_WIDTH, bf16), per(GDN_WIDTH, bf16),
            per(GDN_HEADS * C, bf16),
            (pl.BlockSpec((1, tb // C, 8, LANES), lambda b, i: (b, i, 0, 0)),
             jax.ShapeDtypeStruct((B, T // C, 8, LANES), f32))]
    return pl.pallas_call(
        functools.partial(_gdn_prep_kernel, tb=tb, C=C, t_valid=t_valid),
        grid=(B, nb),
        in_specs=[pl.BlockSpec((1, tb, CH), lambda b, i: (b, i, 0)),
                  pl.BlockSpec((1, 8, CH), lambda b, i: (b, jnp.maximum(i * hb - 1, 0), 0)),
                  pl.BlockSpec((1, 8, CH), lambda b, i: (b, 0, 0)),
                  pl.BlockSpec((GDN_CONV, CH), lambda b, i: (0, 0)),
                  pl.BlockSpec((1, tb, 2 * GDN_HEADS), lambda b, i: (b, i, 0)),
                  pl.BlockSpec((1, GDN_HEADS), lambda b, i: (0, 0)),
                  pl.BlockSpec((1, GDN_HEADS), lambda b, i: (0, 0))],
        out_specs=[o[0] for o in outs],
        out_shape=[o[1] for o in outs],
        scratch_shapes=[pltpu.VMEM((tb + 8, CH), f32)],
        compiler_params=_cparams("parallel", "arbitrary"),
        name="gdn_prep",
    )(qkv, qkv, conv_buf8, conv_w, ba, a_log.reshape(1, GDN_HEADS), dt_bias.reshape(1, GDN_HEADS))


def _gdn_scan_kernel(u_ref, w_ref, qg_ref, kg_ref, qk_ref, eg_ref, z_ref, ng_ref, s0_ref,
                     o_ref, sT_ref, s_s, *, C):
    c = pl.program_id(1)

    @pl.when(c == 0)
    def _():
        s_s[...] = s0_ref[0]

    outs = []
    for h in range(GDN_HEADS):
        cs = slice(h * GDN_DV, (h + 1) * GDN_DV)
        S = s_s[h]
        Sb = S.astype(bf16)
        v_new = u_ref[0, :, cs] - jnp.dot(w_ref[0, :, cs], Sb, preferred_element_type=f32)
        vb = v_new.astype(bf16)
        o = (jnp.dot(qg_ref[0, :, cs], Sb, preferred_element_type=f32)
             + jnp.dot(qk_ref[0, :, h * C:(h + 1) * C], vb, preferred_element_type=f32))
        s_s[h] = S * eg_ref[0, 0, h:h + 1, :] + lax.dot_general(kg_ref[0, :, cs], vb, TN,
                                                                  preferred_element_type=f32)
        z = z_ref[0, :, cs]
        outs.append(_rms(o, ng_ref[...]) * _silu(z))
    o_ref[0] = jnp.concatenate(outs, axis=1).astype(o_ref.dtype)

    @pl.when(c == pl.num_programs(1) - 1)
    def _():
        sT_ref[0] = s_s[...]


def gdn_scan(u, w, qg, kg, qk, eg, z, norm_g, s0, C):
    B, T, _ = u.shape
    per = lambda n: pl.BlockSpec((1, C, n), lambda b, c: (b, c, 0))
    st = pl.BlockSpec((1, GDN_HEADS, GDN_DK, GDN_DV), lambda b, c: (b, 0, 0, 0))
    return pl.pallas_call(
        functools.partial(_gdn_scan_kernel, C=C),
        grid=(B, T // C),
        in_specs=[per(GDN_WIDTH), per(GDN_WIDTH), per(GDN_WIDTH), per(GDN_WIDTH), per(GDN_HEADS * C),
                  pl.BlockSpec((1, 1, 8, LANES), lambda b, c: (b, c, 0, 0)),
                  per(GDN_WIDTH), pl.BlockSpec((1, GDN_DV), lambda b, c: (0, 0)), st],
        out_specs=[per(GDN_WIDTH), st],
        out_shape=[jax.ShapeDtypeStruct((B, T, GDN_WIDTH), bf16),
                   jax.ShapeDtypeStruct((B, GDN_HEADS, GDN_DK, GDN_DV), f32)],
        scratch_shapes=[pltpu.VMEM((GDN_HEADS, GDN_DK, GDN_DV), f32)],
        compiler_params=_cparams("parallel", "arbitrary"),
        name="gdn_scan",
    )(u, w, qg, kg, qk, eg, z, norm_g.reshape(1, GDN_DV), s0)


def _nsa_in_groups(w):
    c1 = NSA_WIDTH
    c2 = c1 + 4 * NSA_KV_COLS
    c3 = c2 + 2 * NSA_KV_COLS
    c4 = c3 + 3 * NSA_HEADS
    wb = w.astype(bf16)
    wq = wb[:, :c1].reshape(-1, NSA_KV_HEADS, NSA_GROUP, HEAD_DIM)
    zq = jnp.zeros_like(wq[:, 0])
    wq = jnp.concatenate([jnp.concatenate([wq[:, 0], zq], axis=-1), jnp.concatenate([zq, wq[:, 1]], axis=-1)], axis=1)
    wq = wq.reshape(-1, NSA_HEADS * LANES)
    return [(wq, bf16, HEAD_DIM ** -0.5), (wb[:, c1:c2], f32, 1.0), (wb[:, c2:c3], f32, 1.0),
            (wb[:, c3:c4], f32, 1.0), (wb[:, c4:], bf16, XA_DIM ** -0.5)]


def _gdn_in_groups(w):
    c1 = GDN_CONV_CH
    c2 = c1 + GDN_WIDTH
    c3 = c2 + 2 * GDN_HEADS
    wb = w.astype(bf16)
    return [(wb[:, :c1], f32, 1.0), (wb[:, c1:c2], f32, 1.0), (wb[:, c2:c3], f32, 1.0),
            (wb[:, c3:], bf16, XA_DIM ** -0.5)]


def _trunk(x, mem_kv, nsa_past, gdn_state, p, t_valid):
    B, T, D = x.shape
    N = B * T
    xf = x.reshape(N, D)

    q, rows, win, gates, xq = norm_proj(xf, p['norm_mix_g'][0], p['nsa_in'])
    q = q.reshape(B, T, -1)
    rows3 = rows.reshape(B, T, -1)
    win3 = win.reshape(B, T, -1)
    gates = gates.reshape(B, T, -1)
    if nsa_past is None:
        kc, vc = cmp_prompt(rows3, p['cmp'])
        o_mix = nsa_prompt(q, gates, rows3, win3, kc, vc)
    else:
        page_table, cache, cache_win = nsa_past
        o_mix = nsa_sample(page_table, cache, q, gates, rows3, win3, cache_win, p['cmp'])
    o_mem = cross_attend(xq.reshape(B, T, -1), mem_kv, 0)
    xf = post_block(xf, o_mix.reshape(N, -1), o_mem.reshape(N, -1), p['w_out'][0], p['norm_ffn_g'][0],
                    p['w_up'][0], p['w_down'][0], p['final_norm_g'], final=False)

    qkv, z, ba, xq = norm_proj(xf, p['norm_mix_g'][1], p['gdn_in'])
    qkv3 = qkv.reshape(B, T, -1)
    S0, conv_buf8 = gdn_state
    C = min(64, T)
    tb = min(256, T)
    u, w, qg, kg, qk, eg = gdn_prep(qkv3, conv_buf8, p['gdn_conv_w'], ba.reshape(B, T, -1), p['gdn_a_log'],
                                    p['gdn_dt_bias'], C, tb, t_valid)
    o_mix, S = gdn_scan(u, w, qg, kg, qk, eg, z.reshape(B, T, -1), p['gdn_norm_g'], S0, C)
    o_mem = cross_attend(xq.reshape(B, T, -1), mem_kv, 1)
    y = post_block(xf, o_mix.reshape(N, -1), o_mem.reshape(N, -1), p['w_out'][1], p['norm_ffn_g'][1],
                   p['w_up'][1], p['w_down'][1], p['final_norm_g'], final=True)
    return y.reshape(B, T, D), rows3, win3, S, qkv3


def kernel(x_prompt, x_sample, mem_prompt, cache_nsa_kv, cache_nsa_win, state_gdn_s, state_gdn_conv,
           cache_mem_kv, page_table, norm_mix_g, norm_mem_g, w_mem_kv, nsa_w_in, cmp_pe_k, cmp_w1_k,
           cmp_w2_k, cmp_pe_v, cmp_w1_v, cmp_w2_v, gdn_w_in, gdn_conv_w, gdn_a_log, gdn_dt_bias,
           gdn_norm_g, w_out, norm_ffn_g, w_up, w_down, final_norm_g):
    B, T, D = x_prompt.shape
    DB, TS, _ = x_sample.shape
    M = mem_prompt.shape[1]
    p = dict(norm_mix_g=norm_mix_g, nsa_in=_nsa_in_groups(nsa_w_in[0]), gdn_in=_gdn_in_groups(gdn_w_in[0]),
             cmp=_cmp_weights(cmp_pe_k[0], cmp_w1_k[0], cmp_w2_k[0]) + _cmp_weights(cmp_pe_v[0], cmp_w1_v[0], cmp_w2_v[0]),
             gdn_conv_w=gdn_conv_w[0], gdn_a_log=gdn_a_log[0], gdn_dt_bias=gdn_dt_bias[0], gdn_norm_g=gdn_norm_g[0],
             w_out=w_out.astype(bf16), norm_ffn_g=norm_ffn_g, w_up=w_up.astype(bf16), w_down=w_down.astype(bf16),
             final_norm_g=final_norm_g)

    mem_flat = mem_prompt.reshape(B * M, D)
    mem_kv = [norm_proj(mem_flat, norm_mem_g[i], [(w_mem_kv[i].astype(bf16), f32, 1.0)])[0].reshape(B, 1, M, -1)
              for i in range(DEPTH)]
    mem_kv_p = jnp.concatenate(mem_kv, axis=1)
    zero_state = (jnp.zeros((B, GDN_HEADS, GDN_DK, GDN_DV), f32), jnp.zeros((B, 8, GDN_CONV_CH), f32))
    y_p, rows_p, win_p, S_p, qkv_p = _trunk(x_prompt, mem_kv_p, None, zero_state, p, 10 ** 9)

    TP = 8
    x_s = jnp.pad(x_sample, ((0, 0), (0, TP - TS), (0, 0)))
    n_pages = page_table.shape[1]
    cache = cache_nsa_kv.reshape(cache_nsa_kv.shape[0], PAGE_SIZE, -1)
    cache_win = cache_nsa_win.reshape(DB, cache_nsa_win.shape[2], -1)
    conv8 = jnp.pad(state_gdn_conv[:, 0], ((0, 0), (8 - (GDN_CONV - 1), 0), (0, 0)))
    mem_kv_s = cache_mem_kv.reshape(DB, DEPTH, M, -1)
    y_s, rows_s, win_s, S_s, qkv_s = _trunk(x_s, mem_kv_s, (page_table, cache, cache_win),
                                            (state_gdn_s[:, 0], conv8), p, TS)

    kvshape = (4, NSA_KV_HEADS, HEAD_DIM)
    nsa_kv_prompt = rows_p.reshape(B, T, 1, *kvshape)
    nsa_kv_sample = rows_s[:, :TS].reshape(DB, TS, 1, *kvshape)
    wshape = (2, NSA_KV_HEADS, HEAD_DIM)
    nsa_win_prompt = win_p[:, -min(WINDOW, T):].reshape(B, 1, -1, *wshape)
    win_cat = jnp.concatenate([cache_win, win_s[:, :TS]], axis=1)
    nsa_win_sample = win_cat[:, -WINDOW:].reshape(DB, 1, -1, *wshape)
    gdn_conv_prompt = qkv_p[:, None, T - (GDN_CONV - 1):]
    conv_cat = jnp.concatenate([state_gdn_conv[:, 0], qkv_s[:, :TS]], axis=1)
    gdn_conv_sample = conv_cat[:, None, -(GDN_CONV - 1):]
    mem_kv_prompt = mem_kv_p.reshape(B, DEPTH, M, 2, XA_HEADS, XA_DIM)
    return (y_p, y_s[:, :TS], nsa_kv_prompt, nsa_kv_sample, nsa_win_prompt, nsa_win_sample,
            S_p[:, None], S_s[:, None], gdn_conv_prompt, gdn_conv_sample, mem_kv_prompt)
```

```python
import functools

import jax
import jax.numpy as jnp
from jax import lax
from jax.experimental import pallas as pl
from jax.experimental.pallas import tpu as pltpu

f32 = jnp.float32
bf16 = jnp.bfloat16

D_MODEL = 1024
DEPTH = 2
PAGE_SIZE = 128
HEAD_DIM = 64
NSA_HEADS = 12
NSA_KV_HEADS = 2
NSA_GROUP = NSA_HEADS // NSA_KV_HEADS
NSA_WIDTH = NSA_HEADS * HEAD_DIM
NSA_KV_COLS = NSA_KV_HEADS * HEAD_DIM
CMP_STRIDE = 16
CMP_BLOCK = 2 * CMP_STRIDE
CMP_HIDDEN = 128
SLC_BLOCK = 64
SLC_TOPN = 16
WINDOW = 512
GDN_HEADS = 6
GDN_DK = 128
GDN_DV = 128
GDN_WIDTH = GDN_HEADS * GDN_DV
GDN_CONV = 4
GDN_CONV_CH = 2 * GDN_HEADS * GDN_DK + GDN_HEADS * GDN_DV
XA_HEADS = 4
XA_DIM = 64
XA_WIDTH = XA_HEADS * XA_DIM
D_FF = 4 * D_MODEL
NORM_EPS = 1e-6
L2_EPS = 1e-6
NEG_INF = -1e30
FORCED = 1e9

VMEM_LIMIT = 52 * 1024 * 1024
LANES = 128

SLOPES = [[2.0 ** (-8.0 * (k * NSA_GROUP + g + 1) / NSA_HEADS) for g in range(NSA_GROUP)]
          for k in range(NSA_KV_HEADS)]

NT = (((1,), (1,)), ((), ()))
TN = (((0,), (0,)), ((), ()))


def _cparams(*sem):
    return pltpu.CompilerParams(dimension_semantics=sem, vmem_limit_bytes=VMEM_LIMIT)


def _rms(x, g):
    r = lax.rsqrt(jnp.mean(x * x, axis=-1, keepdims=True) + NORM_EPS)
    return (x * r) * g


def _sigmoid(x):
    return 1.0 / (1.0 + jnp.exp(-x))


def _silu(x):
    return x * _sigmoid(x)


def _split3(a):
    hi = a.astype(bf16)
    r1 = a - hi.astype(f32)
    mid = r1.astype(bf16)
    lo = (r1 - mid.astype(f32)).astype(bf16)
    return hi, mid, lo


def _dot_exact_rhs(a, b_bf16):
    hi, mid, lo = _split3(a)
    d = lambda x: jnp.dot(x, b_bf16, preferred_element_type=f32)
    return d(hi) + d(mid) + d(lo)


def _split2(a):
    hi = a.astype(bf16)
    return hi, (a - hi.astype(f32)).astype(bf16)


def _dot_parts(a, b, dims=None):
    if dims is None:
        d = lambda x, y: jnp.dot(x, y, preferred_element_type=f32)
    else:
        d = lambda x, y: lax.dot_general(x, y, dims, preferred_element_type=f32)
    return d(a[0], b[0]) + (d(a[0], b[1]) + d(a[1], b[0]))


def _dot_hp(a, b, dims=None):
    return _dot_parts(_split2(a), _split2(b), dims)


def _norm_proj_kernel(x_ref, g_ref, *refs, specs):
    n = len(specs)
    w_refs, o_refs = refs[:n], refs[n:]
    h = _rms(x_ref[...], g_ref[...]).astype(bf16)
    for (scale, transposed), w_ref, o_ref in zip(specs, w_refs, o_refs):
        if transposed:
            y = lax.dot_general(w_ref[...], h, NT, preferred_element_type=f32)
        else:
            y = jnp.dot(h, w_ref[...], preferred_element_type=f32)
        if scale != 1.0:
            y = y * scale
        o_ref[...] = y.astype(o_ref.dtype)


def norm_proj(x, g, groups, tm=256):
    N, D = x.shape
    tm = min(tm, N)
    assert N % tm == 0
    in_specs = [pl.BlockSpec((tm, D), lambda i: (i, 0)), pl.BlockSpec((1, D), lambda i: (0, 0))]
    out_specs, out_shape = [], []
    for w, dt, _, transposed in groups:
        in_specs.append(pl.BlockSpec(w.shape, lambda i: (0, 0)))
        if transposed:
            n = w.shape[0]
            out_specs.append(pl.BlockSpec((n, tm), lambda i: (0, i)))
            out_shape.append(jax.ShapeDtypeStruct((n, N), dt))
        else:
            n = w.shape[1]
            out_specs.append(pl.BlockSpec((tm, n), lambda i: (i, 0)))
            out_shape.append(jax.ShapeDtypeStruct((N, n), dt))
    return pl.pallas_call(
        functools.partial(_norm_proj_kernel, specs=tuple((s, t) for _, _, s, t in groups)),
        grid=(N // tm,),
        in_specs=in_specs,
        out_specs=out_specs,
        out_shape=out_shape,
        compiler_params=_cparams("parallel"),
        name="norm_proj",
    )(x, g.reshape(1, D), *[w for w, _, _, _ in groups])


def _xattn_kernel(q_ref, kv_ref, o_ref):
    q = q_ref[0]
    kv = kv_ref[0, 0]
    outs = []
    for h in range(XA_HEADS):
        qh = q[:, h * XA_DIM:(h + 1) * XA_DIM]
        kh = kv[:, h * XA_DIM:(h + 1) * XA_DIM].astype(bf16)
        vh = kv[:, XA_WIDTH + h * XA_DIM:XA_WIDTH + (h + 1) * XA_DIM].astype(bf16)
        s = lax.dot_general(qh, kh, NT, preferred_element_type=f32)
        m = jnp.max(s, axis=-1, keepdims=True)
        e = jnp.exp(s - m)
        p = e / jnp.sum(e, axis=-1, keepdims=True)
        outs.append(jnp.dot(p.astype(bf16), vh, preferred_element_type=f32))
    o_ref[0] = jnp.concatenate(outs, axis=1).astype(o_ref.dtype)


def cross_attend(xq, mkv, layer, tq=256):
    B, T, _ = xq.shape
    M = mkv.shape[2]
    tq = min(tq, T)
    return pl.pallas_call(
        _xattn_kernel,
        grid=(B, T // tq),
        in_specs=[pl.BlockSpec((1, tq, XA_WIDTH), lambda b, i: (b, i, 0)),
                  pl.BlockSpec((1, 1, M, 2 * XA_WIDTH), lambda b, i: (b, layer, 0, 0))],
        out_specs=pl.BlockSpec((1, tq, XA_WIDTH), lambda b, i: (b, i, 0)),
        out_shape=jax.ShapeDtypeStruct((B, T, XA_WIDTH), bf16),
        compiler_params=_cparams("parallel", "parallel"),
        name="cross_attend",
    )(xq, mkv)


def _post_kernel(x_ref, om_ref, oc_ref, wo_ref, gf_ref, wu_ref, wd_ref, gl_ref, o_ref,
                 x1_s, h_s, acc_s, *, mix_w, final):
    j = pl.program_id(1)

    @pl.when(j == 0)
    def _():
        x1 = x_ref[...] + (jnp.dot(om_ref[...], wo_ref[0:mix_w, :], preferred_element_type=f32)
                           + jnp.dot(oc_ref[...], wo_ref[mix_w:, :], preferred_element_type=f32))
        x1_s[...] = x1
        h_s[...] = _rms(x1, gf_ref[...]).astype(bf16)
        acc_s[...] = jnp.zeros_like(acc_s)

    u = jnp.dot(h_s[...], wu_ref[...], preferred_element_type=f32)
    u = jnp.square(jnp.maximum(u, 0.0)).astype(bf16)
    acc_s[...] += jnp.dot(u, wd_ref[...], preferred_element_type=f32)

    @pl.when(j == pl.num_programs(1) - 1)
    def _():
        x2 = x1_s[...] + acc_s[...]
        if final:
            x2 = _rms(x2, gl_ref[...])
        o_ref[...] = x2


def post_block(x, o_mix, o_mem, w_out, g_ffn, w_up, w_down, g_final, final, tm=1024, tf=512):
    N, D = x.shape
    mix_w = o_mix.shape[1]
    F = w_up.shape[1]
    tm = min(tm, N)
    return pl.pallas_call(
        functools.partial(_post_kernel, mix_w=mix_w, final=final),
        grid=(N // tm, F // tf),
        in_specs=[pl.BlockSpec((tm, D), lambda i, j: (i, 0)),
                  pl.BlockSpec((tm, mix_w), lambda i, j: (i, 0)),
                  pl.BlockSpec((tm, XA_WIDTH), lambda i, j: (i, 0)),
                  pl.BlockSpec((mix_w + XA_WIDTH, D), lambda i, j: (0, 0)),
                  pl.BlockSpec((1, D), lambda i, j: (0, 0)),
                  pl.BlockSpec((D, tf), lambda i, j: (0, j)),
                  pl.BlockSpec((tf, D), lambda i, j: (j, 0)),
                  pl.BlockSpec((1, D), lambda i, j: (0, 0))],
        out_specs=pl.BlockSpec((tm, D), lambda i, j: (i, 0)),
        out_shape=jax.ShapeDtypeStruct((N, D), f32),
        scratch_shapes=[pltpu.VMEM((tm, D), f32), pltpu.VMEM((tm, D), bf16), pltpu.VMEM((tm, D), f32)],
        compiler_params=_cparams("parallel", "arbitrary"),
        name="post_block",
    )(x, o_mix, o_mem, w_out, g_ffn.reshape(1, D), w_up, w_down, g_final.reshape(1, D))


def _compress(load_chunk_rows, nch, pe_ref, w_ref, w2_ref):
    acc_p = jnp.zeros((nch, 2 * CMP_HIDDEN), f32)
    acc_q = jnp.zeros((nch, 2 * CMP_HIDDEN), f32)
    for l in range(CMP_STRIDE):
        x = load_chunk_rows(l)
        xp = (x + pe_ref[l:l + 1, :]).astype(bf16)
        xq = (x + pe_ref[CMP_STRIDE + l:CMP_STRIDE + l + 1, :]).astype(bf16)
        acc_p = acc_p + jnp.dot(xp, w_ref[l, :, 0:2 * CMP_HIDDEN], preferred_element_type=f32)
        acc_q = acc_q + jnp.dot(xq, w_ref[l, :, 2 * CMP_HIDDEN:], preferred_element_type=f32)
    hid = _silu(acc_p + pltpu.roll(acc_q, nch - 1, 0)).astype(bf16)
    outs = [jnp.dot(hid[:, h * CMP_HIDDEN:(h + 1) * CMP_HIDDEN], w2_ref[...], preferred_element_type=f32)
            for h in range(NSA_KV_HEADS)]
    return jnp.concatenate(outs, axis=1)


def _cmp_prompt_kernel(xk_ref, xv_ref, pek_ref, pev_ref, wk_ref, wv_ref, w2k_ref, w2v_ref, kc_ref, vc_ref, *, nch):
    kc_ref[0] = _compress(lambda l: xk_ref[0, pl.ds(l, nch, stride=CMP_STRIDE), :], nch, pek_ref, wk_ref, w2k_ref)
    vc_ref[0] = _compress(lambda l: xv_ref[0, pl.ds(l, nch, stride=CMP_STRIDE), :], nch, pev_ref, wv_ref, w2v_ref).T


def _cmp_weights(pe, w1, w2):
    pe2 = jnp.tile(pe, (1, NSA_KV_HEADS))
    z = jnp.zeros((CMP_STRIDE, HEAD_DIM, CMP_HIDDEN), w1.dtype)
    a, b = w1[:CMP_STRIDE], w1[CMP_STRIDE:]
    top = jnp.concatenate([a, z, b, z], axis=2)
    bot = jnp.concatenate([z, a, z, b], axis=2)
    wbd = jnp.concatenate([top, bot], axis=1).astype(bf16)
    return pe2, wbd, w2.astype(bf16)


def cmp_prompt(rows, cw):
    B, T, _ = rows.shape
    nch = T // CMP_STRIDE
    pek, wk, w2k, pev, wv, w2v = cw
    full = lambda a: pl.BlockSpec(a.shape, lambda b: (0,) * a.ndim)
    return pl.pallas_call(
        functools.partial(_cmp_prompt_kernel, nch=nch),
        grid=(B,),
        in_specs=[pl.BlockSpec((1, T, LANES), lambda b: (b, 0, 0)),
                  pl.BlockSpec((1, T, LANES), lambda b: (b, 0, 1)),
                  full(pek), full(pev), full(wk), full(wv), full(w2k), full(w2v)],
        out_specs=[pl.BlockSpec((1, nch, LANES), lambda b: (b, 0, 0)), pl.BlockSpec((1, LANES, nch), lambda b: (b, 0, 0))],
        out_shape=[jax.ShapeDtypeStruct((B, nch, LANES), f32), jax.ShapeDtypeStruct((B, LANES, nch), f32)],
        compiler_params=_cparams("parallel"),
        name="cmp_prompt",
    )(rows, rows, pek, pev, wk, wv, w2k, w2v)


def _padded_queries(q_all, kvh, tq):
    del tq
    lo = kvh * NSA_GROUP
    return jnp.concatenate([q_all[:, (lo + g) * LANES:(lo + g + 1) * LANES] for g in range(NSA_GROUP)], axis=0)


def _cmp_branch(qm, kc, vc, pos_col, kvh, tq):
    ncp = kc.shape[0]
    s = lax.dot_general(qm, kc.astype(bf16), NT, preferred_element_type=f32)
    n_idx = lax.broadcasted_iota(jnp.int32, (tq, ncp), 1)
    cdist = pos_col - (n_idx * CMP_STRIDE + (CMP_BLOCK - 1))
    cvis = cdist >= 0
    any_vis = pos_col >= (CMP_BLOCK - 1)
    cdf = cdist.astype(f32)
    p_sum = jnp.zeros((tq, ncp), f32)
    ps = []
    for g in range(NSA_GROUP):
        sg = jnp.where(cvis, s[g * tq:(g + 1) * tq] - SLOPES[kvh][g] * cdf, NEG_INF)
        m = jnp.max(sg, axis=-1, keepdims=True)
        e = jnp.exp(sg - m)
        p = e / jnp.sum(e, axis=-1, keepdims=True)
        p = jnp.where(any_vis, p, 0.0)
        p_sum = p_sum + p
        ps.append(p)
    p_all = jnp.concatenate(ps, axis=0).astype(bf16)
    o = jnp.dot(p_all, vc.astype(bf16), preferred_element_type=f32)
    return o, p_sum


def _select_blocks(p_sum, pos_col, ns, nsp, tq):
    ncp = p_sum.shape[1]
    ci = lax.broadcasted_iota(jnp.int32, (ncp, nsp), 0) * CMP_STRIDE
    cj = lax.broadcasted_iota(jnp.int32, (ncp, nsp), 1)
    c2s = ((ci < (cj + 1) * SLC_BLOCK) & (ci + CMP_BLOCK > cj * SLC_BLOCK)).astype(f32).astype(bf16)
    imp = _dot_exact_rhs(p_sum, c2s)
    blk = lax.broadcasted_iota(jnp.int32, (tq, nsp), 1)
    cur = pos_col // SLC_BLOCK
    forced = (blk == 0) | (blk == cur) | (blk == cur - 1)
    imp = jnp.where(forced, FORCED, jnp.where(blk <= cur, imp, NEG_INF))
    rank = jnp.zeros((tq, nsp), jnp.int32)
    for jp in range(ns):
        col = imp[:, jp:jp + 1]
        beats = (col > imp) | ((col == imp) & (blk > jp))
        rank = rank + beats.astype(jnp.int32)
    return ((rank < SLC_TOPN) & (blk <= cur)).astype(f32)


def _combine(gates, kvh, tq, o_cmp, o_slc, o_win):
    lo = kvh * HEAD_DIM
    outs = []
    for g in range(NSA_GROUP):
        h = kvh * NSA_GROUP + g
        r = slice(g * tq, (g + 1) * tq)
        o = (gates[:, 3 * h:3 * h + 1] * o_cmp[r] + gates[:, 3 * h + 1:3 * h + 2] * o_slc[r]
             + gates[:, 3 * h + 2:3 * h + 3] * o_win[r])
        outs.append(o[:, lo:lo + HEAD_DIM])
    return outs


LOG2E = 1.4426950408889634
SLOPES2 = [[v * LOG2E for v in row] for row in SLOPES]

POS_SPLIT = 8
F_SEL0 = 16
MASK_BIG = 1e30


def position_features(pos, blk, ns):
    n = pos.shape[0]
    lane = jnp.arange(HEAD_DIM)[None, :]
    hi = ((pos // POS_SPLIT) * POS_SPLIT).astype(f32)[:, None]
    lo = (pos % POS_SPLIT).astype(f32)[:, None]
    f = jnp.where(lane < 3, hi, jnp.where(lane < 6, lo, jnp.where(lane < 9, 1.0, 0.0)))
    if ns:
        f = jnp.where((lane >= F_SEL0) & (lane < F_SEL0 + ns), (blk[:, None] == lane - F_SEL0).astype(f32), f)
    return jnp.broadcast_to(f, (n, HEAD_DIM)).astype(f32)


def feature_tables(T, nch):
    pos = jnp.arange(T, dtype=jnp.int32)
    kf = position_features(pos, pos // SLC_BLOCK, T // SLC_BLOCK)
    cpos = jnp.arange(nch, dtype=jnp.int32) * CMP_STRIDE + (CMP_BLOCK - 1)
    cf = position_features(cpos, cpos, 0)
    z = lambda a: jnp.zeros_like(a)
    two = lambda a: jnp.stack([jnp.concatenate([z(a), a], axis=1), jnp.concatenate([a, z(a)], axis=1)]).astype(bf16)
    return two(kf), two(cf)


def _query_features(pos_row, selT, kvh, tq):
    posf = pos_row.astype(f32)
    r16 = lax.broadcasted_iota(jnp.int32, (16, tq), 0)
    zeros16 = jnp.zeros((16, tq), f32)
    selbig = (selT - 1.0) * MASK_BIG
    ns = selT.shape[0]
    pad = jnp.zeros((HEAD_DIM - F_SEL0 - ns, tq), f32)
    plain, full = [], []
    for g in range(NSA_GROUP):
        s = jnp.full((1, tq), SLOPES2[kvh][g], f32)
        s_parts = _split3(s)
        a_parts = _split3(s * posf)
        rows = [p.astype(f32) for p in s_parts] * 2 + [-p.astype(f32) for p in a_parts]
        blk0 = zeros16
        for i, rv in enumerate(rows):
            blk0 = jnp.where(r16 == i, rv, blk0)
        plain.append(jnp.concatenate([blk0, jnp.zeros((HEAD_DIM - 16, tq), f32)], axis=0).astype(bf16))
        full.append(jnp.concatenate([blk0, selbig, pad], axis=0).astype(bf16))
    return plain, full


def _aug_queries(qT, feats, kvh, tq):
    out = []
    for g in range(NSA_GROUP):
        h = kvh * NSA_GROUP + g
        qh = qT[h * HEAD_DIM:(h + 1) * HEAD_DIM, :]
        out.append(jnp.concatenate([qh, feats[g]] if kvh == 0 else [feats[g], qh], axis=0))
    return out


def _aug_keys(k_tile, feat_tile, kvh):
    lane = lax.broadcasted_iota(jnp.int32, k_tile.shape, 1)
    own = (lane < HEAD_DIM) if kvh == 0 else (lane >= HEAD_DIM)
    return jnp.where(own, k_tile, feat_tile)


def _with_ones(vT, kvh):
    half = vT.shape[0] // 2
    ones = jnp.ones((half, vT.shape[1]), vT.dtype)
    return jnp.concatenate([vT[:half], ones] if kvh == 0 else [ones, vT[half:]], axis=0)


def _cmp_branch_t(q_aug, kc_aug, vcT, pos_row, tq):
    ncp = kc_aug.shape[0]
    n_idx = lax.broadcasted_iota(jnp.int32, (ncp, tq), 0)
    cvis = pos_row >= (n_idx * CMP_STRIDE + (CMP_BLOCK - 1))
    any_vis = pos_row >= (CMP_BLOCK - 1)
    p_sum = jnp.zeros((ncp, tq), f32)
    outs = []
    for g in range(NSA_GROUP):
        sg = jnp.where(cvis, jnp.dot(kc_aug, q_aug[g], preferred_element_type=f32), NEG_INF)
        e = jnp.exp2(sg - jnp.max(sg, axis=0, keepdims=True))
        p = e / jnp.sum(e, axis=0, keepdims=True)
        p = jnp.where(any_vis, p, 0.0)
        p_sum = p_sum + p
        outs.append(jnp.dot(vcT, p.astype(bf16), preferred_element_type=f32))
    return outs, p_sum


def _select_blocks_t(p_sumT, pos_row, ns, tq):
    ncp = p_sumT.shape[0]
    cj = lax.broadcasted_iota(jnp.int32, (ns, ncp), 0)
    ci = lax.broadcasted_iota(jnp.int32, (ns, ncp), 1) * CMP_STRIDE
    c2sT = ((ci < (cj + 1) * SLC_BLOCK) & (ci + CMP_BLOCK > cj * SLC_BLOCK)).astype(f32).astype(bf16)
    hi, mid, lo = _split3(p_sumT)
    d = lambda x: jnp.dot(c2sT, x, preferred_element_type=f32)
    imp = d(hi) + d(mid) + d(lo)
    blk = lax.broadcasted_iota(jnp.int32, (ns, tq), 0)
    cur = pos_row // SLC_BLOCK
    forced = (blk == 0) | (blk == cur) | (blk == cur - 1)
    imp = jnp.where(forced, FORCED, jnp.where(blk <= cur, imp, NEG_INF))
    rank = jnp.zeros((ns, tq), jnp.int32)
    for jp in range(ns):
        row = imp[jp:jp + 1, :]
        beats = (row > imp) | ((row == imp) & (blk > jp))
        rank = rank + beats.astype(jnp.int32)
    return ((rank < SLC_TOPN) & (blk <= cur)).astype(f32)


def _nsa_prompt_kernel(qT_ref, gT_ref, sk_ref, svT_ref, wk_ref, wvT_ref, kc_ref, vcT_ref, kf_ref, cf_ref, o_ref,
                       m_s, acc_s, *, tq, kt, ns):
    qi = pl.program_id(1)
    p0 = qi * tq
    pos_row = p0 + lax.broadcasted_iota(jnp.int32, (1, tq), 1)
    gates = _sigmoid(gT_ref[...])
    qT = qT_ref[...]
    kc = kc_ref[0].astype(bf16)
    vcT = vcT_ref[0].astype(bf16)
    half = LANES // 2
    KV = range(NSA_KV_HEADS)

    o_cmp, q_pos, q_sel = [], [], []
    for kvh in KV:
        zsel = jnp.zeros((ns, tq), f32)
        plain, _ = _query_features(pos_row, zsel, kvh, tq)
        qa = _aug_queries(qT, plain, kvh, tq)
        oc, p_sum = _cmp_branch_t(qa, _aug_keys(kc, cf_ref[kvh], kvh), vcT, pos_row, tq)
        selT = _select_blocks_t(p_sum, pos_row, ns, tq)
        _, full = _query_features(pos_row, selT, kvh, tq)
        o_cmp.append(oc)
        q_pos.append(qa)
        q_sel.append(_aug_queries(qT, full, kvh, tq))

    m_s[...] = jnp.full(m_s.shape, NEG_INF, f32)
    acc_s[...] = jnp.zeros(acc_s.shape, f32)
    rel = pos_row - lax.broadcasted_iota(jnp.int32, (kt, tq), 0)

    def slc_tile(c, causal):
        k0 = pl.multiple_of(c * kt, kt)
        k_tile = sk_ref[pl.ds(k0, kt), :]
        v_tile = svT_ref[:, pl.ds(k0, kt)]
        ok = (rel - k0) >= 0
        for kvh in KV:
            ka = _aug_keys(k_tile, kf_ref[kvh, pl.ds(k0, kt), :], kvh)
            va = _with_ones(v_tile, kvh)
            for g in range(NSA_GROUP):
                cs = slice(g * tq, (g + 1) * tq)
                sg = jnp.dot(ka, q_sel[kvh][g], preferred_element_type=f32)
                if causal:
                    sg = jnp.where(ok, sg, NEG_INF)
                m_old = m_s[kvh, :, cs]
                m_new = jnp.maximum(m_old, jnp.max(sg, axis=0, keepdims=True))
                alpha = jnp.exp2(m_old - m_new)
                p = jnp.exp2(sg - m_new).astype(bf16)
                acc_s[kvh, :, cs] = alpha * acc_s[kvh, :, cs] + jnp.dot(va, p, preferred_element_type=f32)
                m_s[kvh, :, cs] = m_new

    n_kt = (p0 + tq + kt - 1) // kt

    def slc_body(c, _):
        slc_tile(c, False)
        return 0

    lax.fori_loop(0, n_kt - 1, slc_body, 0)
    slc_tile(n_kt - 1, True)

    wlen = WINDOW + tq
    w0 = pl.multiple_of(jnp.maximum(p0 - WINDOW, 0), tq)
    k_win = wk_ref[pl.ds(w0, wlen), :]
    v_win = wvT_ref[:, pl.ds(w0, wlen)]
    dist = pos_row - (w0 + lax.broadcasted_iota(jnp.int32, (wlen, tq), 0))
    valid = (dist >= 0) & (dist < WINDOW)
    heads = []
    for kvh in KV:
        ka = _aug_keys(k_win, kf_ref[kvh, pl.ds(w0, wlen), :], kvh)
        va = _with_ones(v_win, kvh)
        d0, l0 = (0, half) if kvh == 0 else (half, 0)
        for g in range(NSA_GROUP):
            h = kvh * NSA_GROUP + g
            cs = slice(g * tq, (g + 1) * tq)
            sg = jnp.where(valid, jnp.dot(ka, q_pos[kvh][g], preferred_element_type=f32), NEG_INF)
            p = jnp.exp2(sg - jnp.max(sg, axis=0, keepdims=True)).astype(bf16)
            o_win = jnp.dot(va, p, preferred_element_type=f32)
            o_slc = acc_s[kvh, :, cs]
            o = (gates[3 * h:3 * h + 1] * o_cmp[kvh][g][d0:d0 + half]
                 + gates[3 * h + 1:3 * h + 2] * (o_slc[d0:d0 + half] / o_slc[l0:l0 + 1])
                 + gates[3 * h + 2:3 * h + 3] * (o_win[d0:d0 + half] / o_win[l0:l0 + 1]))
            heads.append(o)
    o_ref[0] = jnp.concatenate(heads, axis=0).T.astype(o_ref.dtype)


def nsa_prompt(qT, gT, sk, svT, wk, wvT, kc, vcT, B, tq=128, kt=256):
    N = sk.shape[0]
    T = N // B
    ns = T // SLC_BLOCK
    nq = T // tq
    ncp = kc.shape[1]
    assert T % kt == 0 and T % tq == 0 and kt % tq == 0 and T >= WINDOW + tq and T <= POS_SPLIT * 256
    assert F_SEL0 + ns <= HEAD_DIM
    kf, cf = feature_tables(T, ncp)
    keys = pl.BlockSpec((T, LANES), lambda b, i: (b, 0))
    vals = pl.BlockSpec((LANES, T), lambda b, i: (0, b))
    return pl.pallas_call(
        functools.partial(_nsa_prompt_kernel, tq=tq, kt=kt, ns=ns),
        grid=(B, nq),
        in_specs=[pl.BlockSpec((NSA_WIDTH, tq), lambda b, i: (0, b * nq + i)),
                  pl.BlockSpec((3 * NSA_HEADS, tq), lambda b, i: (0, b * nq + i)),
                  keys, vals, keys, vals,
                  pl.BlockSpec((1, ncp, LANES), lambda b, i: (b, 0, 0)),
                  pl.BlockSpec((1, LANES, ncp), lambda b, i: (b, 0, 0)),
                  pl.BlockSpec((NSA_KV_HEADS, T, LANES), lambda b, i: (0, 0, 0)),
                  pl.BlockSpec((NSA_KV_HEADS, ncp, LANES), lambda b, i: (0, 0, 0))],
        out_specs=pl.BlockSpec((1, tq, NSA_WIDTH), lambda b, i: (b, i, 0)),
        out_shape=jax.ShapeDtypeStruct((B, T, NSA_WIDTH), bf16),
        scratch_shapes=[pltpu.VMEM((NSA_KV_HEADS, 1, NSA_GROUP * tq), f32),
                        pltpu.VMEM((NSA_KV_HEADS, LANES, NSA_GROUP * tq), f32)],
        compiler_params=_cparams("parallel", "arbitrary"),
        name="nsa_prompt",
    )(qT, gT, sk, svT, wk, wvT, kc, vcT, kf, cf)


def _nsa_sample_kernel(pt_ref, *refs, pps, n_steps, tq, past_len, ns, nsp):
    page_refs = refs[:pps]
    (q_ref, g_ref, rn_ref, wn_ref, cw_ref, pek_ref, pev_ref, wk_ref, wv_ref, w2k_ref, w2v_ref,
     o_ref, xk_s, xv_s, ks_s, vs_s, win_s) = refs[pps:]
    del pt_ref
    step = pl.program_id(1)
    for i in range(pps):
        page = page_refs[i][0]
        r0 = pl.multiple_of((step * pps + i) * PAGE_SIZE, PAGE_SIZE)
        xk_s[pl.ds(r0, PAGE_SIZE), :] = page[:, 0:LANES]
        xv_s[pl.ds(r0, PAGE_SIZE), :] = page[:, LANES:2 * LANES]
        ks_s[pl.ds(r0, PAGE_SIZE), :] = page[:, 2 * LANES:3 * LANES].astype(bf16)
        vs_s[pl.ds(r0, PAGE_SIZE), :] = page[:, 3 * LANES:4 * LANES].astype(bf16)

    @pl.when(step == n_steps - 1)
    def _():
        nk = ks_s.shape[0]
        tail = nk - past_len
        rn = rn_ref[0]
        ztail = jnp.zeros((tail - tq, LANES), f32)
        ks_s[past_len:, :] = jnp.concatenate([rn[:, 2 * LANES:3 * LANES], ztail], axis=0).astype(bf16)
        vs_s[past_len:, :] = jnp.concatenate([rn[:, 3 * LANES:4 * LANES], ztail], axis=0).astype(bf16)
        nwin = cw_ref.shape[1]
        win_s[0:nwin, :] = cw_ref[0]
        win_s[nwin:, :] = jnp.concatenate(
            [wn_ref[0], jnp.zeros((win_s.shape[0] - nwin - tq, 2 * LANES), f32)], axis=0)

        nch = past_len // CMP_STRIDE
        kc = _compress(lambda l: xk_s[pl.ds(l, nch, stride=CMP_STRIDE), :], nch, pek_ref, wk_ref, w2k_ref)
        vc = _compress(lambda l: xv_s[pl.ds(l, nch, stride=CMP_STRIDE), :], nch, pev_ref, wv_ref, w2v_ref)

        pos_col = past_len + lax.broadcasted_iota(jnp.int32, (tq, 1), 0)
        gates = _sigmoid(g_ref[0])
        q_all = q_ref[0]
        ej = lax.broadcasted_iota(jnp.int32, (nsp, nk), 0)
        ek = lax.broadcasted_iota(jnp.int32, (nsp, nk), 1) // SLC_BLOCK
        expand = (ej == ek).astype(f32).astype(bf16)
        kpos = lax.broadcasted_iota(jnp.int32, (tq, nk), 1)
        dist = pos_col - kpos
        distf = dist.astype(f32)
        wlen = win_s.shape[0]
        wpos = (past_len - nwin) + lax.broadcasted_iota(jnp.int32, (tq, wlen), 1)
        wdist = pos_col - wpos
        wvalid = (wdist >= 0) & (wdist < WINDOW)
        wdistf = wdist.astype(f32)
        wk = win_s[:, 0:LANES].astype(bf16)
        wv = win_s[:, LANES:].astype(bf16)
        outs = []
        for kvh in range(NSA_KV_HEADS):
            qm = _padded_queries(q_all, kvh, tq)
            o_cmp, p_sum = _cmp_branch(qm, kc, vc, pos_col, kvh, tq)
            sel = _select_blocks(p_sum, pos_col, ns, nsp, tq).astype(bf16)
            sel_k = jnp.dot(sel, expand, preferred_element_type=f32)
            valid = (sel_k > 0.5) & (dist >= 0)
            s = lax.dot_general(qm, ks_s[...], NT, preferred_element_type=f32)
            sw = lax.dot_general(qm, wk, NT, preferred_element_type=f32)
            ps, pw = [], []
            for g in range(NSA_GROUP):
                r = slice(g * tq, (g + 1) * tq)
                sg = jnp.where(valid, s[r] - SLOPES[kvh][g] * distf, NEG_INF)
                e = jnp.where(valid, jnp.exp(sg - jnp.max(sg, axis=-1, keepdims=True)), 0.0)
                ps.append(e / jnp.sum(e, axis=-1, keepdims=True))
                sg = jnp.where(wvalid, sw[r] - SLOPES[kvh][g] * wdistf, NEG_INF)
                e = jnp.where(wvalid, jnp.exp(sg - jnp.max(sg, axis=-1, keepdims=True)), 0.0)
                pw.append(e / jnp.sum(e, axis=-1, keepdims=True))
            o_slc = jnp.dot(jnp.concatenate(ps, axis=0).astype(bf16), vs_s[...], preferred_element_type=f32)
            o_win = jnp.dot(jnp.concatenate(pw, axis=0).astype(bf16), wv, preferred_element_type=f32)
            outs.extend(_combine(gates, kvh, tq, o_cmp, o_slc, o_win))
        o_ref[0] = jnp.concatenate(outs, axis=1).astype(o_ref.dtype)


def nsa_sample(page_table, cache, q, gates, rows_new, win_new, cache_win, cw, pps=8):
    B, tq, _ = q.shape
    n_pages = page_table.shape[1]
    past_len = n_pages * PAGE_SIZE
    assert n_pages % pps == 0 and past_len % SLC_BLOCK == 0
    n_steps = n_pages // pps
    nk = past_len + 512
    ns = past_len // SLC_BLOCK + 1
    nsp = -(-ns // LANES) * LANES
    nwin = cache_win.shape[1]
    wlen = nwin + LANES
    pek, wk, w2k, pev, wv, w2v = cw
    per_b = lambda n: pl.BlockSpec((1, tq, n), lambda b, s, pt: (b, 0, 0))
    full = lambda a: pl.BlockSpec(a.shape, lambda b, s, pt: (0,) * a.ndim)
    page_specs = [pl.BlockSpec((1, PAGE_SIZE, 4 * LANES),
                               functools.partial(lambda b, s, pt, i: (pt[b, s * pps + i], 0, 0), i=i))
                  for i in range(pps)]
    grid_spec = pltpu.PrefetchScalarGridSpec(
        num_scalar_prefetch=1,
        grid=(B, n_steps),
        in_specs=page_specs + [per_b(NSA_HEADS * LANES), per_b(3 * NSA_HEADS), per_b(4 * LANES), per_b(2 * LANES),
                               pl.BlockSpec((1, nwin, 2 * LANES), lambda b, s, pt: (b, 0, 0)),
                               full(pek), full(pev), full(wk), full(wv), full(w2k), full(w2v)],
        out_specs=per_b(NSA_WIDTH),
        scratch_shapes=[pltpu.VMEM((past_len, LANES), f32), pltpu.VMEM((past_len, LANES), f32),
                        pltpu.VMEM((nk, LANES), bf16), pltpu.VMEM((nk, LANES), bf16),
                        pltpu.VMEM((wlen, 2 * LANES), f32)])
    return pl.pallas_call(
        functools.partial(_nsa_sample_kernel, pps=pps, n_steps=n_steps, tq=tq, past_len=past_len, ns=ns, nsp=nsp),
        grid_spec=grid_spec,
        out_shape=jax.ShapeDtypeStruct((B, tq, NSA_WIDTH), bf16),
        compiler_params=_cparams("parallel", "arbitrary"),
        name="nsa_sample",
    )(page_table, *([cache] * pps), q, gates, rows_new, win_new, cache_win, pek, pev, wk, wv, w2k, w2v)


def _gdn_prep_kernel(x_ref, halo_ref, cb_ref, cw_ref, ba_ref, al_ref, dtb_ref,
                     u_ref, w_ref, qg_ref, kg_ref, qk_ref, eg_ref, xs_s, *, tb, C, t_valid):
    i = pl.program_id(1)
    HK = GDN_HEADS * GDN_DK
    R = min(tb, LANES)
    cpb = R // C

    @pl.when(i == 0)
    def _():
        xs_s[0:8, :] = cb_ref[0]

    @pl.when(i > 0)
    def _():
        xs_s[0:8, :] = halo_ref[0]

    xs_s[8:, :] = x_ref[0]
    ba = ba_ref[0]
    ri = lax.broadcasted_iota(jnp.int32, (R, R), 0)
    ci = lax.broadcasted_iota(jnp.int32, (R, R), 1)
    same = (ri // C) == (ci // C)
    lower = same & (ri >= ci)
    strict = same & (ri > ci)
    eye = (ri == ci).astype(f32)
    rr = lax.broadcasted_iota(jnp.int32, (R, LANES), 0)
    row_t = i * tb + lax.broadcasted_iota(jnp.int32, (tb, 1), 0)
    live = row_t < t_valid

    def conv_act(c0):
        acc = cw_ref[GDN_CONV - 1:GDN_CONV, c0:c0 + LANES] * xs_s[pl.ds(8, tb), c0:c0 + LANES]
        for k in range(1, GDN_CONV):
            acc = acc + cw_ref[GDN_CONV - 1 - k:GDN_CONV - k, c0:c0 + LANES] * xs_s[pl.ds(8 - k, tb), c0:c0 + LANES]
        return _silu(acc)

    chains = []
    for h in range(GDN_HEADS):
        q = conv_act(h * GDN_DK)
        k = conv_act(HK + h * GDN_DK)
        v = conv_act(2 * HK + h * GDN_DV)
        q = q * lax.rsqrt(jnp.sum(q * q, axis=-1, keepdims=True) + L2_EPS) * (GDN_DK ** -0.5)
        k = k * lax.rsqrt(jnp.sum(k * k, axis=-1, keepdims=True) + L2_EPS)
        beta = _sigmoid(ba[:, h:h + 1])
        ar = ba[:, GDN_HEADS + h:GDN_HEADS + h + 1] + dtb_ref[:, h:h + 1]
        softplus = jnp.maximum(ar, 0.0) + jnp.log(1.0 + jnp.exp(-jnp.abs(ar)))
        gcol = -jnp.exp(al_ref[:, h:h + 1]) * softplus
        if t_valid < 10 ** 9:
            beta = jnp.where(live, beta, 0.0)
            gcol = jnp.where(live, gcol, 0.0)
            k = jnp.where(live, k, 0.0)
            v = jnp.where(live, v, 0.0)
        for blk in range(tb // R):
            r = slice(blk * R, (blk + 1) * R)
            qc, kc, vc, bc = q[r], k[r], v[r], beta[r]
            gc = jnp.broadcast_to(gcol[r], (R, LANES))
            sh = 1
            while sh < C:
                gc = gc + jnp.where((rr % C) >= sh, pltpu.roll(gc, sh, 0), 0.0)
                sh *= 2
            gct = gc.T
            decay = jnp.exp(jnp.where(lower, gc[:, 0:R] - gct[0:R, :], NEG_INF))
            kb = kc * bc
            a = jnp.where(strict, _dot_hp(kb, kc, NT) * decay, 0.0)
            qk = jnp.where(lower, lax.dot_general(qc.astype(bf16), kc.astype(bf16), NT,
                                                  preferred_element_type=f32) * decay, 0.0)
            gl = gc[R - 1:R, :]
            for j in range(cpb - 2, -1, -1):
                gl = jnp.where(rr < (j + 1) * C, gc[(j + 1) * C - 1:(j + 1) * C, :], gl)
            chains.append(dict(h=h, r=r, blk=blk, qc=qc, kc=kc, gc=gc, gl=gl, qk=qk, n=-a, t=eye - a,
                               rhs=jnp.concatenate([vc * bc, kb * jnp.exp(gc)], axis=1)))

    for ch in chains:
        ch['n'] = _split2(ch['n'])
    sh = 2
    while sh < C:
        for ch in chains:
            ch['n'] = _split2(_dot_parts(ch['n'], ch['n']))
        for ch in chains:
            ch['t'] = ch['t'] + _dot_parts(_split2(ch['t']), ch['n'])
        sh *= 2

    for ch in chains:
        h, r, gc, gl = ch['h'], ch['r'], ch['gc'], ch['gl']
        sol = ch['rhs'] + _dot_hp(ch['t'] - eye, ch['rhs'])
        cs = slice(h * GDN_DV, (h + 1) * GDN_DV)
        u_ref[0, r, cs] = sol[:, 0:GDN_DV]
        w_ref[0, r, cs] = sol[:, GDN_DV:].astype(w_ref.dtype)
        qg_ref[0, r, cs] = (ch['qc'] * jnp.exp(gc)).astype(qg_ref.dtype)
        kg_ref[0, r, cs] = (ch['kc'] * jnp.exp(gl - gc)).astype(kg_ref.dtype)
        qk = ch['qk'][:, 0:C]
        for j in range(1, cpb):
            qk = jnp.where(ri[:, 0:C] >= j * C, ch['qk'][:, j * C:(j + 1) * C], qk)
        qk_ref[0, r, h * C:(h + 1) * C] = qk.astype(qk_ref.dtype)
        for j in range(cpb):
            eg_ref[0, ch['blk'] * cpb + j, h:h + 1, :] = jnp.exp(gc[(j + 1) * C - 1:(j + 1) * C, :])
    eg_ref[0, :, GDN_HEADS:, :] = jnp.zeros((tb // C, 8 - GDN_HEADS, LANES), f32)


def gdn_prep(qkv, conv_buf8, conv_w, ba, a_log, dt_bias, C, tb, t_valid):
    B, T, CH = qkv.shape
    nb = T // tb
    hb = tb // 8
    per = lambda n, dt: (pl.BlockSpec((1, tb, n), lambda b, i: (b, i, 0)), jax.ShapeDtypeStruct((B, T, n), dt))
    outs = [per(GDN_WIDTH, f32), per(GDN_WIDTH, bf16), per(GDN_WIDTH, bf16), per(GDN_WIDTH, bf16),
            per(GDN_HEADS * C, bf16),
            (pl.BlockSpec((1, tb // C, 8, LANES), lambda b, i: (b, i, 0, 0)),
             jax.ShapeDtypeStruct((B, T // C, 8, LANES), f32))]
    return pl.pallas_call(
        functools.partial(_gdn_prep_kernel, tb=tb, C=C, t_valid=t_valid),
        grid=(B, nb),
        in_specs=[pl.BlockSpec((1, tb, CH), lambda b, i: (b, i, 0)),
                  pl.BlockSpec((1, 8, CH), lambda b, i: (b, jnp.maximum(i * hb - 1, 0), 0)),
                  pl.BlockSpec((1, 8, CH), lambda b, i: (b, 0, 0)),
                  pl.BlockSpec((GDN_CONV, CH), lambda b, i: (0, 0)),
                  pl.BlockSpec((1, tb, 2 * GDN_HEADS), lambda b, i: (b, i, 0)),
                  pl.BlockSpec((1, GDN_HEADS), lambda b, i: (0, 0)),
                  pl.BlockSpec((1, GDN_HEADS), lambda b, i: (0, 0))],
        out_specs=[o[0] for o in outs],
        out_shape=[o[1] for o in outs],
        scratch_shapes=[pltpu.VMEM((tb + 8, CH), f32)],
        compiler_params=_cparams("parallel", "arbitrary"),
        name="gdn_prep",
    )(qkv, qkv, conv_buf8, conv_w, ba, a_log.reshape(1, GDN_HEADS), dt_bias.reshape(1, GDN_HEADS))


def _gdn_scan_kernel(u_ref, w_ref, qg_ref, kg_ref, qk_ref, eg_ref, z_ref, ng_ref, s0_ref,
                     o_ref, sT_ref, s_s, *, C):
    c = pl.program_id(1)

    @pl.when(c == 0)
    def _():
        s_s[...] = s0_ref[0]

    H = range(GDN_HEADS)
    cs = [slice(h * GDN_DV, (h + 1) * GDN_DV) for h in H]
    S = [s_s[h] for h in H]
    Sb = [S[h].astype(bf16) for h in H]
    v_new = [u_ref[0, :, cs[h]] - jnp.dot(w_ref[0, :, cs[h]], Sb[h], preferred_element_type=f32) for h in H]
    vb = [v_new[h].astype(bf16) for h in H]
    for h in H:
        s_s[h] = S[h] * eg_ref[0, 0, h:h + 1, :] + lax.dot_general(kg_ref[0, :, cs[h]], vb[h], TN,
                                                                    preferred_element_type=f32)
    o = [jnp.dot(qg_ref[0, :, cs[h]], Sb[h], preferred_element_type=f32)
         + jnp.dot(qk_ref[0, :, h * C:(h + 1) * C], vb[h], preferred_element_type=f32) for h in H]
    outs = [_rms(o[h], ng_ref[...]) * _silu(z_ref[0, :, cs[h]]) for h in H]
    o_ref[0] = jnp.concatenate(outs, axis=1).astype(o_ref.dtype)

    @pl.when(c == pl.num_programs(1) - 1)
    def _():
        sT_ref[0] = s_s[...]


def gdn_scan(u, w, qg, kg, qk, eg, z, norm_g, s0, C):
    B, T, _ = u.shape
    per = lambda n: pl.BlockSpec((1, C, n), lambda b, c: (b, c, 0))
    st = pl.BlockSpec((1, GDN_HEADS, GDN_DK, GDN_DV), lambda b, c: (b, 0, 0, 0))
    return pl.pallas_call(
        functools.partial(_gdn_scan_kernel, C=C),
        grid=(B, T // C),
        in_specs=[per(GDN_WIDTH), per(GDN_WIDTH), per(GDN_WIDTH), per(GDN_WIDTH), per(GDN_HEADS * C),
                  pl.BlockSpec((1, 1, 8, LANES), lambda b, c: (b, c, 0, 0)),
                  per(GDN_WIDTH), pl.BlockSpec((1, GDN_DV), lambda b, c: (0, 0)), st],
        out_specs=[per(GDN_WIDTH), st],
        out_shape=[jax.ShapeDtypeStruct((B, T, GDN_WIDTH), bf16),
                   jax.ShapeDtypeStruct((B, GDN_HEADS, GDN_DK, GDN_DV), f32)],
        scratch_shapes=[pltpu.VMEM((GDN_HEADS, GDN_DK, GDN_DV), f32)],
        compiler_params=_cparams("parallel", "arbitrary"),
        name="gdn_scan",
    )(u, w, qg, kg, qk, eg, z, norm_g.reshape(1, GDN_DV), s0)


def _nsa_in_groups(w, feature_major):
    c1 = NSA_WIDTH
    c2 = c1 + 4 * NSA_KV_COLS
    c3 = c2 + 2 * NSA_KV_COLS
    c4 = c3 + 3 * NSA_HEADS
    wb = w.astype(bf16)
    common = [(wb[:, c1:c2], f32, 1.0, False), (wb[:, c2:c3], f32, 1.0, False),
              (wb[:, c4:], bf16, XA_DIM ** -0.5, False)]
    if feature_major:
        k0 = c1 + 2 * NSA_KV_COLS
        return common + [(wb[:, :c1].T, bf16, HEAD_DIM ** -0.5 * LOG2E, True), (wb[:, c3:c4].T, f32, 1.0, True),
                         (wb[:, k0:k0 + NSA_KV_COLS], bf16, 1.0, False),
                         (wb[:, k0 + NSA_KV_COLS:c2].T, bf16, 1.0, True),
                         (wb[:, c2:c2 + NSA_KV_COLS], bf16, 1.0, False),
                         (wb[:, c2 + NSA_KV_COLS:c3].T, bf16, 1.0, True)]
    wq = wb[:, :c1].reshape(-1, NSA_KV_HEADS, NSA_GROUP, HEAD_DIM)
    zq = jnp.zeros_like(wq[:, 0])
    wq = jnp.concatenate([jnp.concatenate([wq[:, 0], zq], axis=-1), jnp.concatenate([zq, wq[:, 1]], axis=-1)], axis=1)
    wq = wq.reshape(-1, NSA_HEADS * LANES)
    return common + [(wq, bf16, HEAD_DIM ** -0.5, False), (wb[:, c3:c4], f32, 1.0, False)]


def _gdn_in_groups(w):
    c1 = GDN_CONV_CH
    c2 = c1 + GDN_WIDTH
    c3 = c2 + 2 * GDN_HEADS
    wb = w.astype(bf16)
    return [(wb[:, :c1], f32, 1.0, False), (wb[:, c1:c2], f32, 1.0, False), (wb[:, c2:c3], f32, 1.0, False),
            (wb[:, c3:], bf16, XA_DIM ** -0.5, False)]


def _trunk(x, mem_kv, nsa_past, gdn_state, p, t_valid):
    B, T, D = x.shape
    N = B * T
    xf = x.reshape(N, D)

    if nsa_past is None:
        rows, win, xq, qT, gT, sk, svT, wk, wvT = norm_proj(xf, p['norm_mix_g'][0], p['nsa_in_prompt'])
        rows3 = rows.reshape(B, T, -1)
        win3 = win.reshape(B, T, -1)
        kc, vcT = cmp_prompt(rows3, p['cmp'])
        o_mix = nsa_prompt(qT, gT, sk, svT, wk, wvT, kc, vcT, B)
    else:
        rows, win, xq, q, gates = norm_proj(xf, p['norm_mix_g'][0], p['nsa_in_sample'])
        rows3 = rows.reshape(B, T, -1)
        win3 = win.reshape(B, T, -1)
        page_table, cache, cache_win = nsa_past
        o_mix = nsa_sample(page_table, cache, q.reshape(B, T, -1), gates.reshape(B, T, -1), rows3, win3, cache_win,
                           p['cmp'])
    o_mem = cross_attend(xq.reshape(B, T, -1), mem_kv, 0)
    xf = post_block(xf, o_mix.reshape(N, -1), o_mem.reshape(N, -1), p['w_out'][0], p['norm_ffn_g'][0],
                    p['w_up'][0], p['w_down'][0], p['final_norm_g'], final=False)

    qkv, z, ba, xq = norm_proj(xf, p['norm_mix_g'][1], p['gdn_in'])
    qkv3 = qkv.reshape(B, T, -1)
    S0, conv_buf8 = gdn_state
    C = min(64, T)
    tb = min(256, T)
    u, w, qg, kg, qk, eg = gdn_prep(qkv3, conv_buf8, p['gdn_conv_w'], ba.reshape(B, T, -1), p['gdn_a_log'],
                                    p['gdn_dt_bias'], C, tb, t_valid)
    o_mix, S = gdn_scan(u, w, qg, kg, qk, eg, z.reshape(B, T, -1), p['gdn_norm_g'], S0, C)
    o_mem = cross_attend(xq.reshape(B, T, -1), mem_kv, 1)
    y = post_block(xf, o_mix.reshape(N, -1), o_mem.reshape(N, -1), p['w_out'][1], p['norm_ffn_g'][1],
                   p['w_up'][1], p['w_down'][1], p['final_norm_g'], final=True)
    return y.reshape(B, T, D), rows3, win3, S, qkv3


def kernel(x_prompt, x_sample, mem_prompt, cache_nsa_kv, cache_nsa_win, state_gdn_s, state_gdn_conv,
           cache_mem_kv, page_table, norm_mix_g, norm_mem_g, w_mem_kv, nsa_w_in, cmp_pe_k, cmp_w1_k,
           cmp_w2_k, cmp_pe_v, cmp_w1_v, cmp_w2_v, gdn_w_in, gdn_conv_w, gdn_a_log, gdn_dt_bias,
           gdn_norm_g, w_out, norm_ffn_g, w_up, w_down, final_norm_g):
    B, T, D = x_prompt.shape
    DB, TS, _ = x_sample.shape
    M = mem_prompt.shape[1]
    p = dict(norm_mix_g=norm_mix_g, nsa_in_prompt=_nsa_in_groups(nsa_w_in[0], True),
             nsa_in_sample=_nsa_in_groups(nsa_w_in[0], False), gdn_in=_gdn_in_groups(gdn_w_in[0]),
             cmp=_cmp_weights(cmp_pe_k[0], cmp_w1_k[0], cmp_w2_k[0]) + _cmp_weights(cmp_pe_v[0], cmp_w1_v[0], cmp_w2_v[0]),
             gdn_conv_w=gdn_conv_w[0], gdn_a_log=gdn_a_log[0], gdn_dt_bias=gdn_dt_bias[0], gdn_norm_g=gdn_norm_g[0],
             w_out=w_out.astype(bf16), norm_ffn_g=norm_ffn_g, w_up=w_up.astype(bf16), w_down=w_down.astype(bf16),
             final_norm_g=final_norm_g)

    mem_flat = mem_prompt.reshape(B * M, D)
    mem_kv = [norm_proj(mem_flat, norm_mem_g[i], [(w_mem_kv[i].astype(bf16), f32, 1.0, False)])[0].reshape(B, 1, M, -1)
              for i in range(DEPTH)]
    mem_kv_p = jnp.concatenate(mem_kv, axis=1)
    zero_state = (jnp.zeros((B, GDN_HEADS, GDN_DK, GDN_DV), f32), jnp.zeros((B, 8, GDN_CONV_CH), f32))
    y_p, rows_p, win_p, S_p, qkv_p = _trunk(x_prompt, mem_kv_p, None, zero_state, p, 10 ** 9)

    TP = 8
    x_s = jnp.pad(x_sample, ((0, 0), (0, TP - TS), (0, 0)))
    n_pages = page_table.shape[1]
    cache = cache_nsa_kv.reshape(cache_nsa_kv.shape[0], PAGE_SIZE, -1)
    cache_win = cache_nsa_win.reshape(DB, cache_nsa_win.shape[2], -1)
    conv8 = jnp.pad(state_gdn_conv[:, 0], ((0, 0), (8 - (GDN_CONV - 1), 0), (0, 0)))
    mem_kv_s = cache_mem_kv.reshape(DB, DEPTH, M, -1)
    y_s, rows_s, win_s, S_s, qkv_s = _trunk(x_s, mem_kv_s, (page_table, cache, cache_win),
                                            (state_gdn_s[:, 0], conv8), p, TS)

    kvshape = (4, NSA_KV_HEADS, HEAD_DIM)
    nsa_kv_prompt = rows_p.reshape(B, T, 1, *kvshape)
    nsa_kv_sample = rows_s[:, :TS].reshape(DB, TS, 1, *kvshape)
    wshape = (2, NSA_KV_HEADS, HEAD_DIM)
    nsa_win_prompt = win_p[:, -min(WINDOW, T):].reshape(B, 1, -1, *wshape)
    win_cat = jnp.concatenate([cache_win, win_s[:, :TS]], axis=1)
    nsa_win_sample = win_cat[:, -WINDOW:].reshape(DB, 1, -1, *wshape)
    gdn_conv_prompt = qkv_p[:, None, T - (GDN_CONV - 1):]
    conv_cat = jnp.concatenate([state_gdn_conv[:, 0], qkv_s[:, :TS]], axis=1)
    gdn_conv_sample = conv_cat[:, None, -(GDN_CONV - 1):]
    mem_kv_prompt = mem_kv_p.reshape(B, DEPTH, M, 2, XA_HEADS, XA_DIM)
    return (y_p, y_s[:, :TS], nsa_kv_prompt, nsa_kv_sample, nsa_win_prompt, nsa_win_sample,
            S_p[:, None], S_s[:, None], gdn_conv_prompt, gdn_conv_sample, mem_kv_prompt)
```

```python
import functools

import jax
import jax.numpy as jnp
from jax import lax
from jax.experimental import pallas as pl
from jax.experimental.pallas import tpu as pltpu

f32 = jnp.float32
bf16 = jnp.bfloat16

D_MODEL = 1024
DEPTH = 2
PAGE_SIZE = 128
HEAD_DIM = 64
NSA_HEADS = 12
NSA_KV_HEADS = 2
NSA_GROUP = NSA_HEADS // NSA_KV_HEADS
NSA_WIDTH = NSA_HEADS * HEAD_DIM
NSA_KV_COLS = NSA_KV_HEADS * HEAD_DIM
CMP_STRIDE = 16
CMP_BLOCK = 2 * CMP_STRIDE
CMP_HIDDEN = 128
SLC_BLOCK = 64
SLC_TOPN = 16
WINDOW = 512
GDN_HEADS = 6
GDN_DK = 128
GDN_DV = 128
GDN_WIDTH = GDN_HEADS * GDN_DV
GDN_CONV = 4
GDN_CONV_CH = 2 * GDN_HEADS * GDN_DK + GDN_HEADS * GDN_DV
XA_HEADS = 4
XA_DIM = 64
XA_WIDTH = XA_HEADS * XA_DIM
D_FF = 4 * D_MODEL
NORM_EPS = 1e-6
L2_EPS = 1e-6
NEG_INF = -1e30
FORCED = 1e9

VMEM_LIMIT = 52 * 1024 * 1024
LANES = 128

SLOPES = [[2.0 ** (-8.0 * (k * NSA_GROUP + g + 1) / NSA_HEADS) for g in range(NSA_GROUP)]
          for k in range(NSA_KV_HEADS)]

NT = (((1,), (1,)), ((), ()))
TN = (((0,), (0,)), ((), ()))


def _cparams(*sem):
    return pltpu.CompilerParams(dimension_semantics=sem, vmem_limit_bytes=VMEM_LIMIT)


def _rms(x, g):
    r = lax.rsqrt(jnp.mean(x * x, axis=-1, keepdims=True) + NORM_EPS)
    return (x * r) * g


def _sigmoid(x):
    return 1.0 / (1.0 + jnp.exp(-x))


def _silu(x):
    return x * _sigmoid(x)


def _split3(a):
    hi = a.astype(bf16)
    r1 = a - hi.astype(f32)
    mid = r1.astype(bf16)
    lo = (r1 - mid.astype(f32)).astype(bf16)
    return hi, mid, lo


def _dot_exact_rhs(a, b_bf16):
    hi, mid, lo = _split3(a)
    d = lambda x: jnp.dot(x, b_bf16, preferred_element_type=f32)
    return d(hi) + d(mid) + d(lo)


def _split2(a):
    hi = a.astype(bf16)
    return hi, (a - hi.astype(f32)).astype(bf16)


def _dot_parts(a, b, dims=None):
    if dims is None:
        d = lambda x, y: jnp.dot(x, y, preferred_element_type=f32)
    else:
        d = lambda x, y: lax.dot_general(x, y, dims, preferred_element_type=f32)
    return d(a[0], b[0]) + (d(a[0], b[1]) + d(a[1], b[0]))


def _dot_hp(a, b, dims=None):
    return _dot_parts(_split2(a), _split2(b), dims)


TOKEN_MAJOR, FEATURE_MAJOR, FEATURE_MAJOR_PER_BATCH = 'nt', 'fm', 'fmb'


def _norm_proj_kernel(x_ref, g_ref, *refs, specs):
    n = len(specs)
    w_refs, o_refs = refs[:n], refs[n:]
    h = _rms(x_ref[...], g_ref[...]).astype(bf16)
    for (scale, kind), w_ref, o_ref in zip(specs, w_refs, o_refs):
        if kind == TOKEN_MAJOR:
            y = jnp.dot(h, w_ref[...], preferred_element_type=f32)
        else:
            y = lax.dot_general(w_ref[...], h, NT, preferred_element_type=f32)
        if scale != 1.0:
            y = y * scale
        o_ref[...] = y.astype(o_ref.dtype).reshape(o_ref.shape)


def norm_proj(x, g, groups, tm=256, rows_per_batch=None):
    N, D = x.shape
    tm = min(tm, N)
    assert N % tm == 0
    in_specs = [pl.BlockSpec((tm, D), lambda i: (i, 0)), pl.BlockSpec((1, D), lambda i: (0, 0))]
    out_specs, out_shape = [], []
    for w, dt, _, kind in groups:
        in_specs.append(pl.BlockSpec(w.shape, lambda i: (0, 0)))
        if kind == TOKEN_MAJOR:
            n = w.shape[1]
            out_specs.append(pl.BlockSpec((tm, n), lambda i: (i, 0)))
            out_shape.append(jax.ShapeDtypeStruct((N, n), dt))
        elif kind == FEATURE_MAJOR:
            n = w.shape[0]
            out_specs.append(pl.BlockSpec((n, tm), lambda i: (0, i)))
            out_shape.append(jax.ShapeDtypeStruct((n, N), dt))
        else:
            n = w.shape[0]
            per = rows_per_batch // tm
            assert rows_per_batch % tm == 0
            out_specs.append(pl.BlockSpec((1, n, tm), lambda i: (i // per, 0, i % per)))
            out_shape.append(jax.ShapeDtypeStruct((N // rows_per_batch, n, rows_per_batch), dt))
    return pl.pallas_call(
        functools.partial(_norm_proj_kernel, specs=tuple((s, t) for _, _, s, t in groups)),
        grid=(N // tm,),
        in_specs=in_specs,
        out_specs=out_specs,
        out_shape=out_shape,
        compiler_params=_cparams("parallel"),
        name="norm_proj",
    )(x, g.reshape(1, D), *[w for w, _, _, _ in groups])


def _xattn_kernel(q_ref, kv_ref, o_ref, *, q_feature_major):
    kvT = kv_ref[0, 0]
    qT = q_ref[...] if q_feature_major else q_ref[0].astype(f32).T.astype(bf16)
    outs = []
    for h in range(XA_HEADS):
        kT = kvT[h * XA_DIM:(h + 1) * XA_DIM].astype(bf16)
        vT = kvT[XA_WIDTH + h * XA_DIM:XA_WIDTH + (h + 1) * XA_DIM].astype(bf16)
        s = lax.dot_general(kT, qT[h * XA_DIM:(h + 1) * XA_DIM], TN, preferred_element_type=f32)
        e = jnp.exp2(s - jnp.max(s, axis=0, keepdims=True))
        o = jnp.dot(vT, e.astype(bf16), preferred_element_type=f32)
        outs.append(o / jnp.sum(e, axis=0, keepdims=True))
    o_ref[0] = jnp.concatenate(outs, axis=0).T.astype(o_ref.dtype)


def cross_attend(xq, kvT, layer, B, tq=512):
    q_feature_major = xq.ndim == 2
    T = xq.shape[1] // B if q_feature_major else xq.shape[1]
    M = kvT.shape[3]
    tq = min(tq, T)
    nq = T // tq
    if q_feature_major:
        q_spec = pl.BlockSpec((XA_WIDTH, tq), lambda b, i: (0, b * nq + i))
    else:
        q_spec = pl.BlockSpec((1, tq, XA_WIDTH), lambda b, i: (b, i, 0))
    return pl.pallas_call(
        functools.partial(_xattn_kernel, q_feature_major=q_feature_major),
        grid=(B, nq),
        in_specs=[q_spec, pl.BlockSpec((1, 1, 2 * XA_WIDTH, M), lambda b, i: (b, layer, 0, 0))],
        out_specs=pl.BlockSpec((1, tq, XA_WIDTH), lambda b, i: (b, i, 0)),
        out_shape=jax.ShapeDtypeStruct((B, T, XA_WIDTH), bf16),
        compiler_params=_cparams("parallel", "parallel"),
        name="cross_attend",
    )(xq, kvT)


def _post_kernel(x_ref, om_ref, oc_ref, wo_ref, gf_ref, wu_ref, wd_ref, gl_ref, o_ref,
                 x1_s, h_s, acc_s, *, mix_w, final):
    j = pl.program_id(1)

    @pl.when(j == 0)
    def _():
        x1 = x_ref[...] + (jnp.dot(om_ref[...], wo_ref[0:mix_w, :], preferred_element_type=f32)
                           + jnp.dot(oc_ref[...], wo_ref[mix_w:, :], preferred_element_type=f32))
        x1_s[...] = x1
        h_s[...] = _rms(x1, gf_ref[...]).astype(bf16)
        acc_s[...] = jnp.zeros_like(acc_s)

    u = jnp.dot(h_s[...], wu_ref[...], preferred_element_type=f32)
    u = jnp.square(jnp.maximum(u, 0.0)).astype(bf16)
    acc_s[...] += jnp.dot(u, wd_ref[...], preferred_element_type=f32)

    @pl.when(j == pl.num_programs(1) - 1)
    def _():
        x2 = x1_s[...] + acc_s[...]
        if final:
            x2 = _rms(x2, gl_ref[...])
        o_ref[...] = x2


def post_block(x, o_mix, o_mem, w_out, g_ffn, w_up, w_down, g_final, final, tm=1024, tf=512):
    N, D = x.shape
    mix_w = o_mix.shape[1]
    F = w_up.shape[1]
    tm = min(tm, N)
    return pl.pallas_call(
        functools.partial(_post_kernel, mix_w=mix_w, final=final),
        grid=(N // tm, F // tf),
        in_specs=[pl.BlockSpec((tm, D), lambda i, j: (i, 0)),
                  pl.BlockSpec((tm, mix_w), lambda i, j: (i, 0)),
                  pl.BlockSpec((tm, XA_WIDTH), lambda i, j: (i, 0)),
                  pl.BlockSpec((mix_w + XA_WIDTH, D), lambda i, j: (0, 0)),
                  pl.BlockSpec((1, D), lambda i, j: (0, 0)),
                  pl.BlockSpec((D, tf), lambda i, j: (0, j)),
                  pl.BlockSpec((tf, D), lambda i, j: (j, 0)),
                  pl.BlockSpec((1, D), lambda i, j: (0, 0))],
        out_specs=pl.BlockSpec((tm, D), lambda i, j: (i, 0)),
        out_shape=jax.ShapeDtypeStruct((N, D), f32),
        scratch_shapes=[pltpu.VMEM((tm, D), f32), pltpu.VMEM((tm, D), bf16), pltpu.VMEM((tm, D), f32)],
        compiler_params=_cparams("parallel", "arbitrary"),
        name="post_block",
    )(x, o_mix, o_mem, w_out, g_ffn.reshape(1, D), w_up, w_down, g_final.reshape(1, D))


def _compress(load_chunk_rows, nch, pe_ref, w_ref, w2_ref):
    acc_p = jnp.zeros((nch, 2 * CMP_HIDDEN), f32)
    acc_q = jnp.zeros((nch, 2 * CMP_HIDDEN), f32)
    for l in range(CMP_STRIDE):
        x = load_chunk_rows(l)
        xp = (x + pe_ref[l:l + 1, :]).astype(bf16)
        xq = (x + pe_ref[CMP_STRIDE + l:CMP_STRIDE + l + 1, :]).astype(bf16)
        acc_p = acc_p + jnp.dot(xp, w_ref[l, :, 0:2 * CMP_HIDDEN], preferred_element_type=f32)
        acc_q = acc_q + jnp.dot(xq, w_ref[l, :, 2 * CMP_HIDDEN:], preferred_element_type=f32)
    hid = _silu(acc_p + pltpu.roll(acc_q, nch - 1, 0)).astype(bf16)
    outs = [jnp.dot(hid[:, h * CMP_HIDDEN:(h + 1) * CMP_HIDDEN], w2_ref[...], preferred_element_type=f32)
            for h in range(NSA_KV_HEADS)]
    return jnp.concatenate(outs, axis=1)


def _cmp_prompt_kernel(xk_ref, xv_ref, pek_ref, pev_ref, wk_ref, wv_ref, w2k_ref, w2v_ref, kc_ref, vc_ref, *, nch):
    kc_ref[0] = _compress(lambda l: xk_ref[0, pl.ds(l, nch, stride=CMP_STRIDE), :], nch, pek_ref, wk_ref, w2k_ref)
    vc_ref[0] = _compress(lambda l: xv_ref[0, pl.ds(l, nch, stride=CMP_STRIDE), :], nch, pev_ref, wv_ref, w2v_ref).T


def _cmp_weights(pe, w1, w2):
    pe2 = jnp.tile(pe, (1, NSA_KV_HEADS))
    z = jnp.zeros((CMP_STRIDE, HEAD_DIM, CMP_HIDDEN), w1.dtype)
    a, b = w1[:CMP_STRIDE], w1[CMP_STRIDE:]
    top = jnp.concatenate([a, z, b, z], axis=2)
    bot = jnp.concatenate([z, a, z, b], axis=2)
    wbd = jnp.concatenate([top, bot], axis=1).astype(bf16)
    return pe2, wbd, w2.astype(bf16)


def cmp_prompt(rows, cw):
    B, T, _ = rows.shape
    nch = T // CMP_STRIDE
    pek, wk, w2k, pev, wv, w2v = cw
    full = lambda a: pl.BlockSpec(a.shape, lambda b: (0,) * a.ndim)
    return pl.pallas_call(
        functools.partial(_cmp_prompt_kernel, nch=nch),
        grid=(B,),
        in_specs=[pl.BlockSpec((1, T, LANES), lambda b: (b, 0, 0)),
                  pl.BlockSpec((1, T, LANES), lambda b: (b, 0, 1)),
                  full(pek), full(pev), full(wk), full(wv), full(w2k), full(w2v)],
        out_specs=[pl.BlockSpec((1, nch, LANES), lambda b: (b, 0, 0)), pl.BlockSpec((1, LANES, nch), lambda b: (b, 0, 0))],
        out_shape=[jax.ShapeDtypeStruct((B, nch, LANES), f32), jax.ShapeDtypeStruct((B, LANES, nch), f32)],
        compiler_params=_cparams("parallel"),
        name="cmp_prompt",
    )(rows, rows, pek, pev, wk, wv, w2k, w2v)


def _padded_queries(q_all, kvh, tq):
    del tq
    lo = kvh * NSA_GROUP
    return jnp.concatenate([q_all[:, (lo + g) * LANES:(lo + g + 1) * LANES] for g in range(NSA_GROUP)], axis=0)


def _cmp_branch(qm, kc, vc, pos_col, kvh, tq):
    ncp = kc.shape[0]
    s = lax.dot_general(qm, kc.astype(bf16), NT, preferred_element_type=f32)
    n_idx = lax.broadcasted_iota(jnp.int32, (tq, ncp), 1)
    cdist = pos_col - (n_idx * CMP_STRIDE + (CMP_BLOCK - 1))
    cvis = cdist >= 0
    any_vis = pos_col >= (CMP_BLOCK - 1)
    cdf = cdist.astype(f32)
    p_sum = jnp.zeros((tq, ncp), f32)
    ps = []
    for g in range(NSA_GROUP):
        sg = jnp.where(cvis, s[g * tq:(g + 1) * tq] - SLOPES[kvh][g] * cdf, NEG_INF)
        m = jnp.max(sg, axis=-1, keepdims=True)
        e = jnp.exp(sg - m)
        p = e / jnp.sum(e, axis=-1, keepdims=True)
        p = jnp.where(any_vis, p, 0.0)
        p_sum = p_sum + p
        ps.append(p)
    p_all = jnp.concatenate(ps, axis=0).astype(bf16)
    o = jnp.dot(p_all, vc.astype(bf16), preferred_element_type=f32)
    return o, p_sum


def _select_blocks(p_sum, pos_col, ns, nsp, tq):
    ncp = p_sum.shape[1]
    ci = lax.broadcasted_iota(jnp.int32, (ncp, nsp), 0) * CMP_STRIDE
    cj = lax.broadcasted_iota(jnp.int32, (ncp, nsp), 1)
    c2s = ((ci < (cj + 1) * SLC_BLOCK) & (ci + CMP_BLOCK > cj * SLC_BLOCK)).astype(f32).astype(bf16)
    imp = _dot_exact_rhs(p_sum, c2s)
    blk = lax.broadcasted_iota(jnp.int32, (tq, nsp), 1)
    cur = pos_col // SLC_BLOCK
    forced = (blk == 0) | (blk == cur) | (blk == cur - 1)
    imp = jnp.where(forced, FORCED, jnp.where(blk <= cur, imp, NEG_INF))
    rank = jnp.zeros((tq, nsp), jnp.int32)
    for jp in range(ns):
        col = imp[:, jp:jp + 1]
        beats = (col > imp) | ((col == imp) & (blk > jp))
        rank = rank + beats.astype(jnp.int32)
    return ((rank < SLC_TOPN) & (blk <= cur)).astype(f32)


def _combine(gates, kvh, tq, o_cmp, o_slc, o_win):
    lo = kvh * HEAD_DIM
    outs = []
    for g in range(NSA_GROUP):
        h = kvh * NSA_GROUP + g
        r = slice(g * tq, (g + 1) * tq)
        o = (gates[:, 3 * h:3 * h + 1] * o_cmp[r] + gates[:, 3 * h + 1:3 * h + 2] * o_slc[r]
             + gates[:, 3 * h + 2:3 * h + 3] * o_win[r])
        outs.append(o[:, lo:lo + HEAD_DIM])
    return outs


LOG2E = 1.4426950408889634
SLOPES2 = [[v * LOG2E for v in row] for row in SLOPES]

POS_SPLIT = 8
F_SEL0 = 16
MASK_BIG = 1e30


def position_features(pos, blk, ns):
    n = pos.shape[0]
    lane = jnp.arange(HEAD_DIM)[None, :]
    hi = ((pos // POS_SPLIT) * POS_SPLIT).astype(f32)[:, None]
    lo = (pos % POS_SPLIT).astype(f32)[:, None]
    f = jnp.where(lane < 3, hi, jnp.where(lane < 6, lo, jnp.where(lane < 9, 1.0, 0.0)))
    if ns:
        f = jnp.where((lane >= F_SEL0) & (lane < F_SEL0 + ns), (blk[:, None] == lane - F_SEL0).astype(f32), f)
    return jnp.broadcast_to(f, (n, HEAD_DIM)).astype(f32)


def feature_tables(T, nch):
    pos = jnp.arange(T, dtype=jnp.int32)
    kf = position_features(pos, pos // SLC_BLOCK, T // SLC_BLOCK)
    cpos = jnp.arange(nch, dtype=jnp.int32) * CMP_STRIDE + (CMP_BLOCK - 1)
    cf = position_features(cpos, cpos, 0)
    z = lambda a: jnp.zeros_like(a)
    two = lambda a: jnp.stack([jnp.concatenate([z(a), a], axis=1), jnp.concatenate([a, z(a)], axis=1)]).astype(bf16)
    return two(kf), two(cf)


def _query_features(pos_row, selT, kvh, tq):
    posf = pos_row.astype(f32)
    r16 = lax.broadcasted_iota(jnp.int32, (16, tq), 0)
    zeros16 = jnp.zeros((16, tq), f32)
    selbig = (selT - 1.0) * MASK_BIG
    ns = selT.shape[0]
    pad = jnp.zeros((HEAD_DIM - F_SEL0 - ns, tq), f32)
    plain, full = [], []
    for g in range(NSA_GROUP):
        s = jnp.full((1, tq), SLOPES2[kvh][g], f32)
        s_parts = _split3(s)
        a_parts = _split3(s * posf)
        rows = [p.astype(f32) for p in s_parts] * 2 + [-p.astype(f32) for p in a_parts]
        blk0 = zeros16
        for i, rv in enumerate(rows):
            blk0 = jnp.where(r16 == i, rv, blk0)
        plain.append(jnp.concatenate([blk0, jnp.zeros((HEAD_DIM - 16, tq), f32)], axis=0).astype(bf16))
        full.append(jnp.concatenate([blk0, selbig, pad], axis=0).astype(bf16))
    return plain, full


def _aug_queries(qT, feats, kvh, tq):
    out = []
    for g in range(NSA_GROUP):
        h = kvh * NSA_GROUP + g
        qh = qT[h * HEAD_DIM:(h + 1) * HEAD_DIM, :]
        out.append(jnp.concatenate([qh, feats[g]] if kvh == 0 else [feats[g], qh], axis=0))
    return out


def _aug_keys(k_tile, feat_tile, kvh):
    lane = lax.broadcasted_iota(jnp.int32, k_tile.shape, 1)
    own = (lane < HEAD_DIM) if kvh == 0 else (lane >= HEAD_DIM)
    return jnp.where(own, k_tile, feat_tile)


def _with_ones(vT, kvh):
    half = vT.shape[0] // 2
    ones = jnp.ones((half, vT.shape[1]), vT.dtype)
    return jnp.concatenate([vT[:half], ones] if kvh == 0 else [ones, vT[half:]], axis=0)


def _cmp_branch_t(q_aug, kc_aug, vcT, pos_row, tq):
    ncp = kc_aug.shape[0]
    n_idx = lax.broadcasted_iota(jnp.int32, (ncp, tq), 0)
    cvis = pos_row >= (n_idx * CMP_STRIDE + (CMP_BLOCK - 1))
    any_vis = pos_row >= (CMP_BLOCK - 1)
    p_sum = jnp.zeros((ncp, tq), f32)
    outs = []
    for g in range(NSA_GROUP):
        sg = jnp.where(cvis, jnp.dot(kc_aug, q_aug[g], preferred_element_type=f32), NEG_INF)
        e = jnp.exp2(sg - jnp.max(sg, axis=0, keepdims=True))
        p = e / jnp.sum(e, axis=0, keepdims=True)
        p = jnp.where(any_vis, p, 0.0)
        p_sum = p_sum + p
        outs.append(jnp.dot(vcT, p.astype(bf16), preferred_element_type=f32))
    return outs, p_sum


def _select_blocks_t(p_sumT, pos_row, ns, tq):
    ncp = p_sumT.shape[0]
    cj = lax.broadcasted_iota(jnp.int32, (ns, ncp), 0)
    ci = lax.broadcasted_iota(jnp.int32, (ns, ncp), 1) * CMP_STRIDE
    c2sT = ((ci < (cj + 1) * SLC_BLOCK) & (ci + CMP_BLOCK > cj * SLC_BLOCK)).astype(f32).astype(bf16)
    hi, mid, lo = _split3(p_sumT)
    d = lambda x: jnp.dot(c2sT, x, preferred_element_type=f32)
    imp = d(hi) + d(mid) + d(lo)
    blk = lax.broadcasted_iota(jnp.int32, (ns, tq), 0)
    cur = pos_row // SLC_BLOCK
    forced = (blk == 0) | (blk == cur) | (blk == cur - 1)
    imp = jnp.where(forced, FORCED, jnp.where(blk <= cur, imp, NEG_INF))
    rank = jnp.zeros((ns, tq), jnp.int32)
    for jp in range(ns):
        row = imp[jp:jp + 1, :]
        beats = (row > imp) | ((row == imp) & (blk > jp))
        rank = rank + beats.astype(jnp.int32)
    return ((rank < SLC_TOPN) & (blk <= cur)).astype(f32)


def _nsa_prompt_kernel(qT_ref, gT_ref, sk_ref, svT_ref, wk_ref, wvT_ref, kc_ref, vcT_ref, kf_ref, cf_ref, o_ref,
                       m_s, acc_s, *, tq, kt, ns):
    qi = pl.program_id(1)
    p0 = qi * tq
    pos_row = p0 + lax.broadcasted_iota(jnp.int32, (1, tq), 1)
    gates = _sigmoid(gT_ref[...])
    qT = qT_ref[...]
    kc = kc_ref[0].astype(bf16)
    vcT = vcT_ref[0].astype(bf16)
    half = LANES // 2
    KV = range(NSA_KV_HEADS)

    o_cmp, q_pos, q_sel = [], [], []
    for kvh in KV:
        zsel = jnp.zeros((ns, tq), f32)
        plain, _ = _query_features(pos_row, zsel, kvh, tq)
        qa = _aug_queries(qT, plain, kvh, tq)
        oc, p_sum = _cmp_branch_t(qa, _aug_keys(kc, cf_ref[kvh], kvh), vcT, pos_row, tq)
        selT = _select_blocks_t(p_sum, pos_row, ns, tq)
        _, full = _query_features(pos_row, selT, kvh, tq)
        o_cmp.append(oc)
        q_pos.append(qa)
        q_sel.append(_aug_queries(qT, full, kvh, tq))

    m_s[...] = jnp.full(m_s.shape, NEG_INF, f32)
    acc_s[...] = jnp.zeros(acc_s.shape, f32)
    rel = pos_row - lax.broadcasted_iota(jnp.int32, (kt, tq), 0)

    def slc_tile(c, causal):
        k0 = pl.multiple_of(c * kt, kt)
        k_tile = sk_ref[pl.ds(k0, kt), :]
        v_tile = svT_ref[:, pl.ds(k0, kt)]
        ok = (rel - k0) >= 0
        for kvh in KV:
            ka = _aug_keys(k_tile, kf_ref[kvh, pl.ds(k0, kt), :], kvh)
            va = _with_ones(v_tile, kvh)
            for g in range(NSA_GROUP):
                cs = slice(g * tq, (g + 1) * tq)
                sg = jnp.dot(ka, q_sel[kvh][g], preferred_element_type=f32)
                if causal:
                    sg = jnp.where(ok, sg, NEG_INF)
                m_old = m_s[kvh, :, cs]
                m_new = jnp.maximum(m_old, jnp.max(sg, axis=0, keepdims=True))
                alpha = jnp.exp2(m_old - m_new)
                p = jnp.exp2(sg - m_new).astype(bf16)
                acc_s[kvh, :, cs] = alpha * acc_s[kvh, :, cs] + jnp.dot(va, p, preferred_element_type=f32)
                m_s[kvh, :, cs] = m_new

    n_kt = (p0 + tq + kt - 1) // kt

    def slc_body(c, _):
        slc_tile(c, False)
        return 0

    lax.fori_loop(0, n_kt - 1, slc_body, 0)
    slc_tile(n_kt - 1, True)

    wlen = WINDOW + tq
    w0 = pl.multiple_of(jnp.maximum(p0 - WINDOW, 0), tq)
    k_win = wk_ref[pl.ds(w0, wlen), :]
    v_win = wvT_ref[:, pl.ds(w0, wlen)]
    dist = pos_row - (w0 + lax.broadcasted_iota(jnp.int32, (wlen, tq), 0))
    valid = (dist >= 0) & (dist < WINDOW)
    heads = []
    for kvh in KV:
        ka = _aug_keys(k_win, kf_ref[kvh, pl.ds(w0, wlen), :], kvh)
        va = _with_ones(v_win, kvh)
        d0, l0 = (0, half) if kvh == 0 else (half, 0)
        for g in range(NSA_GROUP):
            h = kvh * NSA_GROUP + g
            cs = slice(g * tq, (g + 1) * tq)
            sg = jnp.where(valid, jnp.dot(ka, q_pos[kvh][g], preferred_element_type=f32), NEG_INF)
            p = jnp.exp2(sg - jnp.max(sg, axis=0, keepdims=True)).astype(bf16)
            o_win = jnp.dot(va, p, preferred_element_type=f32)
            o_slc = acc_s[kvh, :, cs]
            o = (gates[3 * h:3 * h + 1] * o_cmp[kvh][g][d0:d0 + half]
                 + gates[3 * h + 1:3 * h + 2] * (o_slc[d0:d0 + half] / o_slc[l0:l0 + 1])
                 + gates[3 * h + 2:3 * h + 3] * (o_win[d0:d0 + half] / o_win[l0:l0 + 1]))
            heads.append(o)
    o_ref[0] = jnp.concatenate(heads, axis=0).T.astype(o_ref.dtype)


def nsa_prompt(qT, gT, sk, svT, wk, wvT, kc, vcT, B, tq=128, kt=256):
    N = sk.shape[0]
    T = N // B
    ns = T // SLC_BLOCK
    nq = T // tq
    ncp = kc.shape[1]
    assert T % kt == 0 and T % tq == 0 and kt % tq == 0 and T >= WINDOW + tq and T <= POS_SPLIT * 256
    assert F_SEL0 + ns <= HEAD_DIM
    kf, cf = feature_tables(T, ncp)
    keys = pl.BlockSpec((T, LANES), lambda b, i: (b, 0))
    vals = pl.BlockSpec((LANES, T), lambda b, i: (0, b))
    return pl.pallas_call(
        functools.partial(_nsa_prompt_kernel, tq=tq, kt=kt, ns=ns),
        grid=(B, nq),
        in_specs=[pl.BlockSpec((NSA_WIDTH, tq), lambda b, i: (0, b * nq + i)),
                  pl.BlockSpec((3 * NSA_HEADS, tq), lambda b, i: (0, b * nq + i)),
                  keys, vals, keys, vals,
                  pl.BlockSpec((1, ncp, LANES), lambda b, i: (b, 0, 0)),
                  pl.BlockSpec((1, LANES, ncp), lambda b, i: (b, 0, 0)),
                  pl.BlockSpec((NSA_KV_HEADS, T, LANES), lambda b, i: (0, 0, 0)),
                  pl.BlockSpec((NSA_KV_HEADS, ncp, LANES), lambda b, i: (0, 0, 0))],
        out_specs=pl.BlockSpec((1, tq, NSA_WIDTH), lambda b, i: (b, i, 0)),
        out_shape=jax.ShapeDtypeStruct((B, T, NSA_WIDTH), bf16),
        scratch_shapes=[pltpu.VMEM((NSA_KV_HEADS, 1, NSA_GROUP * tq), f32),
                        pltpu.VMEM((NSA_KV_HEADS, LANES, NSA_GROUP * tq), f32)],
        compiler_params=_cparams("parallel", "arbitrary"),
        name="nsa_prompt",
    )(qT, gT, sk, svT, wk, wvT, kc, vcT, kf, cf)


def _nsa_sample_kernel(pt_ref, *refs, pps, n_steps, tq, past_len, ns, nsp):
    page_refs = refs[:pps]
    (q_ref, g_ref, rn_ref, wn_ref, cw_ref, pek_ref, pev_ref, wk_ref, wv_ref, w2k_ref, w2v_ref,
     o_ref, xk_s, xv_s, ks_s, vs_s, win_s) = refs[pps:]
    del pt_ref
    step = pl.program_id(1)
    for i in range(pps):
        page = page_refs[i][0]
        r0 = pl.multiple_of((step * pps + i) * PAGE_SIZE, PAGE_SIZE)
        xk_s[pl.ds(r0, PAGE_SIZE), :] = page[:, 0:LANES]
        xv_s[pl.ds(r0, PAGE_SIZE), :] = page[:, LANES:2 * LANES]
        ks_s[pl.ds(r0, PAGE_SIZE), :] = page[:, 2 * LANES:3 * LANES].astype(bf16)
        vs_s[pl.ds(r0, PAGE_SIZE), :] = page[:, 3 * LANES:4 * LANES].astype(bf16)

    @pl.when(step == n_steps - 1)
    def _():
        nk = ks_s.shape[0]
        tail = nk - past_len
        rn = rn_ref[0]
        ztail = jnp.zeros((tail - tq, LANES), f32)
        ks_s[past_len:, :] = jnp.concatenate([rn[:, 2 * LANES:3 * LANES], ztail], axis=0).astype(bf16)
        vs_s[past_len:, :] = jnp.concatenate([rn[:, 3 * LANES:4 * LANES], ztail], axis=0).astype(bf16)
        nwin = cw_ref.shape[1]
        win_s[0:nwin, :] = cw_ref[0]
        win_s[nwin:, :] = jnp.concatenate(
            [wn_ref[0], jnp.zeros((win_s.shape[0] - nwin - tq, 2 * LANES), f32)], axis=0)

        nch = past_len // CMP_STRIDE
        kc = _compress(lambda l: xk_s[pl.ds(l, nch, stride=CMP_STRIDE), :], nch, pek_ref, wk_ref, w2k_ref)
        vc = _compress(lambda l: xv_s[pl.ds(l, nch, stride=CMP_STRIDE), :], nch, pev_ref, wv_ref, w2v_ref)

        pos_col = past_len + lax.broadcasted_iota(jnp.int32, (tq, 1), 0)
        gates = _sigmoid(g_ref[0])
        q_all = q_ref[0]
        ej = lax.broadcasted_iota(jnp.int32, (nsp, nk), 0)
        ek = lax.broadcasted_iota(jnp.int32, (nsp, nk), 1) // SLC_BLOCK
        expand = (ej == ek).astype(f32).astype(bf16)
        kpos = lax.broadcasted_iota(jnp.int32, (tq, nk), 1)
        dist = pos_col - kpos
        distf = dist.astype(f32)
        wlen = win_s.shape[0]
        wpos = (past_len - nwin) + lax.broadcasted_iota(jnp.int32, (tq, wlen), 1)
        wdist = pos_col - wpos
        wvalid = (wdist >= 0) & (wdist < WINDOW)
        wdistf = wdist.astype(f32)
        wk = win_s[:, 0:LANES].astype(bf16)
        wv = win_s[:, LANES:].astype(bf16)
        outs = []
        for kvh in range(NSA_KV_HEADS):
            qm = _padded_queries(q_all, kvh, tq)
            o_cmp, p_sum = _cmp_branch(qm, kc, vc, pos_col, kvh, tq)
            sel = _select_blocks(p_sum, pos_col, ns, nsp, tq).astype(bf16)
            sel_k = jnp.dot(sel, expand, preferred_element_type=f32)
            valid = (sel_k > 0.5) & (dist >= 0)
            s = lax.dot_general(qm, ks_s[...], NT, preferred_element_type=f32)
            sw = lax.dot_general(qm, wk, NT, preferred_element_type=f32)
            ps, pw = [], []
            for g in range(NSA_GROUP):
                r = slice(g * tq, (g + 1) * tq)
                sg = jnp.where(valid, s[r] - SLOPES[kvh][g] * distf, NEG_INF)
                e = jnp.where(valid, jnp.exp(sg - jnp.max(sg, axis=-1, keepdims=True)), 0.0)
                ps.append(e / jnp.sum(e, axis=-1, keepdims=True))
                sg = jnp.where(wvalid, sw[r] - SLOPES[kvh][g] * wdistf, NEG_INF)
                e = jnp.where(wvalid, jnp.exp(sg - jnp.max(sg, axis=-1, keepdims=True)), 0.0)
                pw.append(e / jnp.sum(e, axis=-1, keepdims=True))
            o_slc = jnp.dot(jnp.concatenate(ps, axis=0).astype(bf16), vs_s[...], preferred_element_type=f32)
            o_win = jnp.dot(jnp.concatenate(pw, axis=0).astype(bf16), wv, preferred_element_type=f32)
            outs.extend(_combine(gates, kvh, tq, o_cmp, o_slc, o_win))
        o_ref[0] = jnp.concatenate(outs, axis=1).astype(o_ref.dtype)


def nsa_sample(page_table, cache, q, gates, rows_new, win_new, cache_win, cw, pps=8):
    B, tq, _ = q.shape
    n_pages = page_table.shape[1]
    past_len = n_pages * PAGE_SIZE
    assert n_pages % pps == 0 and past_len % SLC_BLOCK == 0
    n_steps = n_pages // pps
    nk = past_len + 512
    ns = past_len // SLC_BLOCK + 1
    nsp = -(-ns // LANES) * LANES
    nwin = cache_win.shape[1]
    wlen = nwin + LANES
    pek, wk, w2k, pev, wv, w2v = cw
    per_b = lambda n: pl.BlockSpec((1, tq, n), lambda b, s, pt: (b, 0, 0))
    full = lambda a: pl.BlockSpec(a.shape, lambda b, s, pt: (0,) * a.ndim)
    page_specs = [pl.BlockSpec((1, PAGE_SIZE, 4 * LANES),
                               functools.partial(lambda b, s, pt, i: (pt[b, s * pps + i], 0, 0), i=i))
                  for i in range(pps)]
    grid_spec = pltpu.PrefetchScalarGridSpec(
        num_scalar_prefetch=1,
        grid=(B, n_steps),
        in_specs=page_specs + [per_b(NSA_HEADS * LANES), per_b(3 * NSA_HEADS), per_b(4 * LANES), per_b(2 * LANES),
                               pl.BlockSpec((1, nwin, 2 * LANES), lambda b, s, pt: (b, 0, 0)),
                               full(pek), full(pev), full(wk), full(wv), full(w2k), full(w2v)],
        out_specs=per_b(NSA_WIDTH),
        scratch_shapes=[pltpu.VMEM((past_len, LANES), f32), pltpu.VMEM((past_len, LANES), f32),
                        pltpu.VMEM((nk, LANES), bf16), pltpu.VMEM((nk, LANES), bf16),
                        pltpu.VMEM((wlen, 2 * LANES), f32)])
    return pl.pallas_call(
        functools.partial(_nsa_sample_kernel, pps=pps, n_steps=n_steps, tq=tq, past_len=past_len, ns=ns, nsp=nsp),
        grid_spec=grid_spec,
        out_shape=jax.ShapeDtypeStruct((B, tq, NSA_WIDTH), bf16),
        compiler_params=_cparams("parallel", "arbitrary"),
        name="nsa_sample",
    )(page_table, *([cache] * pps), q, gates, rows_new, win_new, cache_win, pek, pev, wk, wv, w2k, w2v)


def _gdn_prep_kernel(x_ref, halo_ref, cb_ref, cw_ref, ba_ref, al_ref, dtb_ref,
                     u_ref, w_ref, qg_ref, kg_ref, qk_ref, eg_ref, xs_s, *, tb, C, t_valid):
    i = pl.program_id(1)
    HK = GDN_HEADS * GDN_DK
    R = min(tb, LANES)
    cpb = R // C

    @pl.when(i == 0)
    def _():
        xs_s[0:8, :] = cb_ref[0]

    @pl.when(i > 0)
    def _():
        xs_s[0:8, :] = halo_ref[0]

    xs_s[8:, :] = x_ref[0]
    ba = ba_ref[0]
    ri = lax.broadcasted_iota(jnp.int32, (R, R), 0)
    ci = lax.broadcasted_iota(jnp.int32, (R, R), 1)
    same = (ri // C) == (ci // C)
    lower = same & (ri >= ci)
    strict = same & (ri > ci)
    eye = (ri == ci).astype(f32)
    rr = lax.broadcasted_iota(jnp.int32, (R, LANES), 0)
    row_t = i * tb + lax.broadcasted_iota(jnp.int32, (tb, 1), 0)
    live = row_t < t_valid

    def conv_act(c0):
        acc = cw_ref[GDN_CONV - 1:GDN_CONV, c0:c0 + LANES] * xs_s[pl.ds(8, tb), c0:c0 + LANES]
        for k in range(1, GDN_CONV):
            acc = acc + cw_ref[GDN_CONV - 1 - k:GDN_CONV - k, c0:c0 + LANES] * xs_s[pl.ds(8 - k, tb), c0:c0 + LANES]
        return _silu(acc)

    chains = []
    for h in range(GDN_HEADS):
        q = conv_act(h * GDN_DK)
        k = conv_act(HK + h * GDN_DK)
        v = conv_act(2 * HK + h * GDN_DV)
        q = q * lax.rsqrt(jnp.sum(q * q, axis=-1, keepdims=True) + L2_EPS) * (GDN_DK ** -0.5)
        k = k * lax.rsqrt(jnp.sum(k * k, axis=-1, keepdims=True) + L2_EPS)
        beta = _sigmoid(ba[:, h:h + 1])
        ar = ba[:, GDN_HEADS + h:GDN_HEADS + h + 1] + dtb_ref[:, h:h + 1]
        softplus = jnp.maximum(ar, 0.0) + jnp.log(1.0 + jnp.exp(-jnp.abs(ar)))
        gcol = -jnp.exp(al_ref[:, h:h + 1]) * softplus
        if t_valid < 10 ** 9:
            beta = jnp.where(live, beta, 0.0)
            gcol = jnp.where(live, gcol, 0.0)
            k = jnp.where(live, k, 0.0)
            v = jnp.where(live, v, 0.0)
        for blk in range(tb // R):
            r = slice(blk * R, (blk + 1) * R)
            qc, kc, vc, bc = q[r], k[r], v[r], beta[r]
            gc = jnp.broadcast_to(gcol[r], (R, LANES))
            sh = 1
            while sh < C:
                gc = gc + jnp.where((rr % C) >= sh, pltpu.roll(gc, sh, 0), 0.0)
                sh *= 2
            gct = gc.T
            decay = jnp.exp(jnp.where(lower, gc[:, 0:R] - gct[0:R, :], NEG_INF))
            kb = kc * bc
            a = jnp.where(strict, _dot_hp(kb, kc, NT) * decay, 0.0)
            qk = jnp.where(lower, lax.dot_general(qc.astype(bf16), kc.astype(bf16), NT,
                                                  preferred_element_type=f32) * decay, 0.0)
            gl = gc[R - 1:R, :]
            for j in range(cpb - 2, -1, -1):
                gl = jnp.where(rr < (j + 1) * C, gc[(j + 1) * C - 1:(j + 1) * C, :], gl)
            chains.append(dict(h=h, r=r, blk=blk, qc=qc, kc=kc, gc=gc, gl=gl, qk=qk, n=-a, t=eye - a,
                               rhs=jnp.concatenate([vc * bc, kb * jnp.exp(gc)], axis=1)))

    for ch in chains:
        ch['n'] = _split2(ch['n'])
    sh = 2
    while sh < C:
        for ch in chains:
            ch['n'] = _split2(_dot_parts(ch['n'], ch['n']))
        for ch in chains:
            ch['t'] = ch['t'] + _dot_parts(_split2(ch['t']), ch['n'])
        sh *= 2

    for ch in chains:
        h, r, gc, gl = ch['h'], ch['r'], ch['gc'], ch['gl']
        sol = ch['rhs'] + _dot_hp(ch['t'] - eye, ch['rhs'])
        cs = slice(h * GDN_DV, (h + 1) * GDN_DV)
        u_ref[0, r, cs] = sol[:, 0:GDN_DV]
        w_ref[0, r, cs] = sol[:, GDN_DV:].astype(w_ref.dtype)
        qg_ref[0, r, cs] = (ch['qc'] * jnp.exp(gc)).astype(qg_ref.dtype)
        kg_ref[0, r, cs] = (ch['kc'] * jnp.exp(gl - gc)).astype(kg_ref.dtype)
        qk = ch['qk'][:, 0:C]
        for j in range(1, cpb):
            qk = jnp.where(ri[:, 0:C] >= j * C, ch['qk'][:, j * C:(j + 1) * C], qk)
        qk_ref[0, r, h * C:(h + 1) * C] = qk.astype(qk_ref.dtype)
        for j in range(cpb):
            eg_ref[0, ch['blk'] * cpb + j, h:h + 1, :] = jnp.exp(gc[(j + 1) * C - 1:(j + 1) * C, :])
    eg_ref[0, :, GDN_HEADS:, :] = jnp.zeros((tb // C, 8 - GDN_HEADS, LANES), f32)


def gdn_prep(qkv, conv_buf8, conv_w, ba, a_log, dt_bias, C, tb, t_valid):
    B, T, CH = qkv.shape
    nb = T // tb
    hb = tb // 8
    per = lambda n, dt: (pl.BlockSpec((1, tb, n), lambda b, i: (b, i, 0)), jax.ShapeDtypeStruct((B, T, n), dt))
    outs = [per(GDN_WIDTH, f32), per(GDN_WIDTH, bf16), per(GDN_WIDTH, bf16), per(GDN_WIDTH, bf16),
            per(GDN_HEADS * C, bf16),
            (pl.BlockSpec((1, tb // C, 8, LANES), lambda b, i: (b, i, 0, 0)),
             jax.ShapeDtypeStruct((B, T // C, 8, LANES), f32))]
    return pl.pallas_call(
        functools.partial(_gdn_prep_kernel, tb=tb, C=C, t_valid=t_valid),
        grid=(B, nb),
        in_specs=[pl.BlockSpec((1, tb, CH), lambda b, i: (b, i, 0)),
                  pl.BlockSpec((1, 8, CH), lambda b, i: (b, jnp.maximum(i * hb - 1, 0), 0)),
                  pl.BlockSpec((1, 8, CH), lambda b, i: (b, 0, 0)),
                  pl.BlockSpec((GDN_CONV, CH), lambda b, i: (0, 0)),
                  pl.BlockSpec((1, tb, 2 * GDN_HEADS), lambda b, i: (b, i, 0)),
                  pl.BlockSpec((1, GDN_HEADS), lambda b, i: (0, 0)),
                  pl.BlockSpec((1, GDN_HEADS), lambda b, i: (0, 0))],
        out_specs=[o[0] for o in outs],
        out_shape=[o[1] for o in outs],
        scratch_shapes=[pltpu.VMEM((tb + 8, CH), f32)],
        compiler_params=_cparams("parallel", "arbitrary"),
        name="gdn_prep",
    )(qkv, qkv, conv_buf8, conv_w, ba, a_log.reshape(1, GDN_HEADS), dt_bias.reshape(1, GDN_HEADS))


def _gdn_scan_kernel(u_ref, w_ref, qg_ref, kg_ref, qk_ref, eg_ref, z_ref, ng_ref, s0_ref,
                     o_ref, sT_ref, s_s, *, C):
    c = pl.program_id(1)

    @pl.when(c == 0)
    def _():
        s_s[...] = s0_ref[0]

    H = range(GDN_HEADS)
    cs = [slice(h * GDN_DV, (h + 1) * GDN_DV) for h in H]
    S = [s_s[h] for h in H]
    Sb = [S[h].astype(bf16) for h in H]
    v_new = [u_ref[0, :, cs[h]] - jnp.dot(w_ref[0, :, cs[h]], Sb[h], preferred_element_type=f32) for h in H]
    vb = [v_new[h].astype(bf16) for h in H]
    for h in H:
        s_s[h] = S[h] * eg_ref[0, 0, h:h + 1, :] + lax.dot_general(kg_ref[0, :, cs[h]], vb[h], TN,
                                                                    preferred_element_type=f32)
    o = [jnp.dot(qg_ref[0, :, cs[h]], Sb[h], preferred_element_type=f32)
         + jnp.dot(qk_ref[0, :, h * C:(h + 1) * C], vb[h], preferred_element_type=f32) for h in H]
    outs = [_rms(o[h], ng_ref[...]) * _silu(z_ref[0, :, cs[h]]) for h in H]
    o_ref[0] = jnp.concatenate(outs, axis=1).astype(o_ref.dtype)

    @pl.when(c == pl.num_programs(1) - 1)
    def _():
        sT_ref[0] = s_s[...]


def gdn_scan(u, w, qg, kg, qk, eg, z, norm_g, s0, C):
    B, T, _ = u.shape
    per = lambda n: pl.BlockSpec((1, C, n), lambda b, c: (b, c, 0))
    st = pl.BlockSpec((1, GDN_HEADS, GDN_DK, GDN_DV), lambda b, c: (b, 0, 0, 0))
    return pl.pallas_call(
        functools.partial(_gdn_scan_kernel, C=C),
        grid=(B, T // C),
        in_specs=[per(GDN_WIDTH), per(GDN_WIDTH), per(GDN_WIDTH), per(GDN_WIDTH), per(GDN_HEADS * C),
                  pl.BlockSpec((1, 1, 8, LANES), lambda b, c: (b, c, 0, 0)),
                  per(GDN_WIDTH), pl.BlockSpec((1, GDN_DV), lambda b, c: (0, 0)), st],
        out_specs=[per(GDN_WIDTH), st],
        out_shape=[jax.ShapeDtypeStruct((B, T, GDN_WIDTH), bf16),
                   jax.ShapeDtypeStruct((B, GDN_HEADS, GDN_DK, GDN_DV), f32)],
        scratch_shapes=[pltpu.VMEM((GDN_HEADS, GDN_DK, GDN_DV), f32)],
        compiler_params=_cparams("parallel", "arbitrary"),
        name="gdn_scan",
    )(u, w, qg, kg, qk, eg, z, norm_g.reshape(1, GDN_DV), s0)


def _nsa_in_groups(w, feature_major):
    c1 = NSA_WIDTH
    c2 = c1 + 4 * NSA_KV_COLS
    c3 = c2 + 2 * NSA_KV_COLS
    c4 = c3 + 3 * NSA_HEADS
    wb = w.astype(bf16)
    xa_scale = XA_DIM ** -0.5 * LOG2E
    if feature_major:
        k0 = c1 + 2 * NSA_KV_COLS
        return [(wb[:, c1:c2].T, f32, 1.0, FEATURE_MAJOR_PER_BATCH),
                (wb[:, c1:k0], f32, 1.0, TOKEN_MAJOR),
                (wb[:, c2:c3].T, f32, 1.0, FEATURE_MAJOR_PER_BATCH),
                (wb[:, c4:].T, bf16, xa_scale, FEATURE_MAJOR),
                (wb[:, :c1].T, bf16, HEAD_DIM ** -0.5 * LOG2E, FEATURE_MAJOR),
                (wb[:, c3:c4].T, f32, 1.0, FEATURE_MAJOR),
                (wb[:, k0:k0 + NSA_KV_COLS], bf16, 1.0, TOKEN_MAJOR),
                (wb[:, k0 + NSA_KV_COLS:c2].T, bf16, 1.0, FEATURE_MAJOR),
                (wb[:, c2:c2 + NSA_KV_COLS], bf16, 1.0, TOKEN_MAJOR),
                (wb[:, c2 + NSA_KV_COLS:c3].T, bf16, 1.0, FEATURE_MAJOR)]
    wq = wb[:, :c1].reshape(-1, NSA_KV_HEADS, NSA_GROUP, HEAD_DIM)
    zq = jnp.zeros_like(wq[:, 0])
    wq = jnp.concatenate([jnp.concatenate([wq[:, 0], zq], axis=-1), jnp.concatenate([zq, wq[:, 1]], axis=-1)], axis=1)
    wq = wq.reshape(-1, NSA_HEADS * LANES)
    return [(wb[:, c1:c2], f32, 1.0, TOKEN_MAJOR), (wb[:, c2:c3], f32, 1.0, TOKEN_MAJOR),
            (wb[:, c4:], bf16, xa_scale, TOKEN_MAJOR), (wq, bf16, HEAD_DIM ** -0.5, TOKEN_MAJOR),
            (wb[:, c3:c4], f32, 1.0, TOKEN_MAJOR)]


def _gdn_in_groups(w, feature_major):
    c1 = GDN_CONV_CH
    c2 = c1 + GDN_WIDTH
    c3 = c2 + 2 * GDN_HEADS
    wb = w.astype(bf16)
    xq = ((wb[:, c3:].T, bf16, XA_DIM ** -0.5 * LOG2E, FEATURE_MAJOR) if feature_major
          else (wb[:, c3:], bf16, XA_DIM ** -0.5 * LOG2E, TOKEN_MAJOR))
    return [(wb[:, :c1], f32, 1.0, TOKEN_MAJOR), (wb[:, c1:c2], f32, 1.0, TOKEN_MAJOR),
            (wb[:, c2:c3], f32, 1.0, TOKEN_MAJOR), xq]


def _trunk(x, mem_kv, nsa_past, gdn_state, p, t_valid):
    B, T, D = x.shape
    N = B * T
    xf = x.reshape(N, D)

    prompt = nsa_past is None
    if prompt:
        kv_rows, rows_cmp, win_rows, xq, qT, gT, sk, svT, wk, wvT = norm_proj(
            xf, p['norm_mix_g'][0], p['nsa_in_prompt'], rows_per_batch=T)
        kc, vcT = cmp_prompt(rows_cmp.reshape(B, T, -1), p['cmp'])
        o_mix = nsa_prompt(qT, gT, sk, svT, wk, wvT, kc, vcT, B)
    else:
        rows, win, xq, q, gates = norm_proj(xf, p['norm_mix_g'][0], p['nsa_in_sample'])
        kv_rows = rows.reshape(B, T, -1)
        win_rows = win.reshape(B, T, -1)
        xq = xq.reshape(B, T, -1)
        page_table, cache, cache_win = nsa_past
        o_mix = nsa_sample(page_table, cache, q.reshape(B, T, -1), gates.reshape(B, T, -1), kv_rows, win_rows,
                           cache_win, p['cmp'])
    o_mem = cross_attend(xq, mem_kv, 0, B)
    xf = post_block(xf, o_mix.reshape(N, -1), o_mem.reshape(N, -1), p['w_out'][0], p['norm_ffn_g'][0],
                    p['w_up'][0], p['w_down'][0], p['final_norm_g'], final=False)

    qkv, z, ba, xq = norm_proj(xf, p['norm_mix_g'][1], p['gdn_in_prompt' if prompt else 'gdn_in_sample'])
    if not prompt:
        xq = xq.reshape(B, T, -1)
    qkv3 = qkv.reshape(B, T, -1)
    S0, conv_buf8 = gdn_state
    C = min(64, T)
    tb = min(256, T)
    u, w, qg, kg, qk, eg = gdn_prep(qkv3, conv_buf8, p['gdn_conv_w'], ba.reshape(B, T, -1), p['gdn_a_log'],
                                    p['gdn_dt_bias'], C, tb, t_valid)
    o_mix, S = gdn_scan(u, w, qg, kg, qk, eg, z.reshape(B, T, -1), p['gdn_norm_g'], S0, C)
    o_mem = cross_attend(xq, mem_kv, 1, B)
    y = post_block(xf, o_mix.reshape(N, -1), o_mem.reshape(N, -1), p['w_out'][1], p['norm_ffn_g'][1],
                   p['w_up'][1], p['w_down'][1], p['final_norm_g'], final=True)
    return y.reshape(B, T, D), kv_rows, win_rows, S, qkv3


def kernel(x_prompt, x_sample, mem_prompt, cache_nsa_kv, cache_nsa_win, state_gdn_s, state_gdn_conv,
           cache_mem_kv, page_table, norm_mix_g, norm_mem_g, w_mem_kv, nsa_w_in, cmp_pe_k, cmp_w1_k,
           cmp_w2_k, cmp_pe_v, cmp_w1_v, cmp_w2_v, gdn_w_in, gdn_conv_w, gdn_a_log, gdn_dt_bias,
           gdn_norm_g, w_out, norm_ffn_g, w_up, w_down, final_norm_g):
    B, T, D = x_prompt.shape
    DB, TS, _ = x_sample.shape
    M = mem_prompt.shape[1]
    p = dict(norm_mix_g=norm_mix_g, nsa_in_prompt=_nsa_in_groups(nsa_w_in[0], True),
             nsa_in_sample=_nsa_in_groups(nsa_w_in[0], False), gdn_in_prompt=_gdn_in_groups(gdn_w_in[0], True),
             gdn_in_sample=_gdn_in_groups(gdn_w_in[0], False),
             cmp=_cmp_weights(cmp_pe_k[0], cmp_w1_k[0], cmp_w2_k[0]) + _cmp_weights(cmp_pe_v[0], cmp_w1_v[0], cmp_w2_v[0]),
             gdn_conv_w=gdn_conv_w[0], gdn_a_log=gdn_a_log[0], gdn_dt_bias=gdn_dt_bias[0], gdn_norm_g=gdn_norm_g[0],
             w_out=w_out.astype(bf16), norm_ffn_g=norm_ffn_g, w_up=w_up.astype(bf16), w_down=w_down.astype(bf16),
             final_norm_g=final_norm_g)

    mem_flat = mem_prompt.reshape(B * M, D)
    mem_kv = [norm_proj(mem_flat, norm_mem_g[i], [(w_mem_kv[i].astype(bf16).T, f32, 1.0, FEATURE_MAJOR_PER_BATCH)],
                        tm=M, rows_per_batch=M)[0] for i in range(DEPTH)]
    mem_kv_p = jnp.stack(mem_kv, axis=1)
    zero_state = (jnp.zeros((B, GDN_HEADS, GDN_DK, GDN_DV), f32), jnp.zeros((B, 8, GDN_CONV_CH), f32))
    y_p, rows_p, win_p, S_p, qkv_p = _trunk(x_prompt, mem_kv_p, None, zero_state, p, 10 ** 9)

    TP = 8
    x_s = jnp.pad(x_sample, ((0, 0), (0, TP - TS), (0, 0)))
    cache = cache_nsa_kv.reshape(cache_nsa_kv.shape[0], PAGE_SIZE, -1)
    cache_win = cache_nsa_win.reshape(DB, cache_nsa_win.shape[2], -1)
    conv8 = jnp.pad(state_gdn_conv[:, 0], ((0, 0), (8 - (GDN_CONV - 1), 0), (0, 0)))
    mem_kv_s = cache_mem_kv.transpose(0, 1, 3, 4, 5, 2).reshape(DB, DEPTH, 2 * XA_WIDTH, M)
    y_s, rows_s, win_s, S_s, qkv_s = _trunk(x_s, mem_kv_s, (page_table, cache, cache_win),
                                            (state_gdn_s[:, 0], conv8), p, TS)

    kvshape = (4, NSA_KV_HEADS, HEAD_DIM)
    nsa_kv_prompt = rows_p.reshape(B, 1, *kvshape, T).transpose(0, 5, 1, 2, 3, 4)
    nsa_kv_sample = rows_s[:, :TS].reshape(DB, TS, 1, *kvshape)
    wshape = (2, NSA_KV_HEADS, HEAD_DIM)
    wkeep = min(WINDOW, T)
    nsa_win_prompt = win_p[:, :, T - wkeep:].reshape(B, 1, *wshape, wkeep).transpose(0, 1, 5, 2, 3, 4)
    win_cat = jnp.concatenate([cache_win, win_s[:, :TS]], axis=1)
    nsa_win_sample = win_cat[:, -WINDOW:].reshape(DB, 1, -1, *wshape)
    gdn_conv_prompt = qkv_p[:, None, T - (GDN_CONV - 1):]
    conv_cat = jnp.concatenate([state_gdn_conv[:, 0], qkv_s[:, :TS]], axis=1)
    gdn_conv_sample = conv_cat[:, None, -(GDN_CONV - 1):]
    mem_kv_prompt = mem_kv_p.reshape(B, DEPTH, 2, XA_HEADS, XA_DIM, M).transpose(0, 1, 5, 2, 3, 4)
    return (y_p, y_s[:, :TS], nsa_kv_prompt, nsa_kv_sample, nsa_win_prompt, nsa_win_sample,
            S_p[:, None], S_s[:, None], gdn_conv_prompt, gdn_conv_sample, mem_kv_prompt)
```

```python
import functools
import math

import jax
import jax.numpy as jnp
from jax import lax
from jax.experimental import pallas as pl
from jax.experimental.pallas import tpu as pltpu

f32 = jnp.float32
bf16 = jnp.bfloat16

D_MODEL = 1024
DEPTH = 2
PAGE_SIZE = 128
HEAD_DIM = 64
NSA_HEADS = 12
NSA_KV_HEADS = 2
NSA_GROUP = NSA_HEADS // NSA_KV_HEADS
NSA_WIDTH = NSA_HEADS * HEAD_DIM
NSA_KV_COLS = NSA_KV_HEADS * HEAD_DIM
CMP_STRIDE = 16
CMP_BLOCK = 2 * CMP_STRIDE
CMP_HIDDEN = 128
SLC_BLOCK = 64
SLC_TOPN = 16
WINDOW = 512
GDN_HEADS = 6
GDN_DK = 128
GDN_DV = 128
GDN_WIDTH = GDN_HEADS * GDN_DV
GDN_CONV = 4
GDN_CONV_CH = 2 * GDN_HEADS * GDN_DK + GDN_HEADS * GDN_DV
XA_HEADS = 4
XA_DIM = 64
XA_WIDTH = XA_HEADS * XA_DIM
D_FF = 4 * D_MODEL
NORM_EPS = 1e-6
L2_EPS = 1e-6
NEG_INF = -1e30
FORCED = 1e9

VMEM_LIMIT = 52 * 1024 * 1024
LANES = 128

SLOPES = [[2.0 ** (-8.0 * (k * NSA_GROUP + g + 1) / NSA_HEADS) for g in range(NSA_GROUP)]
          for k in range(NSA_KV_HEADS)]

NT = (((1,), (1,)), ((), ()))
TN = (((0,), (0,)), ((), ()))


def _cparams(*sem):
    return pltpu.CompilerParams(dimension_semantics=sem, vmem_limit_bytes=VMEM_LIMIT)


def _rms(x, g):
    r = lax.rsqrt(jnp.mean(x * x, axis=-1, keepdims=True) + NORM_EPS)
    return (x * r) * g


def _sigmoid(x):
    return 1.0 / (1.0 + jnp.exp(-x))


def _silu(x):
    return x * _sigmoid(x)


def _split3(a):
    hi = a.astype(bf16)
    r1 = a - hi.astype(f32)
    mid = r1.astype(bf16)
    lo = (r1 - mid.astype(f32)).astype(bf16)
    return hi, mid, lo


def _dot_exact_rhs(a, b_bf16):
    hi, mid, lo = _split3(a)
    d = lambda x: jnp.dot(x, b_bf16, preferred_element_type=f32)
    return d(hi) + d(mid) + d(lo)


def _split2(a):
    hi = a.astype(bf16)
    return hi, (a - hi.astype(f32)).astype(bf16)


def _dot_parts(a, b, dims=None):
    if dims is None:
        d = lambda x, y: jnp.dot(x, y, preferred_element_type=f32)
    else:
        d = lambda x, y: lax.dot_general(x, y, dims, preferred_element_type=f32)
    return d(a[0], b[0]) + (d(a[0], b[1]) + d(a[1], b[0]))


def _dot_hp(a, b, dims=None):
    return _dot_parts(_split2(a), _split2(b), dims)


TOKEN_MAJOR, FEATURE_MAJOR, FEATURE_MAJOR_PER_BATCH = 'nt', 'fm', 'fmb'


def _norm_proj_kernel(x_ref, g_ref, *refs, specs):
    n = len(specs)
    w_refs, o_refs = refs[:n], refs[n:]
    h = _rms(x_ref[...], g_ref[...]).astype(bf16)
    for (scale, kind), w_ref, o_ref in zip(specs, w_refs, o_refs):
        if kind == TOKEN_MAJOR:
            y = jnp.dot(h, w_ref[...], preferred_element_type=f32)
        else:
            y = lax.dot_general(w_ref[...], h, NT, preferred_element_type=f32)
        if scale != 1.0:
            y = y * scale
        o_ref[...] = y.astype(o_ref.dtype).reshape(o_ref.shape)


def norm_proj(x, g, groups, tm=256, rows_per_batch=None):
    N, D = x.shape
    tm = min(tm, N)
    assert N % tm == 0
    in_specs = [pl.BlockSpec((tm, D), lambda i: (i, 0)), pl.BlockSpec((1, D), lambda i: (0, 0))]
    out_specs, out_shape = [], []
    for w, dt, _, kind in groups:
        in_specs.append(pl.BlockSpec(w.shape, lambda i: (0, 0)))
        if kind == TOKEN_MAJOR:
            n = w.shape[1]
            out_specs.append(pl.BlockSpec((tm, n), lambda i: (i, 0)))
            out_shape.append(jax.ShapeDtypeStruct((N, n), dt))
        elif kind == FEATURE_MAJOR:
            n = w.shape[0]
            out_specs.append(pl.BlockSpec((n, tm), lambda i: (0, i)))
            out_shape.append(jax.ShapeDtypeStruct((n, N), dt))
        else:
            n = w.shape[0]
            per = rows_per_batch // tm
            assert rows_per_batch % tm == 0
            out_specs.append(pl.BlockSpec((1, n, tm), lambda i: (i // per, 0, i % per)))
            out_shape.append(jax.ShapeDtypeStruct((N // rows_per_batch, n, rows_per_batch), dt))
    return pl.pallas_call(
        functools.partial(_norm_proj_kernel, specs=tuple((s, t) for _, _, s, t in groups)),
        grid=(N // tm,),
        in_specs=in_specs,
        out_specs=out_specs,
        out_shape=out_shape,
        compiler_params=_cparams("parallel"),
        name="norm_proj",
    )(x, g.reshape(1, D), *[w for w, _, _, _ in groups])


def _xattn_kernel(q_ref, kv_ref, o_ref, *, q_feature_major):
    kvT = kv_ref[0, 0]
    qT = q_ref[...] if q_feature_major else q_ref[0].astype(f32).T.astype(bf16)
    outs = []
    for h in range(XA_HEADS):
        kT = kvT[h * XA_DIM:(h + 1) * XA_DIM].astype(bf16)
        vT = kvT[XA_WIDTH + h * XA_DIM:XA_WIDTH + (h + 1) * XA_DIM].astype(bf16)
        s = lax.dot_general(kT, qT[h * XA_DIM:(h + 1) * XA_DIM], TN, preferred_element_type=f32)
        e = jnp.exp2(s - jnp.max(s, axis=0, keepdims=True))
        o = jnp.dot(vT, e.astype(bf16), preferred_element_type=f32)
        outs.append(o / jnp.sum(e, axis=0, keepdims=True))
    o_ref[0] = jnp.concatenate(outs, axis=0).T.astype(o_ref.dtype)


def cross_attend(xq, kvT, layer, B, tq=512):
    q_feature_major = xq.ndim == 2
    T = xq.shape[1] // B if q_feature_major else xq.shape[1]
    M = kvT.shape[3]
    tq = min(tq, T)
    nq = T // tq
    if q_feature_major:
        q_spec = pl.BlockSpec((XA_WIDTH, tq), lambda b, i: (0, b * nq + i))
    else:
        q_spec = pl.BlockSpec((1, tq, XA_WIDTH), lambda b, i: (b, i, 0))
    return pl.pallas_call(
        functools.partial(_xattn_kernel, q_feature_major=q_feature_major),
        grid=(B, nq),
        in_specs=[q_spec, pl.BlockSpec((1, 1, 2 * XA_WIDTH, M), lambda b, i: (b, layer, 0, 0))],
        out_specs=pl.BlockSpec((1, tq, XA_WIDTH), lambda b, i: (b, i, 0)),
        out_shape=jax.ShapeDtypeStruct((B, T, XA_WIDTH), bf16),
        compiler_params=_cparams("parallel", "parallel"),
        name="cross_attend",
    )(xq, kvT)


def _post_kernel(x_ref, om_ref, oc_ref, wo_ref, gf_ref, wu_ref, wd_ref, gl_ref, o_ref,
                 x1_s, h_s, acc_s, *, mix_w, final):
    j = pl.program_id(1)

    @pl.when(j == 0)
    def _():
        x1 = x_ref[...] + (jnp.dot(om_ref[...], wo_ref[0:mix_w, :], preferred_element_type=f32)
                           + jnp.dot(oc_ref[...], wo_ref[mix_w:, :], preferred_element_type=f32))
        x1_s[...] = x1
        h_s[...] = _rms(x1, gf_ref[...]).astype(bf16)
        acc_s[...] = jnp.zeros_like(acc_s)

    u = jnp.dot(h_s[...], wu_ref[...], preferred_element_type=f32)
    u = jnp.square(jnp.maximum(u, 0.0)).astype(bf16)
    acc_s[...] += jnp.dot(u, wd_ref[...], preferred_element_type=f32)

    @pl.when(j == pl.num_programs(1) - 1)
    def _():
        x2 = x1_s[...] + acc_s[...]
        if final:
            x2 = _rms(x2, gl_ref[...])
        o_ref[...] = x2


def post_block(x, o_mix, o_mem, w_out, g_ffn, w_up, w_down, g_final, final, tm=1024, tf=512):
    N, D = x.shape
    mix_w = o_mix.shape[1]
    F = w_up.shape[1]
    tm = min(tm, N)
    return pl.pallas_call(
        functools.partial(_post_kernel, mix_w=mix_w, final=final),
        grid=(N // tm, F // tf),
        in_specs=[pl.BlockSpec((tm, D), lambda i, j: (i, 0)),
                  pl.BlockSpec((tm, mix_w), lambda i, j: (i, 0)),
                  pl.BlockSpec((tm, XA_WIDTH), lambda i, j: (i, 0)),
                  pl.BlockSpec((mix_w + XA_WIDTH, D), lambda i, j: (0, 0)),
                  pl.BlockSpec((1, D), lambda i, j: (0, 0)),
                  pl.BlockSpec((D, tf), lambda i, j: (0, j)),
                  pl.BlockSpec((tf, D), lambda i, j: (j, 0)),
                  pl.BlockSpec((1, D), lambda i, j: (0, 0))],
        out_specs=pl.BlockSpec((tm, D), lambda i, j: (i, 0)),
        out_shape=jax.ShapeDtypeStruct((N, D), f32),
        scratch_shapes=[pltpu.VMEM((tm, D), f32), pltpu.VMEM((tm, D), bf16), pltpu.VMEM((tm, D), f32)],
        compiler_params=_cparams("parallel", "arbitrary"),
        name="post_block",
    )(x, o_mix, o_mem, w_out, g_ffn.reshape(1, D), w_up, w_down, g_final.reshape(1, D))


def _compress(load_chunk_rows, nch, pe_ref, w_ref, w2_ref):
    acc_p = jnp.zeros((nch, 2 * CMP_HIDDEN), f32)
    acc_q = jnp.zeros((nch, 2 * CMP_HIDDEN), f32)
    for l in range(CMP_STRIDE):
        x = load_chunk_rows(l)
        xp = (x + pe_ref[l:l + 1, :]).astype(bf16)
        xq = (x + pe_ref[CMP_STRIDE + l:CMP_STRIDE + l + 1, :]).astype(bf16)
        acc_p = acc_p + jnp.dot(xp, w_ref[l, :, 0:2 * CMP_HIDDEN], preferred_element_type=f32)
        acc_q = acc_q + jnp.dot(xq, w_ref[l, :, 2 * CMP_HIDDEN:], preferred_element_type=f32)
    hid = _silu(acc_p + pltpu.roll(acc_q, nch - 1, 0)).astype(bf16)
    outs = [jnp.dot(hid[:, h * CMP_HIDDEN:(h + 1) * CMP_HIDDEN], w2_ref[...], preferred_element_type=f32)
            for h in range(NSA_KV_HEADS)]
    return jnp.concatenate(outs, axis=1)


def _cmp_prompt_kernel(xk_ref, xv_ref, pek_ref, pev_ref, wk_ref, wv_ref, w2k_ref, w2v_ref, kc_ref, vc_ref, *, nch):
    kc_ref[0] = _compress(lambda l: xk_ref[0, pl.ds(l, nch, stride=CMP_STRIDE), :], nch, pek_ref, wk_ref, w2k_ref)
    vc_ref[0] = _compress(lambda l: xv_ref[0, pl.ds(l, nch, stride=CMP_STRIDE), :], nch, pev_ref, wv_ref, w2v_ref).T


def _cmp_weights(pe, w1, w2):
    pe2 = jnp.tile(pe, (1, NSA_KV_HEADS))
    z = jnp.zeros((CMP_STRIDE, HEAD_DIM, CMP_HIDDEN), w1.dtype)
    a, b = w1[:CMP_STRIDE], w1[CMP_STRIDE:]
    top = jnp.concatenate([a, z, b, z], axis=2)
    bot = jnp.concatenate([z, a, z, b], axis=2)
    wbd = jnp.concatenate([top, bot], axis=1).astype(bf16)
    return pe2, wbd, w2.astype(bf16)


def cmp_prompt(rows, cw):
    B, T, _ = rows.shape
    nch = T // CMP_STRIDE
    pek, wk, w2k, pev, wv, w2v = cw
    full = lambda a: pl.BlockSpec(a.shape, lambda b: (0,) * a.ndim)
    return pl.pallas_call(
        functools.partial(_cmp_prompt_kernel, nch=nch),
        grid=(B,),
        in_specs=[pl.BlockSpec((1, T, LANES), lambda b: (b, 0, 0)),
                  pl.BlockSpec((1, T, LANES), lambda b: (b, 0, 1)),
                  full(pek), full(pev), full(wk), full(wv), full(w2k), full(w2v)],
        out_specs=[pl.BlockSpec((1, nch, LANES), lambda b: (b, 0, 0)), pl.BlockSpec((1, LANES, nch), lambda b: (b, 0, 0))],
        out_shape=[jax.ShapeDtypeStruct((B, nch, LANES), f32), jax.ShapeDtypeStruct((B, LANES, nch), f32)],
        compiler_params=_cparams("parallel"),
        name="cmp_prompt",
    )(rows, rows, pek, pev, wk, wv, w2k, w2v)


def _padded_queries(q_all, kvh, tq):
    del tq
    lo = kvh * NSA_GROUP
    return jnp.concatenate([q_all[:, (lo + g) * LANES:(lo + g + 1) * LANES] for g in range(NSA_GROUP)], axis=0)


def _cmp_branch(qm, kc, vc, pos_col, kvh, tq):
    ncp = kc.shape[0]
    s = lax.dot_general(qm, kc.astype(bf16), NT, preferred_element_type=f32)
    n_idx = lax.broadcasted_iota(jnp.int32, (tq, ncp), 1)
    cdist = pos_col - (n_idx * CMP_STRIDE + (CMP_BLOCK - 1))
    cvis = cdist >= 0
    any_vis = pos_col >= (CMP_BLOCK - 1)
    cdf = cdist.astype(f32)
    p_sum = jnp.zeros((tq, ncp), f32)
    ps = []
    for g in range(NSA_GROUP):
        sg = jnp.where(cvis, s[g * tq:(g + 1) * tq] - SLOPES[kvh][g] * cdf, NEG_INF)
        m = jnp.max(sg, axis=-1, keepdims=True)
        e = jnp.exp(sg - m)
        p = e / jnp.sum(e, axis=-1, keepdims=True)
        p = jnp.where(any_vis, p, 0.0)
        p_sum = p_sum + p
        ps.append(p)
    p_all = jnp.concatenate(ps, axis=0).astype(bf16)
    o = jnp.dot(p_all, vc.astype(bf16), preferred_element_type=f32)
    return o, p_sum


def _select_blocks(p_sum, pos_col, ns, nsp, tq):
    ncp = p_sum.shape[1]
    ci = lax.broadcasted_iota(jnp.int32, (ncp, nsp), 0) * CMP_STRIDE
    cj = lax.broadcasted_iota(jnp.int32, (ncp, nsp), 1)
    c2s = ((ci < (cj + 1) * SLC_BLOCK) & (ci + CMP_BLOCK > cj * SLC_BLOCK)).astype(f32).astype(bf16)
    imp = _dot_exact_rhs(p_sum, c2s)
    blk = lax.broadcasted_iota(jnp.int32, (tq, nsp), 1)
    cur = pos_col // SLC_BLOCK
    forced = (blk == 0) | (blk == cur) | (blk == cur - 1)
    imp = jnp.where(forced, FORCED, jnp.where(blk <= cur, imp, NEG_INF))
    rank = jnp.zeros((tq, nsp), jnp.int32)
    for jp in range(ns):
        col = imp[:, jp:jp + 1]
        beats = (col > imp) | ((col == imp) & (blk > jp))
        rank = rank + beats.astype(jnp.int32)
    return ((rank < SLC_TOPN) & (blk <= cur)).astype(f32)


def _combine(gates, kvh, tq, o_cmp, o_slc, o_win):
    lo = kvh * HEAD_DIM
    outs = []
    for g in range(NSA_GROUP):
        h = kvh * NSA_GROUP + g
        r = slice(g * tq, (g + 1) * tq)
        o = (gates[:, 3 * h:3 * h + 1] * o_cmp[r] + gates[:, 3 * h + 1:3 * h + 2] * o_slc[r]
             + gates[:, 3 * h + 2:3 * h + 3] * o_win[r])
        outs.append(o[:, lo:lo + HEAD_DIM])
    return outs


LOG2E = 1.4426950408889634
SLOPES2 = [[v * LOG2E for v in row] for row in SLOPES]

POS_SPLIT = 8
F_SEL0 = 16
MASK_BIG = 1e30


def position_features(pos, blk, ns):
    n = pos.shape[0]
    lane = jnp.arange(HEAD_DIM)[None, :]
    hi = ((pos // POS_SPLIT) * POS_SPLIT).astype(f32)[:, None]
    lo = (pos % POS_SPLIT).astype(f32)[:, None]
    f = jnp.where(lane < 3, hi, jnp.where(lane < 6, lo, jnp.where(lane < 9, 1.0, 0.0)))
    if ns:
        f = jnp.where((lane >= F_SEL0) & (lane < F_SEL0 + ns), (blk[:, None] == lane - F_SEL0).astype(f32), f)
    return jnp.broadcast_to(f, (n, HEAD_DIM)).astype(f32)


def feature_tables(T, nch):
    pos = jnp.arange(T, dtype=jnp.int32)
    kf = position_features(pos, pos // SLC_BLOCK, T // SLC_BLOCK)
    cpos = jnp.arange(nch, dtype=jnp.int32) * CMP_STRIDE + (CMP_BLOCK - 1)
    cf = position_features(cpos, cpos, 0)
    z = lambda a: jnp.zeros_like(a)
    two = lambda a: jnp.stack([jnp.concatenate([z(a), a], axis=1), jnp.concatenate([a, z(a)], axis=1)]).astype(bf16)
    return two(kf), two(cf)


def _query_features(pos_row, selT, kvh, tq):
    posf = pos_row.astype(f32)
    r16 = lax.broadcasted_iota(jnp.int32, (16, tq), 0)
    zeros16 = jnp.zeros((16, tq), f32)
    selbig = (selT - 1.0) * MASK_BIG
    ns = selT.shape[0]
    pad = jnp.zeros((HEAD_DIM - F_SEL0 - ns, tq), f32)
    plain, full = [], []
    for g in range(NSA_GROUP):
        s = jnp.full((1, tq), SLOPES2[kvh][g], f32)
        s_parts = _split3(s)
        a_parts = _split3(s * posf)
        rows = [p.astype(f32) for p in s_parts] * 2 + [-p.astype(f32) for p in a_parts]
        blk0 = zeros16
        for i, rv in enumerate(rows):
            blk0 = jnp.where(r16 == i, rv, blk0)
        plain.append(jnp.concatenate([blk0, jnp.zeros((HEAD_DIM - 16, tq), f32)], axis=0).astype(bf16))
        full.append(jnp.concatenate([blk0, selbig, pad], axis=0).astype(bf16))
    return plain, full


def _aug_queries(qT, feats, kvh, tq):
    out = []
    for g in range(NSA_GROUP):
        h = kvh * NSA_GROUP + g
        qh = qT[h * HEAD_DIM:(h + 1) * HEAD_DIM, :]
        out.append(jnp.concatenate([qh, feats[g]] if kvh == 0 else [feats[g], qh], axis=0))
    return out


def _aug_keys(k_tile, feat_tile, kvh):
    lane = lax.broadcasted_iota(jnp.int32, k_tile.shape, 1)
    own = (lane < HEAD_DIM) if kvh == 0 else (lane >= HEAD_DIM)
    return jnp.where(own, k_tile, feat_tile)


def _with_ones(vT, kvh):
    half = vT.shape[0] // 2
    ones = jnp.ones((half, vT.shape[1]), vT.dtype)
    return jnp.concatenate([vT[:half], ones] if kvh == 0 else [ones, vT[half:]], axis=0)


def _cmp_branch_t(q_aug, kc_aug, vcT, pos_row, tq):
    ncp = kc_aug.shape[0]
    n_idx = lax.broadcasted_iota(jnp.int32, (ncp, tq), 0)
    cvis = pos_row >= (n_idx * CMP_STRIDE + (CMP_BLOCK - 1))
    any_vis = pos_row >= (CMP_BLOCK - 1)
    p_sum = jnp.zeros((ncp, tq), f32)
    outs = []
    for g in range(NSA_GROUP):
        sg = jnp.where(cvis, jnp.dot(kc_aug, q_aug[g], preferred_element_type=f32), NEG_INF)
        e = jnp.exp2(sg - jnp.max(sg, axis=0, keepdims=True))
        p = e / jnp.sum(e, axis=0, keepdims=True)
        p = jnp.where(any_vis, p, 0.0)
        p_sum = p_sum + p
        outs.append(jnp.dot(vcT, p.astype(bf16), preferred_element_type=f32))
    return outs, p_sum


def _select_blocks_t(p_sumT, pos_row, ns, tq):
    ncp = p_sumT.shape[0]
    cj = lax.broadcasted_iota(jnp.int32, (ns, ncp), 0)
    ci = lax.broadcasted_iota(jnp.int32, (ns, ncp), 1) * CMP_STRIDE
    c2sT = ((ci < (cj + 1) * SLC_BLOCK) & (ci + CMP_BLOCK > cj * SLC_BLOCK)).astype(f32).astype(bf16)
    hi, mid, lo = _split3(p_sumT)
    d = lambda x: jnp.dot(c2sT, x, preferred_element_type=f32)
    imp = d(hi) + d(mid) + d(lo)
    blk = lax.broadcasted_iota(jnp.int32, (ns, tq), 0)
    cur = pos_row // SLC_BLOCK
    forced = (blk == 0) | (blk == cur) | (blk == cur - 1)
    imp = jnp.where(forced, FORCED, jnp.where(blk <= cur, imp, NEG_INF))
    rank = jnp.zeros((ns, tq), jnp.int32)
    for jp in range(ns):
        row = imp[jp:jp + 1, :]
        beats = (row > imp) | ((row == imp) & (blk > jp))
        rank = rank + beats.astype(jnp.int32)
    return ((rank < SLC_TOPN) & (blk <= cur)).astype(f32)


def _nsa_prompt_kernel(qT_ref, gT_ref, sk_ref, svT_ref, wk_ref, wvT_ref, kc_ref, vcT_ref, kf_ref, cf_ref, o_ref,
                       m_s, acc_s, *, tq, kt, ns):
    qi = pl.program_id(1)
    p0 = qi * tq
    pos_row = p0 + lax.broadcasted_iota(jnp.int32, (1, tq), 1)
    gates = _sigmoid(gT_ref[...])
    qT = qT_ref[...]
    kc = kc_ref[0].astype(bf16)
    vcT = vcT_ref[0].astype(bf16)
    half = LANES // 2
    KV = range(NSA_KV_HEADS)

    o_cmp, q_pos, q_sel = [], [], []
    for kvh in KV:
        zsel = jnp.zeros((ns, tq), f32)
        plain, _ = _query_features(pos_row, zsel, kvh, tq)
        qa = _aug_queries(qT, plain, kvh, tq)
        oc, p_sum = _cmp_branch_t(qa, _aug_keys(kc, cf_ref[kvh], kvh), vcT, pos_row, tq)
        selT = _select_blocks_t(p_sum, pos_row, ns, tq)
        _, full = _query_features(pos_row, selT, kvh, tq)
        o_cmp.append(oc)
        q_pos.append(qa)
        q_sel.append(_aug_queries(qT, full, kvh, tq))

    m_s[...] = jnp.full(m_s.shape, NEG_INF, f32)
    acc_s[...] = jnp.zeros(acc_s.shape, f32)
    rel = pos_row - lax.broadcasted_iota(jnp.int32, (kt, tq), 0)

    def slc_tile(c, causal):
        k0 = pl.multiple_of(c * kt, kt)
        k_tile = sk_ref[pl.ds(k0, kt), :]
        v_tile = svT_ref[:, pl.ds(k0, kt)]
        ok = (rel - k0) >= 0
        for kvh in KV:
            ka = _aug_keys(k_tile, kf_ref[kvh, pl.ds(k0, kt), :], kvh)
            va = _with_ones(v_tile, kvh)
            for g in range(NSA_GROUP):
                cs = slice(g * tq, (g + 1) * tq)
                sg = jnp.dot(ka, q_sel[kvh][g], preferred_element_type=f32)
                if causal:
                    sg = jnp.where(ok, sg, NEG_INF)
                m_old = m_s[kvh, :, cs]
                m_new = jnp.maximum(m_old, jnp.max(sg, axis=0, keepdims=True))
                alpha = jnp.exp2(m_old - m_new)
                p = jnp.exp2(sg - m_new).astype(bf16)
                acc_s[kvh, :, cs] = alpha * acc_s[kvh, :, cs] + jnp.dot(va, p, preferred_element_type=f32)
                m_s[kvh, :, cs] = m_new

    n_kt = (p0 + tq + kt - 1) // kt

    def slc_body(c, _):
        slc_tile(c, False)
        return 0

    lax.fori_loop(0, n_kt - 1, slc_body, 0)
    slc_tile(n_kt - 1, True)

    wlen = WINDOW + tq
    w0 = pl.multiple_of(jnp.maximum(p0 - WINDOW, 0), tq)
    k_win = wk_ref[pl.ds(w0, wlen), :]
    v_win = wvT_ref[:, pl.ds(w0, wlen)]
    dist = pos_row - (w0 + lax.broadcasted_iota(jnp.int32, (wlen, tq), 0))
    valid = (dist >= 0) & (dist < WINDOW)
    heads = []
    for kvh in KV:
        ka = _aug_keys(k_win, kf_ref[kvh, pl.ds(w0, wlen), :], kvh)
        va = _with_ones(v_win, kvh)
        d0, l0 = (0, half) if kvh == 0 else (half, 0)
        for g in range(NSA_GROUP):
            h = kvh * NSA_GROUP + g
            cs = slice(g * tq, (g + 1) * tq)
            sg = jnp.where(valid, jnp.dot(ka, q_pos[kvh][g], preferred_element_type=f32), NEG_INF)
            p = jnp.exp2(sg - jnp.max(sg, axis=0, keepdims=True)).astype(bf16)
            o_win = jnp.dot(va, p, preferred_element_type=f32)
            o_slc = acc_s[kvh, :, cs]
            o = (gates[3 * h:3 * h + 1] * o_cmp[kvh][g][d0:d0 + half]
                 + gates[3 * h + 1:3 * h + 2] * (o_slc[d0:d0 + half] / o_slc[l0:l0 + 1])
                 + gates[3 * h + 2:3 * h + 3] * (o_win[d0:d0 + half] / o_win[l0:l0 + 1]))
            heads.append(o)
    o_ref[0] = jnp.concatenate(heads, axis=0).T.astype(o_ref.dtype)


def nsa_prompt(qT, gT, sk, svT, wk, wvT, kc, vcT, B, tq=128, kt=256):
    N = sk.shape[0]
    T = N // B
    ns = T // SLC_BLOCK
    nq = T // tq
    ncp = kc.shape[1]
    assert T % kt == 0 and T % tq == 0 and kt % tq == 0 and T >= WINDOW + tq and T <= POS_SPLIT * 256
    assert F_SEL0 + ns <= HEAD_DIM
    kf, cf = feature_tables(T, ncp)
    keys = pl.BlockSpec((T, LANES), lambda b, i: (b, 0))
    vals = pl.BlockSpec((LANES, T), lambda b, i: (0, b))
    return pl.pallas_call(
        functools.partial(_nsa_prompt_kernel, tq=tq, kt=kt, ns=ns),
        grid=(B, nq),
        in_specs=[pl.BlockSpec((NSA_WIDTH, tq), lambda b, i: (0, b * nq + i)),
                  pl.BlockSpec((3 * NSA_HEADS, tq), lambda b, i: (0, b * nq + i)),
                  keys, vals, keys, vals,
                  pl.BlockSpec((1, ncp, LANES), lambda b, i: (b, 0, 0)),
                  pl.BlockSpec((1, LANES, ncp), lambda b, i: (b, 0, 0)),
                  pl.BlockSpec((NSA_KV_HEADS, T, LANES), lambda b, i: (0, 0, 0)),
                  pl.BlockSpec((NSA_KV_HEADS, ncp, LANES), lambda b, i: (0, 0, 0))],
        out_specs=pl.BlockSpec((1, tq, NSA_WIDTH), lambda b, i: (b, i, 0)),
        out_shape=jax.ShapeDtypeStruct((B, T, NSA_WIDTH), bf16),
        scratch_shapes=[pltpu.VMEM((NSA_KV_HEADS, 1, NSA_GROUP * tq), f32),
                        pltpu.VMEM((NSA_KV_HEADS, LANES, NSA_GROUP * tq), f32)],
        compiler_params=_cparams("parallel", "arbitrary"),
        name="nsa_prompt",
    )(qT, gT, sk, svT, wk, wvT, kc, vcT, kf, cf)


SF_SPLIT = 64
SF_BLK0 = 16
SF_ROWS = 256


def sample_key_features(past_len, nk, tq, ns):
    sf_new0 = SF_BLK0 + -(-ns // 8) * 8
    assert sf_new0 + tq + 1 <= SF_ROWS and nk <= SF_SPLIT * 256
    pos = jnp.arange(nk, dtype=jnp.int32)[None, :]
    row = jnp.arange(SF_ROWS, dtype=jnp.int32)[:, None]
    hi = ((pos // SF_SPLIT) * SF_SPLIT).astype(f32)
    lo = (pos % SF_SPLIT).astype(f32)
    f = jnp.where(row < 3, hi, jnp.where(row < 6, lo, jnp.where(row < 9, 1.0, 0.0)))
    blk = (row >= SF_BLK0) & (row < SF_BLK0 + ns) & (pos // SLC_BLOCK == row - SF_BLK0)
    new = (row >= sf_new0) & (row < sf_new0 + tq) & (pos - past_len == row - sf_new0)
    pad = (row == sf_new0 + tq) & (pos >= past_len + tq)
    return jnp.where(blk | new | pad, 1.0, f).astype(bf16), sf_new0


def _sample_query_features(sel, pos_col, kvh, tq, ns, sf_new0):
    lane = lax.broadcasted_iota(jnp.int32, (tq, SF_ROWS), 1)
    t_idx = lax.broadcasted_iota(jnp.int32, (tq, SF_ROWS), 0)
    selbig = (pltpu.roll(sel[:, 0:SF_ROWS], SF_BLK0, 1) - 1.0) * MASK_BIG
    base = jnp.where((lane >= SF_BLK0) & (lane < SF_BLK0 + ns), selbig, 0.0)
    base = jnp.where((lane >= sf_new0) & (lane < sf_new0 + tq) & (lane - sf_new0 > t_idx), -MASK_BIG, base)
    base = jnp.where(lane == sf_new0 + tq, -MASK_BIG, base)
    posf = pos_col.astype(f32)
    out = []
    for g in range(NSA_GROUP):
        s = jnp.full((tq, 1), SLOPES[kvh][g], f32)
        cols = [p.astype(f32) for p in _split3(s)] * 2 + [-p.astype(f32) for p in _split3(s * posf)]
        qf = base
        for i, cv in enumerate(cols):
            qf = jnp.where(lane == i, cv, qf)
        out.append(qf)
    return jnp.concatenate(out, axis=0).astype(bf16)


def _new_rows_t(rows, width):
    tq = rows.shape[0]
    sq = jnp.concatenate([rows, jnp.zeros((LANES - tq, LANES), f32)], axis=0).T
    if width == LANES:
        return sq
    return jnp.concatenate([sq, jnp.zeros((LANES, width - LANES), f32)], axis=1)


def _compress_staged(xs, nch, pe_ref, w_ref, w2_ref):
    acc = jnp.zeros((nch, 4 * CMP_HIDDEN), f32)
    c = jnp.zeros((8, 2 * CMP_HIDDEN), f32)
    for j in range(CMP_STRIDE // 2):
        lhs = jnp.concatenate([xs[2 * j], xs[2 * j + 1]], axis=1)
        acc = acc + jnp.dot(lhs, w_ref[j], preferred_element_type=f32)
        pe_p = jnp.concatenate([pe_ref[2 * j:2 * j + 1, :], pe_ref[2 * j + 1:2 * j + 2, :]], axis=1)
        k = CMP_STRIDE + 2 * j
        pe_q = jnp.concatenate([pe_ref[k:k + 1, :], pe_ref[k + 1:k + 2, :]], axis=1)
        bc = lambda v: jnp.broadcast_to(v, (8, 2 * LANES)).astype(bf16)
        c = c + (jnp.dot(bc(pe_p), w_ref[j, :, 0:2 * CMP_HIDDEN], preferred_element_type=f32)
                 + jnp.dot(bc(pe_q), w_ref[j, :, 2 * CMP_HIDDEN:], preferred_element_type=f32))
    hid = acc[:, 0:2 * CMP_HIDDEN] + pltpu.roll(acc[:, 2 * CMP_HIDDEN:], nch - 1, 0) + c[0:1, :]
    hid = _silu(hid).astype(bf16)
    outs = [jnp.dot(hid[:, h * CMP_HIDDEN:(h + 1) * CMP_HIDDEN], w2_ref[...], preferred_element_type=f32)
            for h in range(NSA_KV_HEADS)]
    return jnp.concatenate(outs, axis=1)


def _nsa_sample_kernel(pt_ref, *refs, pps, n_steps, tq, past_len, ns, nsp, sf_new0):
    page_refs = refs[:pps]
    (q_ref, g_ref, rn_ref, wn_ref, cw_ref, ft_ref, perm_ref, pek_ref, pev_ref, wk_ref, wv_ref, w2k_ref, w2v_ref,
     o_ref, xk_s, xv_s, kT_s, vT_s) = refs[pps:]
    del pt_ref
    step = pl.program_id(1)
    cpp = PAGE_SIZE // CMP_STRIDE
    tk, tv = [], []
    for i in range(pps):
        r0 = pl.multiple_of((step * pps + i) * PAGE_SIZE, PAGE_SIZE)
        pg = page_refs[i]
        a = pg[0, 0:2].reshape(LANES, PAGE_SIZE).astype(bf16)
        tk.append(lax.dot_general(perm_ref[...], a, NT, preferred_element_type=f32))
        a = pg[0, 2:4].reshape(LANES, PAGE_SIZE).astype(bf16)
        tv.append(lax.dot_general(perm_ref[...], a, NT, preferred_element_type=f32))
        kT_s[:, pl.ds(r0, PAGE_SIZE)] = pg[0, 4:6].reshape(LANES, PAGE_SIZE).astype(bf16)
        vT_s[:, pl.ds(r0, PAGE_SIZE)] = pg[0, 6:8].reshape(LANES, PAGE_SIZE).astype(bf16)
    c0 = pl.multiple_of(step * (pps * cpp), pps * cpp)
    for l in range(CMP_STRIDE):
        rows = slice(l * cpp, (l + 1) * cpp)
        xk_s[l, pl.ds(c0, pps * cpp), :] = jnp.concatenate([t[rows] for t in tk], axis=0).astype(bf16)
        xv_s[l, pl.ds(c0, pps * cpp), :] = jnp.concatenate([t[rows] for t in tv], axis=0).astype(bf16)

    @pl.when(step == n_steps - 1)
    def _():
        nk = kT_s.shape[1]
        rn = rn_ref[0]
        kT_s[:, past_len:] = _new_rows_t(rn[:, 2 * LANES:3 * LANES], nk - past_len).astype(bf16)
        vT_s[:, past_len:] = _new_rows_t(rn[:, 3 * LANES:4 * LANES], nk - past_len).astype(bf16)
        nwin = cw_ref.shape[2]
        wn = wn_ref[0]
        wkT = jnp.concatenate([cw_ref[0, 0:LANES, :], _new_rows_t(wn[:, 0:LANES], LANES)], axis=1).astype(bf16)
        wvT = jnp.concatenate([cw_ref[0, LANES:, :], _new_rows_t(wn[:, LANES:], LANES)], axis=1).astype(bf16)
        wlen = nwin + LANES

        nch = past_len // CMP_STRIDE
        kc = _compress_staged(xk_s, nch, pek_ref, wk_ref, w2k_ref)
        vc = _compress_staged(xv_s, nch, pev_ref, wv_ref, w2v_ref)

        pos_col = past_len + lax.broadcasted_iota(jnp.int32, (tq, 1), 0)
        gates = _sigmoid(g_ref[0])
        q_all = q_ref[0]
        wpos = (past_len - nwin) + lax.broadcasted_iota(jnp.int32, (tq, wlen), 1)
        wdist = pos_col - wpos
        wvalid = (wdist >= 0) & (wdist < WINDOW)
        wdistf = wdist.astype(f32)
        outs = []
        for kvh in range(NSA_KV_HEADS):
            qm = _padded_queries(q_all, kvh, tq)
            o_cmp, p_sum = _cmp_branch(qm, kc, vc, pos_col, kvh, tq)
            sel = _select_blocks(p_sum, pos_col, ns, nsp, tq)
            qf = _sample_query_features(sel, pos_col, kvh, tq, ns, sf_new0)
            s = (jnp.dot(qm, kT_s[...], preferred_element_type=f32)
                 + jnp.dot(qf, ft_ref[...], preferred_element_type=f32))
            e = jnp.exp(s - jnp.max(s, axis=-1, keepdims=True))
            o_slc = (lax.dot_general(e.astype(bf16), vT_s[...], NT, preferred_element_type=f32)
                     / jnp.sum(e, axis=-1, keepdims=True))
            sw = jnp.dot(qm, wkT, preferred_element_type=f32)
            pw = []
            for g in range(NSA_GROUP):
                r = slice(g * tq, (g + 1) * tq)
                sg = jnp.where(wvalid, sw[r] - SLOPES[kvh][g] * wdistf, NEG_INF)
                ew = jnp.where(wvalid, jnp.exp(sg - jnp.max(sg, axis=-1, keepdims=True)), 0.0)
                pw.append(ew / jnp.sum(ew, axis=-1, keepdims=True))
            o_win = lax.dot_general(jnp.concatenate(pw, axis=0).astype(bf16), wvT, NT, preferred_element_type=f32)
            outs.extend(_combine(gates, kvh, tq, o_cmp, o_slc, o_win))
        o_ref[0] = jnp.concatenate(outs, axis=1).astype(o_ref.dtype)


def nsa_sample(page_table, cacheT, q, gates, rows_new, win_new, cache_winT, cw, pps=8):
    B, tq, _ = q.shape
    n_pages = page_table.shape[1]
    past_len = n_pages * PAGE_SIZE
    assert n_pages % pps == 0 and past_len % SLC_BLOCK == 0
    n_steps = n_pages // pps
    nk = past_len + 512
    ns = past_len // SLC_BLOCK + 1
    nsp = -(-ns // LANES) * LANES
    nch = past_len // CMP_STRIDE
    nwin = cache_winT.shape[2]
    pek, wk, w2k, pev, wv, w2v = cw
    wk = wk.reshape(CMP_STRIDE // 2, 2 * LANES, -1)
    wv = wv.reshape(CMP_STRIDE // 2, 2 * LANES, -1)
    ft, sf_new0 = sample_key_features(past_len, nk, tq, ns)
    cpp = PAGE_SIZE // CMP_STRIDE
    ro = jnp.arange(PAGE_SIZE)
    perm = (ro[None, :] == (ro[:, None] % cpp) * CMP_STRIDE + ro[:, None] // cpp).astype(bf16)
    per_b = lambda n: pl.BlockSpec((1, tq, n), lambda b, s, pt: (b, 0, 0))
    full = lambda a: pl.BlockSpec(a.shape, lambda b, s, pt: (0,) * a.ndim)
    page_specs = [pl.BlockSpec((1, 8, HEAD_DIM, PAGE_SIZE),
                               functools.partial(lambda b, s, pt, i: (pt[b, s * pps + i], 0, 0, 0), i=i))
                  for i in range(pps)]
    grid_spec = pltpu.PrefetchScalarGridSpec(
        num_scalar_prefetch=1,
        grid=(B, n_steps),
        in_specs=page_specs + [per_b(NSA_HEADS * LANES), per_b(3 * NSA_HEADS), per_b(4 * LANES), per_b(2 * LANES),
                               pl.BlockSpec((1, 2 * LANES, nwin), lambda b, s, pt: (b, 0, 0)),
                               full(ft), full(perm), full(pek), full(pev), full(wk), full(wv), full(w2k), full(w2v)],
        out_specs=per_b(NSA_WIDTH),
        scratch_shapes=[pltpu.VMEM((CMP_STRIDE, nch, LANES), bf16), pltpu.VMEM((CMP_STRIDE, nch, LANES), bf16),
                        pltpu.VMEM((LANES, nk), bf16), pltpu.VMEM((LANES, nk), bf16)])
    return pl.pallas_call(
        functools.partial(_nsa_sample_kernel, pps=pps, n_steps=n_steps, tq=tq, past_len=past_len, ns=ns, nsp=nsp,
                          sf_new0=sf_new0),
        grid_spec=grid_spec,
        out_shape=jax.ShapeDtypeStruct((B, tq, NSA_WIDTH), bf16),
        compiler_params=_cparams("parallel", "arbitrary"),
        name="nsa_sample",
    )(page_table, *([cacheT] * pps), q, gates, rows_new, win_new, cache_winT, ft, perm, pek, pev, wk, wv, w2k, w2v)


def _gdn_prep_kernel(x_ref, halo_ref, cb_ref, cw_ref, ba_ref, al_ref, dtb_ref,
                     u_ref, w_ref, qg_ref, kg_ref, qk_ref, eg_ref, xs_s, *, tb, C, t_valid):
    i = pl.program_id(1)
    HK = GDN_HEADS * GDN_DK
    R = min(tb, LANES)
    cpb = R // C

    @pl.when(i == 0)
    def _():
        xs_s[0:8, :] = cb_ref[0]

    @pl.when(i > 0)
    def _():
        xs_s[0:8, :] = halo_ref[0]

    xs_s[8:, :] = x_ref[0]
    ba = ba_ref[0]
    ri = lax.broadcasted_iota(jnp.int32, (R, R), 0)
    ci = lax.broadcasted_iota(jnp.int32, (R, R), 1)
    same = (ri // C) == (ci // C)
    lower = same & (ri >= ci)
    strict = same & (ri > ci)
    eye = (ri == ci).astype(f32)
    rr = lax.broadcasted_iota(jnp.int32, (R, LANES), 0)
    row_t = i * tb + lax.broadcasted_iota(jnp.int32, (tb, 1), 0)
    live = row_t < t_valid

    def conv_act(c0):
        acc = cw_ref[GDN_CONV - 1:GDN_CONV, c0:c0 + LANES] * xs_s[pl.ds(8, tb), c0:c0 + LANES]
        for k in range(1, GDN_CONV):
            acc = acc + cw_ref[GDN_CONV - 1 - k:GDN_CONV - k, c0:c0 + LANES] * xs_s[pl.ds(8 - k, tb), c0:c0 + LANES]
        return _silu(acc)

    chains = []
    for h in range(GDN_HEADS):
        q = conv_act(h * GDN_DK)
        k = conv_act(HK + h * GDN_DK)
        v = conv_act(2 * HK + h * GDN_DV)
        q = q * lax.rsqrt(jnp.sum(q * q, axis=-1, keepdims=True) + L2_EPS) * (GDN_DK ** -0.5)
        k = k * lax.rsqrt(jnp.sum(k * k, axis=-1, keepdims=True) + L2_EPS)
        beta = _sigmoid(ba[:, h:h + 1])
        ar = ba[:, GDN_HEADS + h:GDN_HEADS + h + 1] + dtb_ref[:, h:h + 1]
        softplus = jnp.maximum(ar, 0.0) + jnp.log(1.0 + jnp.exp(-jnp.abs(ar)))
        gcol = -jnp.exp(al_ref[:, h:h + 1]) * softplus
        if t_valid < 10 ** 9:
            beta = jnp.where(live, beta, 0.0)
            gcol = jnp.where(live, gcol, 0.0)
            k = jnp.where(live, k, 0.0)
            v = jnp.where(live, v, 0.0)
        for blk in range(tb // R):
            r = slice(blk * R, (blk + 1) * R)
            qc, kc, vc, bc = q[r], k[r], v[r], beta[r]
            gc = jnp.broadcast_to(gcol[r], (R, LANES))
            sh = 1
            while sh < C:
                gc = gc + jnp.where((rr % C) >= sh, pltpu.roll(gc, sh, 0), 0.0)
                sh *= 2
            gct = gc.T
            decay = jnp.exp(jnp.where(lower, gc[:, 0:R] - gct[0:R, :], NEG_INF))
            kb = kc * bc
            a = jnp.where(strict, _dot_hp(kb, kc, NT) * decay, 0.0)
            qk = jnp.where(lower, lax.dot_general(qc.astype(bf16), kc.astype(bf16), NT,
                                                  preferred_element_type=f32) * decay, 0.0)
            gl = gc[R - 1:R, :]
            for j in range(cpb - 2, -1, -1):
                gl = jnp.where(rr < (j + 1) * C, gc[(j + 1) * C - 1:(j + 1) * C, :], gl)
            chains.append(dict(h=h, r=r, blk=blk, qc=qc, kc=kc, gc=gc, gl=gl, qk=qk, n=-a, t=eye - a,
                               rhs=jnp.concatenate([vc * bc, kb * jnp.exp(gc)], axis=1)))

    for ch in chains:
        ch['n'] = _split2(ch['n'])
    sh = 2
    while sh < C:
        for ch in chains:
            ch['n'] = _split2(_dot_parts(ch['n'], ch['n']))
        for ch in chains:
            ch['t'] = ch['t'] + _dot_parts(_split2(ch['t']), ch['n'])
        sh *= 2

    for ch in chains:
        h, r, gc, gl = ch['h'], ch['r'], ch['gc'], ch['gl']
        sol = ch['rhs'] + _dot_hp(ch['t'] - eye, ch['rhs'])
        cs = slice(h * GDN_DV, (h + 1) * GDN_DV)
        u_ref[0, r, cs] = sol[:, 0:GDN_DV]
        w_ref[0, r, cs] = sol[:, GDN_DV:].astype(w_ref.dtype)
        qg_ref[0, r, cs] = (ch['qc'] * jnp.exp(gc)).astype(qg_ref.dtype)
        kg_ref[0, r, cs] = (ch['kc'] * jnp.exp(gl - gc)).astype(kg_ref.dtype)
        qk = ch['qk'][:, 0:C]
        for j in range(1, cpb):
            qk = jnp.where(ri[:, 0:C] >= j * C, ch['qk'][:, j * C:(j + 1) * C], qk)
        qk_ref[0, r, h * C:(h + 1) * C] = qk.astype(qk_ref.dtype)
        for j in range(cpb):
            eg_ref[0, ch['blk'] * cpb + j, h:h + 1, :] = jnp.exp(gc[(j + 1) * C - 1:(j + 1) * C, :])
    eg_ref[0, :, GDN_HEADS:, :] = jnp.zeros((tb // C, 8 - GDN_HEADS, LANES), f32)


def gdn_prep(qkv, conv_buf8, conv_w, ba, a_log, dt_bias, C, tb, t_valid):
    B, T, CH = qkv.shape
    nb = T // tb
    hb = tb // 8
    per = lambda n, dt: (pl.BlockSpec((1, tb, n), lambda b, i: (b, i, 0)), jax.ShapeDtypeStruct((B, T, n), dt))
    outs = [per(GDN_WIDTH, f32), per(GDN_WIDTH, bf16), per(GDN_WIDTH, bf16), per(GDN_WIDTH, bf16),
            per(GDN_HEADS * C, bf16),
            (pl.BlockSpec((1, tb // C, 8, LANES), lambda b, i: (b, i, 0, 0)),
             jax.ShapeDtypeStruct((B, T // C, 8, LANES), f32))]
    return pl.pallas_call(
        functools.partial(_gdn_prep_kernel, tb=tb, C=C, t_valid=t_valid),
        grid=(B, nb),
        in_specs=[pl.BlockSpec((1, tb, CH), lambda b, i: (b, i, 0)),
                  pl.BlockSpec((1, 8, CH), lambda b, i: (b, jnp.maximum(i * hb - 1, 0), 0)),
                  pl.BlockSpec((1, 8, CH), lambda b, i: (b, 0, 0)),
                  pl.BlockSpec((GDN_CONV, CH), lambda b, i: (0, 0)),
                  pl.BlockSpec((1, tb, 2 * GDN_HEADS), lambda b, i: (b, i, 0)),
                  pl.BlockSpec((1, GDN_HEADS), lambda b, i: (0, 0)),
                  pl.BlockSpec((1, GDN_HEADS), lambda b, i: (0, 0))],
        out_specs=[o[0] for o in outs],
        out_shape=[o[1] for o in outs],
        scratch_shapes=[pltpu.VMEM((tb + 8, CH), f32)],
        compiler_params=_cparams("parallel", "arbitrary"),
        name="gdn_prep",
    )(qkv, qkv, conv_buf8, conv_w, ba, a_log.reshape(1, GDN_HEADS), dt_bias.reshape(1, GDN_HEADS))


def _gdn_scan_kernel(u_ref, w_ref, qg_ref, kg_ref, qk_ref, eg_ref, z_ref, ng_ref, s0_ref,
                     o_ref, sT_ref, s_s, *, C, nb):
    c = pl.program_id(1)

    @pl.when(c == 0)
    def _():
        s_s[...] = s0_ref[...]

    ch = [(b, h) for b in range(nb) for h in range(GDN_HEADS)]
    cs = [slice(h * GDN_DV, (h + 1) * GDN_DV) for h in range(GDN_HEADS)]
    S = [s_s[b, h] for b, h in ch]
    Sb = [x.astype(bf16) for x in S]
    v_new = [u_ref[b, :, cs[h]] - jnp.dot(w_ref[b, :, cs[h]], Sb[i], preferred_element_type=f32)
             for i, (b, h) in enumerate(ch)]
    vb = [x.astype(bf16) for x in v_new]
    for i, (b, h) in enumerate(ch):
        s_s[b, h] = S[i] * eg_ref[b, 0, h:h + 1, :] + lax.dot_general(kg_ref[b, :, cs[h]], vb[i], TN,
                                                                       preferred_element_type=f32)
    o = [jnp.dot(qg_ref[b, :, cs[h]], Sb[i], preferred_element_type=f32)
         + jnp.dot(qk_ref[b, :, h * C:(h + 1) * C], vb[i], preferred_element_type=f32) for i, (b, h) in enumerate(ch)]
    outs = [_rms(o[i], ng_ref[...]) * _silu(z_ref[b, :, cs[h]]) for i, (b, h) in enumerate(ch)]
    for b in range(nb):
        o_ref[b] = jnp.concatenate(outs[b * GDN_HEADS:(b + 1) * GDN_HEADS], axis=1).astype(o_ref.dtype)

    @pl.when(c == pl.num_programs(1) - 1)
    def _():
        sT_ref[...] = s_s[...]


def gdn_scan(u, w, qg, kg, qk, eg, z, norm_g, s0, C, nb=8):
    B, T, _ = u.shape
    nb = math.gcd(nb, B)
    per = lambda n: pl.BlockSpec((nb, C, n), lambda b, c: (b, c, 0))
    st = pl.BlockSpec((nb, GDN_HEADS, GDN_DK, GDN_DV), lambda b, c: (b, 0, 0, 0))
    return pl.pallas_call(
        functools.partial(_gdn_scan_kernel, C=C, nb=nb),
        grid=(B // nb, T // C),
        in_specs=[per(GDN_WIDTH), per(GDN_WIDTH), per(GDN_WIDTH), per(GDN_WIDTH), per(GDN_HEADS * C),
                  pl.BlockSpec((nb, 1, 8, LANES), lambda b, c: (b, c, 0, 0)),
                  per(GDN_WIDTH), pl.BlockSpec((1, GDN_DV), lambda b, c: (0, 0)), st],
        out_specs=[per(GDN_WIDTH), st],
        out_shape=[jax.ShapeDtypeStruct((B, T, GDN_WIDTH), bf16),
                   jax.ShapeDtypeStruct((B, GDN_HEADS, GDN_DK, GDN_DV), f32)],
        scratch_shapes=[pltpu.VMEM((nb, GDN_HEADS, GDN_DK, GDN_DV), f32)],
        compiler_params=_cparams("parallel", "arbitrary"),
        name="gdn_scan",
    )(u, w, qg, kg, qk, eg, z, norm_g.reshape(1, GDN_DV), s0)


def _nsa_in_groups(w, feature_major):
    c1 = NSA_WIDTH
    c2 = c1 + 4 * NSA_KV_COLS
    c3 = c2 + 2 * NSA_KV_COLS
    c4 = c3 + 3 * NSA_HEADS
    wb = w.astype(bf16)
    xa_scale = XA_DIM ** -0.5 * LOG2E
    if feature_major:
        k0 = c1 + 2 * NSA_KV_COLS
        return [(wb[:, c1:c2].T, f32, 1.0, FEATURE_MAJOR_PER_BATCH),
                (wb[:, c1:k0], f32, 1.0, TOKEN_MAJOR),
                (wb[:, c2:c3].T, f32, 1.0, FEATURE_MAJOR_PER_BATCH),
                (wb[:, c4:].T, bf16, xa_scale, FEATURE_MAJOR),
                (wb[:, :c1].T, bf16, HEAD_DIM ** -0.5 * LOG2E, FEATURE_MAJOR),
                (wb[:, c3:c4].T, f32, 1.0, FEATURE_MAJOR),
                (wb[:, k0:k0 + NSA_KV_COLS], bf16, 1.0, TOKEN_MAJOR),
                (wb[:, k0 + NSA_KV_COLS:c2].T, bf16, 1.0, FEATURE_MAJOR),
                (wb[:, c2:c2 + NSA_KV_COLS], bf16, 1.0, TOKEN_MAJOR),
                (wb[:, c2 + NSA_KV_COLS:c3].T, bf16, 1.0, FEATURE_MAJOR)]
    wq = wb[:, :c1].reshape(-1, NSA_KV_HEADS, NSA_GROUP, HEAD_DIM)
    zq = jnp.zeros_like(wq[:, 0])
    wq = jnp.concatenate([jnp.concatenate([wq[:, 0], zq], axis=-1), jnp.concatenate([zq, wq[:, 1]], axis=-1)], axis=1)
    wq = wq.reshape(-1, NSA_HEADS * LANES)
    return [(wb[:, c1:c2], f32, 1.0, TOKEN_MAJOR), (wb[:, c2:c3], f32, 1.0, TOKEN_MAJOR),
            (wb[:, c4:], bf16, xa_scale, TOKEN_MAJOR), (wq, bf16, HEAD_DIM ** -0.5, TOKEN_MAJOR),
            (wb[:, c3:c4], f32, 1.0, TOKEN_MAJOR)]


def _gdn_in_groups(w, feature_major):
    c1 = GDN_CONV_CH
    c2 = c1 + GDN_WIDTH
    c3 = c2 + 2 * GDN_HEADS
    wb = w.astype(bf16)
    xq = ((wb[:, c3:].T, bf16, XA_DIM ** -0.5 * LOG2E, FEATURE_MAJOR) if feature_major
          else (wb[:, c3:], bf16, XA_DIM ** -0.5 * LOG2E, TOKEN_MAJOR))
    return [(wb[:, :c1], f32, 1.0, TOKEN_MAJOR), (wb[:, c1:c2], f32, 1.0, TOKEN_MAJOR),
            (wb[:, c2:c3], f32, 1.0, TOKEN_MAJOR), xq]


def _trunk(x, mem_kv, nsa_past, gdn_state, p, t_valid):
    B, T, D = x.shape
    N = B * T
    xf = x.reshape(N, D)

    prompt = nsa_past is None
    if prompt:
        kv_rows, rows_cmp, win_rows, xq, qT, gT, sk, svT, wk, wvT = norm_proj(
            xf, p['norm_mix_g'][0], p['nsa_in_prompt'], tm=512, rows_per_batch=T)
        kc, vcT = cmp_prompt(rows_cmp.reshape(B, T, -1), p['cmp'])
        o_mix = nsa_prompt(qT, gT, sk, svT, wk, wvT, kc, vcT, B)
    else:
        rows, win, xq, q, gates = norm_proj(xf, p['norm_mix_g'][0], p['nsa_in_sample'])
        kv_rows = rows.reshape(B, T, -1)
        win_rows = win.reshape(B, T, -1)
        xq = xq.reshape(B, T, -1)
        page_table, cache, cache_win = nsa_past
        o_mix = nsa_sample(page_table, cache, q.reshape(B, T, -1), gates.reshape(B, T, -1), kv_rows, win_rows,
                           cache_win, p['cmp'])
    o_mem = cross_attend(xq, mem_kv, 0, B)
    xf = post_block(xf, o_mix.reshape(N, -1), o_mem.reshape(N, -1), p['w_out'][0], p['norm_ffn_g'][0],
                    p['w_up'][0], p['w_down'][0], p['final_norm_g'], final=False)

    qkv, z, ba, xq = norm_proj(xf, p['norm_mix_g'][1], p['gdn_in_prompt' if prompt else 'gdn_in_sample'], tm=512)
    if not prompt:
        xq = xq.reshape(B, T, -1)
    qkv3 = qkv.reshape(B, T, -1)
    S0, conv_buf8 = gdn_state
    C = min(64, T)
    tb = min(256, T)
    u, w, qg, kg, qk, eg = gdn_prep(qkv3, conv_buf8, p['gdn_conv_w'], ba.reshape(B, T, -1), p['gdn_a_log'],
                                    p['gdn_dt_bias'], C, tb, t_valid)
    o_mix, S = gdn_scan(u, w, qg, kg, qk, eg, z.reshape(B, T, -1), p['gdn_norm_g'], S0, C)
    o_mem = cross_attend(xq, mem_kv, 1, B)
    y = post_block(xf, o_mix.reshape(N, -1), o_mem.reshape(N, -1), p['w_out'][1], p['norm_ffn_g'][1],
                   p['w_up'][1], p['w_down'][1], p['final_norm_g'], final=True)
    return y.reshape(B, T, D), kv_rows, win_rows, S, qkv3


def kernel(x_prompt, x_sample, mem_prompt, cache_nsa_kv, cache_nsa_win, state_gdn_s, state_gdn_conv,
           cache_mem_kv, page_table, norm_mix_g, norm_mem_g, w_mem_kv, nsa_w_in, cmp_pe_k, cmp_w1_k,
           cmp_w2_k, cmp_pe_v, cmp_w1_v, cmp_w2_v, gdn_w_in, gdn_conv_w, gdn_a_log, gdn_dt_bias,
           gdn_norm_g, w_out, norm_ffn_g, w_up, w_down, final_norm_g):
    B, T, D = x_prompt.shape
    DB, TS, _ = x_sample.shape
    M = mem_prompt.shape[1]
    p = dict(norm_mix_g=norm_mix_g, nsa_in_prompt=_nsa_in_groups(nsa_w_in[0], True),
             nsa_in_sample=_nsa_in_groups(nsa_w_in[0], False), gdn_in_prompt=_gdn_in_groups(gdn_w_in[0], True),
             gdn_in_sample=_gdn_in_groups(gdn_w_in[0], False),
             cmp=_cmp_weights(cmp_pe_k[0], cmp_w1_k[0], cmp_w2_k[0]) + _cmp_weights(cmp_pe_v[0], cmp_w1_v[0], cmp_w2_v[0]),
             gdn_conv_w=gdn_conv_w[0], gdn_a_log=gdn_a_log[0], gdn_dt_bias=gdn_dt_bias[0], gdn_norm_g=gdn_norm_g[0],
             w_out=w_out.astype(bf16), norm_ffn_g=norm_ffn_g, w_up=w_up.astype(bf16), w_down=w_down.astype(bf16),
             final_norm_g=final_norm_g)

    mem_flat = mem_prompt.reshape(B * M, D)
    mem_kv = [norm_proj(mem_flat, norm_mem_g[i], [(w_mem_kv[i].astype(bf16).T, f32, 1.0, FEATURE_MAJOR_PER_BATCH)],
                        tm=M, rows_per_batch=M)[0] for i in range(DEPTH)]
    mem_kv_p = jnp.stack(mem_kv, axis=1)
    zero_state = (jnp.zeros((B, GDN_HEADS, GDN_DK, GDN_DV), f32), jnp.zeros((B, 8, GDN_CONV_CH), f32))
    y_p, rows_p, win_p, S_p, qkv_p = _trunk(x_prompt, mem_kv_p, None, zero_state, p, 10 ** 9)

    TP = 8
    x_s = jnp.pad(x_sample, ((0, 0), (0, TP - TS), (0, 0)))
    cache = cache_nsa_kv.transpose(0, 2, 3, 4, 5, 1).reshape(cache_nsa_kv.shape[0], -1, HEAD_DIM, PAGE_SIZE)
    nwin = cache_nsa_win.shape[2]
    cache_win = cache_nsa_win.transpose(0, 1, 3, 4, 5, 2).reshape(DB, -1, nwin)
    conv8 = jnp.pad(state_gdn_conv[:, 0], ((0, 0), (8 - (GDN_CONV - 1), 0), (0, 0)))
    mem_kv_s = cache_mem_kv.transpose(0, 1, 3, 4, 5, 2).reshape(DB, DEPTH, 2 * XA_WIDTH, M)
    y_s, rows_s, win_s, S_s, qkv_s = _trunk(x_s, mem_kv_s, (page_table, cache, cache_win),
                                            (state_gdn_s[:, 0], conv8), p, TS)

    kvshape = (4, NSA_KV_HEADS, HEAD_DIM)
    nsa_kv_prompt = rows_p.reshape(B, 1, *kvshape, T).transpose(0, 5, 1, 2, 3, 4)
    nsa_kv_sample = rows_s[:, :TS].reshape(DB, TS, 1, *kvshape)
    wshape = (2, NSA_KV_HEADS, HEAD_DIM)
    wkeep = min(WINDOW, T)
    nsa_win_prompt = win_p[:, :, T - wkeep:].reshape(B, 1, *wshape, wkeep).transpose(0, 1, 5, 2, 3, 4)
    win_cat = jnp.concatenate([cache_win, win_s[:, :TS].transpose(0, 2, 1)], axis=2)[:, :, -WINDOW:]
    nsa_win_sample = win_cat.reshape(DB, 1, *wshape, -1).transpose(0, 1, 5, 2, 3, 4)
    gdn_conv_prompt = qkv_p[:, None, T - (GDN_CONV - 1):]
    conv_cat = jnp.concatenate([state_gdn_conv[:, 0], qkv_s[:, :TS]], axis=1)
    gdn_conv_sample = conv_cat[:, None, -(GDN_CONV - 1):]
    mem_kv_prompt = mem_kv_p.reshape(B, DEPTH, 2, XA_HEADS, XA_DIM, M).transpose(0, 1, 5, 2, 3, 4)
    return (y_p, y_s[:, :TS], nsa_kv_prompt, nsa_kv_sample, nsa_win_prompt, nsa_win_sample,
            S_p[:, None], S_s[:, None], gdn_conv_prompt, gdn_conv_sample, mem_kv_prompt)
```

```python
import functools
import math

import jax
import jax.numpy as jnp
from jax import lax
from jax.experimental import pallas as pl
from jax.experimental.pallas import tpu as pltpu

f32 = jnp.float32
bf16 = jnp.bfloat16

D_MODEL = 1024
DEPTH = 2
PAGE_SIZE = 128
HEAD_DIM = 64
NSA_HEADS = 12
NSA_KV_HEADS = 2
NSA_GROUP = NSA_HEADS // NSA_KV_HEADS
NSA_WIDTH = NSA_HEADS * HEAD_DIM
NSA_KV_COLS = NSA_KV_HEADS * HEAD_DIM
CMP_STRIDE = 16
CMP_BLOCK = 2 * CMP_STRIDE
CMP_HIDDEN = 128
SLC_BLOCK = 64
SLC_TOPN = 16
WINDOW = 512
GDN_HEADS = 6
GDN_DK = 128
GDN_DV = 128
GDN_WIDTH = GDN_HEADS * GDN_DV
GDN_CONV = 4
GDN_CONV_CH = 2 * GDN_HEADS * GDN_DK + GDN_HEADS * GDN_DV
XA_HEADS = 4
XA_DIM = 64
XA_WIDTH = XA_HEADS * XA_DIM
D_FF = 4 * D_MODEL
NORM_EPS = 1e-6
L2_EPS = 1e-6
NEG_INF = -1e30
FORCED = 1e9

VMEM_LIMIT = 52 * 1024 * 1024
LANES = 128

SLOPES = [[2.0 ** (-8.0 * (k * NSA_GROUP + g + 1) / NSA_HEADS) for g in range(NSA_GROUP)]
          for k in range(NSA_KV_HEADS)]

NT = (((1,), (1,)), ((), ()))
TN = (((0,), (0,)), ((), ()))


def _cparams(*sem):
    return pltpu.CompilerParams(dimension_semantics=sem, vmem_limit_bytes=VMEM_LIMIT)


def _rms(x, g):
    r = lax.rsqrt(jnp.mean(x * x, axis=-1, keepdims=True) + NORM_EPS)
    return (x * r) * g


def _sigmoid(x):
    return 1.0 / (1.0 + jnp.exp(-x))


def _silu(x):
    return x * _sigmoid(x)


def _split3(a):
    hi = a.astype(bf16)
    r1 = a - hi.astype(f32)
    mid = r1.astype(bf16)
    lo = (r1 - mid.astype(f32)).astype(bf16)
    return hi, mid, lo


def _dot_exact_rhs(a, b_bf16):
    hi, mid, lo = _split3(a)
    d = lambda x: jnp.dot(x, b_bf16, preferred_element_type=f32)
    return d(hi) + d(mid) + d(lo)


def _split2(a):
    hi = a.astype(bf16)
    return hi, (a - hi.astype(f32)).astype(bf16)


def _dot_parts(a, b, dims=None):
    if dims is None:
        d = lambda x, y: jnp.dot(x, y, preferred_element_type=f32)
    else:
        d = lambda x, y: lax.dot_general(x, y, dims, preferred_element_type=f32)
    return d(a[0], b[0]) + (d(a[0], b[1]) + d(a[1], b[0]))


def _dot_hp(a, b, dims=None):
    return _dot_parts(_split2(a), _split2(b), dims)


TOKEN_MAJOR, FEATURE_MAJOR, FEATURE_MAJOR_PER_BATCH = 'nt', 'fm', 'fmb'


def _norm_proj_kernel(x_ref, g_ref, *refs, specs):
    n = len(specs)
    w_refs, o_refs = refs[:n], refs[n:]
    h = _rms(x_ref[...], g_ref[...]).astype(bf16)
    for (scale, kind), w_ref, o_ref in zip(specs, w_refs, o_refs):
        if kind == TOKEN_MAJOR:
            y = jnp.dot(h, w_ref[...], preferred_element_type=f32)
        else:
            y = lax.dot_general(w_ref[...], h, NT, preferred_element_type=f32)
        if scale != 1.0:
            y = y * scale
        o_ref[...] = y.astype(o_ref.dtype).reshape(o_ref.shape)


def norm_proj(x, g, groups, tm=256, rows_per_batch=None):
    N, D = x.shape
    tm = min(tm, N)
    assert N % tm == 0
    in_specs = [pl.BlockSpec((tm, D), lambda i: (i, 0)), pl.BlockSpec((1, D), lambda i: (0, 0))]
    out_specs, out_shape = [], []
    for w, dt, _, kind in groups:
        in_specs.append(pl.BlockSpec(w.shape, lambda i: (0, 0)))
        if kind == TOKEN_MAJOR:
            n = w.shape[1]
            out_specs.append(pl.BlockSpec((tm, n), lambda i: (i, 0)))
            out_shape.append(jax.ShapeDtypeStruct((N, n), dt))
        elif kind == FEATURE_MAJOR:
            n = w.shape[0]
            out_specs.append(pl.BlockSpec((n, tm), lambda i: (0, i)))
            out_shape.append(jax.ShapeDtypeStruct((n, N), dt))
        else:
            n = w.shape[0]
            per = rows_per_batch // tm
            assert rows_per_batch % tm == 0
            out_specs.append(pl.BlockSpec((1, n, tm), lambda i: (i // per, 0, i % per)))
            out_shape.append(jax.ShapeDtypeStruct((N // rows_per_batch, n, rows_per_batch), dt))
    return pl.pallas_call(
        functools.partial(_norm_proj_kernel, specs=tuple((s, t) for _, _, s, t in groups)),
        grid=(N // tm,),
        in_specs=in_specs,
        out_specs=out_specs,
        out_shape=out_shape,
        compiler_params=_cparams("parallel"),
        name="norm_proj",
    )(x, g.reshape(1, D), *[w for w, _, _, _ in groups])


def _xattn_kernel(q_ref, kv_ref, o_ref, *, q_feature_major, nb):
    ch = [(b, h) for b in range(nb) for h in range(XA_HEADS)]
    hs = lambda h, off=0: slice(off + h * XA_DIM, off + (h + 1) * XA_DIM)
    qT = [q_ref[...] if q_feature_major else q_ref[b].astype(f32).T.astype(bf16) for b in range(nb)]
    s = [lax.dot_general(kv_ref[b, 0, hs(h), :].astype(bf16), qT[b][hs(h)], TN, preferred_element_type=f32)
         for b, h in ch]
    e = [jnp.exp2(x - jnp.max(x, axis=0, keepdims=True)) for x in s]
    o = [jnp.dot(kv_ref[b, 0, hs(h, XA_WIDTH), :].astype(bf16), e[i].astype(bf16), preferred_element_type=f32)
         / jnp.sum(e[i], axis=0, keepdims=True) for i, (b, h) in enumerate(ch)]
    for b in range(nb):
        o_ref[b] = jnp.concatenate(o[b * XA_HEADS:(b + 1) * XA_HEADS], axis=0).T.astype(o_ref.dtype)


def cross_attend(xq, kvT, layer, B, tq=512, nb=4):
    q_feature_major = xq.ndim == 2
    T = xq.shape[1] // B if q_feature_major else xq.shape[1]
    M = kvT.shape[3]
    tq = min(tq, T)
    nq = T // tq
    if q_feature_major:
        nb = 1
        q_spec = pl.BlockSpec((XA_WIDTH, tq), lambda b, i: (0, b * nq + i))
    else:
        nb = math.gcd(nb, B)
        q_spec = pl.BlockSpec((nb, tq, XA_WIDTH), lambda b, i: (b, i, 0))
    return pl.pallas_call(
        functools.partial(_xattn_kernel, q_feature_major=q_feature_major, nb=nb),
        grid=(B // nb, nq),
        in_specs=[q_spec, pl.BlockSpec((nb, 1, 2 * XA_WIDTH, M), lambda b, i: (b, layer, 0, 0))],
        out_specs=pl.BlockSpec((nb, tq, XA_WIDTH), lambda b, i: (b, i, 0)),
        out_shape=jax.ShapeDtypeStruct((B, T, XA_WIDTH), bf16),
        compiler_params=_cparams("parallel", "parallel"),
        name="cross_attend",
    )(xq, kvT)


def _post_kernel(x_ref, om_ref, oc_ref, wo_ref, gf_ref, wu_ref, wd_ref, gl_ref, o_ref,
                 x1_s, h_s, acc_s, *, mix_w, final):
    j = pl.program_id(1)

    @pl.when(j == 0)
    def _():
        x1 = x_ref[...] + (jnp.dot(om_ref[...], wo_ref[0:mix_w, :], preferred_element_type=f32)
                           + jnp.dot(oc_ref[...], wo_ref[mix_w:, :], preferred_element_type=f32))
        x1_s[...] = x1
        h_s[...] = _rms(x1, gf_ref[...]).astype(bf16)
        acc_s[...] = jnp.zeros_like(acc_s)

    u = jnp.dot(h_s[...], wu_ref[...], preferred_element_type=f32)
    u = jnp.square(jnp.maximum(u, 0.0)).astype(bf16)
    acc_s[...] += jnp.dot(u, wd_ref[...], preferred_element_type=f32)

    @pl.when(j == pl.num_programs(1) - 1)
    def _():
        x2 = x1_s[...] + acc_s[...]
        if final:
            x2 = _rms(x2, gl_ref[...])
        o_ref[...] = x2


def post_block(x, o_mix, o_mem, w_out, g_ffn, w_up, w_down, g_final, final, tm=1024, tf=512):
    N, D = x.shape
    mix_w = o_mix.shape[1]
    F = w_up.shape[1]
    tm = min(tm, N)
    return pl.pallas_call(
        functools.partial(_post_kernel, mix_w=mix_w, final=final),
        grid=(N // tm, F // tf),
        in_specs=[pl.BlockSpec((tm, D), lambda i, j: (i, 0)),
                  pl.BlockSpec((tm, mix_w), lambda i, j: (i, 0)),
                  pl.BlockSpec((tm, XA_WIDTH), lambda i, j: (i, 0)),
                  pl.BlockSpec((mix_w + XA_WIDTH, D), lambda i, j: (0, 0)),
                  pl.BlockSpec((1, D), lambda i, j: (0, 0)),
                  pl.BlockSpec((D, tf), lambda i, j: (0, j)),
                  pl.BlockSpec((tf, D), lambda i, j: (j, 0)),
                  pl.BlockSpec((1, D), lambda i, j: (0, 0))],
        out_specs=pl.BlockSpec((tm, D), lambda i, j: (i, 0)),
        out_shape=jax.ShapeDtypeStruct((N, D), f32),
        scratch_shapes=[pltpu.VMEM((tm, D), f32), pltpu.VMEM((tm, D), bf16), pltpu.VMEM((tm, D), f32)],
        compiler_params=_cparams("parallel", "arbitrary"),
        name="post_block",
    )(x, o_mix, o_mem, w_out, g_ffn.reshape(1, D), w_up, w_down, g_final.reshape(1, D))


def _compress(load_chunk_rows, nch, pe_ref, w_ref, w2_ref):
    acc_p = jnp.zeros((nch, 2 * CMP_HIDDEN), f32)
    acc_q = jnp.zeros((nch, 2 * CMP_HIDDEN), f32)
    for l in range(CMP_STRIDE):
        x = load_chunk_rows(l)
        xp = (x + pe_ref[l:l + 1, :]).astype(bf16)
        xq = (x + pe_ref[CMP_STRIDE + l:CMP_STRIDE + l + 1, :]).astype(bf16)
        acc_p = acc_p + jnp.dot(xp, w_ref[l, :, 0:2 * CMP_HIDDEN], preferred_element_type=f32)
        acc_q = acc_q + jnp.dot(xq, w_ref[l, :, 2 * CMP_HIDDEN:], preferred_element_type=f32)
    hid = _silu(acc_p + pltpu.roll(acc_q, nch - 1, 0)).astype(bf16)
    outs = [jnp.dot(hid[:, h * CMP_HIDDEN:(h + 1) * CMP_HIDDEN], w2_ref[...], preferred_element_type=f32)
            for h in range(NSA_KV_HEADS)]
    return jnp.concatenate(outs, axis=1)


def _cmp_prompt_kernel(xk_ref, xv_ref, pek_ref, pev_ref, wk_ref, wv_ref, w2k_ref, w2v_ref, kc_ref, vc_ref, *, nch):
    kc_ref[0] = _compress(lambda l: xk_ref[0, pl.ds(l, nch, stride=CMP_STRIDE), :], nch, pek_ref, wk_ref, w2k_ref)
    vc_ref[0] = _compress(lambda l: xv_ref[0, pl.ds(l, nch, stride=CMP_STRIDE), :], nch, pev_ref, wv_ref, w2v_ref).T


def _cmp_weights(pe, w1, w2):
    pe2 = jnp.tile(pe, (1, NSA_KV_HEADS))
    z = jnp.zeros((CMP_STRIDE, HEAD_DIM, CMP_HIDDEN), w1.dtype)
    a, b = w1[:CMP_STRIDE], w1[CMP_STRIDE:]
    top = jnp.concatenate([a, z, b, z], axis=2)
    bot = jnp.concatenate([z, a, z, b], axis=2)
    wbd = jnp.concatenate([top, bot], axis=1).astype(bf16)
    return pe2, wbd, w2.astype(bf16)


def cmp_prompt(rows, cw):
    B, T, _ = rows.shape
    nch = T // CMP_STRIDE
    pek, wk, w2k, pev, wv, w2v = cw
    full = lambda a: pl.BlockSpec(a.shape, lambda b: (0,) * a.ndim)
    return pl.pallas_call(
        functools.partial(_cmp_prompt_kernel, nch=nch),
        grid=(B,),
        in_specs=[pl.BlockSpec((1, T, LANES), lambda b: (b, 0, 0)),
                  pl.BlockSpec((1, T, LANES), lambda b: (b, 0, 1)),
                  full(pek), full(pev), full(wk), full(wv), full(w2k), full(w2v)],
        out_specs=[pl.BlockSpec((1, nch, LANES), lambda b: (b, 0, 0)), pl.BlockSpec((1, LANES, nch), lambda b: (b, 0, 0))],
        out_shape=[jax.ShapeDtypeStruct((B, nch, LANES), f32), jax.ShapeDtypeStruct((B, LANES, nch), f32)],
        compiler_params=_cparams("parallel"),
        name="cmp_prompt",
    )(rows, rows, pek, pev, wk, wv, w2k, w2v)


def _padded_queries(q_all, kvh, tq):
    del tq
    lo = kvh * NSA_GROUP
    return jnp.concatenate([q_all[:, (lo + g) * LANES:(lo + g + 1) * LANES] for g in range(NSA_GROUP)], axis=0)


def _cmp_branch(qm, kc, vc, pos_col, kvh, tq):
    ncp = kc.shape[0]
    s = lax.dot_general(qm, kc.astype(bf16), NT, preferred_element_type=f32)
    n_idx = lax.broadcasted_iota(jnp.int32, (tq, ncp), 1)
    cdist = pos_col - (n_idx * CMP_STRIDE + (CMP_BLOCK - 1))
    cvis = cdist >= 0
    any_vis = pos_col >= (CMP_BLOCK - 1)
    cdf = cdist.astype(f32)
    p_sum = jnp.zeros((tq, ncp), f32)
    ps = []
    for g in range(NSA_GROUP):
        sg = jnp.where(cvis, s[g * tq:(g + 1) * tq] - SLOPES[kvh][g] * cdf, NEG_INF)
        m = jnp.max(sg, axis=-1, keepdims=True)
        e = jnp.exp(sg - m)
        p = e / jnp.sum(e, axis=-1, keepdims=True)
        p = jnp.where(any_vis, p, 0.0)
        p_sum = p_sum + p
        ps.append(p)
    p_all = jnp.concatenate(ps, axis=0).astype(bf16)
    o = jnp.dot(p_all, vc.astype(bf16), preferred_element_type=f32)
    return o, p_sum


def _select_blocks(p_sum, pos_col, ns, nsp, tq):
    ncp = p_sum.shape[1]
    ci = lax.broadcasted_iota(jnp.int32, (ncp, nsp), 0) * CMP_STRIDE
    cj = lax.broadcasted_iota(jnp.int32, (ncp, nsp), 1)
    c2s = ((ci < (cj + 1) * SLC_BLOCK) & (ci + CMP_BLOCK > cj * SLC_BLOCK)).astype(f32).astype(bf16)
    imp = _dot_exact_rhs(p_sum, c2s)
    blk = lax.broadcasted_iota(jnp.int32, (tq, nsp), 1)
    cur = pos_col // SLC_BLOCK
    forced = (blk == 0) | (blk == cur) | (blk == cur - 1)
    imp = jnp.where(forced, FORCED, jnp.where(blk <= cur, imp, NEG_INF))
    rank = jnp.zeros((tq, nsp), jnp.int32)
    for jp in range(ns):
        col = imp[:, jp:jp + 1]
        beats = (col > imp) | ((col == imp) & (blk > jp))
        rank = rank + beats.astype(jnp.int32)
    return ((rank < SLC_TOPN) & (blk <= cur)).astype(f32)


def _combine(gates, kvh, tq, o_cmp, o_slc, o_win):
    lo = kvh * HEAD_DIM
    outs = []
    for g in range(NSA_GROUP):
        h = kvh * NSA_GROUP + g
        r = slice(g * tq, (g + 1) * tq)
        o = (gates[:, 3 * h:3 * h + 1] * o_cmp[r] + gates[:, 3 * h + 1:3 * h + 2] * o_slc[r]
             + gates[:, 3 * h + 2:3 * h + 3] * o_win[r])
        outs.append(o[:, lo:lo + HEAD_DIM])
    return outs


LOG2E = 1.4426950408889634
SLOPES2 = [[v * LOG2E for v in row] for row in SLOPES]

POS_SPLIT = 8
F_SEL0 = 16
MASK_BIG = 1e30


def position_features(pos, blk, ns):
    n = pos.shape[0]
    lane = jnp.arange(HEAD_DIM)[None, :]
    hi = ((pos // POS_SPLIT) * POS_SPLIT).astype(f32)[:, None]
    lo = (pos % POS_SPLIT).astype(f32)[:, None]
    f = jnp.where(lane < 3, hi, jnp.where(lane < 6, lo, jnp.where(lane < 9, 1.0, 0.0)))
    if ns:
        f = jnp.where((lane >= F_SEL0) & (lane < F_SEL0 + ns), (blk[:, None] == lane - F_SEL0).astype(f32), f)
    return jnp.broadcast_to(f, (n, HEAD_DIM)).astype(f32)


def feature_tables(T, nch):
    pos = jnp.arange(T, dtype=jnp.int32)
    kf = position_features(pos, pos // SLC_BLOCK, T // SLC_BLOCK)
    cpos = jnp.arange(nch, dtype=jnp.int32) * CMP_STRIDE + (CMP_BLOCK - 1)
    cf = position_features(cpos, cpos, 0)
    z = lambda a: jnp.zeros_like(a)
    two = lambda a: jnp.stack([jnp.concatenate([z(a), a], axis=1), jnp.concatenate([a, z(a)], axis=1)]).astype(bf16)
    return two(kf), two(cf)


def _query_features(pos_row, selT, kvh, tq):
    posf = pos_row.astype(f32)
    r16 = lax.broadcasted_iota(jnp.int32, (16, tq), 0)
    zeros16 = jnp.zeros((16, tq), f32)
    selbig = (selT - 1.0) * MASK_BIG
    ns = selT.shape[0]
    pad = jnp.zeros((HEAD_DIM - F_SEL0 - ns, tq), f32)
    plain, full = [], []
    for g in range(NSA_GROUP):
        s = jnp.full((1, tq), SLOPES2[kvh][g], f32)
        s_parts = _split3(s)
        a_parts = _split3(s * posf)
        rows = [p.astype(f32) for p in s_parts] * 2 + [-p.astype(f32) for p in a_parts]
        blk0 = zeros16
        for i, rv in enumerate(rows):
            blk0 = jnp.where(r16 == i, rv, blk0)
        plain.append(jnp.concatenate([blk0, jnp.zeros((HEAD_DIM - 16, tq), f32)], axis=0).astype(bf16))
        full.append(jnp.concatenate([blk0, selbig, pad], axis=0).astype(bf16))
    return plain, full


def _aug_queries(qT, feats, kvh, tq):
    out = []
    for g in range(NSA_GROUP):
        h = kvh * NSA_GROUP + g
        qh = qT[h * HEAD_DIM:(h + 1) * HEAD_DIM, :]
        out.append(jnp.concatenate([qh, feats[g]] if kvh == 0 else [feats[g], qh], axis=0))
    return out


def _aug_keys(k_tile, feat_tile, kvh):
    lane = lax.broadcasted_iota(jnp.int32, k_tile.shape, 1)
    own = (lane < HEAD_DIM) if kvh == 0 else (lane >= HEAD_DIM)
    return jnp.where(own, k_tile, feat_tile)


def _with_ones(vT, kvh):
    half = vT.shape[0] // 2
    ones = jnp.ones((half, vT.shape[1]), vT.dtype)
    return jnp.concatenate([vT[:half], ones] if kvh == 0 else [ones, vT[half:]], axis=0)


def _cmp_branch_t(q_aug, kc_aug, vcT, pos_row, tq):
    ncp = kc_aug.shape[0]
    n_idx = lax.broadcasted_iota(jnp.int32, (ncp, tq), 0)
    cvis = pos_row >= (n_idx * CMP_STRIDE + (CMP_BLOCK - 1))
    any_vis = pos_row >= (CMP_BLOCK - 1)
    p_sum = jnp.zeros((ncp, tq), f32)
    outs = []
    for g in range(NSA_GROUP):
        sg = jnp.where(cvis, jnp.dot(kc_aug, q_aug[g], preferred_element_type=f32), NEG_INF)
        e = jnp.exp2(sg - jnp.max(sg, axis=0, keepdims=True))
        p = e / jnp.sum(e, axis=0, keepdims=True)
        p = jnp.where(any_vis, p, 0.0)
        p_sum = p_sum + p
        outs.append(jnp.dot(vcT, p.astype(bf16), preferred_element_type=f32))
    return outs, p_sum


def _select_blocks_t(p_sumT, pos_row, ns, tq):
    ncp = p_sumT.shape[0]
    cj = lax.broadcasted_iota(jnp.int32, (ns, ncp), 0)
    ci = lax.broadcasted_iota(jnp.int32, (ns, ncp), 1) * CMP_STRIDE
    c2sT = ((ci < (cj + 1) * SLC_BLOCK) & (ci + CMP_BLOCK > cj * SLC_BLOCK)).astype(f32).astype(bf16)
    hi, mid, lo = _split3(p_sumT)
    d = lambda x: jnp.dot(c2sT, x, preferred_element_type=f32)
    imp = d(hi) + d(mid) + d(lo)
    blk = lax.broadcasted_iota(jnp.int32, (ns, tq), 0)
    cur = pos_row // SLC_BLOCK
    forced = (blk == 0) | (blk == cur) | (blk == cur - 1)
    imp = jnp.where(forced, FORCED, jnp.where(blk <= cur, imp, NEG_INF))
    rank = jnp.zeros((ns, tq), jnp.int32)
    for jp in range(ns):
        row = imp[jp:jp + 1, :]
        beats = (row > imp) | ((row == imp) & (blk > jp))
        rank = rank + beats.astype(jnp.int32)
    return ((rank < SLC_TOPN) & (blk <= cur)).astype(f32)


def _nsa_prompt_kernel(qT_ref, gT_ref, sk_ref, svT_ref, wk_ref, wvT_ref, kc_ref, vcT_ref, kf_ref, cf_ref, o_ref,
                       m_s, acc_s, *, tq, kt, ns):
    qi = pl.program_id(1)
    p0 = qi * tq
    pos_row = p0 + lax.broadcasted_iota(jnp.int32, (1, tq), 1)
    gates = _sigmoid(gT_ref[...])
    qT = qT_ref[...]
    kc = kc_ref[0].astype(bf16)
    vcT = vcT_ref[0].astype(bf16)
    half = LANES // 2
    KV = range(NSA_KV_HEADS)

    o_cmp, q_pos, q_sel = [], [], []
    for kvh in KV:
        zsel = jnp.zeros((ns, tq), f32)
        plain, _ = _query_features(pos_row, zsel, kvh, tq)
        qa = _aug_queries(qT, plain, kvh, tq)
        oc, p_sum = _cmp_branch_t(qa, _aug_keys(kc, cf_ref[kvh], kvh), vcT, pos_row, tq)
        selT = _select_blocks_t(p_sum, pos_row, ns, tq)
        _, full = _query_features(pos_row, selT, kvh, tq)
        o_cmp.append(oc)
        q_pos.append(qa)
        q_sel.append(_aug_queries(qT, full, kvh, tq))

    m_s[...] = jnp.full(m_s.shape, NEG_INF, f32)
    acc_s[...] = jnp.zeros(acc_s.shape, f32)
    rel = pos_row - lax.broadcasted_iota(jnp.int32, (kt, tq), 0)

    def slc_tile(c, causal):
        k0 = pl.multiple_of(c * kt, kt)
        k_tile = sk_ref[pl.ds(k0, kt), :]
        v_tile = svT_ref[:, pl.ds(k0, kt)]
        ok = (rel - k0) >= 0
        for kvh in KV:
            ka = _aug_keys(k_tile, kf_ref[kvh, pl.ds(k0, kt), :], kvh)
            va = _with_ones(v_tile, kvh)
            for g in range(NSA_GROUP):
                cs = slice(g * tq, (g + 1) * tq)
                sg = jnp.dot(ka, q_sel[kvh][g], preferred_element_type=f32)
                if causal:
                    sg = jnp.where(ok, sg, NEG_INF)
                m_old = m_s[kvh, :, cs]
                m_new = jnp.maximum(m_old, jnp.max(sg, axis=0, keepdims=True))
                alpha = jnp.exp2(m_old - m_new)
                p = jnp.exp2(sg - m_new).astype(bf16)
                acc_s[kvh, :, cs] = alpha * acc_s[kvh, :, cs] + jnp.dot(va, p, preferred_element_type=f32)
                m_s[kvh, :, cs] = m_new

    n_kt = (p0 + tq + kt - 1) // kt

    def slc_body(c, _):
        slc_tile(c, False)
        return 0

    lax.fori_loop(0, n_kt - 1, slc_body, 0)
    slc_tile(n_kt - 1, True)

    wlen = WINDOW + tq
    w0 = pl.multiple_of(jnp.maximum(p0 - WINDOW, 0), tq)
    k_win = wk_ref[pl.ds(w0, wlen), :]
    v_win = wvT_ref[:, pl.ds(w0, wlen)]
    dist = pos_row - (w0 + lax.broadcasted_iota(jnp.int32, (wlen, tq), 0))
    valid = (dist >= 0) & (dist < WINDOW)
    heads = []
    for kvh in KV:
        ka = _aug_keys(k_win, kf_ref[kvh, pl.ds(w0, wlen), :], kvh)
        va = _with_ones(v_win, kvh)
        d0, l0 = (0, half) if kvh == 0 else (half, 0)
        for g in range(NSA_GROUP):
            h = kvh * NSA_GROUP + g
            cs = slice(g * tq, (g + 1) * tq)
            sg = jnp.where(valid, jnp.dot(ka, q_pos[kvh][g], preferred_element_type=f32), NEG_INF)
            p = jnp.exp2(sg - jnp.max(sg, axis=0, keepdims=True)).astype(bf16)
            o_win = jnp.dot(va, p, preferred_element_type=f32)
            o_slc = acc_s[kvh, :, cs]
            o = (gates[3 * h:3 * h + 1] * o_cmp[kvh][g][d0:d0 + half]
                 + gates[3 * h + 1:3 * h + 2] * (o_slc[d0:d0 + half] / o_slc[l0:l0 + 1])
                 + gates[3 * h + 2:3 * h + 3] * (o_win[d0:d0 + half] / o_win[l0:l0 + 1]))
            heads.append(o)
    o_ref[0] = jnp.concatenate(heads, axis=0).T.astype(o_ref.dtype)


def nsa_prompt(qT, gT, sk, svT, wk, wvT, kc, vcT, B, tq=128, kt=256):
    N = sk.shape[0]
    T = N // B
    ns = T // SLC_BLOCK
    nq = T // tq
    ncp = kc.shape[1]
    assert T % kt == 0 and T % tq == 0 and kt % tq == 0 and T >= WINDOW + tq and T <= POS_SPLIT * 256
    assert F_SEL0 + ns <= HEAD_DIM
    kf, cf = feature_tables(T, ncp)
    keys = pl.BlockSpec((T, LANES), lambda b, i: (b, 0))
    vals = pl.BlockSpec((LANES, T), lambda b, i: (0, b))
    return pl.pallas_call(
        functools.partial(_nsa_prompt_kernel, tq=tq, kt=kt, ns=ns),
        grid=(B, nq),
        in_specs=[pl.BlockSpec((NSA_WIDTH, tq), lambda b, i: (0, b * nq + i)),
                  pl.BlockSpec((3 * NSA_HEADS, tq), lambda b, i: (0, b * nq + i)),
                  keys, vals, keys, vals,
                  pl.BlockSpec((1, ncp, LANES), lambda b, i: (b, 0, 0)),
                  pl.BlockSpec((1, LANES, ncp), lambda b, i: (b, 0, 0)),
                  pl.BlockSpec((NSA_KV_HEADS, T, LANES), lambda b, i: (0, 0, 0)),
                  pl.BlockSpec((NSA_KV_HEADS, ncp, LANES), lambda b, i: (0, 0, 0))],
        out_specs=pl.BlockSpec((1, tq, NSA_WIDTH), lambda b, i: (b, i, 0)),
        out_shape=jax.ShapeDtypeStruct((B, T, NSA_WIDTH), bf16),
        scratch_shapes=[pltpu.VMEM((NSA_KV_HEADS, 1, NSA_GROUP * tq), f32),
                        pltpu.VMEM((NSA_KV_HEADS, LANES, NSA_GROUP * tq), f32)],
        compiler_params=_cparams("parallel", "arbitrary"),
        name="nsa_prompt",
    )(qT, gT, sk, svT, wk, wvT, kc, vcT, kf, cf)


SF_SPLIT = 64
SF_BLK0 = 16
SF_ROWS = 256


def sample_key_features(past_len, nk, tq, ns):
    sf_new0 = SF_BLK0 + -(-ns // 8) * 8
    assert sf_new0 + tq + 1 <= SF_ROWS and nk <= SF_SPLIT * 256
    pos = jnp.arange(nk, dtype=jnp.int32)[None, :]
    row = jnp.arange(SF_ROWS, dtype=jnp.int32)[:, None]
    hi = ((pos // SF_SPLIT) * SF_SPLIT).astype(f32)
    lo = (pos % SF_SPLIT).astype(f32)
    f = jnp.where(row < 3, hi, jnp.where(row < 6, lo, jnp.where(row < 9, 1.0, 0.0)))
    blk = (row >= SF_BLK0) & (row < SF_BLK0 + ns) & (pos // SLC_BLOCK == row - SF_BLK0)
    new = (row >= sf_new0) & (row < sf_new0 + tq) & (pos - past_len == row - sf_new0)
    pad = (row == sf_new0 + tq) & (pos >= past_len + tq)
    return jnp.where(blk | new | pad, 1.0, f).astype(bf16), sf_new0


def _sample_query_features(sel, pos_col, kvh, tq, ns, sf_new0):
    lane = lax.broadcasted_iota(jnp.int32, (tq, SF_ROWS), 1)
    t_idx = lax.broadcasted_iota(jnp.int32, (tq, SF_ROWS), 0)
    selbig = (pltpu.roll(sel[:, 0:SF_ROWS], SF_BLK0, 1) - 1.0) * MASK_BIG
    base = jnp.where((lane >= SF_BLK0) & (lane < SF_BLK0 + ns), selbig, 0.0)
    base = jnp.where((lane >= sf_new0) & (lane < sf_new0 + tq) & (lane - sf_new0 > t_idx), -MASK_BIG, base)
    base = jnp.where(lane == sf_new0 + tq, -MASK_BIG, base)
    posf = pos_col.astype(f32)
    out = []
    for g in range(NSA_GROUP):
        s = jnp.full((tq, 1), SLOPES[kvh][g], f32)
        cols = [p.astype(f32) for p in _split3(s)] * 2 + [-p.astype(f32) for p in _split3(s * posf)]
        qf = base
        for i, cv in enumerate(cols):
            qf = jnp.where(lane == i, cv, qf)
        out.append(qf)
    return jnp.concatenate(out, axis=0).astype(bf16)


def _new_rows_t(rows, width):
    tq = rows.shape[0]
    sq = jnp.concatenate([rows, jnp.zeros((LANES - tq, LANES), f32)], axis=0).T
    if width == LANES:
        return sq
    return jnp.concatenate([sq, jnp.zeros((LANES, width - LANES), f32)], axis=1)


def _compress_staged(xs, nch, pe_ref, w_ref, w2_ref):
    w = w_ref[...].reshape(CMP_STRIDE * LANES, 4 * CMP_HIDDEN)
    acc = jnp.dot(jnp.concatenate([xs[l] for l in range(CMP_STRIDE)], axis=1), w, preferred_element_type=f32)
    pe_p = jnp.concatenate([pe_ref[l:l + 1, :] for l in range(CMP_STRIDE)], axis=1)
    pe_q = jnp.concatenate([pe_ref[CMP_STRIDE + l:CMP_STRIDE + l + 1, :] for l in range(CMP_STRIDE)], axis=1)
    bc = lambda v: jnp.broadcast_to(v, (8, CMP_STRIDE * LANES)).astype(bf16)
    c = (jnp.dot(bc(pe_p), w[:, 0:2 * CMP_HIDDEN], preferred_element_type=f32)
         + jnp.dot(bc(pe_q), w[:, 2 * CMP_HIDDEN:], preferred_element_type=f32))
    hid = acc[:, 0:2 * CMP_HIDDEN] + pltpu.roll(acc[:, 2 * CMP_HIDDEN:], nch - 1, 0) + c[0:1, :]
    hid = _silu(hid).astype(bf16)
    outs = [jnp.dot(hid[:, h * CMP_HIDDEN:(h + 1) * CMP_HIDDEN], w2_ref[...], preferred_element_type=f32)
            for h in range(NSA_KV_HEADS)]
    return jnp.concatenate(outs, axis=1)


def _nsa_sample_kernel(pt_ref, *refs, pps, n_steps, tq, past_len, ns, nsp, sf_new0):
    page_refs = refs[:pps]
    (q_ref, g_ref, rn_ref, wn_ref, cw_ref, ft_ref, perm_ref, pek_ref, pev_ref, wk_ref, wv_ref, w2k_ref, w2v_ref,
     o_ref, xk_s, xv_s, kT_s, vT_s) = refs[pps:]
    del pt_ref
    step = pl.program_id(1)
    cpp = PAGE_SIZE // CMP_STRIDE
    tkv = []
    for i in range(pps):
        r0 = pl.multiple_of((step * pps + i) * PAGE_SIZE, PAGE_SIZE)
        pg = page_refs[i]
        a = pg[0, 0:4].reshape(2 * LANES, PAGE_SIZE).astype(bf16)
        tkv.append(lax.dot_general(perm_ref[...], a, NT, preferred_element_type=f32))
        kT_s[:, pl.ds(r0, PAGE_SIZE)] = pg[0, 4:6].reshape(LANES, PAGE_SIZE).astype(bf16)
        vT_s[:, pl.ds(r0, PAGE_SIZE)] = pg[0, 6:8].reshape(LANES, PAGE_SIZE).astype(bf16)
    c0 = pl.multiple_of(step * (pps * cpp), pps * cpp)
    for l in range(CMP_STRIDE):
        rows = slice(l * cpp, (l + 1) * cpp)
        both = jnp.concatenate([t[rows] for t in tkv], axis=0).astype(bf16)
        xk_s[l, pl.ds(c0, pps * cpp), :] = both[:, 0:LANES]
        xv_s[l, pl.ds(c0, pps * cpp), :] = both[:, LANES:]

    @pl.when(step == n_steps - 1)
    def _():
        nk = kT_s.shape[1]
        rn = rn_ref[0]
        kT_s[:, past_len:] = _new_rows_t(rn[:, 2 * LANES:3 * LANES], nk - past_len).astype(bf16)
        vT_s[:, past_len:] = _new_rows_t(rn[:, 3 * LANES:4 * LANES], nk - past_len).astype(bf16)
        nwin = cw_ref.shape[2]
        wn = wn_ref[0]
        wkT = jnp.concatenate([cw_ref[0, 0:LANES, :], _new_rows_t(wn[:, 0:LANES], LANES)], axis=1).astype(bf16)
        wvT = jnp.concatenate([cw_ref[0, LANES:, :], _new_rows_t(wn[:, LANES:], LANES)], axis=1).astype(bf16)
        wlen = nwin + LANES

        nch = past_len // CMP_STRIDE
        kc = _compress_staged(xk_s, nch, pek_ref, wk_ref, w2k_ref)
        vc = _compress_staged(xv_s, nch, pev_ref, wv_ref, w2v_ref)

        pos_col = past_len + lax.broadcasted_iota(jnp.int32, (tq, 1), 0)
        gates = _sigmoid(g_ref[0])
        q_all = q_ref[0]
        wpos = (past_len - nwin) + lax.broadcasted_iota(jnp.int32, (tq, wlen), 1)
        wdist = pos_col - wpos
        wvalid = (wdist >= 0) & (wdist < WINDOW)
        wdistf = wdist.astype(f32)
        qms, qfs, o_cmps = [], [], []
        for kvh in range(NSA_KV_HEADS):
            qm = _padded_queries(q_all, kvh, tq)
            o_cmp, p_sum = _cmp_branch(qm, kc, vc, pos_col, kvh, tq)
            sel = _select_blocks(p_sum, pos_col, ns, nsp, tq)
            qms.append(qm)
            qfs.append(_sample_query_features(sel, pos_col, kvh, tq, ns, sf_new0))
            o_cmps.append(o_cmp)
        qm2 = jnp.concatenate(qms, axis=0)
        s = (jnp.dot(qm2, kT_s[...], preferred_element_type=f32)
             + jnp.dot(jnp.concatenate(qfs, axis=0), ft_ref[...], preferred_element_type=f32))
        e = jnp.exp(s - jnp.max(s, axis=-1, keepdims=True))
        o_slc2 = (lax.dot_general(e.astype(bf16), vT_s[...], NT, preferred_element_type=f32)
                  / jnp.sum(e, axis=-1, keepdims=True))
        sw2 = jnp.dot(qm2, wkT, preferred_element_type=f32)
        pw = []
        for kvh in range(NSA_KV_HEADS):
            for g in range(NSA_GROUP):
                r = slice((kvh * NSA_GROUP + g) * tq, (kvh * NSA_GROUP + g + 1) * tq)
                sg = jnp.where(wvalid, sw2[r] - SLOPES[kvh][g] * wdistf, NEG_INF)
                ew = jnp.where(wvalid, jnp.exp(sg - jnp.max(sg, axis=-1, keepdims=True)), 0.0)
                pw.append(ew / jnp.sum(ew, axis=-1, keepdims=True))
        o_win2 = lax.dot_general(jnp.concatenate(pw, axis=0).astype(bf16), wvT, NT, preferred_element_type=f32)
        outs = []
        for kvh in range(NSA_KV_HEADS):
            r = slice(kvh * NSA_GROUP * tq, (kvh + 1) * NSA_GROUP * tq)
            outs.extend(_combine(gates, kvh, tq, o_cmps[kvh], o_slc2[r], o_win2[r]))
        o_ref[0] = jnp.concatenate(outs, axis=1).astype(o_ref.dtype)


def nsa_sample(page_table, cacheT, q, gates, rows_new, win_new, cache_winT, cw, pps=16):
    B, tq, _ = q.shape
    n_pages = page_table.shape[1]
    past_len = n_pages * PAGE_SIZE
    assert n_pages % pps == 0 and past_len % SLC_BLOCK == 0
    n_steps = n_pages // pps
    nk = past_len + 512
    ns = past_len // SLC_BLOCK + 1
    nsp = -(-ns // LANES) * LANES
    nch = past_len // CMP_STRIDE
    nwin = cache_winT.shape[2]
    pek, wk, w2k, pev, wv, w2v = cw
    wk = wk.reshape(CMP_STRIDE // 2, 2 * LANES, -1)
    wv = wv.reshape(CMP_STRIDE // 2, 2 * LANES, -1)
    ft, sf_new0 = sample_key_features(past_len, nk, tq, ns)
    cpp = PAGE_SIZE // CMP_STRIDE
    ro = jnp.arange(PAGE_SIZE)
    perm = (ro[None, :] == (ro[:, None] % cpp) * CMP_STRIDE + ro[:, None] // cpp).astype(bf16)
    per_b = lambda n: pl.BlockSpec((1, tq, n), lambda b, s, pt: (b, 0, 0))
    full = lambda a: pl.BlockSpec(a.shape, lambda b, s, pt: (0,) * a.ndim)
    page_specs = [pl.BlockSpec((1, 8, HEAD_DIM, PAGE_SIZE),
                               functools.partial(lambda b, s, pt, i: (pt[b, s * pps + i], 0, 0, 0), i=i))
                  for i in range(pps)]
    grid_spec = pltpu.PrefetchScalarGridSpec(
        num_scalar_prefetch=1,
        grid=(B, n_steps),
        in_specs=page_specs + [per_b(NSA_HEADS * LANES), per_b(3 * NSA_HEADS), per_b(4 * LANES), per_b(2 * LANES),
                               pl.BlockSpec((1, 2 * LANES, nwin), lambda b, s, pt: (b, 0, 0)),
                               full(ft), full(perm), full(pek), full(pev), full(wk), full(wv), full(w2k), full(w2v)],
        out_specs=per_b(NSA_WIDTH),
        scratch_shapes=[pltpu.VMEM((CMP_STRIDE, nch, LANES), bf16), pltpu.VMEM((CMP_STRIDE, nch, LANES), bf16),
                        pltpu.VMEM((LANES, nk), bf16), pltpu.VMEM((LANES, nk), bf16)])
    return pl.pallas_call(
        functools.partial(_nsa_sample_kernel, pps=pps, n_steps=n_steps, tq=tq, past_len=past_len, ns=ns, nsp=nsp,
                          sf_new0=sf_new0),
        grid_spec=grid_spec,
        out_shape=jax.ShapeDtypeStruct((B, tq, NSA_WIDTH), bf16),
        compiler_params=_cparams("parallel", "arbitrary"),
        name="nsa_sample",
    )(page_table, *([cacheT] * pps), q, gates, rows_new, win_new, cache_winT, ft, perm, pek, pev, wk, wv, w2k, w2v)


def _gdn_prep_kernel(x_ref, halo_ref, cb_ref, cw_ref, ba_ref, al_ref, dtb_ref,
                     u_ref, w_ref, qg_ref, kg_ref, qk_ref, eg_ref, xs_s, *, tb, C, t_valid):
    i = pl.program_id(1)
    HK = GDN_HEADS * GDN_DK
    R = min(tb, LANES)
    cpb = R // C

    @pl.when(i == 0)
    def _():
        xs_s[0:8, :] = cb_ref[0]

    @pl.when(i > 0)
    def _():
        xs_s[0:8, :] = halo_ref[0]

    xs_s[8:, :] = x_ref[0]
    ba = ba_ref[0]
    ri = lax.broadcasted_iota(jnp.int32, (R, R), 0)
    ci = lax.broadcasted_iota(jnp.int32, (R, R), 1)
    same = (ri // C) == (ci // C)
    lower = same & (ri >= ci)
    strict = same & (ri > ci)
    eye = (ri == ci).astype(f32)
    rr = lax.broadcasted_iota(jnp.int32, (R, LANES), 0)
    row_t = i * tb + lax.broadcasted_iota(jnp.int32, (tb, 1), 0)
    live = row_t < t_valid

    def conv_act(c0):
        acc = cw_ref[GDN_CONV - 1:GDN_CONV, c0:c0 + LANES] * xs_s[pl.ds(8, tb), c0:c0 + LANES]
        for k in range(1, GDN_CONV):
            acc = acc + cw_ref[GDN_CONV - 1 - k:GDN_CONV - k, c0:c0 + LANES] * xs_s[pl.ds(8 - k, tb), c0:c0 + LANES]
        return _silu(acc)

    chains = []
    for h in range(GDN_HEADS):
        q = conv_act(h * GDN_DK)
        k = conv_act(HK + h * GDN_DK)
        v = conv_act(2 * HK + h * GDN_DV)
        q = q * lax.rsqrt(jnp.sum(q * q, axis=-1, keepdims=True) + L2_EPS) * (GDN_DK ** -0.5)
        k = k * lax.rsqrt(jnp.sum(k * k, axis=-1, keepdims=True) + L2_EPS)
        beta = _sigmoid(ba[:, h:h + 1])
        ar = ba[:, GDN_HEADS + h:GDN_HEADS + h + 1] + dtb_ref[:, h:h + 1]
        softplus = jnp.maximum(ar, 0.0) + jnp.log(1.0 + jnp.exp(-jnp.abs(ar)))
        gcol = -jnp.exp(al_ref[:, h:h + 1]) * softplus
        if t_valid < 10 ** 9:
            beta = jnp.where(live, beta, 0.0)
            gcol = jnp.where(live, gcol, 0.0)
            k = jnp.where(live, k, 0.0)
            v = jnp.where(live, v, 0.0)
        for blk in range(tb // R):
            r = slice(blk * R, (blk + 1) * R)
            qc, kc, vc, bc = q[r], k[r], v[r], beta[r]
            gc = jnp.broadcast_to(gcol[r], (R, LANES))
            sh = 1
            while sh < C:
                gc = gc + jnp.where((rr % C) >= sh, pltpu.roll(gc, sh, 0), 0.0)
                sh *= 2
            gct = gc.T
            decay = jnp.exp(jnp.where(lower, gc[:, 0:R] - gct[0:R, :], NEG_INF))
            kb = kc * bc
            a = jnp.where(strict, _dot_hp(kb, kc, NT) * decay, 0.0)
            qk = jnp.where(lower, lax.dot_general(qc.astype(bf16), kc.astype(bf16), NT,
                                                  preferred_element_type=f32) * decay, 0.0)
            gl = gc[R - 1:R, :]
            for j in range(cpb - 2, -1, -1):
                gl = jnp.where(rr < (j + 1) * C, gc[(j + 1) * C - 1:(j + 1) * C, :], gl)
            chains.append(dict(h=h, r=r, blk=blk, qc=qc, kc=kc, gc=gc, gl=gl, qk=qk, n=-a, t=eye - a,
                               rhs=jnp.concatenate([vc * bc, kb * jnp.exp(gc)], axis=1)))

    for ch in chains:
        ch['n'] = _split2(ch['n'])
    sh = 2
    while sh < C:
        for ch in chains:
            ch['n'] = _split2(_dot_parts(ch['n'], ch['n']))
        for ch in chains:
            ch['t'] = ch['t'] + _dot_parts(_split2(ch['t']), ch['n'])
        sh *= 2

    for ch in chains:
        h, r, gc, gl = ch['h'], ch['r'], ch['gc'], ch['gl']
        sol = ch['rhs'] + _dot_hp(ch['t'] - eye, ch['rhs'])
        cs = slice(h * GDN_DV, (h + 1) * GDN_DV)
        u_ref[0, r, cs] = sol[:, 0:GDN_DV]
        w_ref[0, r, cs] = sol[:, GDN_DV:].astype(w_ref.dtype)
        qg_ref[0, r, cs] = (ch['qc'] * jnp.exp(gc)).astype(qg_ref.dtype)
        kg_ref[0, r, cs] = (ch['kc'] * jnp.exp(gl - gc)).astype(kg_ref.dtype)
        qk = ch['qk'][:, 0:C]
        for j in range(1, cpb):
            qk = jnp.where(ri[:, 0:C] >= j * C, ch['qk'][:, j * C:(j + 1) * C], qk)
        qk_ref[0, r, h * C:(h + 1) * C] = qk.astype(qk_ref.dtype)
        for j in range(cpb):
            eg_ref[0, ch['blk'] * cpb + j, h:h + 1, :] = jnp.exp(gc[(j + 1) * C - 1:(j + 1) * C, :])
    eg_ref[0, :, GDN_HEADS:, :] = jnp.zeros((tb // C, 8 - GDN_HEADS, LANES), f32)


def gdn_prep(qkv, conv_buf8, conv_w, ba, a_log, dt_bias, C, tb, t_valid):
    B, T, CH = qkv.shape
    nb = T // tb
    hb = tb // 8
    per = lambda n, dt: (pl.BlockSpec((1, tb, n), lambda b, i: (b, i, 0)), jax.ShapeDtypeStruct((B, T, n), dt))
    outs = [per(GDN_WIDTH, f32), per(GDN_WIDTH, bf16), per(GDN_WIDTH, bf16), per(GDN_WIDTH, bf16),
            per(GDN_HEADS * C, bf16),
            (pl.BlockSpec((1, tb // C, 8, LANES), lambda b, i: (b, i, 0, 0)),
             jax.ShapeDtypeStruct((B, T // C, 8, LANES), f32))]
    return pl.pallas_call(
        functools.partial(_gdn_prep_kernel, tb=tb, C=C, t_valid=t_valid),
        grid=(B, nb),
        in_specs=[pl.BlockSpec((1, tb, CH), lambda b, i: (b, i, 0)),
                  pl.BlockSpec((1, 8, CH), lambda b, i: (b, jnp.maximum(i * hb - 1, 0), 0)),
                  pl.BlockSpec((1, 8, CH), lambda b, i: (b, 0, 0)),
                  pl.BlockSpec((GDN_CONV, CH), lambda b, i: (0, 0)),
                  pl.BlockSpec((1, tb, 2 * GDN_HEADS), lambda b, i: (b, i, 0)),
                  pl.BlockSpec((1, GDN_HEADS), lambda b, i: (0, 0)),
                  pl.BlockSpec((1, GDN_HEADS), lambda b, i: (0, 0))],
        out_specs=[o[0] for o in outs],
        out_shape=[o[1] for o in outs],
        scratch_shapes=[pltpu.VMEM((tb + 8, CH), f32)],
        compiler_params=_cparams("parallel", "arbitrary"),
        name="gdn_prep",
    )(qkv, qkv, conv_buf8, conv_w, ba, a_log.reshape(1, GDN_HEADS), dt_bias.reshape(1, GDN_HEADS))


def _gdn_scan_kernel(u_ref, w_ref, qg_ref, kg_ref, qk_ref, eg_ref, z_ref, ng_ref, s0_ref,
                     o_ref, sT_ref, s_s, *, C, nb):
    c = pl.program_id(1)

    @pl.when(c == 0)
    def _():
        s_s[...] = s0_ref[...]

    ch = [(b, h) for b in range(nb) for h in range(GDN_HEADS)]
    cs = [slice(h * GDN_DV, (h + 1) * GDN_DV) for h in range(GDN_HEADS)]
    S = [s_s[b, h] for b, h in ch]
    Sb = [x.astype(bf16) for x in S]
    v_new = [u_ref[b, :, cs[h]] - jnp.dot(w_ref[b, :, cs[h]], Sb[i], preferred_element_type=f32)
             for i, (b, h) in enumerate(ch)]
    vb = [x.astype(bf16) for x in v_new]
    for i, (b, h) in enumerate(ch):
        s_s[b, h] = S[i] * eg_ref[b, 0, h:h + 1, :] + lax.dot_general(kg_ref[b, :, cs[h]], vb[i], TN,
                                                                       preferred_element_type=f32)
    o = [jnp.dot(qg_ref[b, :, cs[h]], Sb[i], preferred_element_type=f32)
         + jnp.dot(qk_ref[b, :, h * C:(h + 1) * C], vb[i], preferred_element_type=f32) for i, (b, h) in enumerate(ch)]
    outs = [_rms(o[i], ng_ref[...]) * _silu(z_ref[b, :, cs[h]]) for i, (b, h) in enumerate(ch)]
    for b in range(nb):
        o_ref[b] = jnp.concatenate(outs[b * GDN_HEADS:(b + 1) * GDN_HEADS], axis=1).astype(o_ref.dtype)

    @pl.when(c == pl.num_programs(1) - 1)
    def _():
        sT_ref[...] = s_s[...]


def gdn_scan(u, w, qg, kg, qk, eg, z, norm_g, s0, C, nb=8):
    B, T, _ = u.shape
    nb = math.gcd(nb, B)
    per = lambda n: pl.BlockSpec((nb, C, n), lambda b, c: (b, c, 0))
    st = pl.BlockSpec((nb, GDN_HEADS, GDN_DK, GDN_DV), lambda b, c: (b, 0, 0, 0))
    return pl.pallas_call(
        functools.partial(_gdn_scan_kernel, C=C, nb=nb),
        grid=(B // nb, T // C),
        in_specs=[per(GDN_WIDTH), per(GDN_WIDTH), per(GDN_WIDTH), per(GDN_WIDTH), per(GDN_HEADS * C),
                  pl.BlockSpec((nb, 1, 8, LANES), lambda b, c: (b, c, 0, 0)),
                  per(GDN_WIDTH), pl.BlockSpec((1, GDN_DV), lambda b, c: (0, 0)), st],
        out_specs=[per(GDN_WIDTH), st],
        out_shape=[jax.ShapeDtypeStruct((B, T, GDN_WIDTH), bf16),
                   jax.ShapeDtypeStruct((B, GDN_HEADS, GDN_DK, GDN_DV), f32)],
        scratch_shapes=[pltpu.VMEM((nb, GDN_HEADS, GDN_DK, GDN_DV), f32)],
        compiler_params=_cparams("parallel", "arbitrary"),
        name="gdn_scan",
    )(u, w, qg, kg, qk, eg, z, norm_g.reshape(1, GDN_DV), s0)


def _nsa_in_groups(w, feature_major):
    c1 = NSA_WIDTH
    c2 = c1 + 4 * NSA_KV_COLS
    c3 = c2 + 2 * NSA_KV_COLS
    c4 = c3 + 3 * NSA_HEADS
    wb = w.astype(bf16)
    xa_scale = XA_DIM ** -0.5 * LOG2E
    if feature_major:
        k0 = c1 + 2 * NSA_KV_COLS
        return [(wb[:, c1:c2].T, f32, 1.0, FEATURE_MAJOR_PER_BATCH),
                (wb[:, c1:k0], f32, 1.0, TOKEN_MAJOR),
                (wb[:, c2:c3].T, f32, 1.0, FEATURE_MAJOR_PER_BATCH),
                (wb[:, c4:].T, bf16, xa_scale, FEATURE_MAJOR),
                (wb[:, :c1].T, bf16, HEAD_DIM ** -0.5 * LOG2E, FEATURE_MAJOR),
                (wb[:, c3:c4].T, f32, 1.0, FEATURE_MAJOR),
                (wb[:, k0:k0 + NSA_KV_COLS], bf16, 1.0, TOKEN_MAJOR),
                (wb[:, k0 + NSA_KV_COLS:c2].T, bf16, 1.0, FEATURE_MAJOR),
                (wb[:, c2:c2 + NSA_KV_COLS], bf16, 1.0, TOKEN_MAJOR),
                (wb[:, c2 + NSA_KV_COLS:c3].T, bf16, 1.0, FEATURE_MAJOR)]
    wq = wb[:, :c1].reshape(-1, NSA_KV_HEADS, NSA_GROUP, HEAD_DIM)
    zq = jnp.zeros_like(wq[:, 0])
    wq = jnp.concatenate([jnp.concatenate([wq[:, 0], zq], axis=-1), jnp.concatenate([zq, wq[:, 1]], axis=-1)], axis=1)
    wq = wq.reshape(-1, NSA_HEADS * LANES)
    return [(wb[:, c1:c2], f32, 1.0, TOKEN_MAJOR), (wb[:, c2:c3], f32, 1.0, TOKEN_MAJOR),
            (wb[:, c4:], bf16, xa_scale, TOKEN_MAJOR), (wq, bf16, HEAD_DIM ** -0.5, TOKEN_MAJOR),
            (wb[:, c3:c4], f32, 1.0, TOKEN_MAJOR)]


def _gdn_in_groups(w, feature_major):
    c1 = GDN_CONV_CH
    c2 = c1 + GDN_WIDTH
    c3 = c2 + 2 * GDN_HEADS
    wb = w.astype(bf16)
    xq = ((wb[:, c3:].T, bf16, XA_DIM ** -0.5 * LOG2E, FEATURE_MAJOR) if feature_major
          else (wb[:, c3:], bf16, XA_DIM ** -0.5 * LOG2E, TOKEN_MAJOR))
    return [(wb[:, :c1], f32, 1.0, TOKEN_MAJOR), (wb[:, c1:c2], f32, 1.0, TOKEN_MAJOR),
            (wb[:, c2:c3], f32, 1.0, TOKEN_MAJOR), xq]


def _trunk(x, mem_kv, nsa_past, gdn_state, p, t_valid):
    B, T, D = x.shape
    N = B * T
    xf = x.reshape(N, D)

    prompt = nsa_past is None
    if prompt:
        kv_rows, rows_cmp, win_rows, xq, qT, gT, sk, svT, wk, wvT = norm_proj(
            xf, p['norm_mix_g'][0], p['nsa_in_prompt'], tm=512, rows_per_batch=T)
        kc, vcT = cmp_prompt(rows_cmp.reshape(B, T, -1), p['cmp'])
        o_mix = nsa_prompt(qT, gT, sk, svT, wk, wvT, kc, vcT, B)
    else:
        rows, win, xq, q, gates = norm_proj(xf, p['norm_mix_g'][0], p['nsa_in_sample'])
        kv_rows = rows.reshape(B, T, -1)
        win_rows = win.reshape(B, T, -1)
        xq = xq.reshape(B, T, -1)
        page_table, cache, cache_win = nsa_past
        o_mix = nsa_sample(page_table, cache, q.reshape(B, T, -1), gates.reshape(B, T, -1), kv_rows, win_rows,
                           cache_win, p['cmp'])
    o_mem = cross_attend(xq, mem_kv, 0, B)
    xf = post_block(xf, o_mix.reshape(N, -1), o_mem.reshape(N, -1), p['w_out'][0], p['norm_ffn_g'][0],
                    p['w_up'][0], p['w_down'][0], p['final_norm_g'], final=False)

    qkv, z, ba, xq = norm_proj(xf, p['norm_mix_g'][1], p['gdn_in_prompt' if prompt else 'gdn_in_sample'], tm=512)
    if not prompt:
        xq = xq.reshape(B, T, -1)
    qkv3 = qkv.reshape(B, T, -1)
    S0, conv_buf8 = gdn_state
    C = min(64, T)
    tb = min(256, T)
    u, w, qg, kg, qk, eg = gdn_prep(qkv3, conv_buf8, p['gdn_conv_w'], ba.reshape(B, T, -1), p['gdn_a_log'],
                                    p['gdn_dt_bias'], C, tb, t_valid)
    o_mix, S = gdn_scan(u, w, qg, kg, qk, eg, z.reshape(B, T, -1), p['gdn_norm_g'], S0, C)
    o_mem = cross_attend(xq, mem_kv, 1, B)
    y = post_block(xf, o_mix.reshape(N, -1), o_mem.reshape(N, -1), p['w_out'][1], p['norm_ffn_g'][1],
                   p['w_up'][1], p['w_down'][1], p['final_norm_g'], final=True)
    return y.reshape(B, T, D), kv_rows, win_rows, S, qkv3


def kernel(x_prompt, x_sample, mem_prompt, cache_nsa_kv, cache_nsa_win, state_gdn_s, state_gdn_conv,
           cache_mem_kv, page_table, norm_mix_g, norm_mem_g, w_mem_kv, nsa_w_in, cmp_pe_k, cmp_w1_k,
           cmp_w2_k, cmp_pe_v, cmp_w1_v, cmp_w2_v, gdn_w_in, gdn_conv_w, gdn_a_log, gdn_dt_bias,
           gdn_norm_g, w_out, norm_ffn_g, w_up, w_down, final_norm_g):
    B, T, D = x_prompt.shape
    DB, TS, _ = x_sample.shape
    M = mem_prompt.shape[1]
    p = dict(norm_mix_g=norm_mix_g, nsa_in_prompt=_nsa_in_groups(nsa_w_in[0], True),
             nsa_in_sample=_nsa_in_groups(nsa_w_in[0], False), gdn_in_prompt=_gdn_in_groups(gdn_w_in[0], True),
             gdn_in_sample=_gdn_in_groups(gdn_w_in[0], False),
             cmp=_cmp_weights(cmp_pe_k[0], cmp_w1_k[0], cmp_w2_k[0]) + _cmp_weights(cmp_pe_v[0], cmp_w1_v[0], cmp_w2_v[0]),
             gdn_conv_w=gdn_conv_w[0], gdn_a_log=gdn_a_log[0], gdn_dt_bias=gdn_dt_bias[0], gdn_norm_g=gdn_norm_g[0],
             w_out=w_out.astype(bf16), norm_ffn_g=norm_ffn_g, w_up=w_up.astype(bf16), w_down=w_down.astype(bf16),
             final_norm_g=final_norm_g)

    mem_flat = mem_prompt.reshape(B * M, D)
    mem_kv = [norm_proj(mem_flat, norm_mem_g[i], [(w_mem_kv[i].astype(bf16).T, f32, 1.0, FEATURE_MAJOR_PER_BATCH)],
                        tm=M, rows_per_batch=M)[0] for i in range(DEPTH)]
    mem_kv_p = jnp.stack(mem_kv, axis=1)
    zero_state = (jnp.zeros((B, GDN_HEADS, GDN_DK, GDN_DV), f32), jnp.zeros((B, 8, GDN_CONV_CH), f32))
    y_p, rows_p, win_p, S_p, qkv_p = _trunk(x_prompt, mem_kv_p, None, zero_state, p, 10 ** 9)

    TP = 8
    x_s = jnp.pad(x_sample, ((0, 0), (0, TP - TS), (0, 0)))
    cache = cache_nsa_kv.transpose(0, 2, 3, 4, 5, 1).reshape(cache_nsa_kv.shape[0], -1, HEAD_DIM, PAGE_SIZE)
    nwin = cache_nsa_win.shape[2]
    cache_win = cache_nsa_win.transpose(0, 1, 3, 4, 5, 2).reshape(DB, -1, nwin)
    conv8 = jnp.pad(state_gdn_conv[:, 0], ((0, 0), (8 - (GDN_CONV - 1), 0), (0, 0)))
    mem_kv_s = cache_mem_kv.transpose(0, 1, 3, 4, 5, 2).reshape(DB, DEPTH, 2 * XA_WIDTH, M)
    y_s, rows_s, win_s, S_s, qkv_s = _trunk(x_s, mem_kv_s, (page_table, cache, cache_win),
                                            (state_gdn_s[:, 0], conv8), p, TS)

    kvshape = (4, NSA_KV_HEADS, HEAD_DIM)
    nsa_kv_prompt = rows_p.reshape(B, 1, *kvshape, T).transpose(0, 5, 1, 2, 3, 4)
    nsa_kv_sample = rows_s[:, :TS].reshape(DB, TS, 1, *kvshape)
    wshape = (2, NSA_KV_HEADS, HEAD_DIM)
    wkeep = min(WINDOW, T)
    nsa_win_prompt = win_p[:, :, T - wkeep:].reshape(B, 1, *wshape, wkeep).transpose(0, 1, 5, 2, 3, 4)
    win_cat = jnp.concatenate([cache_win, win_s[:, :TS].transpose(0, 2, 1)], axis=2)[:, :, -WINDOW:]
    nsa_win_sample = win_cat.reshape(DB, 1, *wshape, -1).transpose(0, 1, 5, 2, 3, 4)
    gdn_conv_prompt = qkv_p[:, None, T - (GDN_CONV - 1):]
    conv_cat = jnp.concatenate([state_gdn_conv[:, 0], qkv_s[:, :TS]], axis=1)
    gdn_conv_sample = conv_cat[:, None, -(GDN_CONV - 1):]
    mem_kv_prompt = mem_kv_p.reshape(B, DEPTH, 2, XA_HEADS, XA_DIM, M).transpose(0, 1, 5, 2, 3, 4)
    return (y_p, y_s[:, :TS], nsa_kv_prompt, nsa_kv_sample, nsa_win_prompt, nsa_win_sample,
            S_p[:, None], S_s[:, None], gdn_conv_prompt, gdn_conv_sample, mem_kv_prompt)
```

```python
import functools
import math

import jax
import jax.numpy as jnp
from jax import lax
from jax.experimental import pallas as pl
from jax.experimental.pallas import tpu as pltpu

f32 = jnp.float32
bf16 = jnp.bfloat16

D_MODEL = 1024
DEPTH = 2
PAGE_SIZE = 128
HEAD_DIM = 64
NSA_HEADS = 12
NSA_KV_HEADS = 2
NSA_GROUP = NSA_HEADS // NSA_KV_HEADS
NSA_WIDTH = NSA_HEADS * HEAD_DIM
NSA_KV_COLS = NSA_KV_HEADS * HEAD_DIM
CMP_STRIDE = 16
CMP_BLOCK = 2 * CMP_STRIDE
CMP_HIDDEN = 128
SLC_BLOCK = 64
SLC_TOPN = 16
WINDOW = 512
GDN_HEADS = 6
GDN_DK = 128
GDN_DV = 128
GDN_WIDTH = GDN_HEADS * GDN_DV
GDN_CONV = 4
GDN_CONV_CH = 2 * GDN_HEADS * GDN_DK + GDN_HEADS * GDN_DV
XA_HEADS = 4
XA_DIM = 64
XA_WIDTH = XA_HEADS * XA_DIM
D_FF = 4 * D_MODEL
NORM_EPS = 1e-6
L2_EPS = 1e-6
NEG_INF = -1e30
FORCED = 1e9

VMEM_LIMIT = 52 * 1024 * 1024
LANES = 128

SLOPES = [[2.0 ** (-8.0 * (k * NSA_GROUP + g + 1) / NSA_HEADS) for g in range(NSA_GROUP)]
          for k in range(NSA_KV_HEADS)]

NT = (((1,), (1,)), ((), ()))
TN = (((0,), (0,)), ((), ()))


def _cparams(*sem):
    return pltpu.CompilerParams(dimension_semantics=sem, vmem_limit_bytes=VMEM_LIMIT)


def _rms(x, g):
    r = lax.rsqrt(jnp.mean(x * x, axis=-1, keepdims=True) + NORM_EPS)
    return (x * r) * g


def _sigmoid(x):
    return 1.0 / (1.0 + jnp.exp(-x))


def _silu(x):
    return x * _sigmoid(x)


def _split3(a):
    hi = a.astype(bf16)
    r1 = a - hi.astype(f32)
    mid = r1.astype(bf16)
    lo = (r1 - mid.astype(f32)).astype(bf16)
    return hi, mid, lo


def _dot_exact_rhs(a, b_bf16):
    hi, mid, lo = _split3(a)
    d = lambda x: jnp.dot(x, b_bf16, preferred_element_type=f32)
    return d(hi) + d(mid) + d(lo)


def _split2(a):
    hi = a.astype(bf16)
    return hi, (a - hi.astype(f32)).astype(bf16)


def _dot_parts(a, b, dims=None):
    if dims is None:
        d = lambda x, y: jnp.dot(x, y, preferred_element_type=f32)
    else:
        d = lambda x, y: lax.dot_general(x, y, dims, preferred_element_type=f32)
    return d(a[0], b[0]) + (d(a[0], b[1]) + d(a[1], b[0]))


def _dot_hp(a, b, dims=None):
    return _dot_parts(_split2(a), _split2(b), dims)


TOKEN_MAJOR, TOKEN_MAJOR_WT, FEATURE_MAJOR, FEATURE_MAJOR_PER_BATCH = 'nt', 'ntw', 'fm', 'fmb'


def _norm_proj_kernel(x_ref, g_ref, *refs, specs):
    n = len(specs)
    w_refs, o_refs = refs[:n], refs[n:]
    h = _rms(x_ref[...], g_ref[...]).astype(bf16)
    for (scale, kind), w_ref, o_ref in zip(specs, w_refs, o_refs):
        if kind == TOKEN_MAJOR:
            y = jnp.dot(h, w_ref[...], preferred_element_type=f32)
        elif kind == TOKEN_MAJOR_WT:
            y = lax.dot_general(h, w_ref[...], NT, preferred_element_type=f32)
        else:
            y = lax.dot_general(w_ref[...], h, NT, preferred_element_type=f32)
        if scale != 1.0:
            y = y * scale
        o_ref[...] = y.astype(o_ref.dtype).reshape(o_ref.shape)


def norm_proj(x, g, groups, tm=256, rows_per_batch=None):
    N, D = x.shape
    tm = min(tm, N)
    assert N % tm == 0
    in_specs = [pl.BlockSpec((tm, D), lambda i: (i, 0)), pl.BlockSpec((1, D), lambda i: (0, 0))]
    out_specs, out_shape = [], []
    for w, dt, _, kind in groups:
        in_specs.append(pl.BlockSpec(w.shape, lambda i: (0, 0)))
        if kind in (TOKEN_MAJOR, TOKEN_MAJOR_WT):
            n = w.shape[1] if kind == TOKEN_MAJOR else w.shape[0]
            out_specs.append(pl.BlockSpec((tm, n), lambda i: (i, 0)))
            out_shape.append(jax.ShapeDtypeStruct((N, n), dt))
        elif kind == FEATURE_MAJOR:
            n = w.shape[0]
            out_specs.append(pl.BlockSpec((n, tm), lambda i: (0, i)))
            out_shape.append(jax.ShapeDtypeStruct((n, N), dt))
        else:
            n = w.shape[0]
            per = rows_per_batch // tm
            assert rows_per_batch % tm == 0
            out_specs.append(pl.BlockSpec((1, n, tm), lambda i: (i // per, 0, i % per)))
            out_shape.append(jax.ShapeDtypeStruct((N // rows_per_batch, n, rows_per_batch), dt))
    return pl.pallas_call(
        functools.partial(_norm_proj_kernel, specs=tuple((s, t) for _, _, s, t in groups)),
        grid=(N // tm,),
        in_specs=in_specs,
        out_specs=out_specs,
        out_shape=out_shape,
        compiler_params=_cparams("parallel"),
        name="norm_proj",
    )(x, g.reshape(1, D), *[w for w, _, _, _ in groups])


def _xattn_kernel(q_ref, kv_ref, o_ref, *, q_feature_major, nb):
    ch = [(b, h) for b in range(nb) for h in range(XA_HEADS)]
    hs = lambda h, off=0: slice(off + h * XA_DIM, off + (h + 1) * XA_DIM)
    qT = [q_ref[...] if q_feature_major else q_ref[b].astype(f32).T.astype(bf16) for b in range(nb)]
    s = [lax.dot_general(kv_ref[b, 0, hs(h), :].astype(bf16), qT[b][hs(h)], TN, preferred_element_type=f32)
         for b, h in ch]
    e = [jnp.exp2(x - jnp.max(x, axis=0, keepdims=True)) for x in s]
    o = [jnp.dot(kv_ref[b, 0, hs(h, XA_WIDTH), :].astype(bf16), e[i].astype(bf16), preferred_element_type=f32)
         / jnp.sum(e[i], axis=0, keepdims=True) for i, (b, h) in enumerate(ch)]
    for b in range(nb):
        o_ref[b] = jnp.concatenate(o[b * XA_HEADS:(b + 1) * XA_HEADS], axis=0).T.astype(o_ref.dtype)


def cross_attend(xq, kvT, layer, B, tq=512, nb=4):
    q_feature_major = xq.ndim == 2
    T = xq.shape[1] // B if q_feature_major else xq.shape[1]
    M = kvT.shape[3]
    tq = min(tq, T)
    nq = T // tq
    if q_feature_major:
        nb = 1
        q_spec = pl.BlockSpec((XA_WIDTH, tq), lambda b, i: (0, b * nq + i))
    else:
        nb = math.gcd(nb, B)
        q_spec = pl.BlockSpec((nb, tq, XA_WIDTH), lambda b, i: (b, i, 0))
    return pl.pallas_call(
        functools.partial(_xattn_kernel, q_feature_major=q_feature_major, nb=nb),
        grid=(B // nb, nq),
        in_specs=[q_spec, pl.BlockSpec((nb, 1, 2 * XA_WIDTH, M), lambda b, i: (b, layer, 0, 0))],
        out_specs=pl.BlockSpec((nb, tq, XA_WIDTH), lambda b, i: (b, i, 0)),
        out_shape=jax.ShapeDtypeStruct((B, T, XA_WIDTH), bf16),
        compiler_params=_cparams("parallel", "parallel"),
        name="cross_attend",
    )(xq, kvT)


def _post_kernel(x_ref, om_ref, oc_ref, wo_ref, gf_ref, wu_ref, wd_ref, gl_ref, o_ref,
                 x1_s, h_s, acc_s, *, mix_w, final):
    j = pl.program_id(1)

    @pl.when(j == 0)
    def _():
        x1 = x_ref[...] + (jnp.dot(om_ref[...], wo_ref[0:mix_w, :], preferred_element_type=f32)
                           + jnp.dot(oc_ref[...], wo_ref[mix_w:, :], preferred_element_type=f32))
        x1_s[...] = x1
        h_s[...] = _rms(x1, gf_ref[...]).astype(bf16)
        acc_s[...] = jnp.zeros_like(acc_s)

    u = jnp.dot(h_s[...], wu_ref[...], preferred_element_type=f32)
    u = jnp.square(jnp.maximum(u, 0.0)).astype(bf16)
    acc_s[...] += jnp.dot(u, wd_ref[...], preferred_element_type=f32)

    @pl.when(j == pl.num_programs(1) - 1)
    def _():
        x2 = x1_s[...] + acc_s[...]
        if final:
            x2 = _rms(x2, gl_ref[...])
        o_ref[...] = x2


def post_block(x, o_mix, o_mem, w_out, g_ffn, w_up, w_down, g_final, final, tm=1024, tf=512):
    N, D = x.shape
    mix_w = o_mix.shape[1]
    F = w_up.shape[1]
    tm = min(tm, N)
    return pl.pallas_call(
        functools.partial(_post_kernel, mix_w=mix_w, final=final),
        grid=(N // tm, F // tf),
        in_specs=[pl.BlockSpec((tm, D), lambda i, j: (i, 0)),
                  pl.BlockSpec((tm, mix_w), lambda i, j: (i, 0)),
                  pl.BlockSpec((tm, XA_WIDTH), lambda i, j: (i, 0)),
                  pl.BlockSpec((mix_w + XA_WIDTH, D), lambda i, j: (0, 0)),
                  pl.BlockSpec((1, D), lambda i, j: (0, 0)),
                  pl.BlockSpec((D, tf), lambda i, j: (0, j)),
                  pl.BlockSpec((tf, D), lambda i, j: (j, 0)),
                  pl.BlockSpec((1, D), lambda i, j: (0, 0))],
        out_specs=pl.BlockSpec((tm, D), lambda i, j: (i, 0)),
        out_shape=jax.ShapeDtypeStruct((N, D), f32),
        scratch_shapes=[pltpu.VMEM((tm, D), f32), pltpu.VMEM((tm, D), bf16), pltpu.VMEM((tm, D), f32)],
        compiler_params=_cparams("parallel", "arbitrary"),
        name="post_block",
    )(x, o_mix, o_mem, w_out, g_ffn.reshape(1, D), w_up, w_down, g_final.reshape(1, D))


def _compress(load_chunk_rows, nch, pe_ref, w_ref, w2_ref):
    acc_p = jnp.zeros((nch, 2 * CMP_HIDDEN), f32)
    acc_q = jnp.zeros((nch, 2 * CMP_HIDDEN), f32)
    for l in range(CMP_STRIDE):
        x = load_chunk_rows(l)
        xp = (x + pe_ref[l:l + 1, :]).astype(bf16)
        xq = (x + pe_ref[CMP_STRIDE + l:CMP_STRIDE + l + 1, :]).astype(bf16)
        acc_p = acc_p + jnp.dot(xp, w_ref[l, :, 0:2 * CMP_HIDDEN], preferred_element_type=f32)
        acc_q = acc_q + jnp.dot(xq, w_ref[l, :, 2 * CMP_HIDDEN:], preferred_element_type=f32)
    hid = _silu(acc_p + pltpu.roll(acc_q, nch - 1, 0)).astype(bf16)
    outs = [jnp.dot(hid[:, h * CMP_HIDDEN:(h + 1) * CMP_HIDDEN], w2_ref[...], preferred_element_type=f32)
            for h in range(NSA_KV_HEADS)]
    return jnp.concatenate(outs, axis=1)


def _cmp_prompt_kernel(xk_ref, xv_ref, pek_ref, pev_ref, wk_ref, wv_ref, w2k_ref, w2v_ref, kc_ref, vc_ref, *, nch):
    kc_ref[0] = _compress(lambda l: xk_ref[0, pl.ds(l, nch, stride=CMP_STRIDE), :], nch, pek_ref, wk_ref, w2k_ref)
    vc_ref[0] = _compress(lambda l: xv_ref[0, pl.ds(l, nch, stride=CMP_STRIDE), :], nch, pev_ref, wv_ref, w2v_ref).T


def _cmp_weights(pe, w1, w2):
    pe2 = jnp.tile(pe, (1, NSA_KV_HEADS))
    z = jnp.zeros((CMP_STRIDE, HEAD_DIM, CMP_HIDDEN), w1.dtype)
    a, b = w1[:CMP_STRIDE], w1[CMP_STRIDE:]
    top = jnp.concatenate([a, z, b, z], axis=2)
    bot = jnp.concatenate([z, a, z, b], axis=2)
    wbd = jnp.concatenate([top, bot], axis=1).astype(bf16)
    return pe2, wbd, w2.astype(bf16)


def _cmp_weights_dense(pe, w1, w2):
    half = CMP_STRIDE * HEAD_DIM
    w = jnp.concatenate([w1[:CMP_STRIDE].reshape(half, CMP_HIDDEN), w1[CMP_STRIDE:].reshape(half, CMP_HIDDEN)], axis=1)
    return pe.reshape(2, half), w.astype(bf16), w2.astype(bf16)


def cmp_prompt(rows, cw):
    B, T, _ = rows.shape
    nch = T // CMP_STRIDE
    pek, wk, w2k, pev, wv, w2v = cw
    full = lambda a: pl.BlockSpec(a.shape, lambda b: (0,) * a.ndim)
    return pl.pallas_call(
        functools.partial(_cmp_prompt_kernel, nch=nch),
        grid=(B,),
        in_specs=[pl.BlockSpec((1, T, LANES), lambda b: (b, 0, 0)),
                  pl.BlockSpec((1, T, LANES), lambda b: (b, 0, 1)),
                  full(pek), full(pev), full(wk), full(wv), full(w2k), full(w2v)],
        out_specs=[pl.BlockSpec((1, nch, LANES), lambda b: (b, 0, 0)), pl.BlockSpec((1, LANES, nch), lambda b: (b, 0, 0))],
        out_shape=[jax.ShapeDtypeStruct((B, nch, LANES), f32), jax.ShapeDtypeStruct((B, LANES, nch), f32)],
        compiler_params=_cparams("parallel"),
        name="cmp_prompt",
    )(rows, rows, pek, pev, wk, wv, w2k, w2v)


def _padded_queries(q_all, kvh, tq):
    del tq
    lo = kvh * NSA_GROUP
    return jnp.concatenate([q_all[:, (lo + g) * LANES:(lo + g + 1) * LANES] for g in range(NSA_GROUP)], axis=0)


def _cmp_branch(qm, kc, vc, pos_col, kvh, tq):
    ncp = kc.shape[0]
    s = lax.dot_general(qm, kc.astype(bf16), NT, preferred_element_type=f32)
    n_idx = lax.broadcasted_iota(jnp.int32, (tq, ncp), 1)
    cdist = pos_col - (n_idx * CMP_STRIDE + (CMP_BLOCK - 1))
    cvis = cdist >= 0
    any_vis = pos_col >= (CMP_BLOCK - 1)
    cdf = cdist.astype(f32)
    p_sum = jnp.zeros((tq, ncp), f32)
    ps = []
    for g in range(NSA_GROUP):
        sg = jnp.where(cvis, s[g * tq:(g + 1) * tq] - SLOPES[kvh][g] * cdf, NEG_INF)
        m = jnp.max(sg, axis=-1, keepdims=True)
        e = jnp.exp(sg - m)
        p = e / jnp.sum(e, axis=-1, keepdims=True)
        p = jnp.where(any_vis, p, 0.0)
        p_sum = p_sum + p
        ps.append(p)
    p_all = jnp.concatenate(ps, axis=0).astype(bf16)
    o = jnp.dot(p_all, vc.astype(bf16), preferred_element_type=f32)
    return o, p_sum


def _select_blocks(p_sum, pos_col, ns, nsp, tq):
    ncp = p_sum.shape[1]
    ci = lax.broadcasted_iota(jnp.int32, (ncp, nsp), 0) * CMP_STRIDE
    cj = lax.broadcasted_iota(jnp.int32, (ncp, nsp), 1)
    c2s = ((ci < (cj + 1) * SLC_BLOCK) & (ci + CMP_BLOCK > cj * SLC_BLOCK)).astype(f32).astype(bf16)
    imp = _dot_exact_rhs(p_sum, c2s)
    blk = lax.broadcasted_iota(jnp.int32, (tq, nsp), 1)
    cur = pos_col // SLC_BLOCK
    forced = (blk == 0) | (blk == cur) | (blk == cur - 1)
    imp = jnp.where(forced, FORCED, jnp.where(blk <= cur, imp, NEG_INF))
    rank = jnp.zeros((tq, nsp), jnp.int32)
    for jp in range(ns):
        col = imp[:, jp:jp + 1]
        beats = (col > imp) | ((col == imp) & (blk > jp))
        rank = rank + beats.astype(jnp.int32)
    return ((rank < SLC_TOPN) & (blk <= cur)).astype(f32)


def _combine(gates, kvh, tq, o_cmp, o_slc, o_win):
    lo = kvh * HEAD_DIM
    outs = []
    for g in range(NSA_GROUP):
        h = kvh * NSA_GROUP + g
        r = slice(g * tq, (g + 1) * tq)
        o = (gates[:, 3 * h:3 * h + 1] * o_cmp[r] + gates[:, 3 * h + 1:3 * h + 2] * o_slc[r]
             + gates[:, 3 * h + 2:3 * h + 3] * o_win[r])
        outs.append(o[:, lo:lo + HEAD_DIM])
    return outs


LOG2E = 1.4426950408889634
SLOPES2 = [[v * LOG2E for v in row] for row in SLOPES]

POS_SPLIT = 8
F_SEL0 = 16
MASK_BIG = 1e30


def position_features(pos, blk, ns):
    n = pos.shape[0]
    lane = jnp.arange(HEAD_DIM)[None, :]
    hi = ((pos // POS_SPLIT) * POS_SPLIT).astype(f32)[:, None]
    lo = (pos % POS_SPLIT).astype(f32)[:, None]
    f = jnp.where(lane < 3, hi, jnp.where(lane < 6, lo, jnp.where(lane < 9, 1.0, 0.0)))
    if ns:
        f = jnp.where((lane >= F_SEL0) & (lane < F_SEL0 + ns), (blk[:, None] == lane - F_SEL0).astype(f32), f)
    return jnp.broadcast_to(f, (n, HEAD_DIM)).astype(f32)


def feature_tables(T, nch):
    pos = jnp.arange(T, dtype=jnp.int32)
    kf = position_features(pos, pos // SLC_BLOCK, T // SLC_BLOCK)
    cpos = jnp.arange(nch, dtype=jnp.int32) * CMP_STRIDE + (CMP_BLOCK - 1)
    cf = position_features(cpos, cpos, 0)
    z = lambda a: jnp.zeros_like(a)
    two = lambda a: jnp.stack([jnp.concatenate([z(a), a], axis=1), jnp.concatenate([a, z(a)], axis=1)]).astype(bf16)
    return two(kf), two(cf)


def _query_features(pos_row, selT, kvh, tq):
    posf = pos_row.astype(f32)
    r16 = lax.broadcasted_iota(jnp.int32, (16, tq), 0)
    zeros16 = jnp.zeros((16, tq), f32)
    selbig = (selT - 1.0) * MASK_BIG
    ns = selT.shape[0]
    pad = jnp.zeros((HEAD_DIM - F_SEL0 - ns, tq), f32)
    plain, full = [], []
    for g in range(NSA_GROUP):
        s = jnp.full((1, tq), SLOPES2[kvh][g], f32)
        s_parts = _split3(s)
        a_parts = _split3(s * posf)
        rows = [p.astype(f32) for p in s_parts] * 2 + [-p.astype(f32) for p in a_parts]
        blk0 = zeros16
        for i, rv in enumerate(rows):
            blk0 = jnp.where(r16 == i, rv, blk0)
        plain.append(jnp.concatenate([blk0, jnp.zeros((HEAD_DIM - 16, tq), f32)], axis=0).astype(bf16))
        full.append(jnp.concatenate([blk0, selbig, pad], axis=0).astype(bf16))
    return plain, full


def _aug_queries(qT, feats, kvh, tq):
    out = []
    for g in range(NSA_GROUP):
        h = kvh * NSA_GROUP + g
        qh = qT[h * HEAD_DIM:(h + 1) * HEAD_DIM, :]
        out.append(jnp.concatenate([qh, feats[g]] if kvh == 0 else [feats[g], qh], axis=0))
    return out


def _aug_keys(k_tile, feat_tile, kvh):
    lane = lax.broadcasted_iota(jnp.int32, k_tile.shape, 1)
    own = (lane < HEAD_DIM) if kvh == 0 else (lane >= HEAD_DIM)
    return jnp.where(own, k_tile, feat_tile)


def _with_ones(vT, kvh):
    half = vT.shape[0] // 2
    ones = jnp.ones((half, vT.shape[1]), vT.dtype)
    return jnp.concatenate([vT[:half], ones] if kvh == 0 else [ones, vT[half:]], axis=0)


def _cmp_branch_t(q_aug, kc_aug, vcT, pos_row, tq):
    ncp = kc_aug.shape[0]
    n_idx = lax.broadcasted_iota(jnp.int32, (ncp, tq), 0)
    cvis = pos_row >= (n_idx * CMP_STRIDE + (CMP_BLOCK - 1))
    any_vis = pos_row >= (CMP_BLOCK - 1)
    p_sum = jnp.zeros((ncp, tq), f32)
    outs = []
    for g in range(NSA_GROUP):
        sg = jnp.where(cvis, jnp.dot(kc_aug, q_aug[g], preferred_element_type=f32), NEG_INF)
        e = jnp.exp2(sg - jnp.max(sg, axis=0, keepdims=True))
        p = e / jnp.sum(e, axis=0, keepdims=True)
        p = jnp.where(any_vis, p, 0.0)
        p_sum = p_sum + p
        outs.append(jnp.dot(vcT, p.astype(bf16), preferred_element_type=f32))
    return outs, p_sum


def _select_blocks_t(p_sumT, pos_row, ns, tq):
    ncp = p_sumT.shape[0]
    cj = lax.broadcasted_iota(jnp.int32, (ns, ncp), 0)
    ci = lax.broadcasted_iota(jnp.int32, (ns, ncp), 1) * CMP_STRIDE
    c2sT = ((ci < (cj + 1) * SLC_BLOCK) & (ci + CMP_BLOCK > cj * SLC_BLOCK)).astype(f32).astype(bf16)
    hi, mid, lo = _split3(p_sumT)
    d = lambda x: jnp.dot(c2sT, x, preferred_element_type=f32)
    imp = d(hi) + d(mid) + d(lo)
    blk = lax.broadcasted_iota(jnp.int32, (ns, tq), 0)
    cur = pos_row // SLC_BLOCK
    forced = (blk == 0) | (blk == cur) | (blk == cur - 1)
    imp = jnp.where(forced, FORCED, jnp.where(blk <= cur, imp, NEG_INF))
    rank = jnp.zeros((ns, tq), jnp.int32)
    for jp in range(ns):
        row = imp[jp:jp + 1, :]
        beats = (row > imp) | ((row == imp) & (blk > jp))
        rank = rank + beats.astype(jnp.int32)
    return ((rank < SLC_TOPN) & (blk <= cur)).astype(f32)


def _nsa_prompt_kernel(qT_ref, gT_ref, sk_ref, svT_ref, wk_ref, wvT_ref, kc_ref, vcT_ref, kf_ref, cf_ref, o_ref,
                       m_s, acc_s, *, tq, kt, ns):
    qi = pl.program_id(1)
    p0 = qi * tq
    pos_row = p0 + lax.broadcasted_iota(jnp.int32, (1, tq), 1)
    gates = _sigmoid(gT_ref[...])
    qT = qT_ref[...]
    kc = kc_ref[0].astype(bf16)
    vcT = vcT_ref[0].astype(bf16)
    half = LANES // 2
    KV = range(NSA_KV_HEADS)

    o_cmp, q_pos, q_sel = [], [], []
    for kvh in KV:
        zsel = jnp.zeros((ns, tq), f32)
        plain, _ = _query_features(pos_row, zsel, kvh, tq)
        qa = _aug_queries(qT, plain, kvh, tq)
        oc, p_sum = _cmp_branch_t(qa, _aug_keys(kc, cf_ref[kvh], kvh), vcT, pos_row, tq)
        selT = _select_blocks_t(p_sum, pos_row, ns, tq)
        _, full = _query_features(pos_row, selT, kvh, tq)
        o_cmp.append(oc)
        q_pos.append(qa)
        q_sel.append(_aug_queries(qT, full, kvh, tq))

    m_s[...] = jnp.full(m_s.shape, NEG_INF, f32)
    acc_s[...] = jnp.zeros(acc_s.shape, f32)
    rel = pos_row - lax.broadcasted_iota(jnp.int32, (kt, tq), 0)

    def slc_tile(c, causal):
        k0 = pl.multiple_of(c * kt, kt)
        k_tile = sk_ref[pl.ds(k0, kt), :]
        v_tile = svT_ref[:, pl.ds(k0, kt)]
        ok = (rel - k0) >= 0
        for kvh in KV:
            ka = _aug_keys(k_tile, kf_ref[kvh, pl.ds(k0, kt), :], kvh)
            va = _with_ones(v_tile, kvh)
            for g in range(NSA_GROUP):
                cs = slice(g * tq, (g + 1) * tq)
                sg = jnp.dot(ka, q_sel[kvh][g], preferred_element_type=f32)
                if causal:
                    sg = jnp.where(ok, sg, NEG_INF)
                m_old = m_s[kvh, :, cs]
                m_new = jnp.maximum(m_old, jnp.max(sg, axis=0, keepdims=True))
                alpha = jnp.exp2(m_old - m_new)
                p = jnp.exp2(sg - m_new).astype(bf16)
                acc_s[kvh, :, cs] = alpha * acc_s[kvh, :, cs] + jnp.dot(va, p, preferred_element_type=f32)
                m_s[kvh, :, cs] = m_new

    n_kt = (p0 + tq + kt - 1) // kt

    def slc_body(c, _):
        slc_tile(c, False)
        return 0

    lax.fori_loop(0, n_kt - 1, slc_body, 0)
    slc_tile(n_kt - 1, True)

    wlen = WINDOW + tq
    w0 = pl.multiple_of(jnp.maximum(p0 - WINDOW, 0), tq)
    k_win = wk_ref[pl.ds(w0, wlen), :]
    v_win = wvT_ref[:, pl.ds(w0, wlen)]
    dist = pos_row - (w0 + lax.broadcasted_iota(jnp.int32, (wlen, tq), 0))
    valid = (dist >= 0) & (dist < WINDOW)
    heads = []
    for kvh in KV:
        ka = _aug_keys(k_win, kf_ref[kvh, pl.ds(w0, wlen), :], kvh)
        va = _with_ones(v_win, kvh)
        d0, l0 = (0, half) if kvh == 0 else (half, 0)
        for g in range(NSA_GROUP):
            h = kvh * NSA_GROUP + g
            cs = slice(g * tq, (g + 1) * tq)
            sg = jnp.where(valid, jnp.dot(ka, q_pos[kvh][g], preferred_element_type=f32), NEG_INF)
            p = jnp.exp2(sg - jnp.max(sg, axis=0, keepdims=True)).astype(bf16)
            o_win = jnp.dot(va, p, preferred_element_type=f32)
            o_slc = acc_s[kvh, :, cs]
            o = (gates[3 * h:3 * h + 1] * o_cmp[kvh][g][d0:d0 + half]
                 + gates[3 * h + 1:3 * h + 2] * (o_slc[d0:d0 + half] / o_slc[l0:l0 + 1])
                 + gates[3 * h + 2:3 * h + 3] * (o_win[d0:d0 + half] / o_win[l0:l0 + 1]))
            heads.append(o)
    o_ref[0] = jnp.concatenate(heads, axis=0).T.astype(o_ref.dtype)


def nsa_prompt(qT, gT, sk, svT, wk, wvT, kc, vcT, B, tq=128, kt=256):
    N = sk.shape[0]
    T = N // B
    ns = T // SLC_BLOCK
    nq = T // tq
    ncp = kc.shape[1]
    assert T % kt == 0 and T % tq == 0 and kt % tq == 0 and T >= WINDOW + tq and T <= POS_SPLIT * 256
    assert F_SEL0 + ns <= HEAD_DIM
    kf, cf = feature_tables(T, ncp)
    keys = pl.BlockSpec((T, LANES), lambda b, i: (b, 0))
    vals = pl.BlockSpec((LANES, T), lambda b, i: (0, b))
    return pl.pallas_call(
        functools.partial(_nsa_prompt_kernel, tq=tq, kt=kt, ns=ns),
        grid=(B, nq),
        in_specs=[pl.BlockSpec((NSA_WIDTH, tq), lambda b, i: (0, b * nq + i)),
                  pl.BlockSpec((3 * NSA_HEADS, tq), lambda b, i: (0, b * nq + i)),
                  keys, vals, keys, vals,
                  pl.BlockSpec((1, ncp, LANES), lambda b, i: (b, 0, 0)),
                  pl.BlockSpec((1, LANES, ncp), lambda b, i: (b, 0, 0)),
                  pl.BlockSpec((NSA_KV_HEADS, T, LANES), lambda b, i: (0, 0, 0)),
                  pl.BlockSpec((NSA_KV_HEADS, ncp, LANES), lambda b, i: (0, 0, 0))],
        out_specs=pl.BlockSpec((1, tq, NSA_WIDTH), lambda b, i: (b, i, 0)),
        out_shape=jax.ShapeDtypeStruct((B, T, NSA_WIDTH), bf16),
        scratch_shapes=[pltpu.VMEM((NSA_KV_HEADS, 1, NSA_GROUP * tq), f32),
                        pltpu.VMEM((NSA_KV_HEADS, LANES, NSA_GROUP * tq), f32)],
        compiler_params=_cparams("parallel", "arbitrary"),
        name="nsa_prompt",
    )(qT, gT, sk, svT, wk, wvT, kc, vcT, kf, cf)


SF_SPLIT = 64
SF_BLK0 = 16
SF_ROWS = 256


def sample_key_features(past_len, nk, tq, ns):
    sf_new0 = SF_BLK0 + -(-ns // 8) * 8
    assert sf_new0 + tq + 1 <= SF_ROWS and nk <= SF_SPLIT * 256
    pos = jnp.arange(nk, dtype=jnp.int32)[None, :]
    row = jnp.arange(SF_ROWS, dtype=jnp.int32)[:, None]
    hi = ((pos // SF_SPLIT) * SF_SPLIT).astype(f32)
    lo = (pos % SF_SPLIT).astype(f32)
    f = jnp.where(row < 3, hi, jnp.where(row < 6, lo, jnp.where(row < 9, 1.0, 0.0)))
    blk = (row >= SF_BLK0) & (row < SF_BLK0 + ns) & (pos // SLC_BLOCK == row - SF_BLK0)
    new = (row >= sf_new0) & (row < sf_new0 + tq) & (pos - past_len == row - sf_new0)
    pad = (row == sf_new0 + tq) & (pos >= past_len + tq)
    return jnp.where(blk | new | pad, 1.0, f).astype(bf16), sf_new0


def _sample_query_features(sel, pos_col, kvh, tq, ns, sf_new0):
    lane = lax.broadcasted_iota(jnp.int32, (tq, SF_ROWS), 1)
    t_idx = lax.broadcasted_iota(jnp.int32, (tq, SF_ROWS), 0)
    selbig = (pltpu.roll(sel[:, 0:SF_ROWS], SF_BLK0, 1) - 1.0) * MASK_BIG
    base = jnp.where((lane >= SF_BLK0) & (lane < SF_BLK0 + ns), selbig, 0.0)
    base = jnp.where((lane >= sf_new0) & (lane < sf_new0 + tq) & (lane - sf_new0 > t_idx), -MASK_BIG, base)
    base = jnp.where(lane == sf_new0 + tq, -MASK_BIG, base)
    posf = pos_col.astype(f32)
    out = []
    for g in range(NSA_GROUP):
        s = jnp.full((tq, 1), SLOPES[kvh][g], f32)
        cols = [p.astype(f32) for p in _split3(s)] * 2 + [-p.astype(f32) for p in _split3(s * posf)]
        qf = base
        for i, cv in enumerate(cols):
            qf = jnp.where(lane == i, cv, qf)
        out.append(qf)
    return jnp.concatenate(out, axis=0).astype(bf16)


def _new_rows_t(rows, width):
    tq = rows.shape[0]
    sq = jnp.concatenate([rows, jnp.zeros((LANES - tq, LANES), f32)], axis=0).T
    if width == LANES:
        return sq
    return jnp.concatenate([sq, jnp.zeros((LANES, width - LANES), f32)], axis=1)


def _compress_staged(xs, t, nch, pe_ref, w_ref, w2_ref):
    lhs = jnp.concatenate([jnp.concatenate([xs[2 * t + h, j] for j in range(CMP_STRIDE // 2)], axis=1)
                           for h in range(NSA_KV_HEADS)], axis=0)
    acc = jnp.dot(lhs, w_ref[...], preferred_element_type=f32)
    bc = lambda v: jnp.broadcast_to(v, (8, v.shape[1])).astype(bf16)
    c = (jnp.dot(bc(pe_ref[0:1, :]), w_ref[:, 0:CMP_HIDDEN], preferred_element_type=f32)
         + jnp.dot(bc(pe_ref[1:2, :]), w_ref[:, CMP_HIDDEN:], preferred_element_type=f32))[0:1, :]
    outs = []
    for h in range(NSA_KV_HEADS):
        r = slice(h * nch, (h + 1) * nch)
        hid = acc[r, 0:CMP_HIDDEN] + pltpu.roll(acc[r, CMP_HIDDEN:], nch - 1, 0) + c
        outs.append(jnp.dot(_silu(hid).astype(bf16), w2_ref[...], preferred_element_type=f32))
    return jnp.concatenate(outs, axis=1)


def _nsa_sample_kernel(pt_ref, *refs, pps, n_steps, tq, past_len, ns, nsp, sf_new0):
    page_refs = refs[:pps]
    (q_ref, g_ref, rn_ref, wn_ref, cw_ref, ft_ref, perm_ref, pek_ref, pev_ref, wk_ref, wv_ref, w2k_ref, w2v_ref,
     o_ref, xs_s, kT_s, vT_s) = refs[pps:]
    del pt_ref
    step = pl.program_id(1)
    cpp = PAGE_SIZE // CMP_STRIDE
    staged = []
    for i in range(pps):
        r0 = pl.multiple_of((step * pps + i) * PAGE_SIZE, PAGE_SIZE)
        pg = page_refs[i]
        slabs = []
        for th in range(4):
            x = pg[0, th]
            slabs += [x, pltpu.roll(x, PAGE_SIZE - 1, 1)]
        a = jnp.concatenate(slabs, axis=0).astype(bf16)
        staged.append(lax.dot_general(perm_ref[...], a, NT, preferred_element_type=f32))
        kT_s[:, pl.ds(r0, PAGE_SIZE)] = pg[0, 4:6].reshape(LANES, PAGE_SIZE).astype(bf16)
        vT_s[:, pl.ds(r0, PAGE_SIZE)] = pg[0, 6:8].reshape(LANES, PAGE_SIZE).astype(bf16)
    c0 = pl.multiple_of(step * (pps * cpp), pps * cpp)
    for j in range(CMP_STRIDE // 2):
        rows = slice(j * cpp, (j + 1) * cpp)
        both = jnp.concatenate([t[rows] for t in staged], axis=0).astype(bf16)
        for th in range(4):
            xs_s[th, j, pl.ds(c0, pps * cpp), :] = both[:, th * LANES:(th + 1) * LANES]

    @pl.when(step == n_steps - 1)
    def _():
        nk = kT_s.shape[1]
        rn = rn_ref[0]
        kT_s[:, past_len:] = _new_rows_t(rn[:, 2 * LANES:3 * LANES], nk - past_len).astype(bf16)
        vT_s[:, past_len:] = _new_rows_t(rn[:, 3 * LANES:4 * LANES], nk - past_len).astype(bf16)
        nwin = cw_ref.shape[2]
        wn = wn_ref[0]
        wkT = jnp.concatenate([cw_ref[0, 0:LANES, :], _new_rows_t(wn[:, 0:LANES], LANES)], axis=1).astype(bf16)
        wvT = jnp.concatenate([cw_ref[0, LANES:, :], _new_rows_t(wn[:, LANES:], LANES)], axis=1).astype(bf16)
        wlen = nwin + LANES

        nch = past_len // CMP_STRIDE
        kc = _compress_staged(xs_s, 0, nch, pek_ref, wk_ref, w2k_ref)
        vc = _compress_staged(xs_s, 1, nch, pev_ref, wv_ref, w2v_ref)

        pos_col = past_len + lax.broadcasted_iota(jnp.int32, (tq, 1), 0)
        gates = _sigmoid(g_ref[0])
        q_all = q_ref[0]
        wpos = (past_len - nwin) + lax.broadcasted_iota(jnp.int32, (tq, wlen), 1)
        wdist = pos_col - wpos
        wvalid = (wdist >= 0) & (wdist < WINDOW)
        wdistf = wdist.astype(f32)
        qms, qfs, o_cmps = [], [], []
        for kvh in range(NSA_KV_HEADS):
            qm = _padded_queries(q_all, kvh, tq)
            o_cmp, p_sum = _cmp_branch(qm, kc, vc, pos_col, kvh, tq)
            sel = _select_blocks(p_sum, pos_col, ns, nsp, tq)
            qms.append(qm)
            qfs.append(_sample_query_features(sel, pos_col, kvh, tq, ns, sf_new0))
            o_cmps.append(o_cmp)
        qm2 = jnp.concatenate(qms, axis=0)
        s = (jnp.dot(qm2, kT_s[...], preferred_element_type=f32)
             + jnp.dot(jnp.concatenate(qfs, axis=0), ft_ref[...], preferred_element_type=f32))
        e = jnp.exp(s - jnp.max(s, axis=-1, keepdims=True))
        o_slc2 = (lax.dot_general(e.astype(bf16), vT_s[...], NT, preferred_element_type=f32)
                  / jnp.sum(e, axis=-1, keepdims=True))
        sw2 = jnp.dot(qm2, wkT, preferred_element_type=f32)
        pw = []
        for kvh in range(NSA_KV_HEADS):
            for g in range(NSA_GROUP):
                r = slice((kvh * NSA_GROUP + g) * tq, (kvh * NSA_GROUP + g + 1) * tq)
                sg = jnp.where(wvalid, sw2[r] - SLOPES[kvh][g] * wdistf, NEG_INF)
                ew = jnp.where(wvalid, jnp.exp(sg - jnp.max(sg, axis=-1, keepdims=True)), 0.0)
                pw.append(ew / jnp.sum(ew, axis=-1, keepdims=True))
        o_win2 = lax.dot_general(jnp.concatenate(pw, axis=0).astype(bf16), wvT, NT, preferred_element_type=f32)
        outs = []
        for kvh in range(NSA_KV_HEADS):
            r = slice(kvh * NSA_GROUP * tq, (kvh + 1) * NSA_GROUP * tq)
            outs.extend(_combine(gates, kvh, tq, o_cmps[kvh], o_slc2[r], o_win2[r]))
        o_ref[0] = jnp.concatenate(outs, axis=1).astype(o_ref.dtype)


def nsa_sample(page_table, cacheT, q, gates, rows_new, win_new, cache_winT, cw, pps=16):
    B, tq, _ = q.shape
    n_pages = page_table.shape[1]
    past_len = n_pages * PAGE_SIZE
    assert n_pages % pps == 0 and past_len % SLC_BLOCK == 0
    n_steps = n_pages // pps
    nk = past_len + 512
    ns = past_len // SLC_BLOCK + 1
    nsp = -(-ns // LANES) * LANES
    nch = past_len // CMP_STRIDE
    nwin = cache_winT.shape[2]
    pek, wk, w2k, pev, wv, w2v = cw
    ft, sf_new0 = sample_key_features(past_len, nk, tq, ns)
    cpp = PAGE_SIZE // CMP_STRIDE
    ro = jnp.arange(PAGE_SIZE)
    ro2 = jnp.arange(PAGE_SIZE // 2)
    perm = (ro[None, :] == (ro2[:, None] % cpp) * CMP_STRIDE + 2 * (ro2[:, None] // cpp)).astype(bf16)
    per_b = lambda n: pl.BlockSpec((1, tq, n), lambda b, s, pt: (b, 0, 0))
    full = lambda a: pl.BlockSpec(a.shape, lambda b, s, pt: (0,) * a.ndim)
    page_specs = [pl.BlockSpec((1, 8, HEAD_DIM, PAGE_SIZE),
                               functools.partial(lambda b, s, pt, i: (pt[b, s * pps + i], 0, 0, 0), i=i))
                  for i in range(pps)]
    grid_spec = pltpu.PrefetchScalarGridSpec(
        num_scalar_prefetch=1,
        grid=(B, n_steps),
        in_specs=page_specs + [per_b(NSA_HEADS * LANES), per_b(3 * NSA_HEADS), per_b(4 * LANES), per_b(2 * LANES),
                               pl.BlockSpec((1, 2 * LANES, nwin), lambda b, s, pt: (b, 0, 0)),
                               full(ft), full(perm), full(pek), full(pev), full(wk), full(wv), full(w2k), full(w2v)],
        out_specs=per_b(NSA_WIDTH),
        scratch_shapes=[pltpu.VMEM((4, CMP_STRIDE // 2, nch, LANES), bf16),
                        pltpu.VMEM((LANES, nk), bf16), pltpu.VMEM((LANES, nk), bf16)])
    return pl.pallas_call(
        functools.partial(_nsa_sample_kernel, pps=pps, n_steps=n_steps, tq=tq, past_len=past_len, ns=ns, nsp=nsp,
                          sf_new0=sf_new0),
        grid_spec=grid_spec,
        out_shape=jax.ShapeDtypeStruct((B, tq, NSA_WIDTH), bf16),
        compiler_params=_cparams("parallel", "arbitrary"),
        name="nsa_sample",
    )(page_table, *([cacheT] * pps), q, gates, rows_new, win_new, cache_winT, ft, perm, pek, pev, wk, wv, w2k, w2v)


def _gdn_prep_kernel(x_ref, halo_ref, cb_ref, cw_ref, ba_ref, al_ref, dtb_ref,
                     u_ref, w_ref, qg_ref, kg_ref, qk_ref, eg_ref, xs_s, *, tb, C, t_valid):
    i = pl.program_id(1)
    HK = GDN_HEADS * GDN_DK
    R = min(tb, LANES)
    cpb = R // C

    @pl.when(i == 0)
    def _():
        xs_s[0:8, :] = cb_ref[0]

    @pl.when(i > 0)
    def _():
        xs_s[0:8, :] = halo_ref[0]

    xs_s[8:, :] = x_ref[0]
    ba = ba_ref[0]
    ri = lax.broadcasted_iota(jnp.int32, (R, R), 0)
    ci = lax.broadcasted_iota(jnp.int32, (R, R), 1)
    same = (ri // C) == (ci // C)
    lower = same & (ri >= ci)
    strict = same & (ri > ci)
    eye = (ri == ci).astype(f32)
    rr = lax.broadcasted_iota(jnp.int32, (R, LANES), 0)
    row_t = i * tb + lax.broadcasted_iota(jnp.int32, (tb, 1), 0)
    live = row_t < t_valid

    def conv_act(c0):
        acc = cw_ref[GDN_CONV - 1:GDN_CONV, c0:c0 + LANES] * xs_s[pl.ds(8, tb), c0:c0 + LANES]
        for k in range(1, GDN_CONV):
            acc = acc + cw_ref[GDN_CONV - 1 - k:GDN_CONV - k, c0:c0 + LANES] * xs_s[pl.ds(8 - k, tb), c0:c0 + LANES]
        return _silu(acc)

    chains = []
    for h in range(GDN_HEADS):
        q = conv_act(h * GDN_DK)
        k = conv_act(HK + h * GDN_DK)
        v = conv_act(2 * HK + h * GDN_DV)
        q = q * lax.rsqrt(jnp.sum(q * q, axis=-1, keepdims=True) + L2_EPS) * (GDN_DK ** -0.5)
        k = k * lax.rsqrt(jnp.sum(k * k, axis=-1, keepdims=True) + L2_EPS)
        beta = _sigmoid(ba[:, h:h + 1])
        ar = ba[:, GDN_HEADS + h:GDN_HEADS + h + 1] + dtb_ref[:, h:h + 1]
        softplus = jnp.maximum(ar, 0.0) + jnp.log(1.0 + jnp.exp(-jnp.abs(ar)))
        gcol = -jnp.exp(al_ref[:, h:h + 1]) * softplus
        if t_valid < 10 ** 9:
            beta = jnp.where(live, beta, 0.0)
            gcol = jnp.where(live, gcol, 0.0)
            k = jnp.where(live, k, 0.0)
            v = jnp.where(live, v, 0.0)
        for blk in range(tb // R):
            r = slice(blk * R, (blk + 1) * R)
            qc, kc, vc, bc = q[r], k[r], v[r], beta[r]
            gc = jnp.broadcast_to(gcol[r], (R, LANES))
            sh = 1
            while sh < C:
                gc = gc + jnp.where((rr % C) >= sh, pltpu.roll(gc, sh, 0), 0.0)
                sh *= 2
            gct = gc.T
            decay = jnp.exp(jnp.where(lower, gc[:, 0:R] - gct[0:R, :], NEG_INF))
            kb = kc * bc
            a = jnp.where(strict, _dot_hp(kb, kc, NT) * decay, 0.0)
            qk = jnp.where(lower, lax.dot_general(qc.astype(bf16), kc.astype(bf16), NT,
                                                  preferred_element_type=f32) * decay, 0.0)
            gl = gc[R - 1:R, :]
            for j in range(cpb - 2, -1, -1):
                gl = jnp.where(rr < (j + 1) * C, gc[(j + 1) * C - 1:(j + 1) * C, :], gl)
            chains.append(dict(h=h, r=r, blk=blk, qc=qc, kc=kc, gc=gc, gl=gl, qk=qk, n=-a, t=eye - a,
                               rhs=jnp.concatenate([vc * bc, kb * jnp.exp(gc)], axis=1)))

    for ch in chains:
        ch['n'] = _split2(ch['n'])
    sh = 2
    while sh < C:
        for ch in chains:
            ch['n'] = _split2(_dot_parts(ch['n'], ch['n']))
        for ch in chains:
            ch['t'] = ch['t'] + _dot_parts(_split2(ch['t']), ch['n'])
        sh *= 2

    for ch in chains:
        h, r, gc, gl = ch['h'], ch['r'], ch['gc'], ch['gl']
        sol = ch['rhs'] + _dot_hp(ch['t'] - eye, ch['rhs'])
        cs = slice(h * GDN_DV, (h + 1) * GDN_DV)
        u_ref[0, r, cs] = sol[:, 0:GDN_DV]
        w_ref[0, r, cs] = sol[:, GDN_DV:].astype(w_ref.dtype)
        qg_ref[0, r, cs] = (ch['qc'] * jnp.exp(gc)).astype(qg_ref.dtype)
        kg_ref[0, r, cs] = (ch['kc'] * jnp.exp(gl - gc)).astype(kg_ref.dtype)
        qk = ch['qk'][:, 0:C]
        for j in range(1, cpb):
            qk = jnp.where(ri[:, 0:C] >= j * C, ch['qk'][:, j * C:(j + 1) * C], qk)
        qk_ref[0, r, h * C:(h + 1) * C] = qk.astype(qk_ref.dtype)
        for j in range(cpb):
            eg_ref[0, ch['blk'] * cpb + j, h:h + 1, :] = jnp.exp(gc[(j + 1) * C - 1:(j + 1) * C, :])
    eg_ref[0, :, GDN_HEADS:, :] = jnp.zeros((tb // C, 8 - GDN_HEADS, LANES), f32)


def gdn_prep(qkv, conv_buf8, conv_w, ba, a_log, dt_bias, C, tb, t_valid):
    B, T, CH = qkv.shape
    nb = T // tb
    hb = tb // 8
    per = lambda n, dt: (pl.BlockSpec((1, tb, n), lambda b, i: (b, i, 0)), jax.ShapeDtypeStruct((B, T, n), dt))
    outs = [per(GDN_WIDTH, f32), per(GDN_WIDTH, bf16), per(GDN_WIDTH, bf16), per(GDN_WIDTH, bf16),
            per(GDN_HEADS * C, bf16),
            (pl.BlockSpec((1, tb // C, 8, LANES), lambda b, i: (b, i, 0, 0)),
             jax.ShapeDtypeStruct((B, T // C, 8, LANES), f32))]
    return pl.pallas_call(
        functools.partial(_gdn_prep_kernel, tb=tb, C=C, t_valid=t_valid),
        grid=(B, nb),
        in_specs=[pl.BlockSpec((1, tb, CH), lambda b, i: (b, i, 0)),
                  pl.BlockSpec((1, 8, CH), lambda b, i: (b, jnp.maximum(i * hb - 1, 0), 0)),
                  pl.BlockSpec((1, 8, CH), lambda b, i: (b, 0, 0)),
                  pl.BlockSpec((GDN_CONV, CH), lambda b, i: (0, 0)),
                  pl.BlockSpec((1, tb, 2 * GDN_HEADS), lambda b, i: (b, i, 0)),
                  pl.BlockSpec((1, GDN_HEADS), lambda b, i: (0, 0)),
                  pl.BlockSpec((1, GDN_HEADS), lambda b, i: (0, 0))],
        out_specs=[o[0] for o in outs],
        out_shape=[o[1] for o in outs],
        scratch_shapes=[pltpu.VMEM((tb + 8, CH), f32)],
        compiler_params=_cparams("parallel", "arbitrary"),
        name="gdn_prep",
    )(qkv, qkv, conv_buf8, conv_w, ba, a_log.reshape(1, GDN_HEADS), dt_bias.reshape(1, GDN_HEADS))


def _gdn_scan_kernel(u_ref, w_ref, qg_ref, kg_ref, qk_ref, eg_ref, z_ref, ng_ref, s0_ref,
                     o_ref, sT_ref, s_s, *, C, nb):
    c = pl.program_id(1)

    @pl.when(c == 0)
    def _():
        s_s[...] = s0_ref[...]

    ch = [(b, h) for b in range(nb) for h in range(GDN_HEADS)]
    cs = [slice(h * GDN_DV, (h + 1) * GDN_DV) for h in range(GDN_HEADS)]
    S = [s_s[b, h] for b, h in ch]
    Sb = [x.astype(bf16) for x in S]
    v_new = [u_ref[b, :, cs[h]] - jnp.dot(w_ref[b, :, cs[h]], Sb[i], preferred_element_type=f32)
             for i, (b, h) in enumerate(ch)]
    vb = [x.astype(bf16) for x in v_new]
    for i, (b, h) in enumerate(ch):
        s_s[b, h] = S[i] * eg_ref[b, 0, h:h + 1, :] + lax.dot_general(kg_ref[b, :, cs[h]], vb[i], TN,
                                                                       preferred_element_type=f32)
    o = [jnp.dot(qg_ref[b, :, cs[h]], Sb[i], preferred_element_type=f32)
         + jnp.dot(qk_ref[b, :, h * C:(h + 1) * C], vb[i], preferred_element_type=f32) for i, (b, h) in enumerate(ch)]
    outs = [_rms(o[i], ng_ref[...]) * _silu(z_ref[b, :, cs[h]]) for i, (b, h) in enumerate(ch)]
    for b in range(nb):
        o_ref[b] = jnp.concatenate(outs[b * GDN_HEADS:(b + 1) * GDN_HEADS], axis=1).astype(o_ref.dtype)

    @pl.when(c == pl.num_programs(1) - 1)
    def _():
        sT_ref[...] = s_s[...]


def gdn_scan(u, w, qg, kg, qk, eg, z, norm_g, s0, C, nb=8):
    B, T, _ = u.shape
    nb = math.gcd(nb, B)
    per = lambda n: pl.BlockSpec((nb, C, n), lambda b, c: (b, c, 0))
    st = pl.BlockSpec((nb, GDN_HEADS, GDN_DK, GDN_DV), lambda b, c: (b, 0, 0, 0))
    return pl.pallas_call(
        functools.partial(_gdn_scan_kernel, C=C, nb=nb),
        grid=(B // nb, T // C),
        in_specs=[per(GDN_WIDTH), per(GDN_WIDTH), per(GDN_WIDTH), per(GDN_WIDTH), per(GDN_HEADS * C),
                  pl.BlockSpec((nb, 1, 8, LANES), lambda b, c: (b, c, 0, 0)),
                  per(GDN_WIDTH), pl.BlockSpec((1, GDN_DV), lambda b, c: (0, 0)), st],
        out_specs=[per(GDN_WIDTH), st],
        out_shape=[jax.ShapeDtypeStruct((B, T, GDN_WIDTH), bf16),
                   jax.ShapeDtypeStruct((B, GDN_HEADS, GDN_DK, GDN_DV), f32)],
        scratch_shapes=[pltpu.VMEM((nb, GDN_HEADS, GDN_DK, GDN_DV), f32)],
        compiler_params=_cparams("parallel", "arbitrary"),
        name="gdn_scan",
    )(u, w, qg, kg, qk, eg, z, norm_g.reshape(1, GDN_DV), s0)


def _nsa_in_groups(w, feature_major):
    c1 = NSA_WIDTH
    c2 = c1 + 4 * NSA_KV_COLS
    c3 = c2 + 2 * NSA_KV_COLS
    c4 = c3 + 3 * NSA_HEADS
    wT = w.T.astype(bf16)
    xa_scale = XA_DIM ** -0.5 * LOG2E
    if feature_major:
        k0 = c1 + 2 * NSA_KV_COLS
        return [(wT[c1:c2], f32, 1.0, FEATURE_MAJOR_PER_BATCH),
                (wT[c1:k0], f32, 1.0, TOKEN_MAJOR_WT),
                (wT[c2:c3], f32, 1.0, FEATURE_MAJOR_PER_BATCH),
                (wT[c4:], bf16, xa_scale, FEATURE_MAJOR),
                (wT[:c1], bf16, HEAD_DIM ** -0.5 * LOG2E, FEATURE_MAJOR),
                (wT[c3:c4], f32, 1.0, FEATURE_MAJOR),
                (wT[k0:k0 + NSA_KV_COLS], bf16, 1.0, TOKEN_MAJOR_WT),
                (wT[k0 + NSA_KV_COLS:c2], bf16, 1.0, FEATURE_MAJOR),
                (wT[c2:c2 + NSA_KV_COLS], bf16, 1.0, TOKEN_MAJOR_WT),
                (wT[c2 + NSA_KV_COLS:c3], bf16, 1.0, FEATURE_MAJOR)]
    wq = wT[:c1].reshape(NSA_KV_HEADS, NSA_GROUP, HEAD_DIM, -1)
    zq = jnp.zeros_like(wq[0])
    wq = jnp.concatenate([jnp.concatenate([wq[0], zq], axis=1), jnp.concatenate([zq, wq[1]], axis=1)], axis=0)
    wq = wq.reshape(NSA_HEADS * LANES, -1)
    return [(wT[c1:c2], f32, 1.0, TOKEN_MAJOR_WT), (wT[c2:c3], f32, 1.0, TOKEN_MAJOR_WT),
            (wT[c4:], bf16, xa_scale, TOKEN_MAJOR_WT), (wq, bf16, HEAD_DIM ** -0.5, TOKEN_MAJOR_WT),
            (wT[c3:c4], f32, 1.0, TOKEN_MAJOR_WT)]


def _gdn_in_groups(w, feature_major):
    c1 = GDN_CONV_CH
    c2 = c1 + GDN_WIDTH
    c3 = c2 + 2 * GDN_HEADS
    wT = w.T.astype(bf16)
    xq = (wT[c3:], bf16, XA_DIM ** -0.5 * LOG2E, FEATURE_MAJOR if feature_major else TOKEN_MAJOR_WT)
    return [(wT[:c1], f32, 1.0, TOKEN_MAJOR_WT), (wT[c1:c2], f32, 1.0, TOKEN_MAJOR_WT),
            (wT[c2:c3], f32, 1.0, TOKEN_MAJOR_WT), xq]


def _trunk(x, mem_kv, nsa_past, gdn_state, p, t_valid):
    B, T, D = x.shape
    N = B * T
    xf = x.reshape(N, D)

    prompt = nsa_past is None
    if prompt:
        kv_rows, rows_cmp, win_rows, xq, qT, gT, sk, svT, wk, wvT = norm_proj(
            xf, p['norm_mix_g'][0], p['nsa_in_prompt'], tm=512, rows_per_batch=T)
        kc, vcT = cmp_prompt(rows_cmp.reshape(B, T, -1), p['cmp'])
        o_mix = nsa_prompt(qT, gT, sk, svT, wk, wvT, kc, vcT, B)
    else:
        rows, win, xq, q, gates = norm_proj(xf, p['norm_mix_g'][0], p['nsa_in_sample'])
        kv_rows = rows.reshape(B, T, -1)
        win_rows = win.reshape(B, T, -1)
        xq = xq.reshape(B, T, -1)
        page_table, cache, cache_win = nsa_past
        o_mix = nsa_sample(page_table, cache, q.reshape(B, T, -1), gates.reshape(B, T, -1), kv_rows, win_rows,
                           cache_win, p['cmp_dense'])
    o_mem = cross_attend(xq, mem_kv, 0, B)
    xf = post_block(xf, o_mix.reshape(N, -1), o_mem.reshape(N, -1), p['w_out'][0], p['norm_ffn_g'][0],
                    p['w_up'][0], p['w_down'][0], p['final_norm_g'], final=False)

    qkv, z, ba, xq = norm_proj(xf, p['norm_mix_g'][1], p['gdn_in_prompt' if prompt else 'gdn_in_sample'], tm=512)
    if not prompt:
        xq = xq.reshape(B, T, -1)
    qkv3 = qkv.reshape(B, T, -1)
    S0, conv_buf8 = gdn_state
    C = min(64, T)
    tb = min(256, T)
    u, w, qg, kg, qk, eg = gdn_prep(qkv3, conv_buf8, p['gdn_conv_w'], ba.reshape(B, T, -1), p['gdn_a_log'],
                                    p['gdn_dt_bias'], C, tb, t_valid)
    o_mix, S = gdn_scan(u, w, qg, kg, qk, eg, z.reshape(B, T, -1), p['gdn_norm_g'], S0, C)
    o_mem = cross_attend(xq, mem_kv, 1, B)
    y = post_block(xf, o_mix.reshape(N, -1), o_mem.reshape(N, -1), p['w_out'][1], p['norm_ffn_g'][1],
                   p['w_up'][1], p['w_down'][1], p['final_norm_g'], final=True)
    return y.reshape(B, T, D), kv_rows, win_rows, S, qkv3


def kernel(x_prompt, x_sample, mem_prompt, cache_nsa_kv, cache_nsa_win, state_gdn_s, state_gdn_conv,
           cache_mem_kv, page_table, norm_mix_g, norm_mem_g, w_mem_kv, nsa_w_in, cmp_pe_k, cmp_w1_k,
           cmp_w2_k, cmp_pe_v, cmp_w1_v, cmp_w2_v, gdn_w_in, gdn_conv_w, gdn_a_log, gdn_dt_bias,
           gdn_norm_g, w_out, norm_ffn_g, w_up, w_down, final_norm_g):
    B, T, D = x_prompt.shape
    DB, TS, _ = x_sample.shape
    M = mem_prompt.shape[1]
    p = dict(norm_mix_g=norm_mix_g, nsa_in_prompt=_nsa_in_groups(nsa_w_in[0], True),
             nsa_in_sample=_nsa_in_groups(nsa_w_in[0], False), gdn_in_prompt=_gdn_in_groups(gdn_w_in[0], True),
             gdn_in_sample=_gdn_in_groups(gdn_w_in[0], False),
             cmp=_cmp_weights(cmp_pe_k[0], cmp_w1_k[0], cmp_w2_k[0]) + _cmp_weights(cmp_pe_v[0], cmp_w1_v[0], cmp_w2_v[0]),
             cmp_dense=(_cmp_weights_dense(cmp_pe_k[0], cmp_w1_k[0], cmp_w2_k[0])
                        + _cmp_weights_dense(cmp_pe_v[0], cmp_w1_v[0], cmp_w2_v[0])),
             gdn_conv_w=gdn_conv_w[0], gdn_a_log=gdn_a_log[0], gdn_dt_bias=gdn_dt_bias[0], gdn_norm_g=gdn_norm_g[0],
             w_out=w_out.astype(bf16), norm_ffn_g=norm_ffn_g, w_up=w_up.astype(bf16), w_down=w_down.astype(bf16),
             final_norm_g=final_norm_g)

    mem_flat = mem_prompt.reshape(B * M, D)
    mem_kv = [norm_proj(mem_flat, norm_mem_g[i], [(w_mem_kv[i].astype(bf16).T, f32, 1.0, FEATURE_MAJOR_PER_BATCH)],
                        tm=M, rows_per_batch=M)[0] for i in range(DEPTH)]
    mem_kv_p = jnp.stack(mem_kv, axis=1)
    zero_state = (jnp.zeros((B, GDN_HEADS, GDN_DK, GDN_DV), f32), jnp.zeros((B, 8, GDN_CONV_CH), f32))
    y_p, rows_p, win_p, S_p, qkv_p = _trunk(x_prompt, mem_kv_p, None, zero_state, p, 10 ** 9)

    TP = 8
    x_s = jnp.pad(x_sample, ((0, 0), (0, TP - TS), (0, 0)))
    cache = cache_nsa_kv.transpose(0, 2, 3, 4, 5, 1).reshape(cache_nsa_kv.shape[0], -1, HEAD_DIM, PAGE_SIZE)
    nwin = cache_nsa_win.shape[2]
    cache_win = cache_nsa_win.transpose(0, 1, 3, 4, 5, 2).reshape(DB, -1, nwin)
    conv8 = jnp.pad(state_gdn_conv[:, 0], ((0, 0), (8 - (GDN_CONV - 1), 0), (0, 0)))
    mem_kv_s = cache_mem_kv.transpose(0, 1, 3, 4, 5, 2).reshape(DB, DEPTH, 2 * XA_WIDTH, M)
    y_s, rows_s, win_s, S_s, qkv_s = _trunk(x_s, mem_kv_s, (page_table, cache, cache_win),
                                            (state_gdn_s[:, 0], conv8), p, TS)

    kvshape = (4, NSA_KV_HEADS, HEAD_DIM)
    nsa_kv_prompt = rows_p.reshape(B, 1, *kvshape, T).transpose(0, 5, 1, 2, 3, 4)
    nsa_kv_sample = rows_s[:, :TS].reshape(DB, TS, 1, *kvshape)
    wshape = (2, NSA_KV_HEADS, HEAD_DIM)
    wkeep = min(WINDOW, T)
    nsa_win_prompt = win_p[:, :, T - wkeep:].reshape(B, 1, *wshape, wkeep).transpose(0, 1, 5, 2, 3, 4)
    win_cat = jnp.concatenate([cache_win, win_s[:, :TS].transpose(0, 2, 1)], axis=2)[:, :, -WINDOW:]
    nsa_win_sample = win_cat.reshape(DB, 1, *wshape, -1).transpose(0, 1, 5, 2, 3, 4)
    gdn_conv_prompt = qkv_p[:, None, T - (GDN_CONV - 1):]
    conv_cat = jnp.concatenate([state_gdn_conv[:, 0], qkv_s[:, :TS]], axis=1)
    gdn_conv_sample = conv_cat[:, None, -(GDN_CONV - 1):]
    mem_kv_prompt = mem_kv_p.reshape(B, DEPTH, 2, XA_HEADS, XA_DIM, M).transpose(0, 1, 5, 2, 3, 4)
    return (y_p, y_s[:, :TS], nsa_kv_prompt, nsa_kv_sample, nsa_win_prompt, nsa_win_sample,
            S_p[:, None], S_s[:, None], gdn_conv_prompt, gdn_conv_sample, mem_kv_prompt)
```

```python
import functools
import math

import jax
import jax.numpy as jnp
from jax import lax
from jax.experimental import pallas as pl
from jax.experimental.pallas import tpu as pltpu

f32 = jnp.float32
bf16 = jnp.bfloat16

D_MODEL = 1024
DEPTH = 2
PAGE_SIZE = 128
HEAD_DIM = 64
NSA_HEADS = 12
NSA_KV_HEADS = 2
NSA_GROUP = NSA_HEADS // NSA_KV_HEADS
NSA_WIDTH = NSA_HEADS * HEAD_DIM
NSA_KV_COLS = NSA_KV_HEADS * HEAD_DIM
CMP_STRIDE = 16
CMP_BLOCK = 2 * CMP_STRIDE
CMP_HIDDEN = 128
SLC_BLOCK = 64
SLC_TOPN = 16
WINDOW = 512
GDN_HEADS = 6
GDN_DK = 128
GDN_DV = 128
GDN_WIDTH = GDN_HEADS * GDN_DV
GDN_CONV = 4
GDN_CONV_CH = 2 * GDN_HEADS * GDN_DK + GDN_HEADS * GDN_DV
XA_HEADS = 4
XA_DIM = 64
XA_WIDTH = XA_HEADS * XA_DIM
D_FF = 4 * D_MODEL
NORM_EPS = 1e-6
L2_EPS = 1e-6
NEG_INF = -1e30
FORCED = 1e9

VMEM_LIMIT = 52 * 1024 * 1024
LANES = 128

SLOPES = [[2.0 ** (-8.0 * (k * NSA_GROUP + g + 1) / NSA_HEADS) for g in range(NSA_GROUP)]
          for k in range(NSA_KV_HEADS)]

NT = (((1,), (1,)), ((), ()))
TN = (((0,), (0,)), ((), ()))


def _cparams(*sem):
    return pltpu.CompilerParams(dimension_semantics=sem, vmem_limit_bytes=VMEM_LIMIT)


def _rms(x, g):
    r = lax.rsqrt(jnp.mean(x * x, axis=-1, keepdims=True) + NORM_EPS)
    return (x * r) * g


def _sigmoid(x):
    return 1.0 / (1.0 + jnp.exp(-x))


def _silu(x):
    return x * _sigmoid(x)


def _split3(a):
    hi = a.astype(bf16)
    r1 = a - hi.astype(f32)
    mid = r1.astype(bf16)
    lo = (r1 - mid.astype(f32)).astype(bf16)
    return hi, mid, lo


def _dot_exact_rhs(a, b_bf16):
    hi, mid, lo = _split3(a)
    d = lambda x: jnp.dot(x, b_bf16, preferred_element_type=f32)
    return d(hi) + d(mid) + d(lo)


def _split2(a):
    hi = a.astype(bf16)
    return hi, (a - hi.astype(f32)).astype(bf16)


def _dot_parts(a, b, dims=None):
    if dims is None:
        d = lambda x, y: jnp.dot(x, y, preferred_element_type=f32)
    else:
        d = lambda x, y: lax.dot_general(x, y, dims, preferred_element_type=f32)
    return d(a[0], b[0]) + (d(a[0], b[1]) + d(a[1], b[0]))


def _dot_hp(a, b, dims=None):
    return _dot_parts(_split2(a), _split2(b), dims)


TOKEN_MAJOR, TOKEN_MAJOR_WT, FEATURE_MAJOR, FEATURE_MAJOR_PER_BATCH = 'nt', 'ntw', 'fm', 'fmb'


def _norm_proj_kernel(x_ref, g_ref, *refs, specs):
    n = len(specs)
    w_refs, o_refs = refs[:n], refs[n:]
    h = _rms(x_ref[...], g_ref[...]).astype(bf16)
    for (scale, kind), w_ref, o_ref in zip(specs, w_refs, o_refs):
        if kind == TOKEN_MAJOR:
            y = jnp.dot(h, w_ref[...], preferred_element_type=f32)
        elif kind == TOKEN_MAJOR_WT:
            y = lax.dot_general(h, w_ref[...], NT, preferred_element_type=f32)
        else:
            y = lax.dot_general(w_ref[...], h, NT, preferred_element_type=f32)
        if scale != 1.0:
            y = y * scale
        o_ref[...] = y.astype(o_ref.dtype).reshape(o_ref.shape)


def norm_proj(x, g, groups, tm=256, rows_per_batch=None):
    N, D = x.shape
    tm = min(tm, N)
    assert N % tm == 0
    in_specs = [pl.BlockSpec((tm, D), lambda i: (i, 0)), pl.BlockSpec((1, D), lambda i: (0, 0))]
    out_specs, out_shape = [], []
    for w, dt, _, kind in groups:
        in_specs.append(pl.BlockSpec(w.shape, lambda i: (0, 0)))
        if kind in (TOKEN_MAJOR, TOKEN_MAJOR_WT):
            n = w.shape[1] if kind == TOKEN_MAJOR else w.shape[0]
            out_specs.append(pl.BlockSpec((tm, n), lambda i: (i, 0)))
            out_shape.append(jax.ShapeDtypeStruct((N, n), dt))
        elif kind == FEATURE_MAJOR:
            n = w.shape[0]
            out_specs.append(pl.BlockSpec((n, tm), lambda i: (0, i)))
            out_shape.append(jax.ShapeDtypeStruct((n, N), dt))
        else:
            n = w.shape[0]
            per = rows_per_batch // tm
            assert rows_per_batch % tm == 0
            out_specs.append(pl.BlockSpec((1, n, tm), lambda i: (i // per, 0, i % per)))
            out_shape.append(jax.ShapeDtypeStruct((N // rows_per_batch, n, rows_per_batch), dt))
    return pl.pallas_call(
        functools.partial(_norm_proj_kernel, specs=tuple((s, t) for _, _, s, t in groups)),
        grid=(N // tm,),
        in_specs=in_specs,
        out_specs=out_specs,
        out_shape=out_shape,
        compiler_params=_cparams("parallel"),
        name="norm_proj",
    )(x, g.reshape(1, D), *[w for w, _, _, _ in groups])


def _xattn_kernel(q_ref, kv_ref, o_ref, *, q_feature_major, nb):
    ch = [(b, h) for b in range(nb) for h in range(XA_HEADS)]
    hs = lambda h, off=0: slice(off + h * XA_DIM, off + (h + 1) * XA_DIM)
    qT = [q_ref[...] if q_feature_major else q_ref[b].astype(f32).T.astype(bf16) for b in range(nb)]
    s = [lax.dot_general(kv_ref[b, 0, hs(h), :].astype(bf16), qT[b][hs(h)], TN, preferred_element_type=f32)
         for b, h in ch]
    e = [jnp.exp2(x - jnp.max(x, axis=0, keepdims=True)) for x in s]
    o = [jnp.dot(kv_ref[b, 0, hs(h, XA_WIDTH), :].astype(bf16), e[i].astype(bf16), preferred_element_type=f32)
         / jnp.sum(e[i], axis=0, keepdims=True) for i, (b, h) in enumerate(ch)]
    for b in range(nb):
        o_ref[b] = jnp.concatenate(o[b * XA_HEADS:(b + 1) * XA_HEADS], axis=0).T.astype(o_ref.dtype)


def cross_attend(xq, kvT, layer, B, tq=512, nb=4):
    q_feature_major = xq.ndim == 2
    T = xq.shape[1] // B if q_feature_major else xq.shape[1]
    M = kvT.shape[3]
    tq = min(tq, T)
    nq = T // tq
    if q_feature_major:
        nb = 1
        q_spec = pl.BlockSpec((XA_WIDTH, tq), lambda b, i: (0, b * nq + i))
    else:
        nb = math.gcd(nb, B)
        q_spec = pl.BlockSpec((nb, tq, XA_WIDTH), lambda b, i: (b, i, 0))
    return pl.pallas_call(
        functools.partial(_xattn_kernel, q_feature_major=q_feature_major, nb=nb),
        grid=(B // nb, nq),
        in_specs=[q_spec, pl.BlockSpec((nb, 1, 2 * XA_WIDTH, M), lambda b, i: (b, layer, 0, 0))],
        out_specs=pl.BlockSpec((nb, tq, XA_WIDTH), lambda b, i: (b, i, 0)),
        out_shape=jax.ShapeDtypeStruct((B, T, XA_WIDTH), bf16),
        compiler_params=_cparams("parallel", "parallel"),
        name="cross_attend",
    )(xq, kvT)


def _post_kernel(x_ref, om_ref, oc_ref, wo_ref, gf_ref, wu_ref, wd_ref, gl_ref, o_ref,
                 x1_s, h_s, acc_s, *, mix_w, final):
    j = pl.program_id(1)

    @pl.when(j == 0)
    def _():
        x1 = x_ref[...] + (jnp.dot(om_ref[...], wo_ref[0:mix_w, :], preferred_element_type=f32)
                           + jnp.dot(oc_ref[...], wo_ref[mix_w:, :], preferred_element_type=f32))
        x1_s[...] = x1
        h_s[...] = _rms(x1, gf_ref[...]).astype(bf16)
        acc_s[...] = jnp.zeros_like(acc_s)

    u = jnp.dot(h_s[...], wu_ref[...], preferred_element_type=f32)
    u = jnp.square(jnp.maximum(u, 0.0)).astype(bf16)
    acc_s[...] += jnp.dot(u, wd_ref[...], preferred_element_type=f32)

    @pl.when(j == pl.num_programs(1) - 1)
    def _():
        x2 = x1_s[...] + acc_s[...]
        if final:
            x2 = _rms(x2, gl_ref[...])
        o_ref[...] = x2


def post_block(x, o_mix, o_mem, w_out, g_ffn, w_up, w_down, g_final, final, tm=1024, tf=512):
    N, D = x.shape
    mix_w = o_mix.shape[1]
    F = w_up.shape[1]
    tm = min(tm, N)
    return pl.pallas_call(
        functools.partial(_post_kernel, mix_w=mix_w, final=final),
        grid=(N // tm, F // tf),
        in_specs=[pl.BlockSpec((tm, D), lambda i, j: (i, 0)),
                  pl.BlockSpec((tm, mix_w), lambda i, j: (i, 0)),
                  pl.BlockSpec((tm, XA_WIDTH), lambda i, j: (i, 0)),
                  pl.BlockSpec((mix_w + XA_WIDTH, D), lambda i, j: (0, 0)),
                  pl.BlockSpec((1, D), lambda i, j: (0, 0)),
                  pl.BlockSpec((D, tf), lambda i, j: (0, j)),
                  pl.BlockSpec((tf, D), lambda i, j: (j, 0)),
                  pl.BlockSpec((1, D), lambda i, j: (0, 0))],
        out_specs=pl.BlockSpec((tm, D), lambda i, j: (i, 0)),
        out_shape=jax.ShapeDtypeStruct((N, D), f32),
        scratch_shapes=[pltpu.VMEM((tm, D), f32), pltpu.VMEM((tm, D), bf16), pltpu.VMEM((tm, D), f32)],
        compiler_params=_cparams("parallel", "arbitrary"),
        name="post_block",
    )(x, o_mix, o_mem, w_out, g_ffn.reshape(1, D), w_up, w_down, g_final.reshape(1, D))


def _compress(load_chunk_rows, nch, pe_ref, w_ref, w2_ref):
    acc_p = jnp.zeros((nch, 2 * CMP_HIDDEN), f32)
    acc_q = jnp.zeros((nch, 2 * CMP_HIDDEN), f32)
    for l in range(CMP_STRIDE):
        x = load_chunk_rows(l)
        xp = (x + pe_ref[l:l + 1, :]).astype(bf16)
        xq = (x + pe_ref[CMP_STRIDE + l:CMP_STRIDE + l + 1, :]).astype(bf16)
        acc_p = acc_p + jnp.dot(xp, w_ref[l, :, 0:2 * CMP_HIDDEN], preferred_element_type=f32)
        acc_q = acc_q + jnp.dot(xq, w_ref[l, :, 2 * CMP_HIDDEN:], preferred_element_type=f32)
    hid = _silu(acc_p + pltpu.roll(acc_q, nch - 1, 0)).astype(bf16)
    outs = [jnp.dot(hid[:, h * CMP_HIDDEN:(h + 1) * CMP_HIDDEN], w2_ref[...], preferred_element_type=f32)
            for h in range(NSA_KV_HEADS)]
    return jnp.concatenate(outs, axis=1)


def _cmp_prompt_kernel(xk_ref, xv_ref, pek_ref, pev_ref, wk_ref, wv_ref, w2k_ref, w2v_ref, kc_ref, vc_ref, *, nch):
    kc_ref[0] = _compress(lambda l: xk_ref[0, pl.ds(l, nch, stride=CMP_STRIDE), :], nch, pek_ref, wk_ref, w2k_ref)
    vc_ref[0] = _compress(lambda l: xv_ref[0, pl.ds(l, nch, stride=CMP_STRIDE), :], nch, pev_ref, wv_ref, w2v_ref).T


def _cmp_weights(pe, w1, w2):
    pe2 = jnp.tile(pe, (1, NSA_KV_HEADS))
    z = jnp.zeros((CMP_STRIDE, HEAD_DIM, CMP_HIDDEN), w1.dtype)
    a, b = w1[:CMP_STRIDE], w1[CMP_STRIDE:]
    top = jnp.concatenate([a, z, b, z], axis=2)
    bot = jnp.concatenate([z, a, z, b], axis=2)
    wbd = jnp.concatenate([top, bot], axis=1).astype(bf16)
    return pe2, wbd, w2.astype(bf16)


def _cmp_weights_dense(pe, w1, w2):
    half = CMP_STRIDE * HEAD_DIM
    w = jnp.concatenate([w1[:CMP_STRIDE].reshape(half, CMP_HIDDEN), w1[CMP_STRIDE:].reshape(half, CMP_HIDDEN)], axis=1)
    return pe.reshape(2, half), w.astype(bf16), w2.astype(bf16)


def cmp_prompt(rows, cw):
    B, T, _ = rows.shape
    nch = T // CMP_STRIDE
    pek, wk, w2k, pev, wv, w2v = cw
    full = lambda a: pl.BlockSpec(a.shape, lambda b: (0,) * a.ndim)
    return pl.pallas_call(
        functools.partial(_cmp_prompt_kernel, nch=nch),
        grid=(B,),
        in_specs=[pl.BlockSpec((1, T, LANES), lambda b: (b, 0, 0)),
                  pl.BlockSpec((1, T, LANES), lambda b: (b, 0, 1)),
                  full(pek), full(pev), full(wk), full(wv), full(w2k), full(w2v)],
        out_specs=[pl.BlockSpec((1, nch, LANES), lambda b: (b, 0, 0)), pl.BlockSpec((1, LANES, nch), lambda b: (b, 0, 0))],
        out_shape=[jax.ShapeDtypeStruct((B, nch, LANES), f32), jax.ShapeDtypeStruct((B, LANES, nch), f32)],
        compiler_params=_cparams("parallel"),
        name="cmp_prompt",
    )(rows, rows, pek, pev, wk, wv, w2k, w2v)


def _padded_queries(q_all, kvh, tq):
    del tq
    lo = kvh * NSA_GROUP
    return jnp.concatenate([q_all[:, (lo + g) * LANES:(lo + g + 1) * LANES] for g in range(NSA_GROUP)], axis=0)


def _cmp_branch(qm, kc, vc, pos_col, kvh, tq):
    ncp = kc.shape[0]
    s = lax.dot_general(qm, kc.astype(bf16), NT, preferred_element_type=f32)
    n_idx = lax.broadcasted_iota(jnp.int32, (tq, ncp), 1)
    cdist = pos_col - (n_idx * CMP_STRIDE + (CMP_BLOCK - 1))
    cvis = cdist >= 0
    any_vis = pos_col >= (CMP_BLOCK - 1)
    cdf = cdist.astype(f32)
    p_sum = jnp.zeros((tq, ncp), f32)
    ps = []
    for g in range(NSA_GROUP):
        sg = jnp.where(cvis, s[g * tq:(g + 1) * tq] - SLOPES[kvh][g] * cdf, NEG_INF)
        m = jnp.max(sg, axis=-1, keepdims=True)
        e = jnp.exp(sg - m)
        p = e / jnp.sum(e, axis=-1, keepdims=True)
        p = jnp.where(any_vis, p, 0.0)
        p_sum = p_sum + p
        ps.append(p)
    p_all = jnp.concatenate(ps, axis=0).astype(bf16)
    o = jnp.dot(p_all, vc.astype(bf16), preferred_element_type=f32)
    return o, p_sum


def _select_blocks(p_sum, pos_col, ns, nsp, tq):
    ncp = p_sum.shape[1]
    ci = lax.broadcasted_iota(jnp.int32, (ncp, nsp), 0) * CMP_STRIDE
    cj = lax.broadcasted_iota(jnp.int32, (ncp, nsp), 1)
    c2s = ((ci < (cj + 1) * SLC_BLOCK) & (ci + CMP_BLOCK > cj * SLC_BLOCK)).astype(f32).astype(bf16)
    imp = _dot_exact_rhs(p_sum, c2s)
    blk = lax.broadcasted_iota(jnp.int32, (tq, nsp), 1)
    cur = pos_col // SLC_BLOCK
    forced = (blk == 0) | (blk == cur) | (blk == cur - 1)
    imp = jnp.where(forced, FORCED, jnp.where(blk <= cur, imp, NEG_INF))
    rank = jnp.zeros((tq, nsp), jnp.int32)
    for jp in range(ns):
        col = imp[:, jp:jp + 1]
        beats = (col > imp) | ((col == imp) & (blk > jp))
        rank = rank + beats.astype(jnp.int32)
    return ((rank < SLC_TOPN) & (blk <= cur)).astype(f32)


def _combine(gates, kvh, tq, o_cmp, o_slc, o_win):
    lo = kvh * HEAD_DIM
    outs = []
    for g in range(NSA_GROUP):
        h = kvh * NSA_GROUP + g
        r = slice(g * tq, (g + 1) * tq)
        o = (gates[:, 3 * h:3 * h + 1] * o_cmp[r] + gates[:, 3 * h + 1:3 * h + 2] * o_slc[r]
             + gates[:, 3 * h + 2:3 * h + 3] * o_win[r])
        outs.append(o[:, lo:lo + HEAD_DIM])
    return outs


LOG2E = 1.4426950408889634
SLOPES2 = [[v * LOG2E for v in row] for row in SLOPES]

POS_SPLIT = 8
F_SEL0 = 16
MASK_BIG = 1e30


def position_features(pos, blk, ns):
    n = pos.shape[0]
    lane = jnp.arange(HEAD_DIM)[None, :]
    hi = ((pos // POS_SPLIT) * POS_SPLIT).astype(f32)[:, None]
    lo = (pos % POS_SPLIT).astype(f32)[:, None]
    f = jnp.where(lane < 3, hi, jnp.where(lane < 6, lo, jnp.where(lane < 9, 1.0, 0.0)))
    if ns:
        f = jnp.where((lane >= F_SEL0) & (lane < F_SEL0 + ns), (blk[:, None] == lane - F_SEL0).astype(f32), f)
    return jnp.broadcast_to(f, (n, HEAD_DIM)).astype(f32)


def feature_tables(T, nch):
    pos = jnp.arange(T, dtype=jnp.int32)
    kf = position_features(pos, pos // SLC_BLOCK, T // SLC_BLOCK)
    cpos = jnp.arange(nch, dtype=jnp.int32) * CMP_STRIDE + (CMP_BLOCK - 1)
    cf = position_features(cpos, cpos, 0)
    z = lambda a: jnp.zeros_like(a)
    two = lambda a: jnp.stack([jnp.concatenate([z(a), a], axis=1), jnp.concatenate([a, z(a)], axis=1)]).astype(bf16)
    return two(kf), two(cf)


def _query_features(pos_row, selT, kvh, tq):
    posf = pos_row.astype(f32)
    r16 = lax.broadcasted_iota(jnp.int32, (16, tq), 0)
    zeros16 = jnp.zeros((16, tq), f32)
    selbig = (selT - 1.0) * MASK_BIG
    ns = selT.shape[0]
    pad = jnp.zeros((HEAD_DIM - F_SEL0 - ns, tq), f32)
    plain, full = [], []
    for g in range(NSA_GROUP):
        s = jnp.full((1, tq), SLOPES2[kvh][g], f32)
        s_parts = _split3(s)
        a_parts = _split3(s * posf)
        rows = [p.astype(f32) for p in s_parts] * 2 + [-p.astype(f32) for p in a_parts]
        blk0 = zeros16
        for i, rv in enumerate(rows):
            blk0 = jnp.where(r16 == i, rv, blk0)
        plain.append(jnp.concatenate([blk0, jnp.zeros((HEAD_DIM - 16, tq), f32)], axis=0).astype(bf16))
        full.append(jnp.concatenate([blk0, selbig, pad], axis=0).astype(bf16))
    return plain, full


def _aug_queries(qT, feats, kvh, tq):
    out = []
    for g in range(NSA_GROUP):
        h = kvh * NSA_GROUP + g
        qh = qT[h * HEAD_DIM:(h + 1) * HEAD_DIM, :]
        out.append(jnp.concatenate([qh, feats[g]] if kvh == 0 else [feats[g], qh], axis=0))
    return out


def _aug_keys(k_tile, feat_tile, kvh):
    lane = lax.broadcasted_iota(jnp.int32, k_tile.shape, 1)
    own = (lane < HEAD_DIM) if kvh == 0 else (lane >= HEAD_DIM)
    return jnp.where(own, k_tile, feat_tile)


def _with_ones(vT, kvh):
    half = vT.shape[0] // 2
    ones = jnp.ones((half, vT.shape[1]), vT.dtype)
    return jnp.concatenate([vT[:half], ones] if kvh == 0 else [ones, vT[half:]], axis=0)


def _cmp_branch_t(q_aug, kc_aug, vcT, pos_row, tq):
    ncp = vcT.shape[1]
    n_idx = lax.broadcasted_iota(jnp.int32, (ncp, tq), 0)
    cvis = pos_row >= (n_idx * CMP_STRIDE + (CMP_BLOCK - 1))
    any_vis = pos_row >= (CMP_BLOCK - 1)
    s = [jnp.where(cvis, jnp.dot(k, q, preferred_element_type=f32), NEG_INF) for k, q in zip(kc_aug, q_aug)]
    e = [jnp.exp2(x - jnp.max(x, axis=0, keepdims=True)) for x in s]
    p = [jnp.where(any_vis, x / jnp.sum(x, axis=0, keepdims=True), 0.0) for x in e]
    outs = [jnp.dot(vcT, x.astype(bf16), preferred_element_type=f32) for x in p]
    return outs, p


def _select_blocks_t(p_sumT, pos_row, ns, tq):
    ncp = p_sumT.shape[0]
    cj = lax.broadcasted_iota(jnp.int32, (ns, ncp), 0)
    ci = lax.broadcasted_iota(jnp.int32, (ns, ncp), 1) * CMP_STRIDE
    c2sT = ((ci < (cj + 1) * SLC_BLOCK) & (ci + CMP_BLOCK > cj * SLC_BLOCK)).astype(f32).astype(bf16)
    hi, mid, lo = _split3(p_sumT)
    d = lambda x: jnp.dot(c2sT, x, preferred_element_type=f32)
    imp = d(hi) + d(mid) + d(lo)
    blk = lax.broadcasted_iota(jnp.int32, (ns, tq), 0)
    cur = pos_row // SLC_BLOCK
    forced = (blk == 0) | (blk == cur) | (blk == cur - 1)
    imp = jnp.where(forced, FORCED, jnp.where(blk <= cur, imp, NEG_INF))
    rank = jnp.zeros((ns, tq), jnp.int32)
    for jp in range(ns):
        row = imp[jp:jp + 1, :]
        beats = (row > imp) | ((row == imp) & (blk > jp))
        rank = rank + beats.astype(jnp.int32)
    return ((rank < SLC_TOPN) & (blk <= cur)).astype(f32)


def _nsa_prompt_kernel(qT_ref, gT_ref, sk_ref, svT_ref, wk_ref, wvT_ref, kc_ref, vcT_ref, kf_ref, cf_ref, o_ref,
                       m_s, acc_s, *, tq, kt, ns):
    qi = pl.program_id(1)
    p0 = qi * tq
    pos_row = p0 + lax.broadcasted_iota(jnp.int32, (1, tq), 1)
    gates = _sigmoid(gT_ref[...])
    qT = qT_ref[...]
    kc = kc_ref[0].astype(bf16)
    vcT = vcT_ref[0].astype(bf16)
    half = LANES // 2
    KV = range(NSA_KV_HEADS)

    zsel = jnp.zeros((ns, tq), f32)
    q_pos = [_aug_queries(qT, _query_features(pos_row, zsel, kvh, tq)[0], kvh, tq) for kvh in KV]
    kc_aug = [_aug_keys(kc, cf_ref[kvh], kvh) for kvh in KV]
    oc, pc = _cmp_branch_t([q for kvh in KV for q in q_pos[kvh]], [kc_aug[kvh] for kvh in KV for _ in range(NSA_GROUP)],
                           vcT, pos_row, tq)
    o_cmp = [oc[kvh * NSA_GROUP:(kvh + 1) * NSA_GROUP] for kvh in KV]
    q_sel = []
    for kvh in KV:
        p_sum = pc[kvh * NSA_GROUP]
        for x in pc[kvh * NSA_GROUP + 1:(kvh + 1) * NSA_GROUP]:
            p_sum = p_sum + x
        selT = _select_blocks_t(p_sum, pos_row, ns, tq)
        q_sel.append(_aug_queries(qT, _query_features(pos_row, selT, kvh, tq)[1], kvh, tq))

    m_s[...] = jnp.full(m_s.shape, NEG_INF, f32)
    acc_s[...] = jnp.zeros(acc_s.shape, f32)
    rel = pos_row - lax.broadcasted_iota(jnp.int32, (kt, tq), 0)

    def slc_tile(c, causal):
        k0 = pl.multiple_of(c * kt, kt)
        k_tile = sk_ref[pl.ds(k0, kt), :]
        v_tile = svT_ref[:, pl.ds(k0, kt)]
        ok = (rel - k0) >= 0
        for kvh in KV:
            ka = _aug_keys(k_tile, kf_ref[kvh, pl.ds(k0, kt), :], kvh)
            va = _with_ones(v_tile, kvh)
            for g in range(NSA_GROUP):
                cs = slice(g * tq, (g + 1) * tq)
                sg = jnp.dot(ka, q_sel[kvh][g], preferred_element_type=f32)
                if causal:
                    sg = jnp.where(ok, sg, NEG_INF)
                m_old = m_s[kvh, :, cs]
                m_new = jnp.maximum(m_old, jnp.max(sg, axis=0, keepdims=True))
                alpha = jnp.exp2(m_old - m_new)
                p = jnp.exp2(sg - m_new).astype(bf16)
                acc_s[kvh, :, cs] = alpha * acc_s[kvh, :, cs] + jnp.dot(va, p, preferred_element_type=f32)
                m_s[kvh, :, cs] = m_new

    n_kt = (p0 + tq + kt - 1) // kt

    def slc_body(c, _):
        slc_tile(c, False)
        return 0

    lax.fori_loop(0, n_kt - 1, slc_body, 0)
    slc_tile(n_kt - 1, True)

    wlen = WINDOW + tq
    w0 = pl.multiple_of(jnp.maximum(p0 - WINDOW, 0), tq)
    k_win = wk_ref[pl.ds(w0, wlen), :]
    v_win = wvT_ref[:, pl.ds(w0, wlen)]
    dist = pos_row - (w0 + lax.broadcasted_iota(jnp.int32, (wlen, tq), 0))
    valid = (dist >= 0) & (dist < WINDOW)
    ch = [(kvh, g) for kvh in KV for g in range(NSA_GROUP)]
    ka = [_aug_keys(k_win, kf_ref[kvh, pl.ds(w0, wlen), :], kvh) for kvh in KV]
    va = [_with_ones(v_win, kvh) for kvh in KV]
    sg = [jnp.where(valid, jnp.dot(ka[kvh], q_pos[kvh][g], preferred_element_type=f32), NEG_INF) for kvh, g in ch]
    p = [jnp.exp2(x - jnp.max(x, axis=0, keepdims=True)).astype(bf16) for x in sg]
    o_win = [jnp.dot(va[kvh], p[i], preferred_element_type=f32) for i, (kvh, g) in enumerate(ch)]
    heads = []
    for i, (kvh, g) in enumerate(ch):
        h = kvh * NSA_GROUP + g
        d0, l0 = (0, half) if kvh == 0 else (half, 0)
        o_slc = acc_s[kvh, :, g * tq:(g + 1) * tq]
        heads.append(gates[3 * h:3 * h + 1] * o_cmp[kvh][g][d0:d0 + half]
                     + gates[3 * h + 1:3 * h + 2] * (o_slc[d0:d0 + half] / o_slc[l0:l0 + 1])
                     + gates[3 * h + 2:3 * h + 3] * (o_win[i][d0:d0 + half] / o_win[i][l0:l0 + 1]))
    o_ref[0] = jnp.concatenate(heads, axis=0).T.astype(o_ref.dtype)


def nsa_prompt(qT, gT, sk, svT, wk, wvT, kc, vcT, B, tq=128, kt=512):
    N = sk.shape[0]
    T = N // B
    ns = T // SLC_BLOCK
    nq = T // tq
    ncp = kc.shape[1]
    assert T % kt == 0 and T % tq == 0 and kt % tq == 0 and T >= WINDOW + tq and T <= POS_SPLIT * 256
    assert F_SEL0 + ns <= HEAD_DIM
    kf, cf = feature_tables(T, ncp)
    keys = pl.BlockSpec((T, LANES), lambda b, i: (b, 0))
    vals = pl.BlockSpec((LANES, T), lambda b, i: (0, b))
    return pl.pallas_call(
        functools.partial(_nsa_prompt_kernel, tq=tq, kt=kt, ns=ns),
        grid=(B, nq),
        in_specs=[pl.BlockSpec((NSA_WIDTH, tq), lambda b, i: (0, b * nq + i)),
                  pl.BlockSpec((3 * NSA_HEADS, tq), lambda b, i: (0, b * nq + i)),
                  keys, vals, keys, vals,
                  pl.BlockSpec((1, ncp, LANES), lambda b, i: (b, 0, 0)),
                  pl.BlockSpec((1, LANES, ncp), lambda b, i: (b, 0, 0)),
                  pl.BlockSpec((NSA_KV_HEADS, T, LANES), lambda b, i: (0, 0, 0)),
                  pl.BlockSpec((NSA_KV_HEADS, ncp, LANES), lambda b, i: (0, 0, 0))],
        out_specs=pl.BlockSpec((1, tq, NSA_WIDTH), lambda b, i: (b, i, 0)),
        out_shape=jax.ShapeDtypeStruct((B, T, NSA_WIDTH), bf16),
        scratch_shapes=[pltpu.VMEM((NSA_KV_HEADS, 1, NSA_GROUP * tq), f32),
                        pltpu.VMEM((NSA_KV_HEADS, LANES, NSA_GROUP * tq), f32)],
        compiler_params=_cparams("parallel", "arbitrary"),
        name="nsa_prompt",
    )(qT, gT, sk, svT, wk, wvT, kc, vcT, kf, cf)


SF_SPLIT = 64
SF_BLK0 = 16
SF_ROWS = 256


def sample_key_features(past_len, nk, tq, ns):
    sf_new0 = SF_BLK0 + -(-ns // 8) * 8
    assert sf_new0 + tq + 1 <= SF_ROWS and nk <= SF_SPLIT * 256
    pos = jnp.arange(nk, dtype=jnp.int32)[None, :]
    row = jnp.arange(SF_ROWS, dtype=jnp.int32)[:, None]
    hi = ((pos // SF_SPLIT) * SF_SPLIT).astype(f32)
    lo = (pos % SF_SPLIT).astype(f32)
    f = jnp.where(row < 3, hi, jnp.where(row < 6, lo, jnp.where(row < 9, 1.0, 0.0)))
    blk = (row >= SF_BLK0) & (row < SF_BLK0 + ns) & (pos // SLC_BLOCK == row - SF_BLK0)
    new = (row >= sf_new0) & (row < sf_new0 + tq) & (pos - past_len == row - sf_new0)
    pad = (row == sf_new0 + tq) & (pos >= past_len + tq)
    return jnp.where(blk | new | pad, 1.0, f).astype(bf16), sf_new0


def _sample_query_features(sel, pos_col, kvh, tq, ns, sf_new0):
    lane = lax.broadcasted_iota(jnp.int32, (tq, SF_ROWS), 1)
    t_idx = lax.broadcasted_iota(jnp.int32, (tq, SF_ROWS), 0)
    selbig = (pltpu.roll(sel[:, 0:SF_ROWS], SF_BLK0, 1) - 1.0) * MASK_BIG
    base = jnp.where((lane >= SF_BLK0) & (lane < SF_BLK0 + ns), selbig, 0.0)
    base = jnp.where((lane >= sf_new0) & (lane < sf_new0 + tq) & (lane - sf_new0 > t_idx), -MASK_BIG, base)
    base = jnp.where(lane == sf_new0 + tq, -MASK_BIG, base)
    posf = pos_col.astype(f32)
    out = []
    for g in range(NSA_GROUP):
        s = jnp.full((tq, 1), SLOPES[kvh][g], f32)
        cols = [p.astype(f32) for p in _split3(s)] * 2 + [-p.astype(f32) for p in _split3(s * posf)]
        qf = base
        for i, cv in enumerate(cols):
            qf = jnp.where(lane == i, cv, qf)
        out.append(qf)
    return jnp.concatenate(out, axis=0).astype(bf16)


def _new_rows_t(rows, width):
    tq = rows.shape[0]
    sq = jnp.concatenate([rows, jnp.zeros((LANES - tq, LANES), f32)], axis=0).T
    if width == LANES:
        return sq
    return jnp.concatenate([sq, jnp.zeros((LANES, width - LANES), f32)], axis=1)


def _compress_staged(xs, t, nch, pe_ref, w_ref, w2_ref):
    lhs = jnp.concatenate([jnp.concatenate([xs[2 * t + h, j] for j in range(CMP_STRIDE // 2)], axis=1)
                           for h in range(NSA_KV_HEADS)], axis=0)
    acc = jnp.dot(lhs, w_ref[...], preferred_element_type=f32)
    bc = lambda v: jnp.broadcast_to(v, (8, v.shape[1])).astype(bf16)
    c = (jnp.dot(bc(pe_ref[0:1, :]), w_ref[:, 0:CMP_HIDDEN], preferred_element_type=f32)
         + jnp.dot(bc(pe_ref[1:2, :]), w_ref[:, CMP_HIDDEN:], preferred_element_type=f32))[0:1, :]
    outs = []
    for h in range(NSA_KV_HEADS):
        r = slice(h * nch, (h + 1) * nch)
        hid = acc[r, 0:CMP_HIDDEN] + pltpu.roll(acc[r, CMP_HIDDEN:], nch - 1, 0) + c
        outs.append(jnp.dot(_silu(hid).astype(bf16), w2_ref[...], preferred_element_type=f32))
    return jnp.concatenate(outs, axis=1)


def _nsa_sample_kernel(pt_ref, *refs, pps, n_steps, tq, past_len, ns, nsp, sf_new0):
    page_refs = refs[:pps]
    (q_ref, g_ref, rn_ref, wn_ref, cw_ref, ft_ref, perm_ref, pek_ref, pev_ref, wk_ref, wv_ref, w2k_ref, w2v_ref,
     o_ref, xs_s, kT_s, vT_s) = refs[pps:]
    del pt_ref
    step = pl.program_id(1)
    cpp = PAGE_SIZE // CMP_STRIDE
    staged = []
    for i in range(pps):
        r0 = pl.multiple_of((step * pps + i) * PAGE_SIZE, PAGE_SIZE)
        pg = page_refs[i]
        slabs = []
        for th in range(4):
            x = pg[0, th]
            slabs += [x, pltpu.roll(x, PAGE_SIZE - 1, 1)]
        a = jnp.concatenate(slabs, axis=0).astype(bf16)
        staged.append(lax.dot_general(perm_ref[...], a, NT, preferred_element_type=f32))
        kT_s[:, pl.ds(r0, PAGE_SIZE)] = pg[0, 4:6].reshape(LANES, PAGE_SIZE).astype(bf16)
        vT_s[:, pl.ds(r0, PAGE_SIZE)] = pg[0, 6:8].reshape(LANES, PAGE_SIZE).astype(bf16)
    c0 = pl.multiple_of(step * (pps * cpp), pps * cpp)
    for j in range(CMP_STRIDE // 2):
        rows = slice(j * cpp, (j + 1) * cpp)
        both = jnp.concatenate([t[rows] for t in staged], axis=0).astype(bf16)
        for th in range(4):
            xs_s[th, j, pl.ds(c0, pps * cpp), :] = both[:, th * LANES:(th + 1) * LANES]

    @pl.when(step == n_steps - 1)
    def _():
        nk = kT_s.shape[1]
        rn = rn_ref[0]
        kT_s[:, past_len:] = _new_rows_t(rn[:, 2 * LANES:3 * LANES], nk - past_len).astype(bf16)
        vT_s[:, past_len:] = _new_rows_t(rn[:, 3 * LANES:4 * LANES], nk - past_len).astype(bf16)
        nwin = cw_ref.shape[2]
        wn = wn_ref[0]
        wkT = jnp.concatenate([cw_ref[0, 0:LANES, :], _new_rows_t(wn[:, 0:LANES], LANES)], axis=1).astype(bf16)
        wvT = jnp.concatenate([cw_ref[0, LANES:, :], _new_rows_t(wn[:, LANES:], LANES)], axis=1).astype(bf16)
        wlen = nwin + LANES

        nch = past_len // CMP_STRIDE
        kc = _compress_staged(xs_s, 0, nch, pek_ref, wk_ref, w2k_ref)
        vc = _compress_staged(xs_s, 1, nch, pev_ref, wv_ref, w2v_ref)

        pos_col = past_len + lax.broadcasted_iota(jnp.int32, (tq, 1), 0)
        gates = _sigmoid(g_ref[0])
        q_all = q_ref[0]
        wpos = (past_len - nwin) + lax.broadcasted_iota(jnp.int32, (tq, wlen), 1)
        wdist = pos_col - wpos
        wvalid = (wdist >= 0) & (wdist < WINDOW)
        wdistf = wdist.astype(f32)
        qms, qfs, o_cmps = [], [], []
        for kvh in range(NSA_KV_HEADS):
            qm = _padded_queries(q_all, kvh, tq)
            o_cmp, p_sum = _cmp_branch(qm, kc, vc, pos_col, kvh, tq)
            sel = _select_blocks(p_sum, pos_col, ns, nsp, tq)
            qms.append(qm)
            qfs.append(_sample_query_features(sel, pos_col, kvh, tq, ns, sf_new0))
            o_cmps.append(o_cmp)
        qm2 = jnp.concatenate(qms, axis=0)
        s = (jnp.dot(qm2, kT_s[...], preferred_element_type=f32)
             + jnp.dot(jnp.concatenate(qfs, axis=0), ft_ref[...], preferred_element_type=f32))
        e = jnp.exp(s - jnp.max(s, axis=-1, keepdims=True))
        o_slc2 = (lax.dot_general(e.astype(bf16), vT_s[...], NT, preferred_element_type=f32)
                  / jnp.sum(e, axis=-1, keepdims=True))
        sw2 = jnp.dot(qm2, wkT, preferred_element_type=f32)
        pw = []
        for kvh in range(NSA_KV_HEADS):
            for g in range(NSA_GROUP):
                r = slice((kvh * NSA_GROUP + g) * tq, (kvh * NSA_GROUP + g + 1) * tq)
                sg = jnp.where(wvalid, sw2[r] - SLOPES[kvh][g] * wdistf, NEG_INF)
                ew = jnp.where(wvalid, jnp.exp(sg - jnp.max(sg, axis=-1, keepdims=True)), 0.0)
                pw.append(ew / jnp.sum(ew, axis=-1, keepdims=True))
        o_win2 = lax.dot_general(jnp.concatenate(pw, axis=0).astype(bf16), wvT, NT, preferred_element_type=f32)
        outs = []
        for kvh in range(NSA_KV_HEADS):
            r = slice(kvh * NSA_GROUP * tq, (kvh + 1) * NSA_GROUP * tq)
            outs.extend(_combine(gates, kvh, tq, o_cmps[kvh], o_slc2[r], o_win2[r]))
        o_ref[0] = jnp.concatenate(outs, axis=1).astype(o_ref.dtype)


def nsa_sample(page_table, cacheT, q, gates, rows_new, win_new, cache_winT, cw, pps=16):
    B, tq, _ = q.shape
    n_pages = page_table.shape[1]
    past_len = n_pages * PAGE_SIZE
    assert n_pages % pps == 0 and past_len % SLC_BLOCK == 0
    n_steps = n_pages // pps
    nk = past_len + 512
    ns = past_len // SLC_BLOCK + 1
    nsp = -(-ns // LANES) * LANES
    nch = past_len // CMP_STRIDE
    nwin = cache_winT.shape[2]
    pek, wk, w2k, pev, wv, w2v = cw
    ft, sf_new0 = sample_key_features(past_len, nk, tq, ns)
    cpp = PAGE_SIZE // CMP_STRIDE
    ro = jnp.arange(PAGE_SIZE)
    ro2 = jnp.arange(PAGE_SIZE // 2)
    perm = (ro[None, :] == (ro2[:, None] % cpp) * CMP_STRIDE + 2 * (ro2[:, None] // cpp)).astype(bf16)
    per_b = lambda n: pl.BlockSpec((1, tq, n), lambda b, s, pt: (b, 0, 0))
    full = lambda a: pl.BlockSpec(a.shape, lambda b, s, pt: (0,) * a.ndim)
    page_specs = [pl.BlockSpec((1, 8, HEAD_DIM, PAGE_SIZE),
                               functools.partial(lambda b, s, pt, i: (pt[b, s * pps + i], 0, 0, 0), i=i))
                  for i in range(pps)]
    grid_spec = pltpu.PrefetchScalarGridSpec(
        num_scalar_prefetch=1,
        grid=(B, n_steps),
        in_specs=page_specs + [per_b(NSA_HEADS * LANES), per_b(3 * NSA_HEADS), per_b(4 * LANES), per_b(2 * LANES),
                               pl.BlockSpec((1, 2 * LANES, nwin), lambda b, s, pt: (b, 0, 0)),
                               full(ft), full(perm), full(pek), full(pev), full(wk), full(wv), full(w2k), full(w2v)],
        out_specs=per_b(NSA_WIDTH),
        scratch_shapes=[pltpu.VMEM((4, CMP_STRIDE // 2, nch, LANES), bf16),
                        pltpu.VMEM((LANES, nk), bf16), pltpu.VMEM((LANES, nk), bf16)])
    return pl.pallas_call(
        functools.partial(_nsa_sample_kernel, pps=pps, n_steps=n_steps, tq=tq, past_len=past_len, ns=ns, nsp=nsp,
                          sf_new0=sf_new0),
        grid_spec=grid_spec,
        out_shape=jax.ShapeDtypeStruct((B, tq, NSA_WIDTH), bf16),
        compiler_params=_cparams("parallel", "arbitrary"),
        name="nsa_sample",
    )(page_table, *([cacheT] * pps), q, gates, rows_new, win_new, cache_winT, ft, perm, pek, pev, wk, wv, w2k, w2v)


def _gdn_prep_kernel(x_ref, halo_ref, cb_ref, cw_ref, ba_ref, al_ref, dtb_ref,
                     u_ref, w_ref, qg_ref, kg_ref, qk_ref, eg_ref, xs_s, *, tb, C, t_valid, nb):
    i = pl.program_id(1)
    HK = GDN_HEADS * GDN_DK
    R = min(tb, LANES)
    cpb = R // C

    @pl.when(i == 0)
    def _():
        xs_s[:, 0:8, :] = cb_ref[...]

    @pl.when(i > 0)
    def _():
        xs_s[:, 0:8, :] = halo_ref[...]

    xs_s[:, 8:, :] = x_ref[...]
    ri = lax.broadcasted_iota(jnp.int32, (R, R), 0)
    ci = lax.broadcasted_iota(jnp.int32, (R, R), 1)
    same = (ri // C) == (ci // C)
    lower = same & (ri >= ci)
    strict = same & (ri > ci)
    eye = (ri == ci).astype(f32)
    rr = lax.broadcasted_iota(jnp.int32, (R, LANES), 0)
    row_t = i * tb + lax.broadcasted_iota(jnp.int32, (tb, 1), 0)
    live = row_t < t_valid

    def conv_act(bb, c0):
        acc = cw_ref[GDN_CONV - 1:GDN_CONV, c0:c0 + LANES] * xs_s[bb, pl.ds(8, tb), c0:c0 + LANES]
        for k in range(1, GDN_CONV):
            acc = acc + cw_ref[GDN_CONV - 1 - k:GDN_CONV - k, c0:c0 + LANES] * xs_s[bb, pl.ds(8 - k, tb), c0:c0 + LANES]
        return _silu(acc)

    chains = []
    for bb, h in [(bb, h) for bb in range(nb) for h in range(GDN_HEADS)]:
        ba = ba_ref[bb]
        q = conv_act(bb, h * GDN_DK)
        k = conv_act(bb, HK + h * GDN_DK)
        v = conv_act(bb, 2 * HK + h * GDN_DV)
        q = q * lax.rsqrt(jnp.sum(q * q, axis=-1, keepdims=True) + L2_EPS) * (GDN_DK ** -0.5)
        k = k * lax.rsqrt(jnp.sum(k * k, axis=-1, keepdims=True) + L2_EPS)
        beta = _sigmoid(ba[:, h:h + 1])
        ar = ba[:, GDN_HEADS + h:GDN_HEADS + h + 1] + dtb_ref[:, h:h + 1]
        softplus = jnp.maximum(ar, 0.0) + jnp.log(1.0 + jnp.exp(-jnp.abs(ar)))
        gcol = -jnp.exp(al_ref[:, h:h + 1]) * softplus
        if t_valid < 10 ** 9:
            beta = jnp.where(live, beta, 0.0)
            gcol = jnp.where(live, gcol, 0.0)
            k = jnp.where(live, k, 0.0)
            v = jnp.where(live, v, 0.0)
        for blk in range(tb // R):
            r = slice(blk * R, (blk + 1) * R)
            qc, kc, vc, bc = q[r], k[r], v[r], beta[r]
            gc = jnp.broadcast_to(gcol[r], (R, LANES))
            sh = 1
            while sh < C:
                gc = gc + jnp.where((rr % C) >= sh, pltpu.roll(gc, sh, 0), 0.0)
                sh *= 2
            gct = gc.T
            decay = jnp.exp(jnp.where(lower, gc[:, 0:R] - gct[0:R, :], NEG_INF))
            kb = kc * bc
            a = jnp.where(strict, _dot_hp(kb, kc, NT) * decay, 0.0)
            qk = jnp.where(lower, lax.dot_general(qc.astype(bf16), kc.astype(bf16), NT,
                                                  preferred_element_type=f32) * decay, 0.0)
            gl = gc[R - 1:R, :]
            for j in range(cpb - 2, -1, -1):
                gl = jnp.where(rr < (j + 1) * C, gc[(j + 1) * C - 1:(j + 1) * C, :], gl)
            chains.append(dict(bb=bb, h=h, r=r, blk=blk, qc=qc, kc=kc, gc=gc, gl=gl, qk=qk, n=-a, t=eye - a,
                               rhs=jnp.concatenate([vc * bc, kb * jnp.exp(gc)], axis=1)))

    for ch in chains:
        ch['n'] = _split2(ch['n'])
    sh = 2
    while sh < C:
        for ch in chains:
            ch['n'] = _split2(_dot_parts(ch['n'], ch['n']))
        for ch in chains:
            ch['t'] = ch['t'] + _dot_parts(_split2(ch['t']), ch['n'])
        sh *= 2

    for ch in chains:
        bb, h, r, gc, gl = ch['bb'], ch['h'], ch['r'], ch['gc'], ch['gl']
        sol = ch['rhs'] + _dot_hp(ch['t'] - eye, ch['rhs'])
        cs = slice(h * GDN_DV, (h + 1) * GDN_DV)
        u_ref[bb, r, cs] = sol[:, 0:GDN_DV]
        w_ref[bb, r, cs] = sol[:, GDN_DV:].astype(w_ref.dtype)
        qg_ref[bb, r, cs] = (ch['qc'] * jnp.exp(gc)).astype(qg_ref.dtype)
        kg_ref[bb, r, cs] = (ch['kc'] * jnp.exp(gl - gc)).astype(kg_ref.dtype)
        qk = ch['qk'][:, 0:C]
        for j in range(1, cpb):
            qk = jnp.where(ri[:, 0:C] >= j * C, ch['qk'][:, j * C:(j + 1) * C], qk)
        qk_ref[bb, r, h * C:(h + 1) * C] = qk.astype(qk_ref.dtype)
        for j in range(cpb):
            eg_ref[bb, ch['blk'] * cpb + j, h:h + 1, :] = jnp.exp(gc[(j + 1) * C - 1:(j + 1) * C, :])
    eg_ref[:, :, GDN_HEADS:, :] = jnp.zeros((nb, tb // C, 8 - GDN_HEADS, LANES), f32)


def gdn_prep(qkv, conv_buf8, conv_w, ba, a_log, dt_bias, C, tb, t_valid, nb=1):
    B, T, CH = qkv.shape
    nb = math.gcd(nb, B)
    hb = tb // 8
    per = lambda n, dt: (pl.BlockSpec((nb, tb, n), lambda b, i: (b, i, 0)), jax.ShapeDtypeStruct((B, T, n), dt))
    outs = [per(GDN_WIDTH, f32), per(GDN_WIDTH, bf16), per(GDN_WIDTH, bf16), per(GDN_WIDTH, bf16),
            per(GDN_HEADS * C, bf16),
            (pl.BlockSpec((nb, tb // C, 8, LANES), lambda b, i: (b, i, 0, 0)),
             jax.ShapeDtypeStruct((B, T // C, 8, LANES), f32))]
    return pl.pallas_call(
        functools.partial(_gdn_prep_kernel, tb=tb, C=C, t_valid=t_valid, nb=nb),
        grid=(B // nb, T // tb),
        in_specs=[pl.BlockSpec((nb, tb, CH), lambda b, i: (b, i, 0)),
                  pl.BlockSpec((nb, 8, CH), lambda b, i: (b, jnp.maximum(i * hb - 1, 0), 0)),
                  pl.BlockSpec((nb, 8, CH), lambda b, i: (b, 0, 0)),
                  pl.BlockSpec((GDN_CONV, CH), lambda b, i: (0, 0)),
                  pl.BlockSpec((nb, tb, 2 * GDN_HEADS), lambda b, i: (b, i, 0)),
                  pl.BlockSpec((1, GDN_HEADS), lambda b, i: (0, 0)),
                  pl.BlockSpec((1, GDN_HEADS), lambda b, i: (0, 0))],
        out_specs=[o[0] for o in outs],
        out_shape=[o[1] for o in outs],
        scratch_shapes=[pltpu.VMEM((nb, tb + 8, CH), f32)],
        compiler_params=_cparams("parallel", "arbitrary"),
        name="gdn_prep",
    )(qkv, qkv, conv_buf8, conv_w, ba, a_log.reshape(1, GDN_HEADS), dt_bias.reshape(1, GDN_HEADS))


def _gdn_scan_kernel(u_ref, w_ref, qg_ref, kg_ref, qk_ref, eg_ref, z_ref, ng_ref, s0_ref,
                     o_ref, sT_ref, s_s, *, C, nb):
    c = pl.program_id(1)

    @pl.when(c == 0)
    def _():
        s_s[...] = s0_ref[...]

    ch = [(b, h) for b in range(nb) for h in range(GDN_HEADS)]
    cs = [slice(h * GDN_DV, (h + 1) * GDN_DV) for h in range(GDN_HEADS)]
    S = [s_s[b, h] for b, h in ch]
    Sb = [x.astype(bf16) for x in S]
    v_new = [u_ref[b, :, cs[h]] - jnp.dot(w_ref[b, :, cs[h]], Sb[i], preferred_element_type=f32)
             for i, (b, h) in enumerate(ch)]
    vb = [x.astype(bf16) for x in v_new]
    for i, (b, h) in enumerate(ch):
        s_s[b, h] = S[i] * eg_ref[b, 0, h:h + 1, :] + lax.dot_general(kg_ref[b, :, cs[h]], vb[i], TN,
                                                                       preferred_element_type=f32)
    o = [jnp.dot(qg_ref[b, :, cs[h]], Sb[i], preferred_element_type=f32)
         + jnp.dot(qk_ref[b, :, h * C:(h + 1) * C], vb[i], preferred_element_type=f32) for i, (b, h) in enumerate(ch)]
    outs = [_rms(o[i], ng_ref[...]) * _silu(z_ref[b, :, cs[h]]) for i, (b, h) in enumerate(ch)]
    for b in range(nb):
        o_ref[b] = jnp.concatenate(outs[b * GDN_HEADS:(b + 1) * GDN_HEADS], axis=1).astype(o_ref.dtype)

    @pl.when(c == pl.num_programs(1) - 1)
    def _():
        sT_ref[...] = s_s[...]


def gdn_scan(u, w, qg, kg, qk, eg, z, norm_g, s0, C, nb=8):
    B, T, _ = u.shape
    nb = math.gcd(nb, B)
    per = lambda n: pl.BlockSpec((nb, C, n), lambda b, c: (b, c, 0))
    st = pl.BlockSpec((nb, GDN_HEADS, GDN_DK, GDN_DV), lambda b, c: (b, 0, 0, 0))
    return pl.pallas_call(
        functools.partial(_gdn_scan_kernel, C=C, nb=nb),
        grid=(B // nb, T // C),
        in_specs=[per(GDN_WIDTH), per(GDN_WIDTH), per(GDN_WIDTH), per(GDN_WIDTH), per(GDN_HEADS * C),
                  pl.BlockSpec((nb, 1, 8, LANES), lambda b, c: (b, c, 0, 0)),
                  per(GDN_WIDTH), pl.BlockSpec((1, GDN_DV), lambda b, c: (0, 0)), st],
        out_specs=[per(GDN_WIDTH), st],
        out_shape=[jax.ShapeDtypeStruct((B, T, GDN_WIDTH), bf16),
                   jax.ShapeDtypeStruct((B, GDN_HEADS, GDN_DK, GDN_DV), f32)],
        scratch_shapes=[pltpu.VMEM((nb, GDN_HEADS, GDN_DK, GDN_DV), f32)],
        compiler_params=_cparams("parallel", "arbitrary"),
        name="gdn_scan",
    )(u, w, qg, kg, qk, eg, z, norm_g.reshape(1, GDN_DV), s0)


def _nsa_in_groups(w, feature_major):
    c1 = NSA_WIDTH
    c2 = c1 + 4 * NSA_KV_COLS
    c3 = c2 + 2 * NSA_KV_COLS
    c4 = c3 + 3 * NSA_HEADS
    wT = w.T.astype(bf16)
    xa_scale = XA_DIM ** -0.5 * LOG2E
    if feature_major:
        k0 = c1 + 2 * NSA_KV_COLS
        return [(wT[c1:c2], f32, 1.0, FEATURE_MAJOR_PER_BATCH),
                (wT[c1:k0], f32, 1.0, TOKEN_MAJOR_WT),
                (wT[c2:c3], f32, 1.0, FEATURE_MAJOR_PER_BATCH),
                (wT[c4:], bf16, xa_scale, FEATURE_MAJOR),
                (wT[:c1], bf16, HEAD_DIM ** -0.5 * LOG2E, FEATURE_MAJOR),
                (wT[c3:c4], f32, 1.0, FEATURE_MAJOR),
                (wT[k0:k0 + NSA_KV_COLS], bf16, 1.0, TOKEN_MAJOR_WT),
                (wT[k0 + NSA_KV_COLS:c2], bf16, 1.0, FEATURE_MAJOR),
                (wT[c2:c2 + NSA_KV_COLS], bf16, 1.0, TOKEN_MAJOR_WT),
                (wT[c2 + NSA_KV_COLS:c3], bf16, 1.0, FEATURE_MAJOR)]
    wq = wT[:c1].reshape(NSA_KV_HEADS, NSA_GROUP, HEAD_DIM, -1)
    zq = jnp.zeros_like(wq[0])
    wq = jnp.concatenate([jnp.concatenate([wq[0], zq], axis=1), jnp.concatenate([zq, wq[1]], axis=1)], axis=0)
    wq = wq.reshape(NSA_HEADS * LANES, -1)
    return [(wT[c1:c2], f32, 1.0, TOKEN_MAJOR_WT), (wT[c2:c3], f32, 1.0, TOKEN_MAJOR_WT),
            (wT[c4:], bf16, xa_scale, TOKEN_MAJOR_WT), (wq, bf16, HEAD_DIM ** -0.5, TOKEN_MAJOR_WT),
            (wT[c3:c4], f32, 1.0, TOKEN_MAJOR_WT)]


def _gdn_in_groups(w, feature_major):
    c1 = GDN_CONV_CH
    c2 = c1 + GDN_WIDTH
    c3 = c2 + 2 * GDN_HEADS
    wT = w.T.astype(bf16)
    xq = (wT[c3:], bf16, XA_DIM ** -0.5 * LOG2E, FEATURE_MAJOR if feature_major else TOKEN_MAJOR_WT)
    return [(wT[:c1], f32, 1.0, TOKEN_MAJOR_WT), (wT[c1:c2], f32, 1.0, TOKEN_MAJOR_WT),
            (wT[c2:c3], f32, 1.0, TOKEN_MAJOR_WT), xq]


def _trunk(x, mem_kv, nsa_past, gdn_state, p, t_valid):
    B, T, D = x.shape
    N = B * T
    xf = x.reshape(N, D)

    prompt = nsa_past is None
    if prompt:
        kv_rows, rows_cmp, win_rows, xq, qT, gT, sk, svT, wk, wvT = norm_proj(
            xf, p['norm_mix_g'][0], p['nsa_in_prompt'], tm=512, rows_per_batch=T)
        kc, vcT = cmp_prompt(rows_cmp.reshape(B, T, -1), p['cmp'])
        o_mix = nsa_prompt(qT, gT, sk, svT, wk, wvT, kc, vcT, B)
    else:
        rows, win, xq, q, gates = norm_proj(xf, p['norm_mix_g'][0], p['nsa_in_sample'])
        kv_rows = rows.reshape(B, T, -1)
        win_rows = win.reshape(B, T, -1)
        xq = xq.reshape(B, T, -1)
        page_table, cache, cache_win = nsa_past
        o_mix = nsa_sample(page_table, cache, q.reshape(B, T, -1), gates.reshape(B, T, -1), kv_rows, win_rows,
                           cache_win, p['cmp_dense'])
    o_mem = cross_attend(xq, mem_kv, 0, B)
    xf = post_block(xf, o_mix.reshape(N, -1), o_mem.reshape(N, -1), p['w_out'][0], p['norm_ffn_g'][0],
                    p['w_up'][0], p['w_down'][0], p['final_norm_g'], final=False)

    qkv, z, ba, xq = norm_proj(xf, p['norm_mix_g'][1], p['gdn_in_prompt' if prompt else 'gdn_in_sample'], tm=512)
    if not prompt:
        xq = xq.reshape(B, T, -1)
    qkv3 = qkv.reshape(B, T, -1)
    S0, conv_buf8 = gdn_state
    C = min(64, T)
    tb = min(256, T)
    u, w, qg, kg, qk, eg = gdn_prep(qkv3, conv_buf8, p['gdn_conv_w'], ba.reshape(B, T, -1), p['gdn_a_log'],
                                    p['gdn_dt_bias'], C, tb, t_valid, nb=1 if prompt else 4)
    o_mix, S = gdn_scan(u, w, qg, kg, qk, eg, z.reshape(B, T, -1), p['gdn_norm_g'], S0, C)
    o_mem = cross_attend(xq, mem_kv, 1, B)
    y = post_block(xf, o_mix.reshape(N, -1), o_mem.reshape(N, -1), p['w_out'][1], p['norm_ffn_g'][1],
                   p['w_up'][1], p['w_down'][1], p['final_norm_g'], final=True)
    return y.reshape(B, T, D), kv_rows, win_rows, S, qkv3


def kernel(x_prompt, x_sample, mem_prompt, cache_nsa_kv, cache_nsa_win, state_gdn_s, state_gdn_conv,
           cache_mem_kv, page_table, norm_mix_g, norm_mem_g, w_mem_kv, nsa_w_in, cmp_pe_k, cmp_w1_k,
           cmp_w2_k, cmp_pe_v, cmp_w1_v, cmp_w2_v, gdn_w_in, gdn_conv_w, gdn_a_log, gdn_dt_bias,
           gdn_norm_g, w_out, norm_ffn_g, w_up, w_down, final_norm_g):
    B, T, D = x_prompt.shape
    DB, TS, _ = x_sample.shape
    M = mem_prompt.shape[1]
    p = dict(norm_mix_g=norm_mix_g, nsa_in_prompt=_nsa_in_groups(nsa_w_in[0], True),
             nsa_in_sample=_nsa_in_groups(nsa_w_in[0], False), gdn_in_prompt=_gdn_in_groups(gdn_w_in[0], True),
             gdn_in_sample=_gdn_in_groups(gdn_w_in[0], False),
             cmp=_cmp_weights(cmp_pe_k[0], cmp_w1_k[0], cmp_w2_k[0]) + _cmp_weights(cmp_pe_v[0], cmp_w1_v[0], cmp_w2_v[0]),
             cmp_dense=(_cmp_weights_dense(cmp_pe_k[0], cmp_w1_k[0], cmp_w2_k[0])
                        + _cmp_weights_dense(cmp_pe_v[0], cmp_w1_v[0], cmp_w2_v[0])),
             gdn_conv_w=gdn_conv_w[0], gdn_a_log=gdn_a_log[0], gdn_dt_bias=gdn_dt_bias[0], gdn_norm_g=gdn_norm_g[0],
             w_out=w_out.astype(bf16), norm_ffn_g=norm_ffn_g, w_up=w_up.astype(bf16), w_down=w_down.astype(bf16),
             final_norm_g=final_norm_g)

    mem_flat = mem_prompt.reshape(B * M, D)
    mem_kv = [norm_proj(mem_flat, norm_mem_g[i], [(w_mem_kv[i].astype(bf16).T, f32, 1.0, FEATURE_MAJOR_PER_BATCH)],
                        tm=M, rows_per_batch=M)[0] for i in range(DEPTH)]
    mem_kv_p = jnp.stack(mem_kv, axis=1)
    zero_state = (jnp.zeros((B, GDN_HEADS, GDN_DK, GDN_DV), f32), jnp.zeros((B, 8, GDN_CONV_CH), f32))
    y_p, rows_p, win_p, S_p, qkv_p = _trunk(x_prompt, mem_kv_p, None, zero_state, p, 10 ** 9)

    TP = 8
    x_s = jnp.pad(x_sample, ((0, 0), (0, TP - TS), (0, 0)))
    cache = cache_nsa_kv.transpose(0, 2, 3, 4, 5, 1).reshape(cache_nsa_kv.shape[0], -1, HEAD_DIM, PAGE_SIZE)
    nwin = cache_nsa_win.shape[2]
    cache_win = cache_nsa_win.transpose(0, 1, 3, 4, 5, 2).reshape(DB, -1, nwin)
    conv8 = jnp.pad(state_gdn_conv[:, 0], ((0, 0), (8 - (GDN_CONV - 1), 0), (0, 0)))
    mem_kv_s = cache_mem_kv.transpose(0, 1, 3, 4, 5, 2).reshape(DB, DEPTH, 2 * XA_WIDTH, M)
    y_s, rows_s, win_s, S_s, qkv_s = _trunk(x_s, mem_kv_s, (page_table, cache, cache_win),
                                            (state_gdn_s[:, 0], conv8), p, TS)

    kvshape = (4, NSA_KV_HEADS, HEAD_DIM)
    nsa_kv_prompt = rows_p.reshape(B, 1, *kvshape, T).transpose(0, 5, 1, 2, 3, 4)
    nsa_kv_sample = rows_s[:, :TS].reshape(DB, TS, 1, *kvshape)
    wshape = (2, NSA_KV_HEADS, HEAD_DIM)
    wkeep = min(WINDOW, T)
    nsa_win_prompt = win_p[:, :, T - wkeep:].reshape(B, 1, *wshape, wkeep).transpose(0, 1, 5, 2, 3, 4)
    win_cat = jnp.concatenate([cache_win, win_s[:, :TS].transpose(0, 2, 1)], axis=2)[:, :, -WINDOW:]
    nsa_win_sample = win_cat.reshape(DB, 1, *wshape, -1).transpose(0, 1, 5, 2, 3, 4)
    gdn_conv_prompt = qkv_p[:, None, T - (GDN_CONV - 1):]
    conv_cat = jnp.concatenate([state_gdn_conv[:, 0], qkv_s[:, :TS]], axis=1)
    gdn_conv_sample = conv_cat[:, None, -(GDN_CONV - 1):]
    mem_kv_prompt = mem_kv_p.reshape(B, DEPTH, 2, XA_HEADS, XA_DIM, M).transpose(0, 1, 5, 2, 3, 4)
    return (y_p, y_s[:, :TS], nsa_kv_prompt, nsa_kv_sample, nsa_win_prompt, nsa_win_sample,
            S_p[:, None], S_s[:, None], gdn_conv_prompt, gdn_conv_sample, mem_kv_prompt)
```

```python
import functools
import math

import jax
import jax.numpy as jnp
import numpy as np
from jax import lax
from jax.experimental import pallas as pl
from jax.experimental.pallas import tpu as pltpu

f32 = jnp.float32
bf16 = jnp.bfloat16

D_MODEL = 1024
DEPTH = 2
PAGE_SIZE = 128
HEAD_DIM = 64
NSA_HEADS = 12
NSA_KV_HEADS = 2
NSA_GROUP = NSA_HEADS // NSA_KV_HEADS
NSA_WIDTH = NSA_HEADS * HEAD_DIM
NSA_KV_COLS = NSA_KV_HEADS * HEAD_DIM
CMP_STRIDE = 16
CMP_BLOCK = 2 * CMP_STRIDE
CMP_HIDDEN = 128
SLC_BLOCK = 64
SLC_TOPN = 16
WINDOW = 512
GDN_HEADS = 6
GDN_DK = 128
GDN_DV = 128
GDN_WIDTH = GDN_HEADS * GDN_DV
GDN_CONV = 4
GDN_CONV_CH = 2 * GDN_HEADS * GDN_DK + GDN_HEADS * GDN_DV
XA_HEADS = 4
XA_DIM = 64
XA_WIDTH = XA_HEADS * XA_DIM
D_FF = 4 * D_MODEL
NORM_EPS = 1e-6
L2_EPS = 1e-6
NEG_INF = -1e30
FORCED = 1e9

VMEM_LIMIT = 52 * 1024 * 1024
LANES = 128

SLOPES = [[2.0 ** (-8.0 * (k * NSA_GROUP + g + 1) / NSA_HEADS) for g in range(NSA_GROUP)]
          for k in range(NSA_KV_HEADS)]

NT = (((1,), (1,)), ((), ()))
TN = (((0,), (0,)), ((), ()))


def _cparams(*sem):
    return pltpu.CompilerParams(dimension_semantics=sem, vmem_limit_bytes=VMEM_LIMIT)


def _rms(x, g):
    r = lax.rsqrt(jnp.mean(x * x, axis=-1, keepdims=True) + NORM_EPS)
    return (x * r) * g


def _sigmoid(x):
    return 1.0 / (1.0 + jnp.exp(-x))


def _silu(x):
    return x * _sigmoid(x)


def _split3(a):
    hi = a.astype(bf16)
    r1 = a - hi.astype(f32)
    mid = r1.astype(bf16)
    lo = (r1 - mid.astype(f32)).astype(bf16)
    return hi, mid, lo


def _dot_exact_rhs(a, b_bf16):
    hi, mid, lo = _split3(a)
    d = lambda x: jnp.dot(x, b_bf16, preferred_element_type=f32)
    return d(hi) + d(mid) + d(lo)


def _split2(a):
    hi = a.astype(bf16)
    return hi, (a - hi.astype(f32)).astype(bf16)


def _dot_parts(a, b, dims=None):
    if dims is None:
        d = lambda x, y: jnp.dot(x, y, preferred_element_type=f32)
    else:
        d = lambda x, y: lax.dot_general(x, y, dims, preferred_element_type=f32)
    return d(a[0], b[0]) + (d(a[0], b[1]) + d(a[1], b[0]))


def _dot_hp(a, b, dims=None):
    return _dot_parts(_split2(a), _split2(b), dims)


TOKEN_MAJOR, TOKEN_MAJOR_WT, FEATURE_MAJOR, FEATURE_MAJOR_PER_BATCH = 'nt', 'ntw', 'fm', 'fmb'


def _norm_proj_kernel(x_ref, g_ref, *refs, specs):
    n = len(specs)
    w_refs, o_refs = refs[:n], refs[n:]
    h = _rms(x_ref[...], g_ref[...]).astype(bf16)
    for (scale, kind), w_ref, o_ref in zip(specs, w_refs, o_refs):
        if kind == TOKEN_MAJOR:
            y = jnp.dot(h, w_ref[...], preferred_element_type=f32)
        elif kind == TOKEN_MAJOR_WT:
            y = lax.dot_general(h, w_ref[...], NT, preferred_element_type=f32)
        else:
            y = lax.dot_general(w_ref[...], h, NT, preferred_element_type=f32)
        if scale != 1.0:
            y = y * scale
        o_ref[...] = y.astype(o_ref.dtype).reshape(o_ref.shape)


def norm_proj(x, g, groups, tm=256, rows_per_batch=None):
    N, D = x.shape
    tm = min(tm, N)
    assert N % tm == 0
    in_specs = [pl.BlockSpec((tm, D), lambda i: (i, 0)), pl.BlockSpec((1, D), lambda i: (0, 0))]
    out_specs, out_shape = [], []
    for w, dt, _, kind in groups:
        in_specs.append(pl.BlockSpec(w.shape, lambda i: (0, 0)))
        if kind in (TOKEN_MAJOR, TOKEN_MAJOR_WT):
            n = w.shape[1] if kind == TOKEN_MAJOR else w.shape[0]
            out_specs.append(pl.BlockSpec((tm, n), lambda i: (i, 0)))
            out_shape.append(jax.ShapeDtypeStruct((N, n), dt))
        elif kind == FEATURE_MAJOR:
            n = w.shape[0]
            out_specs.append(pl.BlockSpec((n, tm), lambda i: (0, i)))
            out_shape.append(jax.ShapeDtypeStruct((n, N), dt))
        else:
            n = w.shape[0]
            per = rows_per_batch // tm
            assert rows_per_batch % tm == 0
            out_specs.append(pl.BlockSpec((1, n, tm), lambda i: (i // per, 0, i % per)))
            out_shape.append(jax.ShapeDtypeStruct((N // rows_per_batch, n, rows_per_batch), dt))
    return pl.pallas_call(
        functools.partial(_norm_proj_kernel, specs=tuple((s, t) for _, _, s, t in groups)),
        grid=(N // tm,),
        in_specs=in_specs,
        out_specs=out_specs,
        out_shape=out_shape,
        compiler_params=_cparams("parallel"),
        name="norm_proj",
    )(x, g.reshape(1, D), *[w for w, _, _, _ in groups])


def _xattn_kernel(q_ref, kv_ref, o_ref, *, q_feature_major, nb):
    ch = [(b, h) for b in range(nb) for h in range(XA_HEADS)]
    hs = lambda h, off=0: slice(off + h * XA_DIM, off + (h + 1) * XA_DIM)
    qT = [q_ref[...] if q_feature_major else q_ref[b].astype(f32).T.astype(bf16) for b in range(nb)]
    s = [lax.dot_general(kv_ref[b, 0, hs(h), :].astype(bf16), qT[b][hs(h)], TN, preferred_element_type=f32)
         for b, h in ch]
    e = [jnp.exp2(x - jnp.max(x, axis=0, keepdims=True)) for x in s]
    o = [jnp.dot(kv_ref[b, 0, hs(h, XA_WIDTH), :].astype(bf16), e[i].astype(bf16), preferred_element_type=f32)
         / jnp.sum(e[i], axis=0, keepdims=True) for i, (b, h) in enumerate(ch)]
    for b in range(nb):
        o_ref[b] = jnp.concatenate(o[b * XA_HEADS:(b + 1) * XA_HEADS], axis=0).T.astype(o_ref.dtype)


def cross_attend(xq, kvT, layer, B, tq=2048, nb=4):
    q_feature_major = xq.ndim == 2
    T = xq.shape[1] // B if q_feature_major else xq.shape[1]
    M = kvT.shape[3]
    tq = min(tq, T)
    nq = T // tq
    if q_feature_major:
        nb = 1
        q_spec = pl.BlockSpec((XA_WIDTH, tq), lambda b, i: (0, b * nq + i))
    else:
        nb = math.gcd(nb, B)
        q_spec = pl.BlockSpec((nb, tq, XA_WIDTH), lambda b, i: (b, i, 0))
    return pl.pallas_call(
        functools.partial(_xattn_kernel, q_feature_major=q_feature_major, nb=nb),
        grid=(B // nb, nq),
        in_specs=[q_spec, pl.BlockSpec((nb, 1, 2 * XA_WIDTH, M), lambda b, i: (b, layer, 0, 0))],
        out_specs=pl.BlockSpec((nb, tq, XA_WIDTH), lambda b, i: (b, i, 0)),
        out_shape=jax.ShapeDtypeStruct((B, T, XA_WIDTH), bf16),
        compiler_params=_cparams("parallel", "parallel"),
        name="cross_attend",
    )(xq, kvT)


def _post_kernel(x_ref, om_ref, oc_ref, wo_ref, gf_ref, wu_ref, wd_ref, gl_ref, o_ref,
                 x1_s, h_s, acc_s, *, mix_w, final):
    j = pl.program_id(1)

    @pl.when(j == 0)
    def _():
        x1 = x_ref[...] + (jnp.dot(om_ref[...], wo_ref[0:mix_w, :], preferred_element_type=f32)
                           + jnp.dot(oc_ref[...], wo_ref[mix_w:, :], preferred_element_type=f32))
        x1_s[...] = x1
        h_s[...] = _rms(x1, gf_ref[...]).astype(bf16)
        acc_s[...] = jnp.zeros_like(acc_s)

    u = jnp.dot(h_s[...], wu_ref[...], preferred_element_type=f32)
    u = jnp.square(jnp.maximum(u, 0.0)).astype(bf16)
    acc_s[...] += jnp.dot(u, wd_ref[...], preferred_element_type=f32)

    @pl.when(j == pl.num_programs(1) - 1)
    def _():
        x2 = x1_s[...] + acc_s[...]
        if final:
            x2 = _rms(x2, gl_ref[...])
        o_ref[...] = x2


def post_block(x, o_mix, o_mem, w_out, g_ffn, w_up, w_down, g_final, final, tm=1024, tf=512):
    N, D = x.shape
    mix_w = o_mix.shape[1]
    F = w_up.shape[1]
    tm = min(tm, N)
    return pl.pallas_call(
        functools.partial(_post_kernel, mix_w=mix_w, final=final),
        grid=(N // tm, F // tf),
        in_specs=[pl.BlockSpec((tm, D), lambda i, j: (i, 0)),
                  pl.BlockSpec((tm, mix_w), lambda i, j: (i, 0)),
                  pl.BlockSpec((tm, XA_WIDTH), lambda i, j: (i, 0)),
                  pl.BlockSpec((mix_w + XA_WIDTH, D), lambda i, j: (0, 0)),
                  pl.BlockSpec((1, D), lambda i, j: (0, 0)),
                  pl.BlockSpec((D, tf), lambda i, j: (0, j)),
                  pl.BlockSpec((tf, D), lambda i, j: (j, 0)),
                  pl.BlockSpec((1, D), lambda i, j: (0, 0))],
        out_specs=pl.BlockSpec((tm, D), lambda i, j: (i, 0)),
        out_shape=jax.ShapeDtypeStruct((N, D), f32),
        scratch_shapes=[pltpu.VMEM((tm, D), f32), pltpu.VMEM((tm, D), bf16), pltpu.VMEM((tm, D), f32)],
        compiler_params=_cparams("parallel", "arbitrary"),
        name="post_block",
    )(x, o_mix, o_mem, w_out, g_ffn.reshape(1, D), w_up, w_down, g_final.reshape(1, D))


def _compress(load_chunk_rows, nch, pe_ref, w_ref, w2_ref):
    acc_p = jnp.zeros((nch, 2 * CMP_HIDDEN), f32)
    acc_q = jnp.zeros((nch, 2 * CMP_HIDDEN), f32)
    for l in range(CMP_STRIDE):
        x = load_chunk_rows(l)
        xp = (x + pe_ref[l:l + 1, :]).astype(bf16)
        xq = (x + pe_ref[CMP_STRIDE + l:CMP_STRIDE + l + 1, :]).astype(bf16)
        acc_p = acc_p + jnp.dot(xp, w_ref[l, :, 0:2 * CMP_HIDDEN], preferred_element_type=f32)
        acc_q = acc_q + jnp.dot(xq, w_ref[l, :, 2 * CMP_HIDDEN:], preferred_element_type=f32)
    hid = _silu(acc_p + pltpu.roll(acc_q, nch - 1, 0)).astype(bf16)
    outs = [jnp.dot(hid[:, h * CMP_HIDDEN:(h + 1) * CMP_HIDDEN], w2_ref[...], preferred_element_type=f32)
            for h in range(NSA_KV_HEADS)]
    return jnp.concatenate(outs, axis=1)


def _cmp_prompt_kernel(xk_ref, xv_ref, pek_ref, pev_ref, wk_ref, wv_ref, w2k_ref, w2v_ref, kc_ref, vc_ref, *, nch):
    kc_ref[0] = _compress(lambda l: xk_ref[0, pl.ds(l, nch, stride=CMP_STRIDE), :], nch, pek_ref, wk_ref, w2k_ref)
    vc_ref[0] = _compress(lambda l: xv_ref[0, pl.ds(l, nch, stride=CMP_STRIDE), :], nch, pev_ref, wv_ref, w2v_ref).T


def _cmp_weights(pe, w1, w2):
    pe2 = jnp.tile(pe, (1, NSA_KV_HEADS))
    z = jnp.zeros((CMP_STRIDE, HEAD_DIM, CMP_HIDDEN), w1.dtype)
    a, b = w1[:CMP_STRIDE], w1[CMP_STRIDE:]
    top = jnp.concatenate([a, z, b, z], axis=2)
    bot = jnp.concatenate([z, a, z, b], axis=2)
    wbd = jnp.concatenate([top, bot], axis=1).astype(bf16)
    return pe2, wbd, w2.astype(bf16)


def _cmp_weights_dense(pe, w1, w2):
    half = CMP_STRIDE * HEAD_DIM
    w = jnp.concatenate([w1[:CMP_STRIDE].reshape(half, CMP_HIDDEN), w1[CMP_STRIDE:].reshape(half, CMP_HIDDEN)], axis=1)
    return pe.reshape(2, half), w.astype(bf16), w2.astype(bf16)


def cmp_prompt(rows, cw):
    B, T, _ = rows.shape
    nch = T // CMP_STRIDE
    pek, wk, w2k, pev, wv, w2v = cw
    full = lambda a: pl.BlockSpec(a.shape, lambda b: (0,) * a.ndim)
    return pl.pallas_call(
        functools.partial(_cmp_prompt_kernel, nch=nch),
        grid=(B,),
        in_specs=[pl.BlockSpec((1, T, LANES), lambda b: (b, 0, 0)),
                  pl.BlockSpec((1, T, LANES), lambda b: (b, 0, 1)),
                  full(pek), full(pev), full(wk), full(wv), full(w2k), full(w2v)],
        out_specs=[pl.BlockSpec((1, nch, LANES), lambda b: (b, 0, 0)), pl.BlockSpec((1, LANES, nch), lambda b: (b, 0, 0))],
        out_shape=[jax.ShapeDtypeStruct((B, nch, LANES), f32), jax.ShapeDtypeStruct((B, LANES, nch), f32)],
        compiler_params=_cparams("parallel"),
        name="cmp_prompt",
    )(rows, rows, pek, pev, wk, wv, w2k, w2v)


def _padded_queries(q_all, kvh, tq):
    del tq
    lo = kvh * NSA_GROUP
    return jnp.concatenate([q_all[:, (lo + g) * LANES:(lo + g + 1) * LANES] for g in range(NSA_GROUP)], axis=0)


def _cmp_branch(qm, kc, vc, pos_col, kvh, tq):
    ncp = kc.shape[0]
    s = lax.dot_general(qm, kc.astype(bf16), NT, preferred_element_type=f32)
    n_idx = lax.broadcasted_iota(jnp.int32, (tq, ncp), 1)
    cdist = pos_col - (n_idx * CMP_STRIDE + (CMP_BLOCK - 1))
    cvis = cdist >= 0
    any_vis = pos_col >= (CMP_BLOCK - 1)
    cdf = cdist.astype(f32)
    p_sum = jnp.zeros((tq, ncp), f32)
    ps = []
    for g in range(NSA_GROUP):
        sg = jnp.where(cvis, s[g * tq:(g + 1) * tq] - SLOPES[kvh][g] * cdf, NEG_INF)
        m = jnp.max(sg, axis=-1, keepdims=True)
        e = jnp.exp(sg - m)
        p = e / jnp.sum(e, axis=-1, keepdims=True)
        p = jnp.where(any_vis, p, 0.0)
        p_sum = p_sum + p
        ps.append(p)
    p_all = jnp.concatenate(ps, axis=0).astype(bf16)
    o = jnp.dot(p_all, vc.astype(bf16), preferred_element_type=f32)
    return o, p_sum


def _select_blocks(p_sum, pos_col, ns, nsp, tq):
    ncp = p_sum.shape[1]
    ci = lax.broadcasted_iota(jnp.int32, (ncp, nsp), 0) * CMP_STRIDE
    cj = lax.broadcasted_iota(jnp.int32, (ncp, nsp), 1)
    c2s = ((ci < (cj + 1) * SLC_BLOCK) & (ci + CMP_BLOCK > cj * SLC_BLOCK)).astype(f32).astype(bf16)
    imp = _dot_exact_rhs(p_sum, c2s)
    blk = lax.broadcasted_iota(jnp.int32, (tq, nsp), 1)
    cur = pos_col // SLC_BLOCK
    forced = (blk == 0) | (blk == cur) | (blk == cur - 1)
    imp = jnp.where(forced, FORCED, jnp.where(blk <= cur, imp, NEG_INF))
    rank = jnp.zeros((tq, nsp), jnp.int32)
    for jp in range(ns):
        col = imp[:, jp:jp + 1]
        beats = (col > imp) | ((col == imp) & (blk > jp))
        rank = rank + beats.astype(jnp.int32)
    return ((rank < SLC_TOPN) & (blk <= cur)).astype(f32)


def _combine(gates, kvh, tq, o_cmp, o_slc, o_win):
    lo = kvh * HEAD_DIM
    outs = []
    for g in range(NSA_GROUP):
        h = kvh * NSA_GROUP + g
        r = slice(g * tq, (g + 1) * tq)
        o = (gates[:, 3 * h:3 * h + 1] * o_cmp[r] + gates[:, 3 * h + 1:3 * h + 2] * o_slc[r]
             + gates[:, 3 * h + 2:3 * h + 3] * o_win[r])
        outs.append(o[:, lo:lo + HEAD_DIM])
    return outs


LOG2E = 1.4426950408889634
SLOPES2 = [[v * LOG2E for v in row] for row in SLOPES]

POS_SPLIT = 8
F_SEL0 = 16
MASK_BIG = 1e30


def position_features(pos, blk, ns):
    n = pos.shape[0]
    lane = np.arange(HEAD_DIM)[None, :]
    hi = ((pos // POS_SPLIT) * POS_SPLIT).astype(np.float32)[:, None]
    lo = (pos % POS_SPLIT).astype(np.float32)[:, None]
    f = np.where(lane < 3, hi, np.where(lane < 6, lo, np.where(lane < 9, 1.0, 0.0)))
    if ns:
        f = np.where((lane >= F_SEL0) & (lane < F_SEL0 + ns), (blk[:, None] == lane - F_SEL0).astype(np.float32), f)
    return np.broadcast_to(f, (n, HEAD_DIM)).astype(np.float32)


def feature_tables(T, nch):
    pos = np.arange(T)
    kf = position_features(pos, pos // SLC_BLOCK, T // SLC_BLOCK)
    cpos = np.arange(nch) * CMP_STRIDE + (CMP_BLOCK - 1)
    cf = position_features(cpos, cpos, 0)
    z = np.zeros_like
    two = lambda a: jnp.asarray(np.stack([np.concatenate([z(a), a], axis=1),
                                          np.concatenate([a, z(a)], axis=1)]).astype(bf16))
    return two(kf), two(cf)


def _query_features(pos_row, selT, kvh, tq):
    posf = pos_row.astype(f32)
    r16 = lax.broadcasted_iota(jnp.int32, (16, tq), 0)
    zeros16 = jnp.zeros((16, tq), f32)
    selbig = (selT - 1.0) * MASK_BIG
    ns = selT.shape[0]
    pad = jnp.zeros((HEAD_DIM - F_SEL0 - ns, tq), f32)
    plain, full = [], []
    for g in range(NSA_GROUP):
        s = jnp.full((1, tq), SLOPES2[kvh][g], f32)
        s_parts = _split3(s)
        a_parts = _split3(s * posf)
        rows = [p.astype(f32) for p in s_parts] * 2 + [-p.astype(f32) for p in a_parts]
        blk0 = zeros16
        for i, rv in enumerate(rows):
            blk0 = jnp.where(r16 == i, rv, blk0)
        plain.append(jnp.concatenate([blk0, jnp.zeros((HEAD_DIM - 16, tq), f32)], axis=0).astype(bf16))
        full.append(jnp.concatenate([blk0, selbig, pad], axis=0).astype(bf16))
    return plain, full


def _aug_queries(qT, feats, kvh, tq):
    out = []
    for g in range(NSA_GROUP):
        h = kvh * NSA_GROUP + g
        qh = qT[h * HEAD_DIM:(h + 1) * HEAD_DIM, :]
        out.append(jnp.concatenate([qh, feats[g]] if kvh == 0 else [feats[g], qh], axis=0))
    return out


def _aug_keys(k_tile, feat_tile, kvh):
    lane = lax.broadcasted_iota(jnp.int32, k_tile.shape, 1)
    own = (lane < HEAD_DIM) if kvh == 0 else (lane >= HEAD_DIM)
    return jnp.where(own, k_tile, feat_tile)


def _with_ones(vT, kvh):
    half = vT.shape[0] // 2
    ones = jnp.ones((half, vT.shape[1]), vT.dtype)
    return jnp.concatenate([vT[:half], ones] if kvh == 0 else [ones, vT[half:]], axis=0)


def _cmp_branch_t(q_aug, kc_aug, vcT, pos_row, tq):
    ncp = vcT.shape[1]
    n_idx = lax.broadcasted_iota(jnp.int32, (ncp, tq), 0)
    cvis = pos_row >= (n_idx * CMP_STRIDE + (CMP_BLOCK - 1))
    any_vis = pos_row >= (CMP_BLOCK - 1)
    s = [jnp.where(cvis, jnp.dot(k, q, preferred_element_type=f32), NEG_INF) for k, q in zip(kc_aug, q_aug)]
    e = [jnp.exp2(x - jnp.max(x, axis=0, keepdims=True)) for x in s]
    p = [jnp.where(any_vis, x / jnp.sum(x, axis=0, keepdims=True), 0.0) for x in e]
    outs = [jnp.dot(vcT, x.astype(bf16), preferred_element_type=f32) for x in p]
    return outs, p


def _select_blocks_t(p_sumT, pos_row, ns, tq):
    ncp = p_sumT.shape[0]
    cj = lax.broadcasted_iota(jnp.int32, (ns, ncp), 0)
    ci = lax.broadcasted_iota(jnp.int32, (ns, ncp), 1) * CMP_STRIDE
    c2sT = ((ci < (cj + 1) * SLC_BLOCK) & (ci + CMP_BLOCK > cj * SLC_BLOCK)).astype(f32).astype(bf16)
    hi, mid, lo = _split3(p_sumT)
    d = lambda x: jnp.dot(c2sT, x, preferred_element_type=f32)
    imp = d(hi) + d(mid) + d(lo)
    blk = lax.broadcasted_iota(jnp.int32, (ns, tq), 0)
    cur = pos_row // SLC_BLOCK
    forced = (blk == 0) | (blk == cur) | (blk == cur - 1)
    imp = jnp.where(forced, FORCED, jnp.where(blk <= cur, imp, NEG_INF))
    rank = jnp.zeros((ns, tq), jnp.int32)
    for jp in range(ns):
        row = imp[jp:jp + 1, :]
        beats = (row > imp) | ((row == imp) & (blk > jp))
        rank = rank + beats.astype(jnp.int32)
    return ((rank < SLC_TOPN) & (blk <= cur)).astype(f32)


def _nsa_prompt_kernel(qT_ref, gT_ref, sk_ref, svT_ref, wk_ref, wvT_ref, kc_ref, vcT_ref, kf_ref, cf_ref, o_ref,
                       m_s, acc_s, *, tq, kt, ns):
    qi = pl.program_id(1)
    p0 = qi * tq
    pos_row = p0 + lax.broadcasted_iota(jnp.int32, (1, tq), 1)
    gates = _sigmoid(gT_ref[...])
    qT = qT_ref[...]
    kc = kc_ref[0].astype(bf16)
    vcT = vcT_ref[0].astype(bf16)
    half = LANES // 2
    KV = range(NSA_KV_HEADS)

    zsel = jnp.zeros((ns, tq), f32)
    q_pos = [_aug_queries(qT, _query_features(pos_row, zsel, kvh, tq)[0], kvh, tq) for kvh in KV]
    kc_aug = [_aug_keys(kc, cf_ref[kvh], kvh) for kvh in KV]
    oc, pc = _cmp_branch_t([q for kvh in KV for q in q_pos[kvh]], [kc_aug[kvh] for kvh in KV for _ in range(NSA_GROUP)],
                           vcT, pos_row, tq)
    o_cmp = [oc[kvh * NSA_GROUP:(kvh + 1) * NSA_GROUP] for kvh in KV]
    q_sel = []
    for kvh in KV:
        p_sum = pc[kvh * NSA_GROUP]
        for x in pc[kvh * NSA_GROUP + 1:(kvh + 1) * NSA_GROUP]:
            p_sum = p_sum + x
        selT = _select_blocks_t(p_sum, pos_row, ns, tq)
        q_sel.append(_aug_queries(qT, _query_features(pos_row, selT, kvh, tq)[1], kvh, tq))

    m_s[...] = jnp.full(m_s.shape, NEG_INF, f32)
    acc_s[...] = jnp.zeros(acc_s.shape, f32)
    rel = pos_row - lax.broadcasted_iota(jnp.int32, (kt, tq), 0)

    def slc_tile(c, causal):
        k0 = pl.multiple_of(c * kt, kt)
        k_tile = sk_ref[pl.ds(k0, kt), :]
        v_tile = svT_ref[:, pl.ds(k0, kt)]
        ok = (rel - k0) >= 0
        for kvh in KV:
            ka = _aug_keys(k_tile, kf_ref[kvh, pl.ds(k0, kt), :], kvh)
            va = _with_ones(v_tile, kvh)
            for g in range(NSA_GROUP):
                cs = slice(g * tq, (g + 1) * tq)
                sg = jnp.dot(ka, q_sel[kvh][g], preferred_element_type=f32)
                if causal:
                    sg = jnp.where(ok, sg, NEG_INF)
                m_old = m_s[kvh, :, cs]
                m_new = jnp.maximum(m_old, jnp.max(sg, axis=0, keepdims=True))
                alpha = jnp.exp2(m_old - m_new)
                p = jnp.exp2(sg - m_new).astype(bf16)
                acc_s[kvh, :, cs] = alpha * acc_s[kvh, :, cs] + jnp.dot(va, p, preferred_element_type=f32)
                m_s[kvh, :, cs] = m_new

    n_kt = (p0 + tq + kt - 1) // kt

    def slc_body(c, _):
        slc_tile(c, False)
        return 0

    lax.fori_loop(0, n_kt - 1, slc_body, 0)
    slc_tile(n_kt - 1, True)

    wlen = WINDOW + tq
    w0 = pl.multiple_of(jnp.maximum(p0 - WINDOW, 0), tq)
    k_win = wk_ref[pl.ds(w0, wlen), :]
    v_win = wvT_ref[:, pl.ds(w0, wlen)]
    dist = pos_row - (w0 + lax.broadcasted_iota(jnp.int32, (wlen, tq), 0))
    valid = (dist >= 0) & (dist < WINDOW)
    ch = [(kvh, g) for kvh in KV for g in range(NSA_GROUP)]
    ka = [_aug_keys(k_win, kf_ref[kvh, pl.ds(w0, wlen), :], kvh) for kvh in KV]
    va = [_with_ones(v_win, kvh) for kvh in KV]
    sg = [jnp.where(valid, jnp.dot(ka[kvh], q_pos[kvh][g], preferred_element_type=f32), NEG_INF) for kvh, g in ch]
    p = [jnp.exp2(x - jnp.max(x, axis=0, keepdims=True)).astype(bf16) for x in sg]
    o_win = [jnp.dot(va[kvh], p[i], preferred_element_type=f32) for i, (kvh, g) in enumerate(ch)]
    heads = []
    for i, (kvh, g) in enumerate(ch):
        h = kvh * NSA_GROUP + g
        d0, l0 = (0, half) if kvh == 0 else (half, 0)
        o_slc = acc_s[kvh, :, g * tq:(g + 1) * tq]
        heads.append(gates[3 * h:3 * h + 1] * o_cmp[kvh][g][d0:d0 + half]
                     + gates[3 * h + 1:3 * h + 2] * (o_slc[d0:d0 + half] / o_slc[l0:l0 + 1])
                     + gates[3 * h + 2:3 * h + 3] * (o_win[i][d0:d0 + half] / o_win[i][l0:l0 + 1]))
    o_ref[0] = jnp.concatenate(heads, axis=0).T.astype(o_ref.dtype)


def nsa_prompt(qT, gT, sk, svT, wk, wvT, kc, vcT, B, tq=128, kt=512):
    N = sk.shape[0]
    T = N // B
    ns = T // SLC_BLOCK
    nq = T // tq
    ncp = kc.shape[1]
    assert T % kt == 0 and T % tq == 0 and kt % tq == 0 and T >= WINDOW + tq and T <= POS_SPLIT * 256
    assert F_SEL0 + ns <= HEAD_DIM
    kf, cf = feature_tables(T, ncp)
    keys = pl.BlockSpec((T, LANES), lambda b, i: (b, 0))
    vals = pl.BlockSpec((LANES, T), lambda b, i: (0, b))
    return pl.pallas_call(
        functools.partial(_nsa_prompt_kernel, tq=tq, kt=kt, ns=ns),
        grid=(B, nq),
        in_specs=[pl.BlockSpec((NSA_WIDTH, tq), lambda b, i: (0, b * nq + i)),
                  pl.BlockSpec((3 * NSA_HEADS, tq), lambda b, i: (0, b * nq + i)),
                  keys, vals, keys, vals,
                  pl.BlockSpec((1, ncp, LANES), lambda b, i: (b, 0, 0)),
                  pl.BlockSpec((1, LANES, ncp), lambda b, i: (b, 0, 0)),
                  pl.BlockSpec((NSA_KV_HEADS, T, LANES), lambda b, i: (0, 0, 0)),
                  pl.BlockSpec((NSA_KV_HEADS, ncp, LANES), lambda b, i: (0, 0, 0))],
        out_specs=pl.BlockSpec((1, tq, NSA_WIDTH), lambda b, i: (b, i, 0)),
        out_shape=jax.ShapeDtypeStruct((B, T, NSA_WIDTH), bf16),
        scratch_shapes=[pltpu.VMEM((NSA_KV_HEADS, 1, NSA_GROUP * tq), f32),
                        pltpu.VMEM((NSA_KV_HEADS, LANES, NSA_GROUP * tq), f32)],
        compiler_params=_cparams("parallel", "arbitrary"),
        name="nsa_prompt",
    )(qT, gT, sk, svT, wk, wvT, kc, vcT, kf, cf)


SF_SPLIT = 64
SF_BLK0 = 16
SF_ROWS = 256


def sample_key_features(past_len, nk, tq, ns):
    sf_new0 = SF_BLK0 + -(-ns // 8) * 8
    assert sf_new0 + tq + 1 <= SF_ROWS and nk <= SF_SPLIT * 256
    pos = np.arange(nk)[None, :]
    row = np.arange(SF_ROWS)[:, None]
    hi = ((pos // SF_SPLIT) * SF_SPLIT).astype(np.float32)
    lo = (pos % SF_SPLIT).astype(np.float32)
    f = np.where(row < 3, hi, np.where(row < 6, lo, np.where(row < 9, 1.0, 0.0)))
    blk = (row >= SF_BLK0) & (row < SF_BLK0 + ns) & (pos // SLC_BLOCK == row - SF_BLK0)
    new = (row >= sf_new0) & (row < sf_new0 + tq) & (pos - past_len == row - sf_new0)
    pad = (row == sf_new0 + tq) & (pos >= past_len + tq)
    return jnp.asarray(np.where(blk | new | pad, 1.0, f).astype(bf16)), sf_new0


def _sample_query_features(sel, pos_col, kvh, tq, ns, sf_new0):
    lane = lax.broadcasted_iota(jnp.int32, (tq, SF_ROWS), 1)
    t_idx = lax.broadcasted_iota(jnp.int32, (tq, SF_ROWS), 0)
    selbig = (pltpu.roll(sel[:, 0:SF_ROWS], SF_BLK0, 1) - 1.0) * MASK_BIG
    base = jnp.where((lane >= SF_BLK0) & (lane < SF_BLK0 + ns), selbig, 0.0)
    base = jnp.where((lane >= sf_new0) & (lane < sf_new0 + tq) & (lane - sf_new0 > t_idx), -MASK_BIG, base)
    base = jnp.where(lane == sf_new0 + tq, -MASK_BIG, base)
    posf = pos_col.astype(f32)
    out = []
    for g in range(NSA_GROUP):
        s = jnp.full((tq, 1), SLOPES[kvh][g], f32)
        cols = [p.astype(f32) for p in _split3(s)] * 2 + [-p.astype(f32) for p in _split3(s * posf)]
        qf = base
        for i, cv in enumerate(cols):
            qf = jnp.where(lane == i, cv, qf)
        out.append(qf)
    return jnp.concatenate(out, axis=0).astype(bf16)


def _new_rows_t(rows, width):
    tq = rows.shape[0]
    sq = jnp.concatenate([rows, jnp.zeros((LANES - tq, LANES), f32)], axis=0).T
    if width == LANES:
        return sq
    return jnp.concatenate([sq, jnp.zeros((LANES, width - LANES), f32)], axis=1)


def _compress_staged(xs, t, nch, pe_ref, w_ref, w2_ref):
    lhs = jnp.concatenate([jnp.concatenate([xs[2 * t + h, j] for j in range(CMP_STRIDE // 2)], axis=1)
                           for h in range(NSA_KV_HEADS)], axis=0)
    acc = jnp.dot(lhs, w_ref[...], preferred_element_type=f32)
    bc = lambda v: jnp.broadcast_to(v, (8, v.shape[1])).astype(bf16)
    c = (jnp.dot(bc(pe_ref[0:1, :]), w_ref[:, 0:CMP_HIDDEN], preferred_element_type=f32)
         + jnp.dot(bc(pe_ref[1:2, :]), w_ref[:, CMP_HIDDEN:], preferred_element_type=f32))[0:1, :]
    outs = []
    for h in range(NSA_KV_HEADS):
        r = slice(h * nch, (h + 1) * nch)
        hid = acc[r, 0:CMP_HIDDEN] + pltpu.roll(acc[r, CMP_HIDDEN:], nch - 1, 0) + c
        outs.append(jnp.dot(_silu(hid).astype(bf16), w2_ref[...], preferred_element_type=f32))
    return jnp.concatenate(outs, axis=1)


def _nsa_sample_kernel(pt_ref, *refs, pps, n_steps, tq, past_len, ns, nsp, sf_new0):
    page_refs = refs[:pps]
    (q_ref, g_ref, rn_ref, wn_ref, cw_ref, ft_ref, perm_ref, pek_ref, pev_ref, wk_ref, wv_ref, w2k_ref, w2v_ref,
     o_ref, xs_s, kT_s, vT_s) = refs[pps:]
    del pt_ref
    step = pl.program_id(1)
    cpp = PAGE_SIZE // CMP_STRIDE
    staged = []
    for i in range(pps):
        r0 = pl.multiple_of((step * pps + i) * PAGE_SIZE, PAGE_SIZE)
        pg = page_refs[i]
        slabs = []
        for th in range(4):
            x = pg[0, th]
            slabs += [x, pltpu.roll(x, PAGE_SIZE - 1, 1)]
        a = jnp.concatenate(slabs, axis=0).astype(bf16)
        staged.append(lax.dot_general(perm_ref[...], a, NT, preferred_element_type=f32))
        kT_s[:, pl.ds(r0, PAGE_SIZE)] = pg[0, 4:6].reshape(LANES, PAGE_SIZE).astype(bf16)
        vT_s[:, pl.ds(r0, PAGE_SIZE)] = pg[0, 6:8].reshape(LANES, PAGE_SIZE).astype(bf16)
    c0 = pl.multiple_of(step * (pps * cpp), pps * cpp)
    for j in range(CMP_STRIDE // 2):
        rows = slice(j * cpp, (j + 1) * cpp)
        both = jnp.concatenate([t[rows] for t in staged], axis=0).astype(bf16)
        for th in range(4):
            xs_s[th, j, pl.ds(c0, pps * cpp), :] = both[:, th * LANES:(th + 1) * LANES]

    @pl.when(step == n_steps - 1)
    def _():
        nk = kT_s.shape[1]
        rn = rn_ref[0]
        kT_s[:, past_len:] = _new_rows_t(rn[:, 2 * LANES:3 * LANES], nk - past_len).astype(bf16)
        vT_s[:, past_len:] = _new_rows_t(rn[:, 3 * LANES:4 * LANES], nk - past_len).astype(bf16)
        nwin = cw_ref.shape[2]
        wn = wn_ref[0]
        wkT = jnp.concatenate([cw_ref[0, 0:LANES, :], _new_rows_t(wn[:, 0:LANES], LANES)], axis=1).astype(bf16)
        wvT = jnp.concatenate([cw_ref[0, LANES:, :], _new_rows_t(wn[:, LANES:], LANES)], axis=1).astype(bf16)
        wlen = nwin + LANES

        nch = past_len // CMP_STRIDE
        kc = _compress_staged(xs_s, 0, nch, pek_ref, wk_ref, w2k_ref)
        vc = _compress_staged(xs_s, 1, nch, pev_ref, wv_ref, w2v_ref)

        pos_col = past_len + lax.broadcasted_iota(jnp.int32, (tq, 1), 0)
        gates = _sigmoid(g_ref[0])
        q_all = q_ref[0]
        wpos = (past_len - nwin) + lax.broadcasted_iota(jnp.int32, (tq, wlen), 1)
        wdist = pos_col - wpos
        wvalid = (wdist >= 0) & (wdist < WINDOW)
        wdistf = wdist.astype(f32)
        qms, qfs, o_cmps = [], [], []
        for kvh in range(NSA_KV_HEADS):
            qm = _padded_queries(q_all, kvh, tq)
            o_cmp, p_sum = _cmp_branch(qm, kc, vc, pos_col, kvh, tq)
            sel = _select_blocks(p_sum, pos_col, ns, nsp, tq)
            qms.append(qm)
            qfs.append(_sample_query_features(sel, pos_col, kvh, tq, ns, sf_new0))
            o_cmps.append(o_cmp)
        qm2 = jnp.concatenate(qms, axis=0)
        s = (jnp.dot(qm2, kT_s[...], preferred_element_type=f32)
             + jnp.dot(jnp.concatenate(qfs, axis=0), ft_ref[...], preferred_element_type=f32))
        e = jnp.exp(s - jnp.max(s, axis=-1, keepdims=True))
        o_slc2 = (lax.dot_general(e.astype(bf16), vT_s[...], NT, preferred_element_type=f32)
                  / jnp.sum(e, axis=-1, keepdims=True))
        sw2 = jnp.dot(qm2, wkT, preferred_element_type=f32)
        pw = []
        for kvh in range(NSA_KV_HEADS):
            for g in range(NSA_GROUP):
                r = slice((kvh * NSA_GROUP + g) * tq, (kvh * NSA_GROUP + g + 1) * tq)
                sg = jnp.where(wvalid, sw2[r] - SLOPES[kvh][g] * wdistf, NEG_INF)
                ew = jnp.where(wvalid, jnp.exp(sg - jnp.max(sg, axis=-1, keepdims=True)), 0.0)
                pw.append(ew / jnp.sum(ew, axis=-1, keepdims=True))
        o_win2 = lax.dot_general(jnp.concatenate(pw, axis=0).astype(bf16), wvT, NT, preferred_element_type=f32)
        outs = []
        for kvh in range(NSA_KV_HEADS):
            r = slice(kvh * NSA_GROUP * tq, (kvh + 1) * NSA_GROUP * tq)
            outs.extend(_combine(gates, kvh, tq, o_cmps[kvh], o_slc2[r], o_win2[r]))
        o_ref[0] = jnp.concatenate(outs, axis=1).astype(o_ref.dtype)


def nsa_sample(page_table, cacheT, q, gates, rows_new, win_new, cache_winT, cw, pps=16):
    B, tq, _ = q.shape
    n_pages = page_table.shape[1]
    past_len = n_pages * PAGE_SIZE
    assert n_pages % pps == 0 and past_len % SLC_BLOCK == 0
    n_steps = n_pages // pps
    nk = past_len + 512
    ns = past_len // SLC_BLOCK + 1
    nsp = -(-ns // LANES) * LANES
    nch = past_len // CMP_STRIDE
    nwin = cache_winT.shape[2]
    pek, wk, w2k, pev, wv, w2v = cw
    ft, sf_new0 = sample_key_features(past_len, nk, tq, ns)
    cpp = PAGE_SIZE // CMP_STRIDE
    ro = np.arange(PAGE_SIZE)
    ro2 = np.arange(PAGE_SIZE // 2)
    perm = jnp.asarray((ro[None, :] == (ro2[:, None] % cpp) * CMP_STRIDE + 2 * (ro2[:, None] // cpp))
                       .astype(bf16))
    per_b = lambda n: pl.BlockSpec((1, tq, n), lambda b, s, pt: (b, 0, 0))
    full = lambda a: pl.BlockSpec(a.shape, lambda b, s, pt: (0,) * a.ndim)
    page_specs = [pl.BlockSpec((1, 8, HEAD_DIM, PAGE_SIZE),
                               functools.partial(lambda b, s, pt, i: (pt[b, s * pps + i], 0, 0, 0), i=i))
                  for i in range(pps)]
    grid_spec = pltpu.PrefetchScalarGridSpec(
        num_scalar_prefetch=1,
        grid=(B, n_steps),
        in_specs=page_specs + [per_b(NSA_HEADS * LANES), per_b(3 * NSA_HEADS), per_b(4 * LANES), per_b(2 * LANES),
                               pl.BlockSpec((1, 2 * LANES, nwin), lambda b, s, pt: (b, 0, 0)),
                               full(ft), full(perm), full(pek), full(pev), full(wk), full(wv), full(w2k), full(w2v)],
        out_specs=per_b(NSA_WIDTH),
        scratch_shapes=[pltpu.VMEM((4, CMP_STRIDE // 2, nch, LANES), bf16),
                        pltpu.VMEM((LANES, nk), bf16), pltpu.VMEM((LANES, nk), bf16)])
    return pl.pallas_call(
        functools.partial(_nsa_sample_kernel, pps=pps, n_steps=n_steps, tq=tq, past_len=past_len, ns=ns, nsp=nsp,
                          sf_new0=sf_new0),
        grid_spec=grid_spec,
        out_shape=jax.ShapeDtypeStruct((B, tq, NSA_WIDTH), bf16),
        compiler_params=_cparams("parallel", "arbitrary"),
        name="nsa_sample",
    )(page_table, *([cacheT] * pps), q, gates, rows_new, win_new, cache_winT, ft, perm, pek, pev, wk, wv, w2k, w2v)


def _gdn_prep_kernel(x_ref, halo_ref, cb_ref, cw_ref, ba_ref, al_ref, dtb_ref,
                     u_ref, w_ref, qg_ref, kg_ref, qk_ref, eg_ref, xs_s, *, tb, C, t_valid, nb):
    i = pl.program_id(1)
    HK = GDN_HEADS * GDN_DK
    R = min(tb, LANES)
    cpb = R // C

    @pl.when(i == 0)
    def _():
        xs_s[:, 0:8, :] = cb_ref[...]

    @pl.when(i > 0)
    def _():
        xs_s[:, 0:8, :] = halo_ref[...]

    xs_s[:, 8:, :] = x_ref[...]
    ri = lax.broadcasted_iota(jnp.int32, (R, R), 0)
    ci = lax.broadcasted_iota(jnp.int32, (R, R), 1)
    same = (ri // C) == (ci // C)
    lower = same & (ri >= ci)
    strict = same & (ri > ci)
    eye = (ri == ci).astype(f32)
    rr = lax.broadcasted_iota(jnp.int32, (R, LANES), 0)
    row_t = i * tb + lax.broadcasted_iota(jnp.int32, (tb, 1), 0)
    live = row_t < t_valid

    def conv_act(bb, c0):
        acc = cw_ref[GDN_CONV - 1:GDN_CONV, c0:c0 + LANES] * xs_s[bb, pl.ds(8, tb), c0:c0 + LANES]
        for k in range(1, GDN_CONV):
            acc = acc + cw_ref[GDN_CONV - 1 - k:GDN_CONV - k, c0:c0 + LANES] * xs_s[bb, pl.ds(8 - k, tb), c0:c0 + LANES]
        return _silu(acc)

    chains = []
    for bb, h in [(bb, h) for bb in range(nb) for h in range(GDN_HEADS)]:
        ba = ba_ref[bb]
        q = conv_act(bb, h * GDN_DK)
        k = conv_act(bb, HK + h * GDN_DK)
        v = conv_act(bb, 2 * HK + h * GDN_DV)
        q = q * lax.rsqrt(jnp.sum(q * q, axis=-1, keepdims=True) + L2_EPS) * (GDN_DK ** -0.5)
        k = k * lax.rsqrt(jnp.sum(k * k, axis=-1, keepdims=True) + L2_EPS)
        beta = _sigmoid(ba[:, h:h + 1])
        ar = ba[:, GDN_HEADS + h:GDN_HEADS + h + 1] + dtb_ref[:, h:h + 1]
        softplus = jnp.maximum(ar, 0.0) + jnp.log(1.0 + jnp.exp(-jnp.abs(ar)))
        gcol = -jnp.exp(al_ref[:, h:h + 1]) * softplus
        if t_valid < 10 ** 9:
            beta = jnp.where(live, beta, 0.0)
            gcol = jnp.where(live, gcol, 0.0)
            k = jnp.where(live, k, 0.0)
            v = jnp.where(live, v, 0.0)
        for blk in range(tb // R):
            r = slice(blk * R, (blk + 1) * R)
            qc, kc, vc, bc = q[r], k[r], v[r], beta[r]
            gc = jnp.broadcast_to(gcol[r], (R, LANES))
            sh = 1
            while sh < C:
                gc = gc + jnp.where((rr % C) >= sh, pltpu.roll(gc, sh, 0), 0.0)
                sh *= 2
            gct = gc.T
            decay = jnp.exp(jnp.where(lower, gc[:, 0:R] - gct[0:R, :], NEG_INF))
            kb = kc * bc
            a = jnp.where(strict, _dot_hp(kb, kc, NT) * decay, 0.0)
            qk = jnp.where(lower, lax.dot_general(qc.astype(bf16), kc.astype(bf16), NT,
                                                  preferred_element_type=f32) * decay, 0.0)
            gl = gc[R - 1:R, :]
            for j in range(cpb - 2, -1, -1):
                gl = jnp.where(rr < (j + 1) * C, gc[(j + 1) * C - 1:(j + 1) * C, :], gl)
            chains.append(dict(bb=bb, h=h, r=r, blk=blk, qc=qc, kc=kc, gc=gc, gl=gl, qk=qk, n=-a, t=eye - a,
                               rhs=jnp.concatenate([vc * bc, kb * jnp.exp(gc)], axis=1)))

    for ch in chains:
        ch['n'] = _split2(ch['n'])
    sh = 2
    while sh < C:
        for ch in chains:
            ch['n'] = _split2(_dot_parts(ch['n'], ch['n']))
        for ch in chains:
            ch['t'] = ch['t'] + _dot_parts(_split2(ch['t']), ch['n'])
        sh *= 2

    for ch in chains:
        bb, h, r, gc, gl = ch['bb'], ch['h'], ch['r'], ch['gc'], ch['gl']
        sol = ch['rhs'] + _dot_hp(ch['t'] - eye, ch['rhs'])
        cs = slice(h * GDN_DV, (h + 1) * GDN_DV)
        u_ref[bb, r, cs] = sol[:, 0:GDN_DV]
        w_ref[bb, r, cs] = sol[:, GDN_DV:].astype(w_ref.dtype)
        qg_ref[bb, r, cs] = (ch['qc'] * jnp.exp(gc)).astype(qg_ref.dtype)
        kg_ref[bb, r, cs] = (ch['kc'] * jnp.exp(gl - gc)).astype(kg_ref.dtype)
        qk = ch['qk'][:, 0:C]
        for j in range(1, cpb):
            qk = jnp.where(ri[:, 0:C] >= j * C, ch['qk'][:, j * C:(j + 1) * C], qk)
        qk_ref[bb, r, h * C:(h + 1) * C] = qk.astype(qk_ref.dtype)
        for j in range(cpb):
            eg_ref[bb, ch['blk'] * cpb + j, h:h + 1, :] = jnp.exp(gc[(j + 1) * C - 1:(j + 1) * C, :])
    eg_ref[:, :, GDN_HEADS:, :] = jnp.zeros((nb, tb // C, 8 - GDN_HEADS, LANES), f32)


def gdn_prep(qkv, conv_buf8, conv_w, ba, a_log, dt_bias, C, tb, t_valid, nb=1):
    B, T, CH = qkv.shape
    nb = math.gcd(nb, B)
    hb = tb // 8
    per = lambda n, dt: (pl.BlockSpec((nb, tb, n), lambda b, i: (b, i, 0)), jax.ShapeDtypeStruct((B, T, n), dt))
    outs = [per(GDN_WIDTH, f32), per(GDN_WIDTH, bf16), per(GDN_WIDTH, bf16), per(GDN_WIDTH, bf16),
            per(GDN_HEADS * C, bf16),
            (pl.BlockSpec((nb, tb // C, 8, LANES), lambda b, i: (b, i, 0, 0)),
             jax.ShapeDtypeStruct((B, T // C, 8, LANES), f32))]
    return pl.pallas_call(
        functools.partial(_gdn_prep_kernel, tb=tb, C=C, t_valid=t_valid, nb=nb),
        grid=(B // nb, T // tb),
        in_specs=[pl.BlockSpec((nb, tb, CH), lambda b, i: (b, i, 0)),
                  pl.BlockSpec((nb, 8, CH), lambda b, i: (b, jnp.maximum(i * hb - 1, 0), 0)),
                  pl.BlockSpec((nb, 8, CH), lambda b, i: (b, 0, 0)),
                  pl.BlockSpec((GDN_CONV, CH), lambda b, i: (0, 0)),
                  pl.BlockSpec((nb, tb, 2 * GDN_HEADS), lambda b, i: (b, i, 0)),
                  pl.BlockSpec((1, GDN_HEADS), lambda b, i: (0, 0)),
                  pl.BlockSpec((1, GDN_HEADS), lambda b, i: (0, 0))],
        out_specs=[o[0] for o in outs],
        out_shape=[o[1] for o in outs],
        scratch_shapes=[pltpu.VMEM((nb, tb + 8, CH), f32)],
        compiler_params=_cparams("parallel", "arbitrary"),
        name="gdn_prep",
    )(qkv, qkv, conv_buf8, conv_w, ba, a_log.reshape(1, GDN_HEADS), dt_bias.reshape(1, GDN_HEADS))


def _gdn_scan_kernel(u_ref, w_ref, qg_ref, kg_ref, qk_ref, eg_ref, z_ref, ng_ref, s0_ref,
                     o_ref, sT_ref, s_s, *, C, nb):
    c = pl.program_id(1)

    @pl.when(c == 0)
    def _():
        s_s[...] = s0_ref[...]

    ch = [(b, h) for b in range(nb) for h in range(GDN_HEADS)]
    cs = [slice(h * GDN_DV, (h + 1) * GDN_DV) for h in range(GDN_HEADS)]
    S = [s_s[b, h] for b, h in ch]
    Sb = [x.astype(bf16) for x in S]
    v_new = [u_ref[b, :, cs[h]] - jnp.dot(w_ref[b, :, cs[h]], Sb[i], preferred_element_type=f32)
             for i, (b, h) in enumerate(ch)]
    vb = [x.astype(bf16) for x in v_new]
    for i, (b, h) in enumerate(ch):
        s_s[b, h] = S[i] * eg_ref[b, 0, h:h + 1, :] + lax.dot_general(kg_ref[b, :, cs[h]], vb[i], TN,
                                                                       preferred_element_type=f32)
    o = [jnp.dot(qg_ref[b, :, cs[h]], Sb[i], preferred_element_type=f32)
         + jnp.dot(qk_ref[b, :, h * C:(h + 1) * C], vb[i], preferred_element_type=f32) for i, (b, h) in enumerate(ch)]
    outs = [_rms(o[i], ng_ref[...]) * _silu(z_ref[b, :, cs[h]]) for i, (b, h) in enumerate(ch)]
    for b in range(nb):
        o_ref[b] = jnp.concatenate(outs[b * GDN_HEADS:(b + 1) * GDN_HEADS], axis=1).astype(o_ref.dtype)

    @pl.when(c == pl.num_programs(1) - 1)
    def _():
        sT_ref[...] = s_s[...]


def gdn_scan(u, w, qg, kg, qk, eg, z, norm_g, s0, C, nb=8):
    B, T, _ = u.shape
    nb = math.gcd(nb, B)
    per = lambda n: pl.BlockSpec((nb, C, n), lambda b, c: (b, c, 0))
    st = pl.BlockSpec((nb, GDN_HEADS, GDN_DK, GDN_DV), lambda b, c: (b, 0, 0, 0))
    return pl.pallas_call(
        functools.partial(_gdn_scan_kernel, C=C, nb=nb),
        grid=(B // nb, T // C),
        in_specs=[per(GDN_WIDTH), per(GDN_WIDTH), per(GDN_WIDTH), per(GDN_WIDTH), per(GDN_HEADS * C),
                  pl.BlockSpec((nb, 1, 8, LANES), lambda b, c: (b, c, 0, 0)),
                  per(GDN_WIDTH), pl.BlockSpec((1, GDN_DV), lambda b, c: (0, 0)), st],
        out_specs=[per(GDN_WIDTH), st],
        out_shape=[jax.ShapeDtypeStruct((B, T, GDN_WIDTH), bf16),
                   jax.ShapeDtypeStruct((B, GDN_HEADS, GDN_DK, GDN_DV), f32)],
        scratch_shapes=[pltpu.VMEM((nb, GDN_HEADS, GDN_DK, GDN_DV), f32)],
        compiler_params=_cparams("parallel", "arbitrary"),
        name="gdn_scan",
    )(u, w, qg, kg, qk, eg, z, norm_g.reshape(1, GDN_DV), s0)


def _nsa_in_groups(w, feature_major):
    c1 = NSA_WIDTH
    c2 = c1 + 4 * NSA_KV_COLS
    c3 = c2 + 2 * NSA_KV_COLS
    c4 = c3 + 3 * NSA_HEADS
    wT = w.T.astype(bf16)
    xa_scale = XA_DIM ** -0.5 * LOG2E
    if feature_major:
        k0 = c1 + 2 * NSA_KV_COLS
        return [(wT[c1:c2], f32, 1.0, FEATURE_MAJOR_PER_BATCH),
                (wT[c1:k0], f32, 1.0, TOKEN_MAJOR_WT),
                (wT[c2:c3], f32, 1.0, FEATURE_MAJOR_PER_BATCH),
                (wT[c4:], bf16, xa_scale, FEATURE_MAJOR),
                (wT[:c1], bf16, HEAD_DIM ** -0.5 * LOG2E, FEATURE_MAJOR),
                (wT[c3:c4], f32, 1.0, FEATURE_MAJOR),
                (wT[k0:k0 + NSA_KV_COLS], bf16, 1.0, TOKEN_MAJOR_WT),
                (wT[k0 + NSA_KV_COLS:c2], bf16, 1.0, FEATURE_MAJOR),
                (wT[c2:c2 + NSA_KV_COLS], bf16, 1.0, TOKEN_MAJOR_WT),
                (wT[c2 + NSA_KV_COLS:c3], bf16, 1.0, FEATURE_MAJOR)]
    wq = wT[:c1].reshape(NSA_KV_HEADS, NSA_GROUP, HEAD_DIM, -1)
    zq = jnp.zeros_like(wq[0])
    wq = jnp.concatenate([jnp.concatenate([wq[0], zq], axis=1), jnp.concatenate([zq, wq[1]], axis=1)], axis=0)
    wq = wq.reshape(NSA_HEADS * LANES, -1)
    return [(wT[c1:c2], f32, 1.0, TOKEN_MAJOR_WT), (wT[c2:c3], f32, 1.0, TOKEN_MAJOR_WT),
            (wT[c4:], bf16, xa_scale, TOKEN_MAJOR_WT), (wq, bf16, HEAD_DIM ** -0.5, TOKEN_MAJOR_WT),
            (wT[c3:c4], f32, 1.0, TOKEN_MAJOR_WT)]


def _gdn_in_groups(w, feature_major):
    c1 = GDN_CONV_CH
    c2 = c1 + GDN_WIDTH
    c3 = c2 + 2 * GDN_HEADS
    wT = w.T.astype(bf16)
    xq = (wT[c3:], bf16, XA_DIM ** -0.5 * LOG2E, FEATURE_MAJOR if feature_major else TOKEN_MAJOR_WT)
    return [(wT[:c1], f32, 1.0, TOKEN_MAJOR_WT), (wT[c1:c2], f32, 1.0, TOKEN_MAJOR_WT),
            (wT[c2:c3], f32, 1.0, TOKEN_MAJOR_WT), xq]


def _trunk(x, mem_kv, nsa_past, gdn_state, p, t_valid):
    B, T, D = x.shape
    N = B * T
    xf = x.reshape(N, D)

    prompt = nsa_past is None
    if prompt:
        kv_rows, rows_cmp, win_rows, xq, qT, gT, sk, svT, wk, wvT = norm_proj(
            xf, p['norm_mix_g'][0], p['nsa_in_prompt'], tm=512, rows_per_batch=T)
        kc, vcT = cmp_prompt(rows_cmp.reshape(B, T, -1), p['cmp'])
        o_mix = nsa_prompt(qT, gT, sk, svT, wk, wvT, kc, vcT, B)
    else:
        rows, win, xq, q, gates = norm_proj(xf, p['norm_mix_g'][0], p['nsa_in_sample'])
        kv_rows = rows.reshape(B, T, -1)
        win_rows = win.reshape(B, T, -1)
        xq = xq.reshape(B, T, -1)
        page_table, cache, cache_win = nsa_past
        o_mix = nsa_sample(page_table, cache, q.reshape(B, T, -1), gates.reshape(B, T, -1), kv_rows, win_rows,
                           cache_win, p['cmp_dense'])
    o_mem = cross_attend(xq, mem_kv, 0, B)
    xf = post_block(xf, o_mix.reshape(N, -1), o_mem.reshape(N, -1), p['w_out'][0], p['norm_ffn_g'][0],
                    p['w_up'][0], p['w_down'][0], p['final_norm_g'], final=False)

    qkv, z, ba, xq = norm_proj(xf, p['norm_mix_g'][1], p['gdn_in_prompt' if prompt else 'gdn_in_sample'], tm=512)
    if not prompt:
        xq = xq.reshape(B, T, -1)
    qkv3 = qkv.reshape(B, T, -1)
    S0, conv_buf8 = gdn_state
    C = min(64, T)
    tb = min(256, T)
    u, w, qg, kg, qk, eg = gdn_prep(qkv3, conv_buf8, p['gdn_conv_w'], ba.reshape(B, T, -1), p['gdn_a_log'],
                                    p['gdn_dt_bias'], C, tb, t_valid, nb=1 if prompt else 4)
    o_mix, S = gdn_scan(u, w, qg, kg, qk, eg, z.reshape(B, T, -1), p['gdn_norm_g'], S0, C)
    o_mem = cross_attend(xq, mem_kv, 1, B)
    y = post_block(xf, o_mix.reshape(N, -1), o_mem.reshape(N, -1), p['w_out'][1], p['norm_ffn_g'][1],
                   p['w_up'][1], p['w_down'][1], p['final_norm_g'], final=True)
    return y.reshape(B, T, D), kv_rows, win_rows, S, qkv3


def kernel(x_prompt, x_sample, mem_prompt, cache_nsa_kv, cache_nsa_win, state_gdn_s, state_gdn_conv,
           cache_mem_kv, page_table, norm_mix_g, norm_mem_g, w_mem_kv, nsa_w_in, cmp_pe_k, cmp_w1_k,
           cmp_w2_k, cmp_pe_v, cmp_w1_v, cmp_w2_v, gdn_w_in, gdn_conv_w, gdn_a_log, gdn_dt_bias,
           gdn_norm_g, w_out, norm_ffn_g, w_up, w_down, final_norm_g):
    B, T, D = x_prompt.shape
    DB, TS, _ = x_sample.shape
    M = mem_prompt.shape[1]
    p = dict(norm_mix_g=norm_mix_g, nsa_in_prompt=_nsa_in_groups(nsa_w_in[0], True),
             nsa_in_sample=_nsa_in_groups(nsa_w_in[0], False), gdn_in_prompt=_gdn_in_groups(gdn_w_in[0], True),
             gdn_in_sample=_gdn_in_groups(gdn_w_in[0], False),
             cmp=_cmp_weights(cmp_pe_k[0], cmp_w1_k[0], cmp_w2_k[0]) + _cmp_weights(cmp_pe_v[0], cmp_w1_v[0], cmp_w2_v[0]),
             cmp_dense=(_cmp_weights_dense(cmp_pe_k[0], cmp_w1_k[0], cmp_w2_k[0])
                        + _cmp_weights_dense(cmp_pe_v[0], cmp_w1_v[0], cmp_w2_v[0])),
             gdn_conv_w=gdn_conv_w[0], gdn_a_log=gdn_a_log[0], gdn_dt_bias=gdn_dt_bias[0], gdn_norm_g=gdn_norm_g[0],
             w_out=w_out.astype(bf16), norm_ffn_g=norm_ffn_g, w_up=w_up.astype(bf16), w_down=w_down.astype(bf16),
             final_norm_g=final_norm_g)

    mem_flat = mem_prompt.reshape(B * M, D)
    mem_kv = [norm_proj(mem_flat, norm_mem_g[i], [(w_mem_kv[i].astype(bf16).T, f32, 1.0, FEATURE_MAJOR_PER_BATCH)],
                        tm=M, rows_per_batch=M)[0] for i in range(DEPTH)]
    mem_kv_p = jnp.stack(mem_kv, axis=1)
    zero_state = (jnp.zeros((B, GDN_HEADS, GDN_DK, GDN_DV), f32), jnp.zeros((B, 8, GDN_CONV_CH), f32))
    y_p, rows_p, win_p, S_p, qkv_p = _trunk(x_prompt, mem_kv_p, None, zero_state, p, 10 ** 9)

    TP = 8
    x_s = jnp.pad(x_sample, ((0, 0), (0, TP - TS), (0, 0)))
    cache = cache_nsa_kv.transpose(0, 2, 3, 4, 5, 1).reshape(cache_nsa_kv.shape[0], -1, HEAD_DIM, PAGE_SIZE)
    nwin = cache_nsa_win.shape[2]
    cache_win = cache_nsa_win.transpose(0, 1, 3, 4, 5, 2).reshape(DB, -1, nwin)
    conv8 = jnp.pad(state_gdn_conv[:, 0], ((0, 0), (8 - (GDN_CONV - 1), 0), (0, 0)))
    mem_kv_s = cache_mem_kv.transpose(0, 1, 3, 4, 5, 2).reshape(DB, DEPTH, 2 * XA_WIDTH, M)
    y_s, rows_s, win_s, S_s, qkv_s = _trunk(x_s, mem_kv_s, (page_table, cache, cache_win),
                                            (state_gdn_s[:, 0], conv8), p, TS)

    kvshape = (4, NSA_KV_HEADS, HEAD_DIM)
    nsa_kv_prompt = rows_p.reshape(B, 1, *kvshape, T).transpose(0, 5, 1, 2, 3, 4)
    nsa_kv_sample = rows_s[:, :TS].reshape(DB, TS, 1, *kvshape)
    wshape = (2, NSA_KV_HEADS, HEAD_DIM)
    wkeep = min(WINDOW, T)
    nsa_win_prompt = win_p[:, :, T - wkeep:].reshape(B, 1, *wshape, wkeep).transpose(0, 1, 5, 2, 3, 4)
    win_cat = jnp.concatenate([cache_win, win_s[:, :TS].transpose(0, 2, 1)], axis=2)[:, :, -WINDOW:]
    nsa_win_sample = win_cat.reshape(DB, 1, *wshape, -1).transpose(0, 1, 5, 2, 3, 4)
    gdn_conv_prompt = qkv_p[:, None, T - (GDN_CONV - 1):]
    conv_cat = jnp.concatenate([state_gdn_conv[:, 0], qkv_s[:, :TS]], axis=1)
    gdn_conv_sample = conv_cat[:, None, -(GDN_CONV - 1):]
    mem_kv_prompt = mem_kv_p.reshape(B, DEPTH, 2, XA_HEADS, XA_DIM, M).transpose(0, 1, 5, 2, 3, 4)
    return (y_p, y_s[:, :TS], nsa_kv_prompt, nsa_kv_sample, nsa_win_prompt, nsa_win_sample,
            S_p[:, None], S_s[:, None], gdn_conv_prompt, gdn_conv_sample, mem_kv_prompt)
```

```python
import functools
import math

import jax
import jax.numpy as jnp
import numpy as np
from jax import lax
from jax.experimental import pallas as pl
from jax.experimental.pallas import tpu as pltpu

f32 = jnp.float32
bf16 = jnp.bfloat16

DEPTH = 2
PAGE_SIZE = 128
HEAD_DIM = 64
NSA_HEADS = 12
NSA_KV_HEADS = 2
NSA_GROUP = NSA_HEADS // NSA_KV_HEADS
NSA_WIDTH = NSA_HEADS * HEAD_DIM
NSA_KV_COLS = NSA_KV_HEADS * HEAD_DIM
CMP_STRIDE = 16
CMP_BLOCK = 2 * CMP_STRIDE
CMP_HIDDEN = 128
SLC_BLOCK = 64
SLC_TOPN = 16
WINDOW = 512
GDN_HEADS = 6
GDN_DK = 128
GDN_DV = 128
GDN_WIDTH = GDN_HEADS * GDN_DV
GDN_CONV = 4
GDN_CONV_CH = 2 * GDN_HEADS * GDN_DK + GDN_HEADS * GDN_DV
XA_HEADS = 4
XA_DIM = 64
XA_WIDTH = XA_HEADS * XA_DIM
NORM_EPS = 1e-6
L2_EPS = 1e-6
NEG_INF = -1e30
FORCED = 1e9

VMEM_LIMIT = 52 * 1024 * 1024
LANES = 128

SLOPES = [[2.0 ** (-8.0 * (k * NSA_GROUP + g + 1) / NSA_HEADS) for g in range(NSA_GROUP)]
          for k in range(NSA_KV_HEADS)]

NT = (((1,), (1,)), ((), ()))
TN = (((0,), (0,)), ((), ()))


def _cparams(*sem):
    return pltpu.CompilerParams(dimension_semantics=sem, vmem_limit_bytes=VMEM_LIMIT)


def _rms(x, g):
    r = lax.rsqrt(jnp.mean(x * x, axis=-1, keepdims=True) + NORM_EPS)
    return (x * r) * g


def _sigmoid(x):
    return 1.0 / (1.0 + jnp.exp(-x))


def _silu(x):
    return x * _sigmoid(x)


def _split3(a):
    hi = a.astype(bf16)
    r1 = a - hi.astype(f32)
    mid = r1.astype(bf16)
    lo = (r1 - mid.astype(f32)).astype(bf16)
    return hi, mid, lo


def _dot_exact_rhs(a, b_bf16):
    hi, mid, lo = _split3(a)
    d = lambda x: jnp.dot(x, b_bf16, preferred_element_type=f32)
    return d(hi) + d(mid) + d(lo)


def _split2(a):
    hi = a.astype(bf16)
    return hi, (a - hi.astype(f32)).astype(bf16)


def _dot_parts(a, b, dims=None):
    if dims is None:
        d = lambda x, y: jnp.dot(x, y, preferred_element_type=f32)
    else:
        d = lambda x, y: lax.dot_general(x, y, dims, preferred_element_type=f32)
    return d(a[0], b[0]) + (d(a[0], b[1]) + d(a[1], b[0]))


def _dot_hp(a, b, dims=None):
    return _dot_parts(_split2(a), _split2(b), dims)


TOKEN_MAJOR, FEATURE_MAJOR, FEATURE_MAJOR_PER_BATCH = 'nt', 'fm', 'fmb'


def _norm_proj_kernel(x_ref, g_ref, *refs, specs):
    n = len(specs)
    w_refs, o_refs = refs[:n], refs[n:]
    h = _rms(x_ref[...], g_ref[...]).astype(bf16)
    for (scale, kind), w_ref, o_ref in zip(specs, w_refs, o_refs):
        if kind == TOKEN_MAJOR:
            y = lax.dot_general(h, w_ref[...], NT, preferred_element_type=f32)
        else:
            y = lax.dot_general(w_ref[...], h, NT, preferred_element_type=f32)
        if scale != 1.0:
            y = y * scale
        o_ref[...] = y.astype(o_ref.dtype).reshape(o_ref.shape)


def norm_proj(x, g, groups, tm=256, rows_per_batch=None):
    N, D = x.shape
    tm = min(tm, N)
    assert N % tm == 0
    in_specs = [pl.BlockSpec((tm, D), lambda i: (i, 0)), pl.BlockSpec((1, D), lambda i: (0, 0))]
    out_specs, out_shape = [], []
    for w, dt, _, kind in groups:
        in_specs.append(pl.BlockSpec(w.shape, lambda i: (0, 0)))
        n = w.shape[0]
        if kind == TOKEN_MAJOR:
            out_specs.append(pl.BlockSpec((tm, n), lambda i: (i, 0)))
            out_shape.append(jax.ShapeDtypeStruct((N, n), dt))
        elif kind == FEATURE_MAJOR:
            out_specs.append(pl.BlockSpec((n, tm), lambda i: (0, i)))
            out_shape.append(jax.ShapeDtypeStruct((n, N), dt))
        else:
            per = rows_per_batch // tm
            assert rows_per_batch % tm == 0
            out_specs.append(pl.BlockSpec((1, n, tm), lambda i: (i // per, 0, i % per)))
            out_shape.append(jax.ShapeDtypeStruct((N // rows_per_batch, n, rows_per_batch), dt))
    return pl.pallas_call(
        functools.partial(_norm_proj_kernel, specs=tuple((s, t) for _, _, s, t in groups)),
        grid=(N // tm,),
        in_specs=in_specs,
        out_specs=out_specs,
        out_shape=out_shape,
        compiler_params=_cparams("parallel"),
        name="norm_proj",
    )(x, g.reshape(1, D), *[w for w, _, _, _ in groups])


def _xattn_kernel(q_ref, kv_ref, o_ref, *, q_feature_major, nb):
    ch = [(b, h) for b in range(nb) for h in range(XA_HEADS)]
    hs = lambda h, off=0: slice(off + h * XA_DIM, off + (h + 1) * XA_DIM)
    qT = [q_ref[...] if q_feature_major else q_ref[b].astype(f32).T.astype(bf16) for b in range(nb)]
    s = [lax.dot_general(kv_ref[b, 0, hs(h), :].astype(bf16), qT[b][hs(h)], TN, preferred_element_type=f32)
         for b, h in ch]
    e = [jnp.exp2(x - jnp.max(x, axis=0, keepdims=True)) for x in s]
    o = [jnp.dot(kv_ref[b, 0, hs(h, XA_WIDTH), :].astype(bf16), e[i].astype(bf16), preferred_element_type=f32)
         / jnp.sum(e[i], axis=0, keepdims=True) for i, (b, h) in enumerate(ch)]
    for b in range(nb):
        o_ref[b] = jnp.concatenate(o[b * XA_HEADS:(b + 1) * XA_HEADS], axis=0).T.astype(o_ref.dtype)


def cross_attend(xq, kvT, layer, B, tq=2048, nb=4):
    q_feature_major = xq.ndim == 2
    T = xq.shape[1] // B if q_feature_major else xq.shape[1]
    M = kvT.shape[3]
    tq = min(tq, T)
    nq = T // tq
    if q_feature_major:
        nb = 1
        q_spec = pl.BlockSpec((XA_WIDTH, tq), lambda b, i: (0, b * nq + i))
    else:
        nb = math.gcd(nb, B)
        q_spec = pl.BlockSpec((nb, tq, XA_WIDTH), lambda b, i: (b, i, 0))
    return pl.pallas_call(
        functools.partial(_xattn_kernel, q_feature_major=q_feature_major, nb=nb),
        grid=(B // nb, nq),
        in_specs=[q_spec, pl.BlockSpec((nb, 1, 2 * XA_WIDTH, M), lambda b, i: (b, layer, 0, 0))],
        out_specs=pl.BlockSpec((nb, tq, XA_WIDTH), lambda b, i: (b, i, 0)),
        out_shape=jax.ShapeDtypeStruct((B, T, XA_WIDTH), bf16),
        compiler_params=_cparams("parallel", "parallel"),
        name="cross_attend",
    )(xq, kvT)


def _post_kernel(x_ref, om_ref, oc_ref, wo_ref, gf_ref, wu_ref, wd_ref, gl_ref, o_ref,
                 x1_s, h_s, acc_s, *, mix_w, final):
    j = pl.program_id(1)

    @pl.when(j == 0)
    def _():
        x1 = x_ref[...] + (jnp.dot(om_ref[...], wo_ref[0:mix_w, :], preferred_element_type=f32)
                           + jnp.dot(oc_ref[...], wo_ref[mix_w:, :], preferred_element_type=f32))
        x1_s[...] = x1
        h_s[...] = _rms(x1, gf_ref[...]).astype(bf16)
        acc_s[...] = jnp.zeros_like(acc_s)

    u = jnp.dot(h_s[...], wu_ref[...], preferred_element_type=f32)
    u = jnp.square(jnp.maximum(u, 0.0)).astype(bf16)
    acc_s[...] += jnp.dot(u, wd_ref[...], preferred_element_type=f32)

    @pl.when(j == pl.num_programs(1) - 1)
    def _():
        x2 = x1_s[...] + acc_s[...]
        if final:
            x2 = _rms(x2, gl_ref[...])
        o_ref[...] = x2


def post_block(x, o_mix, o_mem, w_out, g_ffn, w_up, w_down, g_final, final, tm=1024, tf=512):
    N, D = x.shape
    mix_w = o_mix.shape[1]
    F = w_up.shape[1]
    tm = min(tm, N)
    return pl.pallas_call(
        functools.partial(_post_kernel, mix_w=mix_w, final=final),
        grid=(N // tm, F // tf),
        in_specs=[pl.BlockSpec((tm, D), lambda i, j: (i, 0)),
                  pl.BlockSpec((tm, mix_w), lambda i, j: (i, 0)),
                  pl.BlockSpec((tm, XA_WIDTH), lambda i, j: (i, 0)),
                  pl.BlockSpec((mix_w + XA_WIDTH, D), lambda i, j: (0, 0)),
                  pl.BlockSpec((1, D), lambda i, j: (0, 0)),
                  pl.BlockSpec((D, tf), lambda i, j: (0, j)),
                  pl.BlockSpec((tf, D), lambda i, j: (j, 0)),
                  pl.BlockSpec((1, D), lambda i, j: (0, 0))],
        out_specs=pl.BlockSpec((tm, D), lambda i, j: (i, 0)),
        out_shape=jax.ShapeDtypeStruct((N, D), f32),
        scratch_shapes=[pltpu.VMEM((tm, D), f32), pltpu.VMEM((tm, D), bf16), pltpu.VMEM((tm, D), f32)],
        compiler_params=_cparams("parallel", "arbitrary"),
        name="post_block",
    )(x, o_mix, o_mem, w_out, g_ffn.reshape(1, D), w_up, w_down, g_final.reshape(1, D))


def _compress(load_chunk_rows, nch, pe_ref, w_ref, w2_ref):
    acc_p = jnp.zeros((nch, 2 * CMP_HIDDEN), f32)
    acc_q = jnp.zeros((nch, 2 * CMP_HIDDEN), f32)
    for l in range(CMP_STRIDE):
        x = load_chunk_rows(l)
        xp = (x + pe_ref[l:l + 1, :]).astype(bf16)
        xq = (x + pe_ref[CMP_STRIDE + l:CMP_STRIDE + l + 1, :]).astype(bf16)
        acc_p = acc_p + jnp.dot(xp, w_ref[l, :, 0:2 * CMP_HIDDEN], preferred_element_type=f32)
        acc_q = acc_q + jnp.dot(xq, w_ref[l, :, 2 * CMP_HIDDEN:], preferred_element_type=f32)
    hid = _silu(acc_p + pltpu.roll(acc_q, nch - 1, 0)).astype(bf16)
    outs = [jnp.dot(hid[:, h * CMP_HIDDEN:(h + 1) * CMP_HIDDEN], w2_ref[...], preferred_element_type=f32)
            for h in range(NSA_KV_HEADS)]
    return jnp.concatenate(outs, axis=1)


def _cmp_prompt_kernel(xk_ref, xv_ref, pek_ref, pev_ref, wk_ref, wv_ref, w2k_ref, w2v_ref, kc_ref, vc_ref, *, nch):
    kc_ref[0] = _compress(lambda l: xk_ref[0, pl.ds(l, nch, stride=CMP_STRIDE), :], nch, pek_ref, wk_ref, w2k_ref)
    vc_ref[0] = _compress(lambda l: xv_ref[0, pl.ds(l, nch, stride=CMP_STRIDE), :], nch, pev_ref, wv_ref, w2v_ref).T


def _cmp_weights(pe, w1, w2):
    pe2 = jnp.tile(pe, (1, NSA_KV_HEADS))
    z = jnp.zeros((CMP_STRIDE, HEAD_DIM, CMP_HIDDEN), w1.dtype)
    a, b = w1[:CMP_STRIDE], w1[CMP_STRIDE:]
    top = jnp.concatenate([a, z, b, z], axis=2)
    bot = jnp.concatenate([z, a, z, b], axis=2)
    wbd = jnp.concatenate([top, bot], axis=1).astype(bf16)
    return pe2, wbd, w2.astype(bf16)


def _cmp_weights_dense(pe, w1, w2):
    half = CMP_STRIDE * HEAD_DIM
    w = jnp.concatenate([w1[:CMP_STRIDE].reshape(half, CMP_HIDDEN), w1[CMP_STRIDE:].reshape(half, CMP_HIDDEN)], axis=1)
    return pe.reshape(2, half), w.astype(bf16), w2.astype(bf16)


def cmp_prompt(rows, cw):
    B, T, _ = rows.shape
    nch = T // CMP_STRIDE
    pek, wk, w2k, pev, wv, w2v = cw
    full = lambda a: pl.BlockSpec(a.shape, lambda b: (0,) * a.ndim)
    return pl.pallas_call(
        functools.partial(_cmp_prompt_kernel, nch=nch),
        grid=(B,),
        in_specs=[pl.BlockSpec((1, T, LANES), lambda b: (b, 0, 0)),
                  pl.BlockSpec((1, T, LANES), lambda b: (b, 0, 1)),
                  full(pek), full(pev), full(wk), full(wv), full(w2k), full(w2v)],
        out_specs=[pl.BlockSpec((1, nch, LANES), lambda b: (b, 0, 0)), pl.BlockSpec((1, LANES, nch), lambda b: (b, 0, 0))],
        out_shape=[jax.ShapeDtypeStruct((B, nch, LANES), f32), jax.ShapeDtypeStruct((B, LANES, nch), f32)],
        compiler_params=_cparams("parallel"),
        name="cmp_prompt",
    )(rows, rows, pek, pev, wk, wv, w2k, w2v)


def _padded_queries(q_all, kvh, tq):
    del tq
    lo = kvh * NSA_GROUP
    return jnp.concatenate([q_all[:, (lo + g) * LANES:(lo + g + 1) * LANES] for g in range(NSA_GROUP)], axis=0)


def _cmp_branch(qm, kc, vc, pos_col, kvh, tq):
    ncp = kc.shape[0]
    s = lax.dot_general(qm, kc.astype(bf16), NT, preferred_element_type=f32)
    n_idx = lax.broadcasted_iota(jnp.int32, (tq, ncp), 1)
    cdist = pos_col - (n_idx * CMP_STRIDE + (CMP_BLOCK - 1))
    cvis = cdist >= 0
    any_vis = pos_col >= (CMP_BLOCK - 1)
    cdf = cdist.astype(f32)
    p_sum = jnp.zeros((tq, ncp), f32)
    ps = []
    for g in range(NSA_GROUP):
        sg = jnp.where(cvis, s[g * tq:(g + 1) * tq] - SLOPES[kvh][g] * cdf, NEG_INF)
        m = jnp.max(sg, axis=-1, keepdims=True)
        e = jnp.exp(sg - m)
        p = e / jnp.sum(e, axis=-1, keepdims=True)
        p = jnp.where(any_vis, p, 0.0)
        p_sum = p_sum + p
        ps.append(p)
    p_all = jnp.concatenate(ps, axis=0).astype(bf16)
    o = jnp.dot(p_all, vc.astype(bf16), preferred_element_type=f32)
    return o, p_sum


def _select_blocks(p_sum, pos_col, ns, nsp, tq):
    ncp = p_sum.shape[1]
    ci = lax.broadcasted_iota(jnp.int32, (ncp, nsp), 0) * CMP_STRIDE
    cj = lax.broadcasted_iota(jnp.int32, (ncp, nsp), 1)
    c2s = ((ci < (cj + 1) * SLC_BLOCK) & (ci + CMP_BLOCK > cj * SLC_BLOCK)).astype(f32).astype(bf16)
    imp = _dot_exact_rhs(p_sum, c2s)
    blk = lax.broadcasted_iota(jnp.int32, (tq, nsp), 1)
    cur = pos_col // SLC_BLOCK
    forced = (blk == 0) | (blk == cur) | (blk == cur - 1)
    imp = jnp.where(forced, FORCED, jnp.where(blk <= cur, imp, NEG_INF))
    rank = jnp.zeros((tq, nsp), jnp.int32)
    for jp in range(ns):
        col = imp[:, jp:jp + 1]
        beats = (col > imp) | ((col == imp) & (blk > jp))
        rank = rank + beats.astype(jnp.int32)
    return ((rank < SLC_TOPN) & (blk <= cur)).astype(f32)


def _combine(gates, kvh, tq, o_cmp, o_slc, o_win):
    lo = kvh * HEAD_DIM
    outs = []
    for g in range(NSA_GROUP):
        h = kvh * NSA_GROUP + g
        r = slice(g * tq, (g + 1) * tq)
        o = (gates[:, 3 * h:3 * h + 1] * o_cmp[r] + gates[:, 3 * h + 1:3 * h + 2] * o_slc[r]
             + gates[:, 3 * h + 2:3 * h + 3] * o_win[r])
        outs.append(o[:, lo:lo + HEAD_DIM])
    return outs


LOG2E = 1.4426950408889634
SLOPES2 = [[v * LOG2E for v in row] for row in SLOPES]

POS_SPLIT = 8
F_SEL0 = 16
MASK_BIG = 1e30


def position_features(pos, blk, ns):
    n = pos.shape[0]
    lane = np.arange(HEAD_DIM)[None, :]
    hi = ((pos // POS_SPLIT) * POS_SPLIT).astype(np.float32)[:, None]
    lo = (pos % POS_SPLIT).astype(np.float32)[:, None]
    f = np.where(lane < 3, hi, np.where(lane < 6, lo, np.where(lane < 9, 1.0, 0.0)))
    if ns:
        f = np.where((lane >= F_SEL0) & (lane < F_SEL0 + ns), (blk[:, None] == lane - F_SEL0).astype(np.float32), f)
    return np.broadcast_to(f, (n, HEAD_DIM)).astype(np.float32)


def feature_tables(T, nch):
    pos = np.arange(T)
    kf = position_features(pos, pos // SLC_BLOCK, T // SLC_BLOCK)
    cpos = np.arange(nch) * CMP_STRIDE + (CMP_BLOCK - 1)
    cf = position_features(cpos, cpos, 0)
    z = np.zeros_like
    two = lambda a: jnp.asarray(np.stack([np.concatenate([z(a), a], axis=1),
                                          np.concatenate([a, z(a)], axis=1)]).astype(bf16))
    return two(kf), two(cf)


def _query_features(pos_row, selT, kvh, tq):
    posf = pos_row.astype(f32)
    r16 = lax.broadcasted_iota(jnp.int32, (16, tq), 0)
    zeros16 = jnp.zeros((16, tq), f32)
    selbig = (selT - 1.0) * MASK_BIG
    ns = selT.shape[0]
    pad = jnp.zeros((HEAD_DIM - F_SEL0 - ns, tq), f32)
    plain, full = [], []
    for g in range(NSA_GROUP):
        s = jnp.full((1, tq), SLOPES2[kvh][g], f32)
        s_parts = _split3(s)
        a_parts = _split3(s * posf)
        rows = [p.astype(f32) for p in s_parts] * 2 + [-p.astype(f32) for p in a_parts]
        blk0 = zeros16
        for i, rv in enumerate(rows):
            blk0 = jnp.where(r16 == i, rv, blk0)
        plain.append(jnp.concatenate([blk0, jnp.zeros((HEAD_DIM - 16, tq), f32)], axis=0).astype(bf16))
        full.append(jnp.concatenate([blk0, selbig, pad], axis=0).astype(bf16))
    return plain, full


def _aug_queries(qT, feats, kvh, tq):
    out = []
    for g in range(NSA_GROUP):
        h = kvh * NSA_GROUP + g
        qh = qT[h * HEAD_DIM:(h + 1) * HEAD_DIM, :]
        out.append(jnp.concatenate([qh, feats[g]] if kvh == 0 else [feats[g], qh], axis=0))
    return out


def _aug_keys(k_tile, feat_tile, kvh):
    lane = lax.broadcasted_iota(jnp.int32, k_tile.shape, 1)
    own = (lane < HEAD_DIM) if kvh == 0 else (lane >= HEAD_DIM)
    return jnp.where(own, k_tile, feat_tile)


def _with_ones(vT, kvh):
    half = vT.shape[0] // 2
    ones = jnp.ones((half, vT.shape[1]), vT.dtype)
    return jnp.concatenate([vT[:half], ones] if kvh == 0 else [ones, vT[half:]], axis=0)


def _cmp_branch_t(q_aug, kc_aug, vcT, pos_row, tq):
    ncp = vcT.shape[1]
    n_idx = lax.broadcasted_iota(jnp.int32, (ncp, tq), 0)
    cvis = pos_row >= (n_idx * CMP_STRIDE + (CMP_BLOCK - 1))
    any_vis = pos_row >= (CMP_BLOCK - 1)
    s = [jnp.where(cvis, jnp.dot(k, q, preferred_element_type=f32), NEG_INF) for k, q in zip(kc_aug, q_aug)]
    e = [jnp.exp2(x - jnp.max(x, axis=0, keepdims=True)) for x in s]
    p = [jnp.where(any_vis, x / jnp.sum(x, axis=0, keepdims=True), 0.0) for x in e]
    outs = [jnp.dot(vcT, x.astype(bf16), preferred_element_type=f32) for x in p]
    return outs, p


def _select_blocks_t(p_sumT, pos_row, ns, tq):
    ncp = p_sumT.shape[0]
    cj = lax.broadcasted_iota(jnp.int32, (ns, ncp), 0)
    ci = lax.broadcasted_iota(jnp.int32, (ns, ncp), 1) * CMP_STRIDE
    c2sT = ((ci < (cj + 1) * SLC_BLOCK) & (ci + CMP_BLOCK > cj * SLC_BLOCK)).astype(f32).astype(bf16)
    hi, mid, lo = _split3(p_sumT)
    d = lambda x: jnp.dot(c2sT, x, preferred_element_type=f32)
    imp = d(hi) + d(mid) + d(lo)
    blk = lax.broadcasted_iota(jnp.int32, (ns, tq), 0)
    cur = pos_row // SLC_BLOCK
    forced = (blk == 0) | (blk == cur) | (blk == cur - 1)
    imp = jnp.where(forced, FORCED, jnp.where(blk <= cur, imp, NEG_INF))
    rank = jnp.zeros((ns, tq), jnp.int32)
    for jp in range(ns):
        row = imp[jp:jp + 1, :]
        beats = (row > imp) | ((row == imp) & (blk > jp))
        rank = rank + beats.astype(jnp.int32)
    return ((rank < SLC_TOPN) & (blk <= cur)).astype(f32)


def _nsa_prompt_kernel(qT_ref, gT_ref, sk_ref, svT_ref, wk_ref, wvT_ref, kc_ref, vcT_ref, kf_ref, cf_ref, o_ref,
                       m_s, acc_s, *, tq, kt, ns):
    qi = pl.program_id(1)
    p0 = qi * tq
    pos_row = p0 + lax.broadcasted_iota(jnp.int32, (1, tq), 1)
    gates = _sigmoid(gT_ref[...])
    qT = qT_ref[...]
    kc = kc_ref[0].astype(bf16)
    vcT = vcT_ref[0].astype(bf16)
    half = LANES // 2
    KV = range(NSA_KV_HEADS)

    zsel = jnp.zeros((ns, tq), f32)
    q_pos = [_aug_queries(qT, _query_features(pos_row, zsel, kvh, tq)[0], kvh, tq) for kvh in KV]
    kc_aug = [_aug_keys(kc, cf_ref[kvh], kvh) for kvh in KV]
    oc, pc = _cmp_branch_t([q for kvh in KV for q in q_pos[kvh]], [kc_aug[kvh] for kvh in KV for _ in range(NSA_GROUP)],
                           vcT, pos_row, tq)
    o_cmp = [oc[kvh * NSA_GROUP:(kvh + 1) * NSA_GROUP] for kvh in KV]
    q_sel = []
    for kvh in KV:
        p_sum = pc[kvh * NSA_GROUP]
        for x in pc[kvh * NSA_GROUP + 1:(kvh + 1) * NSA_GROUP]:
            p_sum = p_sum + x
        selT = _select_blocks_t(p_sum, pos_row, ns, tq)
        q_sel.append(_aug_queries(qT, _query_features(pos_row, selT, kvh, tq)[1], kvh, tq))

    m_s[...] = jnp.full(m_s.shape, NEG_INF, f32)
    acc_s[...] = jnp.zeros(acc_s.shape, f32)
    rel = pos_row - lax.broadcasted_iota(jnp.int32, (kt, tq), 0)

    def slc_tile(c, causal):
        k0 = pl.multiple_of(c * kt, kt)
        k_tile = sk_ref[pl.ds(k0, kt), :]
        v_tile = svT_ref[:, pl.ds(k0, kt)]
        ok = (rel - k0) >= 0
        for kvh in KV:
            ka = _aug_keys(k_tile, kf_ref[kvh, pl.ds(k0, kt), :], kvh)
            va = _with_ones(v_tile, kvh)
            for g in range(NSA_GROUP):
                cs = slice(g * tq, (g + 1) * tq)
                sg = jnp.dot(ka, q_sel[kvh][g], preferred_element_type=f32)
                if causal:
                    sg = jnp.where(ok, sg, NEG_INF)
                m_old = m_s[kvh, :, cs]
                m_new = jnp.maximum(m_old, jnp.max(sg, axis=0, keepdims=True))
                alpha = jnp.exp2(m_old - m_new)
                p = jnp.exp2(sg - m_new).astype(bf16)
                acc_s[kvh, :, cs] = alpha * acc_s[kvh, :, cs] + jnp.dot(va, p, preferred_element_type=f32)
                m_s[kvh, :, cs] = m_new

    n_kt = (p0 + tq + kt - 1) // kt

    def slc_body(c, _):
        slc_tile(c, False)
        return 0

    lax.fori_loop(0, n_kt - 1, slc_body, 0)
    slc_tile(n_kt - 1, True)

    wlen = WINDOW + tq
    w0 = pl.multiple_of(jnp.maximum(p0 - WINDOW, 0), tq)
    k_win = wk_ref[pl.ds(w0, wlen), :]
    v_win = wvT_ref[:, pl.ds(w0, wlen)]
    dist = pos_row - (w0 + lax.broadcasted_iota(jnp.int32, (wlen, tq), 0))
    valid = (dist >= 0) & (dist < WINDOW)
    ch = [(kvh, g) for kvh in KV for g in range(NSA_GROUP)]
    ka = [_aug_keys(k_win, kf_ref[kvh, pl.ds(w0, wlen), :], kvh) for kvh in KV]
    va = [_with_ones(v_win, kvh) for kvh in KV]
    sg = [jnp.where(valid, jnp.dot(ka[kvh], q_pos[kvh][g], preferred_element_type=f32), NEG_INF) for kvh, g in ch]
    p = [jnp.exp2(x - jnp.max(x, axis=0, keepdims=True)).astype(bf16) for x in sg]
    o_win = [jnp.dot(va[kvh], p[i], preferred_element_type=f32) for i, (kvh, g) in enumerate(ch)]
    heads = []
    for i, (kvh, g) in enumerate(ch):
        h = kvh * NSA_GROUP + g
        d0, l0 = (0, half) if kvh == 0 else (half, 0)
        o_slc = acc_s[kvh, :, g * tq:(g + 1) * tq]
        heads.append(gates[3 * h:3 * h + 1] * o_cmp[kvh][g][d0:d0 + half]
                     + gates[3 * h + 1:3 * h + 2] * (o_slc[d0:d0 + half] / o_slc[l0:l0 + 1])
                     + gates[3 * h + 2:3 * h + 3] * (o_win[i][d0:d0 + half] / o_win[i][l0:l0 + 1]))
    o_ref[0] = jnp.concatenate(heads, axis=0).T.astype(o_ref.dtype)


def nsa_prompt(qT, gT, sk, svT, wk, wvT, kc, vcT, B, tq=128, kt=512):
    N = sk.shape[0]
    T = N // B
    ns = T // SLC_BLOCK
    nq = T // tq
    ncp = kc.shape[1]
    assert T % kt == 0 and T % tq == 0 and kt % tq == 0 and T >= WINDOW + tq and T <= POS_SPLIT * 256
    assert F_SEL0 + ns <= HEAD_DIM
    kf, cf = feature_tables(T, ncp)
    keys = pl.BlockSpec((T, LANES), lambda b, i: (b, 0))
    vals = pl.BlockSpec((LANES, T), lambda b, i: (0, b))
    return pl.pallas_call(
        functools.partial(_nsa_prompt_kernel, tq=tq, kt=kt, ns=ns),
        grid=(B, nq),
        in_specs=[pl.BlockSpec((NSA_WIDTH, tq), lambda b, i: (0, b * nq + i)),
                  pl.BlockSpec((3 * NSA_HEADS, tq), lambda b, i: (0, b * nq + i)),
                  keys, vals, keys, vals,
                  pl.BlockSpec((1, ncp, LANES), lambda b, i: (b, 0, 0)),
                  pl.BlockSpec((1, LANES, ncp), lambda b, i: (b, 0, 0)),
                  pl.BlockSpec((NSA_KV_HEADS, T, LANES), lambda b, i: (0, 0, 0)),
                  pl.BlockSpec((NSA_KV_HEADS, ncp, LANES), lambda b, i: (0, 0, 0))],
        out_specs=pl.BlockSpec((1, tq, NSA_WIDTH), lambda b, i: (b, i, 0)),
        out_shape=jax.ShapeDtypeStruct((B, T, NSA_WIDTH), bf16),
        scratch_shapes=[pltpu.VMEM((NSA_KV_HEADS, 1, NSA_GROUP * tq), f32),
                        pltpu.VMEM((NSA_KV_HEADS, LANES, NSA_GROUP * tq), f32)],
        compiler_params=_cparams("parallel", "arbitrary"),
        name="nsa_prompt",
    )(qT, gT, sk, svT, wk, wvT, kc, vcT, kf, cf)


SF_SPLIT = 64
SF_BLK0 = 16
SF_ROWS = 256


def sample_key_features(past_len, nk, tq, ns):
    sf_new0 = SF_BLK0 + -(-ns // 8) * 8
    assert sf_new0 + tq + 1 <= SF_ROWS and nk <= SF_SPLIT * 256
    pos = np.arange(nk)[None, :]
    row = np.arange(SF_ROWS)[:, None]
    hi = ((pos // SF_SPLIT) * SF_SPLIT).astype(np.float32)
    lo = (pos % SF_SPLIT).astype(np.float32)
    f = np.where(row < 3, hi, np.where(row < 6, lo, np.where(row < 9, 1.0, 0.0)))
    blk = (row >= SF_BLK0) & (row < SF_BLK0 + ns) & (pos // SLC_BLOCK == row - SF_BLK0)
    new = (row >= sf_new0) & (row < sf_new0 + tq) & (pos - past_len == row - sf_new0)
    pad = (row == sf_new0 + tq) & (pos >= past_len + tq)
    return jnp.asarray(np.where(blk | new | pad, 1.0, f).astype(bf16)), sf_new0


def _sample_query_features(sel, pos_col, kvh, tq, ns, sf_new0):
    lane = lax.broadcasted_iota(jnp.int32, (tq, SF_ROWS), 1)
    t_idx = lax.broadcasted_iota(jnp.int32, (tq, SF_ROWS), 0)
    selbig = (pltpu.roll(sel[:, 0:SF_ROWS], SF_BLK0, 1) - 1.0) * MASK_BIG
    base = jnp.where((lane >= SF_BLK0) & (lane < SF_BLK0 + ns), selbig, 0.0)
    base = jnp.where((lane >= sf_new0) & (lane < sf_new0 + tq) & (lane - sf_new0 > t_idx), -MASK_BIG, base)
    base = jnp.where(lane == sf_new0 + tq, -MASK_BIG, base)
    posf = pos_col.astype(f32)
    out = []
    for g in range(NSA_GROUP):
        s = jnp.full((tq, 1), SLOPES[kvh][g], f32)
        cols = [p.astype(f32) for p in _split3(s)] * 2 + [-p.astype(f32) for p in _split3(s * posf)]
        qf = base
        for i, cv in enumerate(cols):
            qf = jnp.where(lane == i, cv, qf)
        out.append(qf)
    return jnp.concatenate(out, axis=0).astype(bf16)


def _new_rows_t(rows, width):
    tq = rows.shape[0]
    sq = jnp.concatenate([rows, jnp.zeros((LANES - tq, LANES), f32)], axis=0).T
    if width == LANES:
        return sq
    return jnp.concatenate([sq, jnp.zeros((LANES, width - LANES), f32)], axis=1)


def _compress_staged(xs, t, nch, pe_ref, w_ref, w2_ref):
    lhs = jnp.concatenate([jnp.concatenate([xs[2 * t + h, j] for j in range(CMP_STRIDE // 2)], axis=1)
                           for h in range(NSA_KV_HEADS)], axis=0)
    acc = jnp.dot(lhs, w_ref[...], preferred_element_type=f32)
    bc = lambda v: jnp.broadcast_to(v, (8, v.shape[1])).astype(bf16)
    c = (jnp.dot(bc(pe_ref[0:1, :]), w_ref[:, 0:CMP_HIDDEN], preferred_element_type=f32)
         + jnp.dot(bc(pe_ref[1:2, :]), w_ref[:, CMP_HIDDEN:], preferred_element_type=f32))[0:1, :]
    outs = []
    for h in range(NSA_KV_HEADS):
        r = slice(h * nch, (h + 1) * nch)
        hid = acc[r, 0:CMP_HIDDEN] + pltpu.roll(acc[r, CMP_HIDDEN:], nch - 1, 0) + c
        outs.append(jnp.dot(_silu(hid).astype(bf16), w2_ref[...], preferred_element_type=f32))
    return jnp.concatenate(outs, axis=1)


def _nsa_sample_kernel(pt_ref, *refs, pps, n_steps, tq, past_len, ns, nsp, sf_new0):
    page_refs = refs[:pps]
    (q_ref, g_ref, rn_ref, wn_ref, cw_ref, ft_ref, perm_ref, pek_ref, pev_ref, wk_ref, wv_ref, w2k_ref, w2v_ref,
     o_ref, xs_s, kT_s, vT_s) = refs[pps:]
    del pt_ref
    step = pl.program_id(1)
    cpp = PAGE_SIZE // CMP_STRIDE
    staged = []
    for i in range(pps):
        r0 = pl.multiple_of((step * pps + i) * PAGE_SIZE, PAGE_SIZE)
        pg = page_refs[i]
        slabs = []
        for th in range(4):
            x = pg[0, th]
            slabs += [x, pltpu.roll(x, PAGE_SIZE - 1, 1)]
        a = jnp.concatenate(slabs, axis=0).astype(bf16)
        staged.append(lax.dot_general(perm_ref[...], a, NT, preferred_element_type=f32))
        kT_s[:, pl.ds(r0, PAGE_SIZE)] = pg[0, 4:6].reshape(LANES, PAGE_SIZE).astype(bf16)
        vT_s[:, pl.ds(r0, PAGE_SIZE)] = pg[0, 6:8].reshape(LANES, PAGE_SIZE).astype(bf16)
    c0 = pl.multiple_of(step * (pps * cpp), pps * cpp)
    for j in range(CMP_STRIDE // 2):
        rows = slice(j * cpp, (j + 1) * cpp)
        both = jnp.concatenate([t[rows] for t in staged], axis=0).astype(bf16)
        for th in range(4):
            xs_s[th, j, pl.ds(c0, pps * cpp), :] = both[:, th * LANES:(th + 1) * LANES]

    @pl.when(step == n_steps - 1)
    def _():
        nk = kT_s.shape[1]
        rn = rn_ref[0]
        kT_s[:, past_len:] = _new_rows_t(rn[:, 2 * LANES:3 * LANES], nk - past_len).astype(bf16)
        vT_s[:, past_len:] = _new_rows_t(rn[:, 3 * LANES:4 * LANES], nk - past_len).astype(bf16)
        nwin = cw_ref.shape[2]
        wn = wn_ref[0]
        wkT = jnp.concatenate([cw_ref[0, 0:LANES, :], _new_rows_t(wn[:, 0:LANES], LANES)], axis=1).astype(bf16)
        wvT = jnp.concatenate([cw_ref[0, LANES:, :], _new_rows_t(wn[:, LANES:], LANES)], axis=1).astype(bf16)
        wlen = nwin + LANES

        nch = past_len // CMP_STRIDE
        kc = _compress_staged(xs_s, 0, nch, pek_ref, wk_ref, w2k_ref)
        vc = _compress_staged(xs_s, 1, nch, pev_ref, wv_ref, w2v_ref)

        pos_col = past_len + lax.broadcasted_iota(jnp.int32, (tq, 1), 0)
        gates = _sigmoid(g_ref[0])
        q_all = q_ref[0]
        wpos = (past_len - nwin) + lax.broadcasted_iota(jnp.int32, (tq, wlen), 1)
        wdist = pos_col - wpos
        wvalid = (wdist >= 0) & (wdist < WINDOW)
        wdistf = wdist.astype(f32)
        qms, qfs, o_cmps = [], [], []
        for kvh in range(NSA_KV_HEADS):
            qm = _padded_queries(q_all, kvh, tq)
            o_cmp, p_sum = _cmp_branch(qm, kc, vc, pos_col, kvh, tq)
            sel = _select_blocks(p_sum, pos_col, ns, nsp, tq)
            qms.append(qm)
            qfs.append(_sample_query_features(sel, pos_col, kvh, tq, ns, sf_new0))
            o_cmps.append(o_cmp)
        qm2 = jnp.concatenate(qms, axis=0)
        s = (jnp.dot(qm2, kT_s[...], preferred_element_type=f32)
             + jnp.dot(jnp.concatenate(qfs, axis=0), ft_ref[...], preferred_element_type=f32))
        e = jnp.exp(s - jnp.max(s, axis=-1, keepdims=True))
        o_slc2 = (lax.dot_general(e.astype(bf16), vT_s[...], NT, preferred_element_type=f32)
                  / jnp.sum(e, axis=-1, keepdims=True))
        sw2 = jnp.dot(qm2, wkT, preferred_element_type=f32)
        pw = []
        for kvh in range(NSA_KV_HEADS):
            for g in range(NSA_GROUP):
                r = slice((kvh * NSA_GROUP + g) * tq, (kvh * NSA_GROUP + g + 1) * tq)
                sg = jnp.where(wvalid, sw2[r] - SLOPES[kvh][g] * wdistf, NEG_INF)
                ew = jnp.where(wvalid, jnp.exp(sg - jnp.max(sg, axis=-1, keepdims=True)), 0.0)
                pw.append(ew / jnp.sum(ew, axis=-1, keepdims=True))
        o_win2 = lax.dot_general(jnp.concatenate(pw, axis=0).astype(bf16), wvT, NT, preferred_element_type=f32)
        outs = []
        for kvh in range(NSA_KV_HEADS):
            r = slice(kvh * NSA_GROUP * tq, (kvh + 1) * NSA_GROUP * tq)
            outs.extend(_combine(gates, kvh, tq, o_cmps[kvh], o_slc2[r], o_win2[r]))
        o_ref[0] = jnp.concatenate(outs, axis=1).astype(o_ref.dtype)


def nsa_sample(page_table, cacheT, q, gates, rows_new, win_new, cache_winT, cw, pps=16):
    B, tq, _ = q.shape
    n_pages = page_table.shape[1]
    past_len = n_pages * PAGE_SIZE
    assert n_pages % pps == 0 and past_len % SLC_BLOCK == 0
    n_steps = n_pages // pps
    nk = past_len + LANES
    ns = past_len // SLC_BLOCK + 1
    nsp = -(-ns // LANES) * LANES
    nch = past_len // CMP_STRIDE
    nwin = cache_winT.shape[2]
    pek, wk, w2k, pev, wv, w2v = cw
    ft, sf_new0 = sample_key_features(past_len, nk, tq, ns)
    cpp = PAGE_SIZE // CMP_STRIDE
    ro = np.arange(PAGE_SIZE)
    ro2 = np.arange(PAGE_SIZE // 2)
    perm = jnp.asarray((ro[None, :] == (ro2[:, None] % cpp) * CMP_STRIDE + 2 * (ro2[:, None] // cpp))
                       .astype(bf16))
    per_b = lambda n: pl.BlockSpec((1, tq, n), lambda b, s, pt: (b, 0, 0))
    full = lambda a: pl.BlockSpec(a.shape, lambda b, s, pt: (0,) * a.ndim)
    page_specs = [pl.BlockSpec((1, 8, HEAD_DIM, PAGE_SIZE),
                               functools.partial(lambda b, s, pt, i: (pt[b, s * pps + i], 0, 0, 0), i=i))
                  for i in range(pps)]
    grid_spec = pltpu.PrefetchScalarGridSpec(
        num_scalar_prefetch=1,
        grid=(B, n_steps),
        in_specs=page_specs + [per_b(NSA_HEADS * LANES), per_b(3 * NSA_HEADS), per_b(4 * LANES), per_b(2 * LANES),
                               pl.BlockSpec((1, 2 * LANES, nwin), lambda b, s, pt: (b, 0, 0)),
                               full(ft), full(perm), full(pek), full(pev), full(wk), full(wv), full(w2k), full(w2v)],
        out_specs=per_b(NSA_WIDTH),
        scratch_shapes=[pltpu.VMEM((4, CMP_STRIDE // 2, nch, LANES), bf16),
                        pltpu.VMEM((LANES, nk), bf16), pltpu.VMEM((LANES, nk), bf16)])
    return pl.pallas_call(
        functools.partial(_nsa_sample_kernel, pps=pps, n_steps=n_steps, tq=tq, past_len=past_len, ns=ns, nsp=nsp,
                          sf_new0=sf_new0),
        grid_spec=grid_spec,
        out_shape=jax.ShapeDtypeStruct((B, tq, NSA_WIDTH), bf16),
        compiler_params=_cparams("parallel", "arbitrary"),
        name="nsa_sample",
    )(page_table, *([cacheT] * pps), q, gates, rows_new, win_new, cache_winT, ft, perm, pek, pev, wk, wv, w2k, w2v)


def _gdn_prep_kernel(x_ref, halo_ref, cb_ref, cw_ref, ba_ref, al_ref, dtb_ref,
                     u_ref, w_ref, qg_ref, kg_ref, qk_ref, eg_ref, xs_s, *, tb, C, t_valid, nb):
    i = pl.program_id(1)
    HK = GDN_HEADS * GDN_DK
    R = min(tb, LANES)
    cpb = R // C

    @pl.when(i == 0)
    def _():
        xs_s[:, 0:8, :] = cb_ref[...]

    @pl.when(i > 0)
    def _():
        xs_s[:, 0:8, :] = halo_ref[...]

    xs_s[:, 8:, :] = x_ref[...]
    ri = lax.broadcasted_iota(jnp.int32, (R, R), 0)
    ci = lax.broadcasted_iota(jnp.int32, (R, R), 1)
    same = (ri // C) == (ci // C)
    lower = same & (ri >= ci)
    strict = same & (ri > ci)
    eye = (ri == ci).astype(f32)
    rr = lax.broadcasted_iota(jnp.int32, (R, LANES), 0)
    row_t = i * tb + lax.broadcasted_iota(jnp.int32, (tb, 1), 0)
    live = row_t < t_valid

    def conv_act(bb, c0):
        acc = cw_ref[GDN_CONV - 1:GDN_CONV, c0:c0 + LANES] * xs_s[bb, pl.ds(8, tb), c0:c0 + LANES]
        for k in range(1, GDN_CONV):
            acc = acc + cw_ref[GDN_CONV - 1 - k:GDN_CONV - k, c0:c0 + LANES] * xs_s[bb, pl.ds(8 - k, tb), c0:c0 + LANES]
        return _silu(acc)

    chains = []
    for bb, h in [(bb, h) for bb in range(nb) for h in range(GDN_HEADS)]:
        ba = ba_ref[bb]
        q = conv_act(bb, h * GDN_DK)
        k = conv_act(bb, HK + h * GDN_DK)
        v = conv_act(bb, 2 * HK + h * GDN_DV)
        q = q * lax.rsqrt(jnp.sum(q * q, axis=-1, keepdims=True) + L2_EPS) * (GDN_DK ** -0.5)
        k = k * lax.rsqrt(jnp.sum(k * k, axis=-1, keepdims=True) + L2_EPS)
        beta = _sigmoid(ba[:, h:h + 1])
        ar = ba[:, GDN_HEADS + h:GDN_HEADS + h + 1] + dtb_ref[:, h:h + 1]
        softplus = jnp.maximum(ar, 0.0) + jnp.log(1.0 + jnp.exp(-jnp.abs(ar)))
        gcol = -jnp.exp(al_ref[:, h:h + 1]) * softplus
        if t_valid < 10 ** 9:
            beta = jnp.where(live, beta, 0.0)
            gcol = jnp.where(live, gcol, 0.0)
            k = jnp.where(live, k, 0.0)
            v = jnp.where(live, v, 0.0)
        for blk in range(tb // R):
            r = slice(blk * R, (blk + 1) * R)
            qc, kc, vc, bc = q[r], k[r], v[r], beta[r]
            gc = jnp.broadcast_to(gcol[r], (R, LANES))
            sh = 1
            while sh < C:
                gc = gc + jnp.where((rr % C) >= sh, pltpu.roll(gc, sh, 0), 0.0)
                sh *= 2
            gct = gc.T
            decay = jnp.exp(jnp.where(lower, gc[:, 0:R] - gct[0:R, :], NEG_INF))
            kb = kc * bc
            a = jnp.where(strict, _dot_hp(kb, kc, NT) * decay, 0.0)
            qk = jnp.where(lower, lax.dot_general(qc.astype(bf16), kc.astype(bf16), NT,
                                                  preferred_element_type=f32) * decay, 0.0)
            gl = gc[R - 1:R, :]
            for j in range(cpb - 2, -1, -1):
                gl = jnp.where(rr < (j + 1) * C, gc[(j + 1) * C - 1:(j + 1) * C, :], gl)
            chains.append(dict(bb=bb, h=h, r=r, blk=blk, qc=qc, kc=kc, gc=gc, gl=gl, qk=qk, n=-a, t=eye - a,
                               rhs=jnp.concatenate([vc * bc, kb * jnp.exp(gc)], axis=1)))

    for ch in chains:
        ch['n'] = _split2(ch['n'])
    sh = 2
    while sh < C:
        for ch in chains:
            ch['n'] = _split2(_dot_parts(ch['n'], ch['n']))
        for ch in chains:
            ch['t'] = ch['t'] + _dot_parts(_split2(ch['t']), ch['n'])
        sh *= 2

    for ch in chains:
        bb, h, r, gc, gl = ch['bb'], ch['h'], ch['r'], ch['gc'], ch['gl']
        sol = ch['rhs'] + _dot_hp(ch['t'] - eye, ch['rhs'])
        cs = slice(h * GDN_DV, (h + 1) * GDN_DV)
        u_ref[bb, r, cs] = sol[:, 0:GDN_DV]
        w_ref[bb, r, cs] = sol[:, GDN_DV:].astype(w_ref.dtype)
        qg_ref[bb, r, cs] = (ch['qc'] * jnp.exp(gc)).astype(qg_ref.dtype)
        kg_ref[bb, r, cs] = (ch['kc'] * jnp.exp(gl - gc)).astype(kg_ref.dtype)
        qk = ch['qk'][:, 0:C]
        for j in range(1, cpb):
            qk = jnp.where(ri[:, 0:C] >= j * C, ch['qk'][:, j * C:(j + 1) * C], qk)
        qk_ref[bb, r, h * C:(h + 1) * C] = qk.astype(qk_ref.dtype)
        for j in range(cpb):
            eg_ref[bb, ch['blk'] * cpb + j, h:h + 1, :] = jnp.exp(gc[(j + 1) * C - 1:(j + 1) * C, :])
    eg_ref[:, :, GDN_HEADS:, :] = jnp.zeros((nb, tb // C, 8 - GDN_HEADS, LANES), f32)


def gdn_prep(qkv, conv_buf8, conv_w, ba, a_log, dt_bias, C, tb, t_valid, nb=1):
    B, T, CH = qkv.shape
    nb = math.gcd(nb, B)
    hb = tb // 8
    per = lambda n, dt: (pl.BlockSpec((nb, tb, n), lambda b, i: (b, i, 0)), jax.ShapeDtypeStruct((B, T, n), dt))
    outs = [per(GDN_WIDTH, f32), per(GDN_WIDTH, bf16), per(GDN_WIDTH, bf16), per(GDN_WIDTH, bf16),
            per(GDN_HEADS * C, bf16),
            (pl.BlockSpec((nb, tb // C, 8, LANES), lambda b, i: (b, i, 0, 0)),
             jax.ShapeDtypeStruct((B, T // C, 8, LANES), f32))]
    return pl.pallas_call(
        functools.partial(_gdn_prep_kernel, tb=tb, C=C, t_valid=t_valid, nb=nb),
        grid=(B // nb, T // tb),
        in_specs=[pl.BlockSpec((nb, tb, CH), lambda b, i: (b, i, 0)),
                  pl.BlockSpec((nb, 8, CH), lambda b, i: (b, jnp.maximum(i * hb - 1, 0), 0)),
                  pl.BlockSpec((nb, 8, CH), lambda b, i: (b, 0, 0)),
                  pl.BlockSpec((GDN_CONV, CH), lambda b, i: (0, 0)),
                  pl.BlockSpec((nb, tb, 2 * GDN_HEADS), lambda b, i: (b, i, 0)),
                  pl.BlockSpec((1, GDN_HEADS), lambda b, i: (0, 0)),
                  pl.BlockSpec((1, GDN_HEADS), lambda b, i: (0, 0))],
        out_specs=[o[0] for o in outs],
        out_shape=[o[1] for o in outs],
        scratch_shapes=[pltpu.VMEM((nb, tb + 8, CH), f32)],
        compiler_params=_cparams("parallel", "arbitrary"),
        name="gdn_prep",
    )(qkv, qkv, conv_buf8, conv_w, ba, a_log.reshape(1, GDN_HEADS), dt_bias.reshape(1, GDN_HEADS))


def _gdn_scan_kernel(u_ref, w_ref, qg_ref, kg_ref, qk_ref, eg_ref, z_ref, ng_ref, s0_ref,
                     o_ref, sT_ref, s_s, *, C, nb):
    c = pl.program_id(1)

    @pl.when(c == 0)
    def _():
        s_s[...] = s0_ref[...]

    ch = [(b, h) for b in range(nb) for h in range(GDN_HEADS)]
    cs = [slice(h * GDN_DV, (h + 1) * GDN_DV) for h in range(GDN_HEADS)]
    S = [s_s[b, h] for b, h in ch]
    Sb = [x.astype(bf16) for x in S]
    v_new = [u_ref[b, :, cs[h]] - jnp.dot(w_ref[b, :, cs[h]], Sb[i], preferred_element_type=f32)
             for i, (b, h) in enumerate(ch)]
    vb = [x.astype(bf16) for x in v_new]
    for i, (b, h) in enumerate(ch):
        s_s[b, h] = S[i] * eg_ref[b, 0, h:h + 1, :] + lax.dot_general(kg_ref[b, :, cs[h]], vb[i], TN,
                                                                       preferred_element_type=f32)
    o = [jnp.dot(qg_ref[b, :, cs[h]], Sb[i], preferred_element_type=f32)
         + jnp.dot(qk_ref[b, :, h * C:(h + 1) * C], vb[i], preferred_element_type=f32) for i, (b, h) in enumerate(ch)]
    outs = [_rms(o[i], ng_ref[...]) * _silu(z_ref[b, :, cs[h]]) for i, (b, h) in enumerate(ch)]
    for b in range(nb):
        o_ref[b] = jnp.concatenate(outs[b * GDN_HEADS:(b + 1) * GDN_HEADS], axis=1).astype(o_ref.dtype)

    @pl.when(c == pl.num_programs(1) - 1)
    def _():
        sT_ref[...] = s_s[...]


def gdn_scan(u, w, qg, kg, qk, eg, z, norm_g, s0, C, nb=8):
    B, T, _ = u.shape
    nb = math.gcd(nb, B)
    per = lambda n: pl.BlockSpec((nb, C, n), lambda b, c: (b, c, 0))
    st = pl.BlockSpec((nb, GDN_HEADS, GDN_DK, GDN_DV), lambda b, c: (b, 0, 0, 0))
    return pl.pallas_call(
        functools.partial(_gdn_scan_kernel, C=C, nb=nb),
        grid=(B // nb, T // C),
        in_specs=[per(GDN_WIDTH), per(GDN_WIDTH), per(GDN_WIDTH), per(GDN_WIDTH), per(GDN_HEADS * C),
                  pl.BlockSpec((nb, 1, 8, LANES), lambda b, c: (b, c, 0, 0)),
                  per(GDN_WIDTH), pl.BlockSpec((1, GDN_DV), lambda b, c: (0, 0)), st],
        out_specs=[per(GDN_WIDTH), st],
        out_shape=[jax.ShapeDtypeStruct((B, T, GDN_WIDTH), bf16),
                   jax.ShapeDtypeStruct((B, GDN_HEADS, GDN_DK, GDN_DV), f32)],
        scratch_shapes=[pltpu.VMEM((nb, GDN_HEADS, GDN_DK, GDN_DV), f32)],
        compiler_params=_cparams("parallel", "arbitrary"),
        name="gdn_scan",
    )(u, w, qg, kg, qk, eg, z, norm_g.reshape(1, GDN_DV), s0)


def _nsa_in_groups(w, feature_major):
    c1 = NSA_WIDTH
    c2 = c1 + 4 * NSA_KV_COLS
    c3 = c2 + 2 * NSA_KV_COLS
    c4 = c3 + 3 * NSA_HEADS
    wT = w.T.astype(bf16)
    xa_scale = XA_DIM ** -0.5 * LOG2E
    if feature_major:
        k0 = c1 + 2 * NSA_KV_COLS
        return [(wT[c1:c2], f32, 1.0, FEATURE_MAJOR_PER_BATCH),
                (wT[c1:k0], f32, 1.0, TOKEN_MAJOR),
                (wT[c2:c3], f32, 1.0, FEATURE_MAJOR_PER_BATCH),
                (wT[c4:], bf16, xa_scale, FEATURE_MAJOR),
                (wT[:c1], bf16, HEAD_DIM ** -0.5 * LOG2E, FEATURE_MAJOR),
                (wT[c3:c4], f32, 1.0, FEATURE_MAJOR),
                (wT[k0:k0 + NSA_KV_COLS], bf16, 1.0, TOKEN_MAJOR),
                (wT[k0 + NSA_KV_COLS:c2], bf16, 1.0, FEATURE_MAJOR),
                (wT[c2:c2 + NSA_KV_COLS], bf16, 1.0, TOKEN_MAJOR),
                (wT[c2 + NSA_KV_COLS:c3], bf16, 1.0, FEATURE_MAJOR)]
    wq = wT[:c1].reshape(NSA_KV_HEADS, NSA_GROUP, HEAD_DIM, -1)
    zq = jnp.zeros_like(wq[0])
    wq = jnp.concatenate([jnp.concatenate([wq[0], zq], axis=1), jnp.concatenate([zq, wq[1]], axis=1)], axis=0)
    wq = wq.reshape(NSA_HEADS * LANES, -1)
    return [(wT[c1:c2], f32, 1.0, TOKEN_MAJOR), (wT[c2:c3], f32, 1.0, TOKEN_MAJOR),
            (wT[c4:], bf16, xa_scale, TOKEN_MAJOR), (wq, bf16, HEAD_DIM ** -0.5, TOKEN_MAJOR),
            (wT[c3:c4], f32, 1.0, TOKEN_MAJOR)]


def _gdn_in_groups(w, feature_major):
    c1 = GDN_CONV_CH
    c2 = c1 + GDN_WIDTH
    c3 = c2 + 2 * GDN_HEADS
    wT = w.T.astype(bf16)
    xq = (wT[c3:], bf16, XA_DIM ** -0.5 * LOG2E, FEATURE_MAJOR if feature_major else TOKEN_MAJOR)
    return [(wT[:c1], f32, 1.0, TOKEN_MAJOR), (wT[c1:c2], f32, 1.0, TOKEN_MAJOR),
            (wT[c2:c3], f32, 1.0, TOKEN_MAJOR), xq]


def _trunk(x, mem_kv, nsa_past, gdn_state, p, t_valid):
    B, T, D = x.shape
    N = B * T
    xf = x.reshape(N, D)

    prompt = nsa_past is None
    if prompt:
        kv_rows, rows_cmp, win_rows, xq, qT, gT, sk, svT, wk, wvT = norm_proj(
            xf, p['norm_mix_g'][0], p['nsa_in_prompt'], tm=512, rows_per_batch=T)
        kc, vcT = cmp_prompt(rows_cmp.reshape(B, T, -1), p['cmp'])
        o_mix = nsa_prompt(qT, gT, sk, svT, wk, wvT, kc, vcT, B)
    else:
        rows, win, xq, q, gates = norm_proj(xf, p['norm_mix_g'][0], p['nsa_in_sample'])
        kv_rows = rows.reshape(B, T, -1)
        win_rows = win.reshape(B, T, -1)
        xq = xq.reshape(B, T, -1)
        page_table, cache, cache_win = nsa_past
        o_mix = nsa_sample(page_table, cache, q.reshape(B, T, -1), gates.reshape(B, T, -1), kv_rows, win_rows,
                           cache_win, p['cmp_dense'])
    o_mem = cross_attend(xq, mem_kv, 0, B)
    xf = post_block(xf, o_mix.reshape(N, -1), o_mem.reshape(N, -1), p['w_out'][0], p['norm_ffn_g'][0],
                    p['w_up'][0], p['w_down'][0], p['final_norm_g'], final=False)

    qkv, z, ba, xq = norm_proj(xf, p['norm_mix_g'][1], p['gdn_in_prompt' if prompt else 'gdn_in_sample'], tm=512)
    if not prompt:
        xq = xq.reshape(B, T, -1)
    qkv3 = qkv.reshape(B, T, -1)
    S0, conv_buf8 = gdn_state
    C = min(64, T)
    tb = min(256, T)
    u, w, qg, kg, qk, eg = gdn_prep(qkv3, conv_buf8, p['gdn_conv_w'], ba.reshape(B, T, -1), p['gdn_a_log'],
                                    p['gdn_dt_bias'], C, tb, t_valid, nb=1 if prompt else 4)
    o_mix, S = gdn_scan(u, w, qg, kg, qk, eg, z.reshape(B, T, -1), p['gdn_norm_g'], S0, C)
    o_mem = cross_attend(xq, mem_kv, 1, B)
    y = post_block(xf, o_mix.reshape(N, -1), o_mem.reshape(N, -1), p['w_out'][1], p['norm_ffn_g'][1],
                   p['w_up'][1], p['w_down'][1], p['final_norm_g'], final=True)
    return y.reshape(B, T, D), kv_rows, win_rows, S, qkv3


def kernel(x_prompt, x_sample, mem_prompt, cache_nsa_kv, cache_nsa_win, state_gdn_s, state_gdn_conv,
           cache_mem_kv, page_table, norm_mix_g, norm_mem_g, w_mem_kv, nsa_w_in, cmp_pe_k, cmp_w1_k,
           cmp_w2_k, cmp_pe_v, cmp_w1_v, cmp_w2_v, gdn_w_in, gdn_conv_w, gdn_a_log, gdn_dt_bias,
           gdn_norm_g, w_out, norm_ffn_g, w_up, w_down, final_norm_g):
    B, T, D = x_prompt.shape
    DB, TS, _ = x_sample.shape
    M = mem_prompt.shape[1]
    p = dict(norm_mix_g=norm_mix_g, nsa_in_prompt=_nsa_in_groups(nsa_w_in[0], True),
             nsa_in_sample=_nsa_in_groups(nsa_w_in[0], False), gdn_in_prompt=_gdn_in_groups(gdn_w_in[0], True),
             gdn_in_sample=_gdn_in_groups(gdn_w_in[0], False),
             cmp=_cmp_weights(cmp_pe_k[0], cmp_w1_k[0], cmp_w2_k[0]) + _cmp_weights(cmp_pe_v[0], cmp_w1_v[0], cmp_w2_v[0]),
             cmp_dense=(_cmp_weights_dense(cmp_pe_k[0], cmp_w1_k[0], cmp_w2_k[0])
                        + _cmp_weights_dense(cmp_pe_v[0], cmp_w1_v[0], cmp_w2_v[0])),
             gdn_conv_w=gdn_conv_w[0], gdn_a_log=gdn_a_log[0], gdn_dt_bias=gdn_dt_bias[0], gdn_norm_g=gdn_norm_g[0],
             w_out=w_out.astype(bf16), norm_ffn_g=norm_ffn_g, w_up=w_up.astype(bf16), w_down=w_down.astype(bf16),
             final_norm_g=final_norm_g)

    mem_flat = mem_prompt.reshape(B * M, D)
    mem_kv = [norm_proj(mem_flat, norm_mem_g[i], [(w_mem_kv[i].astype(bf16).T, f32, 1.0, FEATURE_MAJOR_PER_BATCH)],
                        tm=M, rows_per_batch=M)[0] for i in range(DEPTH)]
    mem_kv_p = jnp.stack(mem_kv, axis=1)
    zero_state = (jnp.zeros((B, GDN_HEADS, GDN_DK, GDN_DV), f32), jnp.zeros((B, 8, GDN_CONV_CH), f32))
    y_p, rows_p, win_p, S_p, qkv_p = _trunk(x_prompt, mem_kv_p, None, zero_state, p, 10 ** 9)

    TP = 8
    x_s = jnp.pad(x_sample, ((0, 0), (0, TP - TS), (0, 0)))
    cache = cache_nsa_kv.transpose(0, 2, 3, 4, 5, 1).reshape(cache_nsa_kv.shape[0], -1, HEAD_DIM, PAGE_SIZE)
    nwin = cache_nsa_win.shape[2]
    cache_win = cache_nsa_win.transpose(0, 1, 3, 4, 5, 2).reshape(DB, -1, nwin)
    conv8 = jnp.pad(state_gdn_conv[:, 0], ((0, 0), (8 - (GDN_CONV - 1), 0), (0, 0)))
    mem_kv_s = cache_mem_kv.transpose(0, 1, 3, 4, 5, 2).reshape(DB, DEPTH, 2 * XA_WIDTH, M)
    y_s, rows_s, win_s, S_s, qkv_s = _trunk(x_s, mem_kv_s, (page_table, cache, cache_win),
                                            (state_gdn_s[:, 0], conv8), p, TS)

    kvshape = (4, NSA_KV_HEADS, HEAD_DIM)
    nsa_kv_prompt = rows_p.reshape(B, 1, *kvshape, T).transpose(0, 5, 1, 2, 3, 4)
    nsa_kv_sample = rows_s[:, :TS].reshape(DB, TS, 1, *kvshape)
    wshape = (2, NSA_KV_HEADS, HEAD_DIM)
    wkeep = min(WINDOW, T)
    nsa_win_prompt = win_p[:, :, T - wkeep:].reshape(B, 1, *wshape, wkeep).transpose(0, 1, 5, 2, 3, 4)
    win_cat = jnp.concatenate([cache_win, win_s[:, :TS].transpose(0, 2, 1)], axis=2)[:, :, -WINDOW:]
    nsa_win_sample = win_cat.reshape(DB, 1, *wshape, -1).transpose(0, 1, 5, 2, 3, 4)
    gdn_conv_prompt = qkv_p[:, None, T - (GDN_CONV - 1):]
    conv_cat = jnp.concatenate([state_gdn_conv[:, 0], qkv_s[:, :TS]], axis=1)
    gdn_conv_sample = conv_cat[:, None, -(GDN_CONV - 1):]
    mem_kv_prompt = mem_kv_p.reshape(B, DEPTH, 2, XA_HEADS, XA_DIM, M).transpose(0, 1, 5, 2, 3, 4)
    return (y_p, y_s[:, :TS], nsa_kv_prompt, nsa_kv_sample, nsa_win_prompt, nsa_win_sample,
            S_p[:, None], S_s[:, None], gdn_conv_prompt, gdn_conv_sample, mem_kv_prompt)
```

```python
import functools
import math

import jax
import jax.numpy as jnp
import numpy as np
from jax import lax
from jax.experimental import pallas as pl
from jax.experimental.pallas import tpu as pltpu

f32 = jnp.float32
bf16 = jnp.bfloat16

DEPTH = 2
PAGE_SIZE = 128
HEAD_DIM = 64
NSA_HEADS = 12
NSA_KV_HEADS = 2
NSA_GROUP = NSA_HEADS // NSA_KV_HEADS
NSA_WIDTH = NSA_HEADS * HEAD_DIM
NSA_KV_COLS = NSA_KV_HEADS * HEAD_DIM
CMP_STRIDE = 16
CMP_BLOCK = 2 * CMP_STRIDE
CMP_HIDDEN = 128
SLC_BLOCK = 64
SLC_TOPN = 16
WINDOW = 512
GDN_HEADS = 6
GDN_DK = 128
GDN_DV = 128
GDN_WIDTH = GDN_HEADS * GDN_DV
GDN_CONV = 4
GDN_CONV_CH = 2 * GDN_HEADS * GDN_DK + GDN_HEADS * GDN_DV
XA_HEADS = 4
XA_DIM = 64
XA_WIDTH = XA_HEADS * XA_DIM
NORM_EPS = 1e-6
L2_EPS = 1e-6
NEG_INF = -1e30
FORCED = 1e9

VMEM_LIMIT = 52 * 1024 * 1024
LANES = 128

SLOPES = [[2.0 ** (-8.0 * (k * NSA_GROUP + g + 1) / NSA_HEADS) for g in range(NSA_GROUP)]
          for k in range(NSA_KV_HEADS)]

NT = (((1,), (1,)), ((), ()))
TN = (((0,), (0,)), ((), ()))


def _cparams(*sem):
    return pltpu.CompilerParams(dimension_semantics=sem, vmem_limit_bytes=VMEM_LIMIT)


def _rms(x, g):
    r = lax.rsqrt(jnp.mean(x * x, axis=-1, keepdims=True) + NORM_EPS)
    return (x * r) * g


def _sigmoid(x):
    return 1.0 / (1.0 + jnp.exp(-x))


def _silu(x):
    return x * _sigmoid(x)


def _split3(a):
    hi = a.astype(bf16)
    r1 = a - hi.astype(f32)
    mid = r1.astype(bf16)
    lo = (r1 - mid.astype(f32)).astype(bf16)
    return hi, mid, lo


def _dot_exact_rhs(a, b_bf16):
    hi, mid, lo = _split3(a)
    d = lambda x: jnp.dot(x, b_bf16, preferred_element_type=f32)
    return d(hi) + d(mid) + d(lo)


def _split2(a):
    hi = a.astype(bf16)
    return hi, (a - hi.astype(f32)).astype(bf16)


def _dot_parts(a, b, dims=None):
    if dims is None:
        d = lambda x, y: jnp.dot(x, y, preferred_element_type=f32)
    else:
        d = lambda x, y: lax.dot_general(x, y, dims, preferred_element_type=f32)
    return d(a[0], b[0]) + (d(a[0], b[1]) + d(a[1], b[0]))


def _dot_hp(a, b, dims=None):
    return _dot_parts(_split2(a), _split2(b), dims)


TOKEN_MAJOR, FEATURE_MAJOR, FEATURE_MAJOR_PER_BATCH = 'nt', 'fm', 'fmb'


def _norm_proj_kernel(x_ref, g_ref, *refs, specs):
    n = len(specs)
    w_refs, o_refs = refs[:n], refs[n:]
    h = _rms(x_ref[...], g_ref[...]).astype(bf16)
    for (scale, kind), w_ref, o_ref in zip(specs, w_refs, o_refs):
        if kind == TOKEN_MAJOR:
            y = lax.dot_general(h, w_ref[...], NT, preferred_element_type=f32)
        else:
            y = lax.dot_general(w_ref[...], h, NT, preferred_element_type=f32)
        if scale != 1.0:
            y = y * scale
        o_ref[...] = y.astype(o_ref.dtype).reshape(o_ref.shape)


def norm_proj(x, g, groups, tm=256, rows_per_batch=None):
    N, D = x.shape
    tm = min(tm, N)
    assert N % tm == 0
    in_specs = [pl.BlockSpec((tm, D), lambda i: (i, 0)), pl.BlockSpec((1, D), lambda i: (0, 0))]
    out_specs, out_shape = [], []
    for w, dt, _, kind in groups:
        in_specs.append(pl.BlockSpec(w.shape, lambda i: (0, 0)))
        n = w.shape[0]
        if kind == TOKEN_MAJOR:
            out_specs.append(pl.BlockSpec((tm, n), lambda i: (i, 0)))
            out_shape.append(jax.ShapeDtypeStruct((N, n), dt))
        elif kind == FEATURE_MAJOR:
            out_specs.append(pl.BlockSpec((n, tm), lambda i: (0, i)))
            out_shape.append(jax.ShapeDtypeStruct((n, N), dt))
        else:
            per = rows_per_batch // tm
            assert rows_per_batch % tm == 0
            out_specs.append(pl.BlockSpec((1, n, tm), lambda i: (i // per, 0, i % per)))
            out_shape.append(jax.ShapeDtypeStruct((N // rows_per_batch, n, rows_per_batch), dt))
    return pl.pallas_call(
        functools.partial(_norm_proj_kernel, specs=tuple((s, t) for _, _, s, t in groups)),
        grid=(N // tm,),
        in_specs=in_specs,
        out_specs=out_specs,
        out_shape=out_shape,
        compiler_params=_cparams("parallel"),
        name="norm_proj",
    )(x, g.reshape(1, D), *[w for w, _, _, _ in groups])


def _xattn_kernel(q_ref, kv_ref, o_ref, *, q_feature_major, nb):
    ch = [(b, h) for b in range(nb) for h in range(XA_HEADS)]
    hs = lambda h, off=0: slice(off + h * XA_DIM, off + (h + 1) * XA_DIM)
    qT = [q_ref[...] if q_feature_major else q_ref[b].astype(f32).T.astype(bf16) for b in range(nb)]
    s = [lax.dot_general(kv_ref[b, 0, hs(h), :].astype(bf16), qT[b][hs(h)], TN, preferred_element_type=f32)
         for b, h in ch]
    e = [jnp.exp2(x - jnp.max(x, axis=0, keepdims=True)) for x in s]
    o = [jnp.dot(kv_ref[b, 0, hs(h, XA_WIDTH), :].astype(bf16), e[i].astype(bf16), preferred_element_type=f32)
         / jnp.sum(e[i], axis=0, keepdims=True) for i, (b, h) in enumerate(ch)]
    for b in range(nb):
        o_ref[b] = jnp.concatenate(o[b * XA_HEADS:(b + 1) * XA_HEADS], axis=0).T.astype(o_ref.dtype)


def cross_attend(xq, kvT, layer, B, tq=2048, nb=4):
    q_feature_major = xq.ndim == 2
    T = xq.shape[1] // B if q_feature_major else xq.shape[1]
    M = kvT.shape[3]
    tq = min(tq, T)
    nq = T // tq
    if q_feature_major:
        nb = 1
        q_spec = pl.BlockSpec((XA_WIDTH, tq), lambda b, i: (0, b * nq + i))
    else:
        nb = math.gcd(nb, B)
        q_spec = pl.BlockSpec((nb, tq, XA_WIDTH), lambda b, i: (b, i, 0))
    return pl.pallas_call(
        functools.partial(_xattn_kernel, q_feature_major=q_feature_major, nb=nb),
        grid=(B // nb, nq),
        in_specs=[q_spec, pl.BlockSpec((nb, 1, 2 * XA_WIDTH, M), lambda b, i: (b, layer, 0, 0))],
        out_specs=pl.BlockSpec((nb, tq, XA_WIDTH), lambda b, i: (b, i, 0)),
        out_shape=jax.ShapeDtypeStruct((B, T, XA_WIDTH), bf16),
        compiler_params=_cparams("parallel", "parallel"),
        name="cross_attend",
    )(xq, kvT)


def _post_kernel(x_ref, om_ref, oc_ref, wo_ref, gf_ref, wu_ref, wd_ref, gl_ref, o_ref,
                 x1_s, h_s, acc_s, *, mix_w, final):
    j = pl.program_id(1)

    @pl.when(j == 0)
    def _():
        x1 = x_ref[...] + (jnp.dot(om_ref[...], wo_ref[0:mix_w, :], preferred_element_type=f32)
                           + jnp.dot(oc_ref[...], wo_ref[mix_w:, :], preferred_element_type=f32))
        x1_s[...] = x1
        h_s[...] = _rms(x1, gf_ref[...]).astype(bf16)
        acc_s[...] = jnp.zeros_like(acc_s)

    u = jnp.dot(h_s[...], wu_ref[...], preferred_element_type=f32)
    u = jnp.square(jnp.maximum(u, 0.0)).astype(bf16)
    acc_s[...] += jnp.dot(u, wd_ref[...], preferred_element_type=f32)

    @pl.when(j == pl.num_programs(1) - 1)
    def _():
        x2 = x1_s[...] + acc_s[...]
        if final:
            x2 = _rms(x2, gl_ref[...])
        o_ref[...] = x2


def post_block(x, o_mix, o_mem, w_out, g_ffn, w_up, w_down, g_final, final, tm=1024, tf=512):
    N, D = x.shape
    mix_w = o_mix.shape[1]
    F = w_up.shape[1]
    tm = min(tm, N)
    return pl.pallas_call(
        functools.partial(_post_kernel, mix_w=mix_w, final=final),
        grid=(N // tm, F // tf),
        in_specs=[pl.BlockSpec((tm, D), lambda i, j: (i, 0)),
                  pl.BlockSpec((tm, mix_w), lambda i, j: (i, 0)),
                  pl.BlockSpec((tm, XA_WIDTH), lambda i, j: (i, 0)),
                  pl.BlockSpec((mix_w + XA_WIDTH, D), lambda i, j: (0, 0)),
                  pl.BlockSpec((1, D), lambda i, j: (0, 0)),
                  pl.BlockSpec((D, tf), lambda i, j: (0, j)),
                  pl.BlockSpec((tf, D), lambda i, j: (j, 0)),
                  pl.BlockSpec((1, D), lambda i, j: (0, 0))],
        out_specs=pl.BlockSpec((tm, D), lambda i, j: (i, 0)),
        out_shape=jax.ShapeDtypeStruct((N, D), f32),
        scratch_shapes=[pltpu.VMEM((tm, D), f32), pltpu.VMEM((tm, D), bf16), pltpu.VMEM((tm, D), f32)],
        compiler_params=_cparams("parallel", "arbitrary"),
        name="post_block",
    )(x, o_mix, o_mem, w_out, g_ffn.reshape(1, D), w_up, w_down, g_final.reshape(1, D))


def _compress(load_chunk_rows, nch, pe_ref, w_ref, w2_ref):
    acc_p = jnp.zeros((nch, 2 * CMP_HIDDEN), f32)
    acc_q = jnp.zeros((nch, 2 * CMP_HIDDEN), f32)
    for l in range(CMP_STRIDE):
        x = load_chunk_rows(l)
        xp = (x + pe_ref[l:l + 1, :]).astype(bf16)
        xq = (x + pe_ref[CMP_STRIDE + l:CMP_STRIDE + l + 1, :]).astype(bf16)
        acc_p = acc_p + jnp.dot(xp, w_ref[l, :, 0:2 * CMP_HIDDEN], preferred_element_type=f32)
        acc_q = acc_q + jnp.dot(xq, w_ref[l, :, 2 * CMP_HIDDEN:], preferred_element_type=f32)
    hid = _silu(acc_p + pltpu.roll(acc_q, nch - 1, 0)).astype(bf16)
    outs = [jnp.dot(hid[:, h * CMP_HIDDEN:(h + 1) * CMP_HIDDEN], w2_ref[...], preferred_element_type=f32)
            for h in range(NSA_KV_HEADS)]
    return jnp.concatenate(outs, axis=1)


def _cmp_prompt_kernel(xk_ref, xv_ref, pek_ref, pev_ref, wk_ref, wv_ref, w2k_ref, w2v_ref, kc_ref, vc_ref, *, nch):
    kc_ref[0] = _compress(lambda l: xk_ref[0, pl.ds(l, nch, stride=CMP_STRIDE), :], nch, pek_ref, wk_ref, w2k_ref)
    vc_ref[0] = _compress(lambda l: xv_ref[0, pl.ds(l, nch, stride=CMP_STRIDE), :], nch, pev_ref, wv_ref, w2v_ref).T


def _cmp_weights(pe, w1, w2):
    pe2 = jnp.tile(pe, (1, NSA_KV_HEADS))
    z = jnp.zeros((CMP_STRIDE, HEAD_DIM, CMP_HIDDEN), w1.dtype)
    a, b = w1[:CMP_STRIDE], w1[CMP_STRIDE:]
    top = jnp.concatenate([a, z, b, z], axis=2)
    bot = jnp.concatenate([z, a, z, b], axis=2)
    wbd = jnp.concatenate([top, bot], axis=1).astype(bf16)
    return pe2, wbd, w2.astype(bf16)


def _cmp_weights_dense(pe, w1, w2):
    half = CMP_STRIDE * HEAD_DIM
    w = jnp.concatenate([w1[:CMP_STRIDE].reshape(half, CMP_HIDDEN), w1[CMP_STRIDE:].reshape(half, CMP_HIDDEN)], axis=1)
    return pe.reshape(2, half), w.astype(bf16), w2.astype(bf16)


def cmp_prompt(rows, cw):
    B, T, _ = rows.shape
    nch = T // CMP_STRIDE
    pek, wk, w2k, pev, wv, w2v = cw
    full = lambda a: pl.BlockSpec(a.shape, lambda b: (0,) * a.ndim)
    return pl.pallas_call(
        functools.partial(_cmp_prompt_kernel, nch=nch),
        grid=(B,),
        in_specs=[pl.BlockSpec((1, T, LANES), lambda b: (b, 0, 0)),
                  pl.BlockSpec((1, T, LANES), lambda b: (b, 0, 1)),
                  full(pek), full(pev), full(wk), full(wv), full(w2k), full(w2v)],
        out_specs=[pl.BlockSpec((1, nch, LANES), lambda b: (b, 0, 0)), pl.BlockSpec((1, LANES, nch), lambda b: (b, 0, 0))],
        out_shape=[jax.ShapeDtypeStruct((B, nch, LANES), f32), jax.ShapeDtypeStruct((B, LANES, nch), f32)],
        compiler_params=_cparams("parallel"),
        name="cmp_prompt",
    )(rows, rows, pek, pev, wk, wv, w2k, w2v)


def _padded_queries(q_all, kvh, tq):
    del tq
    lo = kvh * NSA_GROUP
    return jnp.concatenate([q_all[:, (lo + g) * LANES:(lo + g + 1) * LANES] for g in range(NSA_GROUP)], axis=0)


def _cmp_probs(qm, kc, pos_col, kvh, tq):
    ncp = kc.shape[0]
    s = lax.dot_general(qm, kc.astype(bf16), NT, preferred_element_type=f32)
    n_idx = lax.broadcasted_iota(jnp.int32, (tq, ncp), 1)
    cdist = pos_col - (n_idx * CMP_STRIDE + (CMP_BLOCK - 1))
    cvis = cdist >= 0
    any_vis = pos_col >= (CMP_BLOCK - 1)
    cdf = cdist.astype(f32)
    p_sum = jnp.zeros((tq, ncp), f32)
    ps = []
    for g in range(NSA_GROUP):
        sg = jnp.where(cvis, s[g * tq:(g + 1) * tq] - SLOPES[kvh][g] * cdf, NEG_INF)
        m = jnp.max(sg, axis=-1, keepdims=True)
        e = jnp.exp(sg - m)
        p = e / jnp.sum(e, axis=-1, keepdims=True)
        p = jnp.where(any_vis, p, 0.0)
        p_sum = p_sum + p
        ps.append(p)
    return jnp.concatenate(ps, axis=0).astype(bf16), p_sum


def _select_blocks(p_sum, pos_col, ns, nsp, tq):
    ncp = p_sum.shape[1]
    ci = lax.broadcasted_iota(jnp.int32, (ncp, nsp), 0) * CMP_STRIDE
    cj = lax.broadcasted_iota(jnp.int32, (ncp, nsp), 1)
    c2s = ((ci < (cj + 1) * SLC_BLOCK) & (ci + CMP_BLOCK > cj * SLC_BLOCK)).astype(f32).astype(bf16)
    imp = _dot_exact_rhs(p_sum, c2s)
    blk = lax.broadcasted_iota(jnp.int32, (tq, nsp), 1)
    cur = pos_col // SLC_BLOCK
    forced = (blk == 0) | (blk == cur) | (blk == cur - 1)
    imp = jnp.where(forced, FORCED, jnp.where(blk <= cur, imp, NEG_INF))
    rank = jnp.zeros((tq, nsp), jnp.int32)
    for jp in range(ns):
        col = imp[:, jp:jp + 1]
        beats = (col > imp) | ((col == imp) & (blk > jp))
        rank = rank + beats.astype(jnp.int32)
    return ((rank < SLC_TOPN) & (blk <= cur)).astype(f32)


def _combine(gates, kvh, tq, o_cmp, o_slc, o_win):
    lo = kvh * HEAD_DIM
    outs = []
    for g in range(NSA_GROUP):
        h = kvh * NSA_GROUP + g
        r = slice(g * tq, (g + 1) * tq)
        o = (gates[:, 3 * h:3 * h + 1] * o_cmp[r] + gates[:, 3 * h + 1:3 * h + 2] * o_slc[r]
             + gates[:, 3 * h + 2:3 * h + 3] * o_win[r])
        outs.append(o[:, lo:lo + HEAD_DIM])
    return outs


LOG2E = 1.4426950408889634
SLOPES2 = [[v * LOG2E for v in row] for row in SLOPES]

POS_SPLIT = 8
F_SEL0 = 16
MASK_BIG = 1e30


def position_features(pos, blk, ns):
    n = pos.shape[0]
    lane = np.arange(HEAD_DIM)[None, :]
    hi = ((pos // POS_SPLIT) * POS_SPLIT).astype(np.float32)[:, None]
    lo = (pos % POS_SPLIT).astype(np.float32)[:, None]
    f = np.where(lane < 3, hi, np.where(lane < 6, lo, np.where(lane < 9, 1.0, 0.0)))
    if ns:
        f = np.where((lane >= F_SEL0) & (lane < F_SEL0 + ns), (blk[:, None] == lane - F_SEL0).astype(np.float32), f)
    return np.broadcast_to(f, (n, HEAD_DIM)).astype(np.float32)


def feature_tables(T, nch):
    pos = np.arange(T)
    kf = position_features(pos, pos // SLC_BLOCK, T // SLC_BLOCK)
    cpos = np.arange(nch) * CMP_STRIDE + (CMP_BLOCK - 1)
    cf = position_features(cpos, cpos, 0)
    z = np.zeros_like
    two = lambda a: jnp.asarray(np.stack([np.concatenate([z(a), a], axis=1),
                                          np.concatenate([a, z(a)], axis=1)]).astype(bf16))
    return two(kf), two(cf)


def _query_features(pos_row, selT, kvh, tq):
    posf = pos_row.astype(f32)
    r16 = lax.broadcasted_iota(jnp.int32, (16, tq), 0)
    zeros16 = jnp.zeros((16, tq), f32)
    selbig = (selT - 1.0) * MASK_BIG
    ns = selT.shape[0]
    pad = jnp.zeros((HEAD_DIM - F_SEL0 - ns, tq), f32)
    plain, full = [], []
    for g in range(NSA_GROUP):
        s = jnp.full((1, tq), SLOPES2[kvh][g], f32)
        s_parts = _split3(s)
        a_parts = _split3(s * posf)
        rows = [p.astype(f32) for p in s_parts] * 2 + [-p.astype(f32) for p in a_parts]
        blk0 = zeros16
        for i, rv in enumerate(rows):
            blk0 = jnp.where(r16 == i, rv, blk0)
        plain.append(jnp.concatenate([blk0, jnp.zeros((HEAD_DIM - 16, tq), f32)], axis=0).astype(bf16))
        full.append(jnp.concatenate([blk0, selbig, pad], axis=0).astype(bf16))
    return plain, full


def _aug_queries(qT, feats, kvh, tq):
    out = []
    for g in range(NSA_GROUP):
        h = kvh * NSA_GROUP + g
        qh = qT[h * HEAD_DIM:(h + 1) * HEAD_DIM, :]
        out.append(jnp.concatenate([qh, feats[g]] if kvh == 0 else [feats[g], qh], axis=0))
    return out


def _aug_keys(k_tile, feat_tile, kvh):
    lane = lax.broadcasted_iota(jnp.int32, k_tile.shape, 1)
    own = (lane < HEAD_DIM) if kvh == 0 else (lane >= HEAD_DIM)
    return jnp.where(own, k_tile, feat_tile)


def _with_ones(vT, kvh):
    half = vT.shape[0] // 2
    ones = jnp.ones((half, vT.shape[1]), vT.dtype)
    return jnp.concatenate([vT[:half], ones] if kvh == 0 else [ones, vT[half:]], axis=0)


def _cmp_branch_t(q_aug, kc_aug, vcT, pos_row, tq):
    ncp = vcT.shape[1]
    n_idx = lax.broadcasted_iota(jnp.int32, (ncp, tq), 0)
    cvis = pos_row >= (n_idx * CMP_STRIDE + (CMP_BLOCK - 1))
    any_vis = pos_row >= (CMP_BLOCK - 1)
    s = [jnp.where(cvis, jnp.dot(k, q, preferred_element_type=f32), NEG_INF) for k, q in zip(kc_aug, q_aug)]
    e = [jnp.exp2(x - jnp.max(x, axis=0, keepdims=True)) for x in s]
    p = [jnp.where(any_vis, x / jnp.sum(x, axis=0, keepdims=True), 0.0) for x in e]
    outs = [jnp.dot(vcT, x.astype(bf16), preferred_element_type=f32) for x in p]
    return outs, p


def _select_blocks_t(p_sumT, pos_row, ns, tq):
    ncp = p_sumT.shape[0]
    cj = lax.broadcasted_iota(jnp.int32, (ns, ncp), 0)
    ci = lax.broadcasted_iota(jnp.int32, (ns, ncp), 1) * CMP_STRIDE
    c2sT = ((ci < (cj + 1) * SLC_BLOCK) & (ci + CMP_BLOCK > cj * SLC_BLOCK)).astype(f32).astype(bf16)
    hi, mid, lo = _split3(p_sumT)
    d = lambda x: jnp.dot(c2sT, x, preferred_element_type=f32)
    imp = d(hi) + d(mid) + d(lo)
    blk = lax.broadcasted_iota(jnp.int32, (ns, tq), 0)
    cur = pos_row // SLC_BLOCK
    forced = (blk == 0) | (blk == cur) | (blk == cur - 1)
    imp = jnp.where(forced, FORCED, jnp.where(blk <= cur, imp, NEG_INF))
    rank = jnp.zeros((ns, tq), jnp.int32)
    for jp in range(ns):
        row = imp[jp:jp + 1, :]
        beats = (row > imp) | ((row == imp) & (blk > jp))
        rank = rank + beats.astype(jnp.int32)
    return ((rank < SLC_TOPN) & (blk <= cur)).astype(f32)


def _nsa_prompt_kernel(qT_ref, gT_ref, sk_ref, svT_ref, wk_ref, wvT_ref, kc_ref, vcT_ref, kf_ref, cf_ref, o_ref,
                       m_s, acc_s, *, tq, kt, ns):
    qi = pl.program_id(1)
    p0 = qi * tq
    pos_row = p0 + lax.broadcasted_iota(jnp.int32, (1, tq), 1)
    gates = _sigmoid(gT_ref[...])
    qT = qT_ref[...]
    kc = kc_ref[0].astype(bf16)
    vcT = vcT_ref[0].astype(bf16)
    half = LANES // 2
    KV = range(NSA_KV_HEADS)

    zsel = jnp.zeros((ns, tq), f32)
    q_pos = [_aug_queries(qT, _query_features(pos_row, zsel, kvh, tq)[0], kvh, tq) for kvh in KV]
    kc_aug = [_aug_keys(kc, cf_ref[kvh], kvh) for kvh in KV]
    oc, pc = _cmp_branch_t([q for kvh in KV for q in q_pos[kvh]], [kc_aug[kvh] for kvh in KV for _ in range(NSA_GROUP)],
                           vcT, pos_row, tq)
    o_cmp = [oc[kvh * NSA_GROUP:(kvh + 1) * NSA_GROUP] for kvh in KV]
    q_sel = []
    for kvh in KV:
        p_sum = pc[kvh * NSA_GROUP]
        for x in pc[kvh * NSA_GROUP + 1:(kvh + 1) * NSA_GROUP]:
            p_sum = p_sum + x
        selT = _select_blocks_t(p_sum, pos_row, ns, tq)
        q_sel.append(_aug_queries(qT, _query_features(pos_row, selT, kvh, tq)[1], kvh, tq))

    m_s[...] = jnp.full(m_s.shape, NEG_INF, f32)
    acc_s[...] = jnp.zeros(acc_s.shape, f32)
    rel = pos_row - lax.broadcasted_iota(jnp.int32, (kt, tq), 0)

    def slc_tile(c, causal):
        k0 = pl.multiple_of(c * kt, kt)
        k_tile = sk_ref[pl.ds(k0, kt), :]
        v_tile = svT_ref[:, pl.ds(k0, kt)]
        ok = (rel - k0) >= 0
        for kvh in KV:
            ka = _aug_keys(k_tile, kf_ref[kvh, pl.ds(k0, kt), :], kvh)
            va = _with_ones(v_tile, kvh)
            for g in range(NSA_GROUP):
                cs = slice(g * tq, (g + 1) * tq)
                sg = jnp.dot(ka, q_sel[kvh][g], preferred_element_type=f32)
                if causal:
                    sg = jnp.where(ok, sg, NEG_INF)
                m_old = m_s[kvh, :, cs]
                m_new = jnp.maximum(m_old, jnp.max(sg, axis=0, keepdims=True))
                alpha = jnp.exp2(m_old - m_new)
                p = jnp.exp2(sg - m_new).astype(bf16)
                acc_s[kvh, :, cs] = alpha * acc_s[kvh, :, cs] + jnp.dot(va, p, preferred_element_type=f32)
                m_s[kvh, :, cs] = m_new

    n_kt = (p0 + tq + kt - 1) // kt

    def slc_body(c, _):
        slc_tile(c, False)
        return 0

    lax.fori_loop(0, n_kt - 1, slc_body, 0)
    slc_tile(n_kt - 1, True)

    wlen = WINDOW + tq
    w0 = pl.multiple_of(jnp.maximum(p0 - WINDOW, 0), tq)
    k_win = wk_ref[pl.ds(w0, wlen), :]
    v_win = wvT_ref[:, pl.ds(w0, wlen)]
    dist = pos_row - (w0 + lax.broadcasted_iota(jnp.int32, (wlen, tq), 0))
    valid = (dist >= 0) & (dist < WINDOW)
    ch = [(kvh, g) for kvh in KV for g in range(NSA_GROUP)]
    ka = [_aug_keys(k_win, kf_ref[kvh, pl.ds(w0, wlen), :], kvh) for kvh in KV]
    va = [_with_ones(v_win, kvh) for kvh in KV]
    sg = [jnp.where(valid, jnp.dot(ka[kvh], q_pos[kvh][g], preferred_element_type=f32), NEG_INF) for kvh, g in ch]
    p = [jnp.exp2(x - jnp.max(x, axis=0, keepdims=True)).astype(bf16) for x in sg]
    o_win = [jnp.dot(va[kvh], p[i], preferred_element_type=f32) for i, (kvh, g) in enumerate(ch)]
    heads = []
    for i, (kvh, g) in enumerate(ch):
        h = kvh * NSA_GROUP + g
        d0, l0 = (0, half) if kvh == 0 else (half, 0)
        o_slc = acc_s[kvh, :, g * tq:(g + 1) * tq]
        heads.append(gates[3 * h:3 * h + 1] * o_cmp[kvh][g][d0:d0 + half]
                     + gates[3 * h + 1:3 * h + 2] * (o_slc[d0:d0 + half] / o_slc[l0:l0 + 1])
                     + gates[3 * h + 2:3 * h + 3] * (o_win[i][d0:d0 + half] / o_win[i][l0:l0 + 1]))
    o_ref[0] = jnp.concatenate(heads, axis=0).T.astype(o_ref.dtype)


def nsa_prompt(qT, gT, sk, svT, wk, wvT, kc, vcT, B, tq=128, kt=512):
    N = sk.shape[0]
    T = N // B
    ns = T // SLC_BLOCK
    nq = T // tq
    ncp = kc.shape[1]
    assert T % kt == 0 and T % tq == 0 and kt % tq == 0 and T >= WINDOW + tq and T <= POS_SPLIT * 256
    assert F_SEL0 + ns <= HEAD_DIM
    kf, cf = feature_tables(T, ncp)
    keys = pl.BlockSpec((T, LANES), lambda b, i: (b, 0))
    vals = pl.BlockSpec((LANES, T), lambda b, i: (0, b))
    return pl.pallas_call(
        functools.partial(_nsa_prompt_kernel, tq=tq, kt=kt, ns=ns),
        grid=(B, nq),
        in_specs=[pl.BlockSpec((NSA_WIDTH, tq), lambda b, i: (0, b * nq + i)),
                  pl.BlockSpec((3 * NSA_HEADS, tq), lambda b, i: (0, b * nq + i)),
                  keys, vals, keys, vals,
                  pl.BlockSpec((1, ncp, LANES), lambda b, i: (b, 0, 0)),
                  pl.BlockSpec((1, LANES, ncp), lambda b, i: (b, 0, 0)),
                  pl.BlockSpec((NSA_KV_HEADS, T, LANES), lambda b, i: (0, 0, 0)),
                  pl.BlockSpec((NSA_KV_HEADS, ncp, LANES), lambda b, i: (0, 0, 0))],
        out_specs=pl.BlockSpec((1, tq, NSA_WIDTH), lambda b, i: (b, i, 0)),
        out_shape=jax.ShapeDtypeStruct((B, T, NSA_WIDTH), bf16),
        scratch_shapes=[pltpu.VMEM((NSA_KV_HEADS, 1, NSA_GROUP * tq), f32),
                        pltpu.VMEM((NSA_KV_HEADS, LANES, NSA_GROUP * tq), f32)],
        compiler_params=_cparams("parallel", "arbitrary"),
        name="nsa_prompt",
    )(qT, gT, sk, svT, wk, wvT, kc, vcT, kf, cf)


SF_SPLIT = 64
SF_BLK0 = 16
SF_ROWS = 256


def sample_key_features(past_len, nk, tq, ns):
    sf_new0 = SF_BLK0 + -(-ns // 8) * 8
    assert sf_new0 + tq + 1 <= SF_ROWS and nk <= SF_SPLIT * 256
    pos = np.arange(nk)[None, :]
    row = np.arange(SF_ROWS)[:, None]
    hi = ((pos // SF_SPLIT) * SF_SPLIT).astype(np.float32)
    lo = (pos % SF_SPLIT).astype(np.float32)
    f = np.where(row < 3, hi, np.where(row < 6, lo, np.where(row < 9, 1.0, 0.0)))
    blk = (row >= SF_BLK0) & (row < SF_BLK0 + ns) & (pos // SLC_BLOCK == row - SF_BLK0)
    new = (row >= sf_new0) & (row < sf_new0 + tq) & (pos - past_len == row - sf_new0)
    pad = (row == sf_new0 + tq) & (pos >= past_len + tq)
    return jnp.asarray(np.where(blk | new | pad, 1.0, f).astype(bf16)), sf_new0


def _sample_query_features(sel, pos_col, kvh, tq, ns, sf_new0):
    lane = lax.broadcasted_iota(jnp.int32, (tq, SF_ROWS), 1)
    t_idx = lax.broadcasted_iota(jnp.int32, (tq, SF_ROWS), 0)
    selbig = (pltpu.roll(sel[:, 0:SF_ROWS], SF_BLK0, 1) - 1.0) * MASK_BIG
    base = jnp.where((lane >= SF_BLK0) & (lane < SF_BLK0 + ns), selbig, 0.0)
    base = jnp.where((lane >= sf_new0) & (lane < sf_new0 + tq) & (lane - sf_new0 > t_idx), -MASK_BIG, base)
    base = jnp.where(lane == sf_new0 + tq, -MASK_BIG, base)
    posf = pos_col.astype(f32)
    out = []
    for g in range(NSA_GROUP):
        s = jnp.full((tq, 1), SLOPES[kvh][g], f32)
        cols = [p.astype(f32) for p in _split3(s)] * 2 + [-p.astype(f32) for p in _split3(s * posf)]
        qf = base
        for i, cv in enumerate(cols):
            qf = jnp.where(lane == i, cv, qf)
        out.append(qf)
    return jnp.concatenate(out, axis=0).astype(bf16)


def _new_rows_t(rows, width):
    tq = rows.shape[0]
    sq = jnp.concatenate([rows, jnp.zeros((LANES - tq, LANES), f32)], axis=0).T
    if width == LANES:
        return sq
    return jnp.concatenate([sq, jnp.zeros((LANES, width - LANES), f32)], axis=1)


def _compress_staged(xs, t, nch, pe_ref, w_ref, w2_ref):
    lhs = jnp.concatenate([jnp.concatenate([xs[2 * t + h, j] for j in range(CMP_STRIDE // 2)], axis=1)
                           for h in range(NSA_KV_HEADS)], axis=0)
    acc = jnp.dot(lhs, w_ref[...], preferred_element_type=f32)
    bc = lambda v: jnp.broadcast_to(v, (8, v.shape[1])).astype(bf16)
    c = (jnp.dot(bc(pe_ref[0:1, :]), w_ref[:, 0:CMP_HIDDEN], preferred_element_type=f32)
         + jnp.dot(bc(pe_ref[1:2, :]), w_ref[:, CMP_HIDDEN:], preferred_element_type=f32))[0:1, :]
    outs = []
    for h in range(NSA_KV_HEADS):
        r = slice(h * nch, (h + 1) * nch)
        hid = acc[r, 0:CMP_HIDDEN] + pltpu.roll(acc[r, CMP_HIDDEN:], nch - 1, 0) + c
        outs.append(jnp.dot(_silu(hid).astype(bf16), w2_ref[...], preferred_element_type=f32))
    return jnp.concatenate(outs, axis=1)


def _nsa_sample_kernel(pt_ref, *refs, pps, n_steps, tq, past_len, ns, nsp, sf_new0):
    page_refs = refs[:pps]
    (q_ref, g_ref, rn_ref, wn_ref, cw_ref, ft_ref, perm_ref, pek_ref, pev_ref, wk_ref, wv_ref, w2k_ref, w2v_ref,
     o_ref, xs_s, kT_s, vT_s) = refs[pps:]
    del pt_ref
    step = pl.program_id(1)
    cpp = PAGE_SIZE // CMP_STRIDE
    staged = []
    for i in range(pps):
        r0 = pl.multiple_of((step * pps + i) * PAGE_SIZE, PAGE_SIZE)
        pg = page_refs[i]
        slabs = []
        for th in range(4):
            x = pg[0, th]
            slabs += [x, pltpu.roll(x, PAGE_SIZE - 1, 1)]
        a = jnp.concatenate(slabs, axis=0).astype(bf16)
        staged.append(lax.dot_general(perm_ref[...], a, NT, preferred_element_type=f32))
        kT_s[:, pl.ds(r0, PAGE_SIZE)] = pg[0, 4:6].reshape(LANES, PAGE_SIZE).astype(bf16)
        vT_s[:, pl.ds(r0, PAGE_SIZE)] = pg[0, 6:8].reshape(LANES, PAGE_SIZE).astype(bf16)
    c0 = pl.multiple_of(step * (pps * cpp), pps * cpp)
    for j in range(CMP_STRIDE // 2):
        rows = slice(j * cpp, (j + 1) * cpp)
        both = jnp.concatenate([t[rows] for t in staged], axis=0).astype(bf16)
        for th in range(4):
            xs_s[th, j, pl.ds(c0, pps * cpp), :] = both[:, th * LANES:(th + 1) * LANES]

    @pl.when(step == n_steps - 1)
    def _():
        nk = kT_s.shape[1]
        rn = rn_ref[0]
        kT_s[:, past_len:] = _new_rows_t(rn[:, 2 * LANES:3 * LANES], nk - past_len).astype(bf16)
        vT_s[:, past_len:] = _new_rows_t(rn[:, 3 * LANES:4 * LANES], nk - past_len).astype(bf16)
        nwin = cw_ref.shape[2]
        wn = wn_ref[0]
        wkT = jnp.concatenate([cw_ref[0, 0:LANES, :], _new_rows_t(wn[:, 0:LANES], LANES)], axis=1).astype(bf16)
        wvT = jnp.concatenate([cw_ref[0, LANES:, :], _new_rows_t(wn[:, LANES:], LANES)], axis=1).astype(bf16)
        wlen = nwin + LANES

        nch = past_len // CMP_STRIDE
        kc = _compress_staged(xs_s, 0, nch, pek_ref, wk_ref, w2k_ref)

        pos_col = past_len + lax.broadcasted_iota(jnp.int32, (tq, 1), 0)
        gates = _sigmoid(g_ref[0])
        q_all = q_ref[0]
        wpos = (past_len - nwin) + lax.broadcasted_iota(jnp.int32, (tq, wlen), 1)
        wdist = pos_col - wpos
        wvalid = (wdist >= 0) & (wdist < WINDOW)
        wdistf = wdist.astype(f32)
        qms = [_padded_queries(q_all, kvh, tq) for kvh in range(NSA_KV_HEADS)]
        probs = [_cmp_probs(qms[kvh], kc, pos_col, kvh, tq) for kvh in range(NSA_KV_HEADS)]
        vc = _compress_staged(xs_s, 1, nch, pev_ref, wv_ref, w2v_ref).astype(bf16)
        qfs = [_sample_query_features(_select_blocks(probs[kvh][1], pos_col, ns, nsp, tq), pos_col, kvh, tq, ns, sf_new0)
               for kvh in range(NSA_KV_HEADS)]
        o_cmps = [jnp.dot(probs[kvh][0], vc, preferred_element_type=f32) for kvh in range(NSA_KV_HEADS)]
        qm2 = jnp.concatenate(qms, axis=0)
        s = (jnp.dot(qm2, kT_s[...], preferred_element_type=f32)
             + jnp.dot(jnp.concatenate(qfs, axis=0), ft_ref[...], preferred_element_type=f32))
        e = jnp.exp(s - jnp.max(s, axis=-1, keepdims=True))
        o_slc2 = (lax.dot_general(e.astype(bf16), vT_s[...], NT, preferred_element_type=f32)
                  / jnp.sum(e, axis=-1, keepdims=True))
        sw2 = jnp.dot(qm2, wkT, preferred_element_type=f32)
        pw = []
        for kvh in range(NSA_KV_HEADS):
            for g in range(NSA_GROUP):
                r = slice((kvh * NSA_GROUP + g) * tq, (kvh * NSA_GROUP + g + 1) * tq)
                sg = jnp.where(wvalid, sw2[r] - SLOPES[kvh][g] * wdistf, NEG_INF)
                ew = jnp.where(wvalid, jnp.exp(sg - jnp.max(sg, axis=-1, keepdims=True)), 0.0)
                pw.append(ew / jnp.sum(ew, axis=-1, keepdims=True))
        o_win2 = lax.dot_general(jnp.concatenate(pw, axis=0).astype(bf16), wvT, NT, preferred_element_type=f32)
        outs = []
        for kvh in range(NSA_KV_HEADS):
            r = slice(kvh * NSA_GROUP * tq, (kvh + 1) * NSA_GROUP * tq)
            outs.extend(_combine(gates, kvh, tq, o_cmps[kvh], o_slc2[r], o_win2[r]))
        o_ref[0] = jnp.concatenate(outs, axis=1).astype(o_ref.dtype)


def nsa_sample(page_table, cacheT, q, gates, rows_new, win_new, cache_winT, cw, pps=16):
    B, tq, _ = q.shape
    n_pages = page_table.shape[1]
    past_len = n_pages * PAGE_SIZE
    assert n_pages % pps == 0 and past_len % SLC_BLOCK == 0
    n_steps = n_pages // pps
    nk = past_len + LANES
    ns = past_len // SLC_BLOCK + 1
    nsp = -(-ns // LANES) * LANES
    nch = past_len // CMP_STRIDE
    nwin = cache_winT.shape[2]
    pek, wk, w2k, pev, wv, w2v = cw
    ft, sf_new0 = sample_key_features(past_len, nk, tq, ns)
    cpp = PAGE_SIZE // CMP_STRIDE
    ro = np.arange(PAGE_SIZE)
    ro2 = np.arange(PAGE_SIZE // 2)
    perm = jnp.asarray((ro[None, :] == (ro2[:, None] % cpp) * CMP_STRIDE + 2 * (ro2[:, None] // cpp))
                       .astype(bf16))
    per_b = lambda n: pl.BlockSpec((1, tq, n), lambda b, s, pt: (b, 0, 0))
    full = lambda a: pl.BlockSpec(a.shape, lambda b, s, pt: (0,) * a.ndim)
    page_specs = [pl.BlockSpec((1, 8, HEAD_DIM, PAGE_SIZE),
                               functools.partial(lambda b, s, pt, i: (pt[b, s * pps + i], 0, 0, 0), i=i))
                  for i in range(pps)]
    grid_spec = pltpu.PrefetchScalarGridSpec(
        num_scalar_prefetch=1,
        grid=(B, n_steps),
        in_specs=page_specs + [per_b(NSA_HEADS * LANES), per_b(3 * NSA_HEADS), per_b(4 * LANES), per_b(2 * LANES),
                               pl.BlockSpec((1, 2 * LANES, nwin), lambda b, s, pt: (b, 0, 0)),
                               full(ft), full(perm), full(pek), full(pev), full(wk), full(wv), full(w2k), full(w2v)],
        out_specs=per_b(NSA_WIDTH),
        scratch_shapes=[pltpu.VMEM((4, CMP_STRIDE // 2, nch, LANES), bf16),
                        pltpu.VMEM((LANES, nk), bf16), pltpu.VMEM((LANES, nk), bf16)])
    return pl.pallas_call(
        functools.partial(_nsa_sample_kernel, pps=pps, n_steps=n_steps, tq=tq, past_len=past_len, ns=ns, nsp=nsp,
                          sf_new0=sf_new0),
        grid_spec=grid_spec,
        out_shape=jax.ShapeDtypeStruct((B, tq, NSA_WIDTH), bf16),
        compiler_params=_cparams("parallel", "arbitrary"),
        name="nsa_sample",
    )(page_table, *([cacheT] * pps), q, gates, rows_new, win_new, cache_winT, ft, perm, pek, pev, wk, wv, w2k, w2v)


def _gdn_prep_kernel(x_ref, halo_ref, cb_ref, cw_ref, ba_ref, al_ref, dtb_ref,
                     u_ref, w_ref, qg_ref, kg_ref, qk_ref, eg_ref, xs_s, *, tb, C, t_valid, nb):
    i = pl.program_id(1)
    HK = GDN_HEADS * GDN_DK
    R = min(tb, LANES)
    cpb = R // C

    @pl.when(i == 0)
    def _():
        xs_s[:, 0:8, :] = cb_ref[...]

    @pl.when(i > 0)
    def _():
        xs_s[:, 0:8, :] = halo_ref[...]

    xs_s[:, 8:, :] = x_ref[...]
    ri = lax.broadcasted_iota(jnp.int32, (R, R), 0)
    ci = lax.broadcasted_iota(jnp.int32, (R, R), 1)
    same = (ri // C) == (ci // C)
    lower = same & (ri >= ci)
    strict = same & (ri > ci)
    eye = (ri == ci).astype(f32)
    rr = lax.broadcasted_iota(jnp.int32, (R, LANES), 0)
    row_t = i * tb + lax.broadcasted_iota(jnp.int32, (tb, 1), 0)
    live = row_t < t_valid

    def conv_act(bb, c0):
        acc = cw_ref[GDN_CONV - 1:GDN_CONV, c0:c0 + LANES] * xs_s[bb, pl.ds(8, tb), c0:c0 + LANES]
        for k in range(1, GDN_CONV):
            acc = acc + cw_ref[GDN_CONV - 1 - k:GDN_CONV - k, c0:c0 + LANES] * xs_s[bb, pl.ds(8 - k, tb), c0:c0 + LANES]
        return _silu(acc)

    chains = []
    for bb, h in [(bb, h) for bb in range(nb) for h in range(GDN_HEADS)]:
        ba = ba_ref[bb]
        q = conv_act(bb, h * GDN_DK)
        k = conv_act(bb, HK + h * GDN_DK)
        v = conv_act(bb, 2 * HK + h * GDN_DV)
        q = q * lax.rsqrt(jnp.sum(q * q, axis=-1, keepdims=True) + L2_EPS) * (GDN_DK ** -0.5)
        k = k * lax.rsqrt(jnp.sum(k * k, axis=-1, keepdims=True) + L2_EPS)
        beta = _sigmoid(ba[:, h:h + 1])
        ar = ba[:, GDN_HEADS + h:GDN_HEADS + h + 1] + dtb_ref[:, h:h + 1]
        softplus = jnp.maximum(ar, 0.0) + jnp.log(1.0 + jnp.exp(-jnp.abs(ar)))
        gcol = -jnp.exp(al_ref[:, h:h + 1]) * softplus
        if t_valid < 10 ** 9:
            beta = jnp.where(live, beta, 0.0)
            gcol = jnp.where(live, gcol, 0.0)
            k = jnp.where(live, k, 0.0)
            v = jnp.where(live, v, 0.0)
        for blk in range(tb // R):
            r = slice(blk * R, (blk + 1) * R)
            qc, kc, vc, bc = q[r], k[r], v[r], beta[r]
            gc = jnp.broadcast_to(gcol[r], (R, LANES))
            sh = 1
            while sh < C:
                gc = gc + jnp.where((rr % C) >= sh, pltpu.roll(gc, sh, 0), 0.0)
                sh *= 2
            gct = gc.T
            decay = jnp.exp(jnp.where(lower, gc[:, 0:R] - gct[0:R, :], NEG_INF))
            kb = kc * bc
            a = jnp.where(strict, _dot_hp(kb, kc, NT) * decay, 0.0)
            qk = jnp.where(lower, lax.dot_general(qc.astype(bf16), kc.astype(bf16), NT,
                                                  preferred_element_type=f32) * decay, 0.0)
            gl = gc[R - 1:R, :]
            for j in range(cpb - 2, -1, -1):
                gl = jnp.where(rr < (j + 1) * C, gc[(j + 1) * C - 1:(j + 1) * C, :], gl)
            chains.append(dict(bb=bb, h=h, r=r, blk=blk, qc=qc, kc=kc, gc=gc, gl=gl, qk=qk, n=-a, t=eye - a,
                               rhs=jnp.concatenate([vc * bc, kb * jnp.exp(gc)], axis=1)))

    for ch in chains:
        ch['n'] = _split2(ch['n'])
    sh = 2
    while sh < C:
        for ch in chains:
            ch['n'] = _split2(_dot_parts(ch['n'], ch['n']))
        for ch in chains:
            ch['t'] = ch['t'] + _dot_parts(_split2(ch['t']), ch['n'])
        sh *= 2

    for ch in chains:
        bb, h, r, gc, gl = ch['bb'], ch['h'], ch['r'], ch['gc'], ch['gl']
        sol = ch['rhs'] + _dot_hp(ch['t'] - eye, ch['rhs'])
        cs = slice(h * GDN_DV, (h + 1) * GDN_DV)
        u_ref[bb, r, cs] = sol[:, 0:GDN_DV]
        w_ref[bb, r, cs] = sol[:, GDN_DV:].astype(w_ref.dtype)
        qg_ref[bb, r, cs] = (ch['qc'] * jnp.exp(gc)).astype(qg_ref.dtype)
        kg_ref[bb, r, cs] = (ch['kc'] * jnp.exp(gl - gc)).astype(kg_ref.dtype)
        qk = ch['qk'][:, 0:C]
        for j in range(1, cpb):
            qk = jnp.where(ri[:, 0:C] >= j * C, ch['qk'][:, j * C:(j + 1) * C], qk)
        qk_ref[bb, r, h * C:(h + 1) * C] = qk.astype(qk_ref.dtype)
        for j in range(cpb):
            eg_ref[bb, ch['blk'] * cpb + j, h:h + 1, :] = jnp.exp(gc[(j + 1) * C - 1:(j + 1) * C, :])
    eg_ref[:, :, GDN_HEADS:, :] = jnp.zeros((nb, tb // C, 8 - GDN_HEADS, LANES), f32)


def gdn_prep(qkv, conv_buf8, conv_w, ba, a_log, dt_bias, C, tb, t_valid, nb=1):
    B, T, CH = qkv.shape
    nb = math.gcd(nb, B)
    hb = tb // 8
    per = lambda n, dt: (pl.BlockSpec((nb, tb, n), lambda b, i: (b, i, 0)), jax.ShapeDtypeStruct((B, T, n), dt))
    outs = [per(GDN_WIDTH, f32), per(GDN_WIDTH, bf16), per(GDN_WIDTH, bf16), per(GDN_WIDTH, bf16),
            per(GDN_HEADS * C, bf16),
            (pl.BlockSpec((nb, tb // C, 8, LANES), lambda b, i: (b, i, 0, 0)),
             jax.ShapeDtypeStruct((B, T // C, 8, LANES), f32))]
    return pl.pallas_call(
        functools.partial(_gdn_prep_kernel, tb=tb, C=C, t_valid=t_valid, nb=nb),
        grid=(B // nb, T // tb),
        in_specs=[pl.BlockSpec((nb, tb, CH), lambda b, i: (b, i, 0)),
                  pl.BlockSpec((nb, 8, CH), lambda b, i: (b, jnp.maximum(i * hb - 1, 0), 0)),
                  pl.BlockSpec((nb, 8, CH), lambda b, i: (b, 0, 0)),
                  pl.BlockSpec((GDN_CONV, CH), lambda b, i: (0, 0)),
                  pl.BlockSpec((nb, tb, 2 * GDN_HEADS), lambda b, i: (b, i, 0)),
                  pl.BlockSpec((1, GDN_HEADS), lambda b, i: (0, 0)),
                  pl.BlockSpec((1, GDN_HEADS), lambda b, i: (0, 0))],
        out_specs=[o[0] for o in outs],
        out_shape=[o[1] for o in outs],
        scratch_shapes=[pltpu.VMEM((nb, tb + 8, CH), f32)],
        compiler_params=_cparams("parallel", "arbitrary"),
        name="gdn_prep",
    )(qkv, qkv, conv_buf8, conv_w, ba, a_log.reshape(1, GDN_HEADS), dt_bias.reshape(1, GDN_HEADS))


def _gdn_scan_kernel(u_ref, w_ref, qg_ref, kg_ref, qk_ref, eg_ref, z_ref, ng_ref, s0_ref,
                     o_ref, sT_ref, s_s, *, C, nb):
    c = pl.program_id(1)

    @pl.when(c == 0)
    def _():
        s_s[...] = s0_ref[...]

    ch = [(b, h) for b in range(nb) for h in range(GDN_HEADS)]
    cs = [slice(h * GDN_DV, (h + 1) * GDN_DV) for h in range(GDN_HEADS)]
    S = [s_s[b, h] for b, h in ch]
    Sb = [x.astype(bf16) for x in S]
    v_new = [u_ref[b, :, cs[h]] - jnp.dot(w_ref[b, :, cs[h]], Sb[i], preferred_element_type=f32)
             for i, (b, h) in enumerate(ch)]
    vb = [x.astype(bf16) for x in v_new]
    for i, (b, h) in enumerate(ch):
        s_s[b, h] = S[i] * eg_ref[b, 0, h:h + 1, :] + lax.dot_general(kg_ref[b, :, cs[h]], vb[i], TN,
                                                                       preferred_element_type=f32)
    o = [jnp.dot(qg_ref[b, :, cs[h]], Sb[i], preferred_element_type=f32)
         + jnp.dot(qk_ref[b, :, h * C:(h + 1) * C], vb[i], preferred_element_type=f32) for i, (b, h) in enumerate(ch)]
    outs = [_rms(o[i], ng_ref[...]) * _silu(z_ref[b, :, cs[h]]) for i, (b, h) in enumerate(ch)]
    for b in range(nb):
        o_ref[b] = jnp.concatenate(outs[b * GDN_HEADS:(b + 1) * GDN_HEADS], axis=1).astype(o_ref.dtype)

    @pl.when(c == pl.num_programs(1) - 1)
    def _():
        sT_ref[...] = s_s[...]


def gdn_scan(u, w, qg, kg, qk, eg, z, norm_g, s0, C, nb=8):
    B, T, _ = u.shape
    nb = math.gcd(nb, B)
    per = lambda n: pl.BlockSpec((nb, C, n), lambda b, c: (b, c, 0))
    st = pl.BlockSpec((nb, GDN_HEADS, GDN_DK, GDN_DV), lambda b, c: (b, 0, 0, 0))
    return pl.pallas_call(
        functools.partial(_gdn_scan_kernel, C=C, nb=nb),
        grid=(B // nb, T // C),
        in_specs=[per(GDN_WIDTH), per(GDN_WIDTH), per(GDN_WIDTH), per(GDN_WIDTH), per(GDN_HEADS * C),
                  pl.BlockSpec((nb, 1, 8, LANES), lambda b, c: (b, c, 0, 0)),
                  per(GDN_WIDTH), pl.BlockSpec((1, GDN_DV), lambda b, c: (0, 0)), st],
        out_specs=[per(GDN_WIDTH), st],
        out_shape=[jax.ShapeDtypeStruct((B, T, GDN_WIDTH), bf16),
                   jax.ShapeDtypeStruct((B, GDN_HEADS, GDN_DK, GDN_DV), f32)],
        scratch_shapes=[pltpu.VMEM((nb, GDN_HEADS, GDN_DK, GDN_DV), f32)],
        compiler_params=_cparams("parallel", "arbitrary"),
        name="gdn_scan",
    )(u, w, qg, kg, qk, eg, z, norm_g.reshape(1, GDN_DV), s0)


def _nsa_in_groups(w, feature_major):
    c1 = NSA_WIDTH
    c2 = c1 + 4 * NSA_KV_COLS
    c3 = c2 + 2 * NSA_KV_COLS
    c4 = c3 + 3 * NSA_HEADS
    wT = w.T.astype(bf16)
    xa_scale = XA_DIM ** -0.5 * LOG2E
    if feature_major:
        k0 = c1 + 2 * NSA_KV_COLS
        return [(wT[c1:c2], f32, 1.0, FEATURE_MAJOR_PER_BATCH),
                (wT[c1:k0], f32, 1.0, TOKEN_MAJOR),
                (wT[c2:c3], f32, 1.0, FEATURE_MAJOR_PER_BATCH),
                (wT[c4:], bf16, xa_scale, FEATURE_MAJOR),
                (wT[:c1], bf16, HEAD_DIM ** -0.5 * LOG2E, FEATURE_MAJOR),
                (wT[c3:c4], f32, 1.0, FEATURE_MAJOR),
                (wT[k0:k0 + NSA_KV_COLS], bf16, 1.0, TOKEN_MAJOR),
                (wT[k0 + NSA_KV_COLS:c2], bf16, 1.0, FEATURE_MAJOR),
                (wT[c2:c2 + NSA_KV_COLS], bf16, 1.0, TOKEN_MAJOR),
                (wT[c2 + NSA_KV_COLS:c3], bf16, 1.0, FEATURE_MAJOR)]
    wq = wT[:c1].reshape(NSA_KV_HEADS, NSA_GROUP, HEAD_DIM, -1)
    zq = jnp.zeros_like(wq[0])
    wq = jnp.concatenate([jnp.concatenate([wq[0], zq], axis=1), jnp.concatenate([zq, wq[1]], axis=1)], axis=0)
    wq = wq.reshape(NSA_HEADS * LANES, -1)
    return [(wT[c1:c2], f32, 1.0, TOKEN_MAJOR), (wT[c2:c3], f32, 1.0, TOKEN_MAJOR),
            (wT[c4:], bf16, xa_scale, TOKEN_MAJOR), (wq, bf16, HEAD_DIM ** -0.5, TOKEN_MAJOR),
            (wT[c3:c4], f32, 1.0, TOKEN_MAJOR)]


def _gdn_in_groups(w, feature_major):
    c1 = GDN_CONV_CH
    c2 = c1 + GDN_WIDTH
    c3 = c2 + 2 * GDN_HEADS
    wT = w.T.astype(bf16)
    xq = (wT[c3:], bf16, XA_DIM ** -0.5 * LOG2E, FEATURE_MAJOR if feature_major else TOKEN_MAJOR)
    return [(wT[:c1], f32, 1.0, TOKEN_MAJOR), (wT[c1:c2], f32, 1.0, TOKEN_MAJOR),
            (wT[c2:c3], f32, 1.0, TOKEN_MAJOR), xq]


def _trunk(x, mem_kv, nsa_past, gdn_state, p, t_valid):
    B, T, D = x.shape
    N = B * T
    xf = x.reshape(N, D)

    prompt = nsa_past is None
    if prompt:
        kv_rows, rows_cmp, win_rows, xq, qT, gT, sk, svT, wk, wvT = norm_proj(
            xf, p['norm_mix_g'][0], p['nsa_in_prompt'], tm=512, rows_per_batch=T)
        kc, vcT = cmp_prompt(rows_cmp.reshape(B, T, -1), p['cmp'])
        o_mix = nsa_prompt(qT, gT, sk, svT, wk, wvT, kc, vcT, B)
    else:
        rows, win, xq, q, gates = norm_proj(xf, p['norm_mix_g'][0], p['nsa_in_sample'])
        kv_rows = rows.reshape(B, T, -1)
        win_rows = win.reshape(B, T, -1)
        xq = xq.reshape(B, T, -1)
        page_table, cache, cache_win = nsa_past
        o_mix = nsa_sample(page_table, cache, q.reshape(B, T, -1), gates.reshape(B, T, -1), kv_rows, win_rows,
                           cache_win, p['cmp_dense'])
    o_mem = cross_attend(xq, mem_kv, 0, B)
    xf = post_block(xf, o_mix.reshape(N, -1), o_mem.reshape(N, -1), p['w_out'][0], p['norm_ffn_g'][0],
                    p['w_up'][0], p['w_down'][0], p['final_norm_g'], final=False)

    qkv, z, ba, xq = norm_proj(xf, p['norm_mix_g'][1], p['gdn_in_prompt' if prompt else 'gdn_in_sample'], tm=512)
    if not prompt:
        xq = xq.reshape(B, T, -1)
    qkv3 = qkv.reshape(B, T, -1)
    S0, conv_buf8 = gdn_state
    C = min(64, T)
    tb = min(256, T)
    u, w, qg, kg, qk, eg = gdn_prep(qkv3, conv_buf8, p['gdn_conv_w'], ba.reshape(B, T, -1), p['gdn_a_log'],
                                    p['gdn_dt_bias'], C, tb, t_valid, nb=1 if prompt else 4)
    o_mix, S = gdn_scan(u, w, qg, kg, qk, eg, z.reshape(B, T, -1), p['gdn_norm_g'], S0, C)
    o_mem = cross_attend(xq, mem_kv, 1, B)
    y = post_block(xf, o_mix.reshape(N, -1), o_mem.reshape(N, -1), p['w_out'][1], p['norm_ffn_g'][1],
                   p['w_up'][1], p['w_down'][1], p['final_norm_g'], final=True)
    return y.reshape(B, T, D), kv_rows, win_rows, S, qkv3


def kernel(x_prompt, x_sample, mem_prompt, cache_nsa_kv, cache_nsa_win, state_gdn_s, state_gdn_conv,
           cache_mem_kv, page_table, norm_mix_g, norm_mem_g, w_mem_kv, nsa_w_in, cmp_pe_k, cmp_w1_k,
           cmp_w2_k, cmp_pe_v, cmp_w1_v, cmp_w2_v, gdn_w_in, gdn_conv_w, gdn_a_log, gdn_dt_bias,
           gdn_norm_g, w_out, norm_ffn_g, w_up, w_down, final_norm_g):
    B, T, D = x_prompt.shape
    DB, TS, _ = x_sample.shape
    M = mem_prompt.shape[1]
    p = dict(norm_mix_g=norm_mix_g, nsa_in_prompt=_nsa_in_groups(nsa_w_in[0], True),
             nsa_in_sample=_nsa_in_groups(nsa_w_in[0], False), gdn_in_prompt=_gdn_in_groups(gdn_w_in[0], True),
             gdn_in_sample=_gdn_in_groups(gdn_w_in[0], False),
             cmp=_cmp_weights(cmp_pe_k[0], cmp_w1_k[0], cmp_w2_k[0]) + _cmp_weights(cmp_pe_v[0], cmp_w1_v[0], cmp_w2_v[0]),
             cmp_dense=(_cmp_weights_dense(cmp_pe_k[0], cmp_w1_k[0], cmp_w2_k[0])
                        + _cmp_weights_dense(cmp_pe_v[0], cmp_w1_v[0], cmp_w2_v[0])),
             gdn_conv_w=gdn_conv_w[0], gdn_a_log=gdn_a_log[0], gdn_dt_bias=gdn_dt_bias[0], gdn_norm_g=gdn_norm_g[0],
             w_out=w_out.astype(bf16), norm_ffn_g=norm_ffn_g, w_up=w_up.astype(bf16), w_down=w_down.astype(bf16),
             final_norm_g=final_norm_g)

    mem_flat = mem_prompt.reshape(B * M, D)
    mem_kv = [norm_proj(mem_flat, norm_mem_g[i], [(w_mem_kv[i].astype(bf16).T, f32, 1.0, FEATURE_MAJOR_PER_BATCH)],
                        tm=M, rows_per_batch=M)[0] for i in range(DEPTH)]
    mem_kv_p = jnp.stack(mem_kv, axis=1)
    zero_state = (jnp.zeros((B, GDN_HEADS, GDN_DK, GDN_DV), f32), jnp.zeros((B, 8, GDN_CONV_CH), f32))
    y_p, rows_p, win_p, S_p, qkv_p = _trunk(x_prompt, mem_kv_p, None, zero_state, p, 10 ** 9)

    TP = 8
    x_s = jnp.pad(x_sample, ((0, 0), (0, TP - TS), (0, 0)))
    cache = cache_nsa_kv.transpose(0, 2, 3, 4, 5, 1).reshape(cache_nsa_kv.shape[0], -1, HEAD_DIM, PAGE_SIZE)
    nwin = cache_nsa_win.shape[2]
    cache_win = cache_nsa_win.transpose(0, 1, 3, 4, 5, 2).reshape(DB, -1, nwin)
    conv8 = jnp.pad(state_gdn_conv[:, 0], ((0, 0), (8 - (GDN_CONV - 1), 0), (0, 0)))
    mem_kv_s = cache_mem_kv.transpose(0, 1, 3, 4, 5, 2).reshape(DB, DEPTH, 2 * XA_WIDTH, M)
    y_s, rows_s, win_s, S_s, qkv_s = _trunk(x_s, mem_kv_s, (page_table, cache, cache_win),
                                            (state_gdn_s[:, 0], conv8), p, TS)

    kvshape = (4, NSA_KV_HEADS, HEAD_DIM)
    nsa_kv_prompt = rows_p.reshape(B, 1, *kvshape, T).transpose(0, 5, 1, 2, 3, 4)
    nsa_kv_sample = rows_s[:, :TS].reshape(DB, TS, 1, *kvshape)
    wshape = (2, NSA_KV_HEADS, HEAD_DIM)
    wkeep = min(WINDOW, T)
    nsa_win_prompt = win_p[:, :, T - wkeep:].reshape(B, 1, *wshape, wkeep).transpose(0, 1, 5, 2, 3, 4)
    win_cat = jnp.concatenate([cache_win, win_s[:, :TS].transpose(0, 2, 1)], axis=2)[:, :, -WINDOW:]
    nsa_win_sample = win_cat.reshape(DB, 1, *wshape, -1).transpose(0, 1, 5, 2, 3, 4)
    gdn_conv_prompt = qkv_p[:, None, T - (GDN_CONV - 1):]
    conv_cat = jnp.concatenate([state_gdn_conv[:, 0], qkv_s[:, :TS]], axis=1)
    gdn_conv_sample = conv_cat[:, None, -(GDN_CONV - 1):]
    mem_kv_prompt = mem_kv_p.reshape(B, DEPTH, 2, XA_HEADS, XA_DIM, M).transpose(0, 1, 5, 2, 3, 4)
    return (y_p, y_s[:, :TS], nsa_kv_prompt, nsa_kv_sample, nsa_win_prompt, nsa_win_sample,
            S_p[:, None], S_s[:, None], gdn_conv_prompt, gdn_conv_sample, mem_kv_prompt)
```

```python
import functools
import math

import jax
import jax.numpy as jnp
import numpy as np
from jax import lax
from jax.experimental import pallas as pl
from jax.experimental.pallas import tpu as pltpu

f32 = jnp.float32
bf16 = jnp.bfloat16

DEPTH = 2
PAGE_SIZE = 128
HEAD_DIM = 64
NSA_HEADS = 12
NSA_KV_HEADS = 2
NSA_GROUP = NSA_HEADS // NSA_KV_HEADS
NSA_WIDTH = NSA_HEADS * HEAD_DIM
NSA_KV_COLS = NSA_KV_HEADS * HEAD_DIM
CMP_STRIDE = 16
CMP_BLOCK = 2 * CMP_STRIDE
CMP_HIDDEN = 128
SLC_BLOCK = 64
SLC_TOPN = 16
WINDOW = 512
GDN_HEADS = 6
GDN_DK = 128
GDN_DV = 128
GDN_WIDTH = GDN_HEADS * GDN_DV
GDN_CONV = 4
GDN_CONV_CH = 2 * GDN_HEADS * GDN_DK + GDN_HEADS * GDN_DV
XA_HEADS = 4
XA_DIM = 64
XA_WIDTH = XA_HEADS * XA_DIM
NORM_EPS = 1e-6
L2_EPS = 1e-6
NEG_INF = -1e30
FORCED = 1e9

VMEM_LIMIT = 52 * 1024 * 1024
LANES = 128

SLOPES = [[2.0 ** (-8.0 * (k * NSA_GROUP + g + 1) / NSA_HEADS) for g in range(NSA_GROUP)]
          for k in range(NSA_KV_HEADS)]

NT = (((1,), (1,)), ((), ()))
TN = (((0,), (0,)), ((), ()))


def _cparams(*sem):
    return pltpu.CompilerParams(dimension_semantics=sem, vmem_limit_bytes=VMEM_LIMIT)


def _rms(x, g):
    r = lax.rsqrt(jnp.mean(x * x, axis=-1, keepdims=True) + NORM_EPS)
    return (x * r) * g


def _sigmoid(x):
    return 1.0 / (1.0 + jnp.exp(-x))


def _silu(x):
    return x * _sigmoid(x)


def _split3(a):
    hi = a.astype(bf16)
    r1 = a - hi.astype(f32)
    mid = r1.astype(bf16)
    lo = (r1 - mid.astype(f32)).astype(bf16)
    return hi, mid, lo


def _dot_exact_rhs(a, b_bf16):
    hi, mid, lo = _split3(a)
    d = lambda x: jnp.dot(x, b_bf16, preferred_element_type=f32)
    return d(hi) + d(mid) + d(lo)


def _split2(a):
    hi = a.astype(bf16)
    return hi, (a - hi.astype(f32)).astype(bf16)


def _dot_parts(a, b, dims=None):
    if dims is None:
        d = lambda x, y: jnp.dot(x, y, preferred_element_type=f32)
    else:
        d = lambda x, y: lax.dot_general(x, y, dims, preferred_element_type=f32)
    return d(a[0], b[0]) + (d(a[0], b[1]) + d(a[1], b[0]))


def _dot_hp(a, b, dims=None):
    return _dot_parts(_split2(a), _split2(b), dims)


TOKEN_MAJOR, FEATURE_MAJOR, FEATURE_MAJOR_PER_BATCH = 'nt', 'fm', 'fmb'


def _norm_proj_kernel(x_ref, g_ref, *refs, specs):
    n = len(specs)
    w_refs, o_refs = refs[:n], refs[n:]
    h = _rms(x_ref[...], g_ref[...]).astype(bf16)
    for (scale, kind), w_ref, o_ref in zip(specs, w_refs, o_refs):
        if kind == TOKEN_MAJOR:
            y = lax.dot_general(h, w_ref[...], NT, preferred_element_type=f32)
        else:
            y = lax.dot_general(w_ref[...], h, NT, preferred_element_type=f32)
        if scale != 1.0:
            y = y * scale
        o_ref[...] = y.astype(o_ref.dtype).reshape(o_ref.shape)


def norm_proj(x, g, groups, tm=256, rows_per_batch=None):
    N, D = x.shape
    tm = min(tm, N)
    assert N % tm == 0
    in_specs = [pl.BlockSpec((tm, D), lambda i: (i, 0)), pl.BlockSpec((1, D), lambda i: (0, 0))]
    out_specs, out_shape = [], []
    for w, dt, _, kind in groups:
        in_specs.append(pl.BlockSpec(w.shape, lambda i: (0, 0)))
        n = w.shape[0]
        if kind == TOKEN_MAJOR:
            out_specs.append(pl.BlockSpec((tm, n), lambda i: (i, 0)))
            out_shape.append(jax.ShapeDtypeStruct((N, n), dt))
        elif kind == FEATURE_MAJOR:
            out_specs.append(pl.BlockSpec((n, tm), lambda i: (0, i)))
            out_shape.append(jax.ShapeDtypeStruct((n, N), dt))
        else:
            per = rows_per_batch // tm
            assert rows_per_batch % tm == 0
            out_specs.append(pl.BlockSpec((1, n, tm), lambda i: (i // per, 0, i % per)))
            out_shape.append(jax.ShapeDtypeStruct((N // rows_per_batch, n, rows_per_batch), dt))
    return pl.pallas_call(
        functools.partial(_norm_proj_kernel, specs=tuple((s, t) for _, _, s, t in groups)),
        grid=(N // tm,),
        in_specs=in_specs,
        out_specs=out_specs,
        out_shape=out_shape,
        compiler_params=_cparams("parallel"),
        name="norm_proj",
    )(x, g.reshape(1, D), *[w for w, _, _, _ in groups])


def _xattn_kernel(q_ref, kv_ref, o_ref, *, q_feature_major, nb):
    ch = [(b, h) for b in range(nb) for h in range(XA_HEADS)]
    hs = lambda h, off=0: slice(off + h * XA_DIM, off + (h + 1) * XA_DIM)
    qT = [q_ref[...] if q_feature_major else q_ref[b].astype(f32).T.astype(bf16) for b in range(nb)]
    s = [lax.dot_general(kv_ref[b, 0, hs(h), :].astype(bf16), qT[b][hs(h)], TN, preferred_element_type=f32)
         for b, h in ch]
    e = [jnp.exp2(x - jnp.max(x, axis=0, keepdims=True)) for x in s]
    o = [jnp.dot(kv_ref[b, 0, hs(h, XA_WIDTH), :].astype(bf16), e[i].astype(bf16), preferred_element_type=f32)
         / jnp.sum(e[i], axis=0, keepdims=True) for i, (b, h) in enumerate(ch)]
    for b in range(nb):
        o_ref[b] = jnp.concatenate(o[b * XA_HEADS:(b + 1) * XA_HEADS], axis=0).T.astype(o_ref.dtype)


def cross_attend(xq, kvT, layer, B, tq=2048, nb=4):
    q_feature_major = xq.ndim == 2
    T = xq.shape[1] // B if q_feature_major else xq.shape[1]
    M = kvT.shape[3]
    tq = min(tq, T)
    nq = T // tq
    if q_feature_major:
        nb = 1
        q_spec = pl.BlockSpec((XA_WIDTH, tq), lambda b, i: (0, b * nq + i))
    else:
        nb = math.gcd(nb, B)
        q_spec = pl.BlockSpec((nb, tq, XA_WIDTH), lambda b, i: (b, i, 0))
    return pl.pallas_call(
        functools.partial(_xattn_kernel, q_feature_major=q_feature_major, nb=nb),
        grid=(B // nb, nq),
        in_specs=[q_spec, pl.BlockSpec((nb, 1, 2 * XA_WIDTH, M), lambda b, i: (b, layer, 0, 0))],
        out_specs=pl.BlockSpec((nb, tq, XA_WIDTH), lambda b, i: (b, i, 0)),
        out_shape=jax.ShapeDtypeStruct((B, T, XA_WIDTH), bf16),
        compiler_params=_cparams("parallel", "parallel"),
        name="cross_attend",
    )(xq, kvT)


def _post_kernel(x_ref, om_ref, oc_ref, wo_ref, gf_ref, wu_ref, wd_ref, gl_ref, o_ref,
                 x1_s, h_s, acc_s, *, mix_w, final):
    j = pl.program_id(1)

    @pl.when(j == 0)
    def _():
        x1 = x_ref[...] + (jnp.dot(om_ref[...], wo_ref[0:mix_w, :], preferred_element_type=f32)
                           + jnp.dot(oc_ref[...], wo_ref[mix_w:, :], preferred_element_type=f32))
        x1_s[...] = x1
        h_s[...] = _rms(x1, gf_ref[...]).astype(bf16)
        acc_s[...] = jnp.zeros_like(acc_s)

    u = jnp.dot(h_s[...], wu_ref[...], preferred_element_type=f32)
    u = jnp.square(jnp.maximum(u, 0.0)).astype(bf16)
    acc_s[...] += jnp.dot(u, wd_ref[...], preferred_element_type=f32)

    @pl.when(j == pl.num_programs(1) - 1)
    def _():
        x2 = x1_s[...] + acc_s[...]
        if final:
            x2 = _rms(x2, gl_ref[...])
        o_ref[...] = x2


def post_block(x, o_mix, o_mem, w_out, g_ffn, w_up, w_down, g_final, final, tm=1024, tf=512):
    N, D = x.shape
    mix_w = o_mix.shape[1]
    F = w_up.shape[1]
    tm = min(tm, N)
    return pl.pallas_call(
        functools.partial(_post_kernel, mix_w=mix_w, final=final),
        grid=(N // tm, F // tf),
        in_specs=[pl.BlockSpec((tm, D), lambda i, j: (i, 0)),
                  pl.BlockSpec((tm, mix_w), lambda i, j: (i, 0)),
                  pl.BlockSpec((tm, XA_WIDTH), lambda i, j: (i, 0)),
                  pl.BlockSpec((mix_w + XA_WIDTH, D), lambda i, j: (0, 0)),
                  pl.BlockSpec((1, D), lambda i, j: (0, 0)),
                  pl.BlockSpec((D, tf), lambda i, j: (0, j)),
                  pl.BlockSpec((tf, D), lambda i, j: (j, 0)),
                  pl.BlockSpec((1, D), lambda i, j: (0, 0))],
        out_specs=pl.BlockSpec((tm, D), lambda i, j: (i, 0)),
        out_shape=jax.ShapeDtypeStruct((N, D), f32),
        scratch_shapes=[pltpu.VMEM((tm, D), f32), pltpu.VMEM((tm, D), bf16), pltpu.VMEM((tm, D), f32)],
        compiler_params=_cparams("parallel", "arbitrary"),
        name="post_block",
    )(x, o_mix, o_mem, w_out, g_ffn.reshape(1, D), w_up, w_down, g_final.reshape(1, D))


def _compress(load_chunk_rows, nch, pe_ref, w_ref, w2_ref):
    acc_p = jnp.zeros((nch, 2 * CMP_HIDDEN), f32)
    acc_q = jnp.zeros((nch, 2 * CMP_HIDDEN), f32)
    for l in range(CMP_STRIDE):
        x = load_chunk_rows(l)
        xp = (x + pe_ref[l:l + 1, :]).astype(bf16)
        xq = (x + pe_ref[CMP_STRIDE + l:CMP_STRIDE + l + 1, :]).astype(bf16)
        acc_p = acc_p + jnp.dot(xp, w_ref[l, :, 0:2 * CMP_HIDDEN], preferred_element_type=f32)
        acc_q = acc_q + jnp.dot(xq, w_ref[l, :, 2 * CMP_HIDDEN:], preferred_element_type=f32)
    hid = _silu(acc_p + pltpu.roll(acc_q, nch - 1, 0)).astype(bf16)
    outs = [jnp.dot(hid[:, h * CMP_HIDDEN:(h + 1) * CMP_HIDDEN], w2_ref[...], preferred_element_type=f32)
            for h in range(NSA_KV_HEADS)]
    return jnp.concatenate(outs, axis=1)


def _cmp_prompt_kernel(xk_ref, xv_ref, pek_ref, pev_ref, wk_ref, wv_ref, w2k_ref, w2v_ref, kc_ref, vc_ref, *, nch):
    kc_ref[0] = _compress(lambda l: xk_ref[0, pl.ds(l, nch, stride=CMP_STRIDE), :], nch, pek_ref, wk_ref, w2k_ref)
    vc_ref[0] = _compress(lambda l: xv_ref[0, pl.ds(l, nch, stride=CMP_STRIDE), :], nch, pev_ref, wv_ref, w2v_ref).T


def _cmp_weights(pe, w1, w2):
    pe2 = jnp.tile(pe, (1, NSA_KV_HEADS))
    z = jnp.zeros((CMP_STRIDE, HEAD_DIM, CMP_HIDDEN), w1.dtype)
    a, b = w1[:CMP_STRIDE], w1[CMP_STRIDE:]
    top = jnp.concatenate([a, z, b, z], axis=2)
    bot = jnp.concatenate([z, a, z, b], axis=2)
    wbd = jnp.concatenate([top, bot], axis=1).astype(bf16)
    return pe2, wbd, w2.astype(bf16)


def _cmp_weights_dense(pe, w1, w2):
    half = CMP_STRIDE * HEAD_DIM
    w = jnp.concatenate([w1[:CMP_STRIDE].reshape(half, CMP_HIDDEN), w1[CMP_STRIDE:].reshape(half, CMP_HIDDEN)], axis=1)
    return pe.reshape(2, half), w.astype(bf16), w2.astype(bf16)


def cmp_prompt(rows, cw):
    B, T, _ = rows.shape
    nch = T // CMP_STRIDE
    pek, wk, w2k, pev, wv, w2v = cw
    full = lambda a: pl.BlockSpec(a.shape, lambda b: (0,) * a.ndim)
    return pl.pallas_call(
        functools.partial(_cmp_prompt_kernel, nch=nch),
        grid=(B,),
        in_specs=[pl.BlockSpec((1, T, LANES), lambda b: (b, 0, 0)),
                  pl.BlockSpec((1, T, LANES), lambda b: (b, 0, 1)),
                  full(pek), full(pev), full(wk), full(wv), full(w2k), full(w2v)],
        out_specs=[pl.BlockSpec((1, nch, LANES), lambda b: (b, 0, 0)), pl.BlockSpec((1, LANES, nch), lambda b: (b, 0, 0))],
        out_shape=[jax.ShapeDtypeStruct((B, nch, LANES), f32), jax.ShapeDtypeStruct((B, LANES, nch), f32)],
        compiler_params=_cparams("parallel"),
        name="cmp_prompt",
    )(rows, rows, pek, pev, wk, wv, w2k, w2v)


def _padded_queries(q_all, kvh, tq):
    del tq
    lo = kvh * NSA_GROUP
    return jnp.concatenate([q_all[:, (lo + g) * LANES:(lo + g + 1) * LANES] for g in range(NSA_GROUP)], axis=0)


def _cmp_probs(qm, kc, pos_col, kvh, tq):
    ncp = kc.shape[0]
    s = lax.dot_general(qm, kc.astype(bf16), NT, preferred_element_type=f32)
    n_idx = lax.broadcasted_iota(jnp.int32, (tq, ncp), 1)
    cdist = pos_col - (n_idx * CMP_STRIDE + (CMP_BLOCK - 1))
    cvis = cdist >= 0
    any_vis = pos_col >= (CMP_BLOCK - 1)
    cdf = cdist.astype(f32)
    p_sum = jnp.zeros((tq, ncp), f32)
    ps = []
    for g in range(NSA_GROUP):
        sg = jnp.where(cvis, s[g * tq:(g + 1) * tq] - SLOPES[kvh][g] * cdf, NEG_INF)
        m = jnp.max(sg, axis=-1, keepdims=True)
        e = jnp.exp(sg - m)
        p = e / jnp.sum(e, axis=-1, keepdims=True)
        p = jnp.where(any_vis, p, 0.0)
        p_sum = p_sum + p
        ps.append(p)
    return jnp.concatenate(ps, axis=0).astype(bf16), p_sum


def _select_blocks(p_sum, pos_col, ns, nsp, tq):
    ncp = p_sum.shape[1]
    ci = lax.broadcasted_iota(jnp.int32, (ncp, nsp), 0) * CMP_STRIDE
    cj = lax.broadcasted_iota(jnp.int32, (ncp, nsp), 1)
    c2s = ((ci < (cj + 1) * SLC_BLOCK) & (ci + CMP_BLOCK > cj * SLC_BLOCK)).astype(f32).astype(bf16)
    imp = _dot_exact_rhs(p_sum, c2s)
    blk = lax.broadcasted_iota(jnp.int32, (tq, nsp), 1)
    cur = pos_col // SLC_BLOCK
    forced = (blk == 0) | (blk == cur) | (blk == cur - 1)
    imp = jnp.where(forced, FORCED, jnp.where(blk <= cur, imp, NEG_INF))
    rank = jnp.zeros((tq, nsp), jnp.int32)
    for jp in range(ns):
        col = imp[:, jp:jp + 1]
        beats = (col > imp) | ((col == imp) & (blk > jp))
        rank = rank + beats.astype(jnp.int32)
    return ((rank < SLC_TOPN) & (blk <= cur)).astype(f32)


def _combine(gates, kvh, tq, o_cmp, o_slc, o_win):
    lo = kvh * HEAD_DIM
    outs = []
    for g in range(NSA_GROUP):
        h = kvh * NSA_GROUP + g
        r = slice(g * tq, (g + 1) * tq)
        o = (gates[:, 3 * h:3 * h + 1] * o_cmp[r] + gates[:, 3 * h + 1:3 * h + 2] * o_slc[r]
             + gates[:, 3 * h + 2:3 * h + 3] * o_win[r])
        outs.append(o[:, lo:lo + HEAD_DIM])
    return outs


LOG2E = 1.4426950408889634
SLOPES2 = [[v * LOG2E for v in row] for row in SLOPES]

POS_SPLIT = 8
F_SEL0 = 16
MASK_BIG = 1e30


def position_features(pos, blk, ns):
    n = pos.shape[0]
    lane = np.arange(HEAD_DIM)[None, :]
    hi = ((pos // POS_SPLIT) * POS_SPLIT).astype(np.float32)[:, None]
    lo = (pos % POS_SPLIT).astype(np.float32)[:, None]
    f = np.where(lane < 3, hi, np.where(lane < 6, lo, np.where(lane < 9, 1.0, 0.0)))
    if ns:
        f = np.where((lane >= F_SEL0) & (lane < F_SEL0 + ns), (blk[:, None] == lane - F_SEL0).astype(np.float32), f)
    return np.broadcast_to(f, (n, HEAD_DIM)).astype(np.float32)


def feature_tables(T, nch):
    pos = np.arange(T)
    kf = position_features(pos, pos // SLC_BLOCK, T // SLC_BLOCK)
    cpos = np.arange(nch) * CMP_STRIDE + (CMP_BLOCK - 1)
    cf = position_features(cpos, cpos, 0)
    z = np.zeros_like
    two = lambda a: jnp.asarray(np.stack([np.concatenate([z(a), a], axis=1),
                                          np.concatenate([a, z(a)], axis=1)]).astype(bf16))
    return two(kf), two(cf)


def _query_features(pos_row, selT, kvh, tq):
    posf = pos_row.astype(f32)
    r16 = lax.broadcasted_iota(jnp.int32, (16, tq), 0)
    zeros16 = jnp.zeros((16, tq), f32)
    selbig = (selT - 1.0) * MASK_BIG
    ns = selT.shape[0]
    pad = jnp.zeros((HEAD_DIM - F_SEL0 - ns, tq), f32)
    plain, full = [], []
    for g in range(NSA_GROUP):
        s = jnp.full((1, tq), SLOPES2[kvh][g], f32)
        s_parts = _split3(s)
        a_parts = _split3(s * posf)
        rows = [p.astype(f32) for p in s_parts] * 2 + [-p.astype(f32) for p in a_parts]
        blk0 = zeros16
        for i, rv in enumerate(rows):
            blk0 = jnp.where(r16 == i, rv, blk0)
        plain.append(jnp.concatenate([blk0, jnp.zeros((HEAD_DIM - 16, tq), f32)], axis=0).astype(bf16))
        full.append(jnp.concatenate([blk0, selbig, pad], axis=0).astype(bf16))
    return plain, full


def _aug_queries(qT, feats, kvh, tq):
    out = []
    for g in range(NSA_GROUP):
        h = kvh * NSA_GROUP + g
        qh = qT[h * HEAD_DIM:(h + 1) * HEAD_DIM, :]
        out.append(jnp.concatenate([qh, feats[g]] if kvh == 0 else [feats[g], qh], axis=0))
    return out


def _aug_keys(k_tile, feat_tile, kvh):
    lane = lax.broadcasted_iota(jnp.int32, k_tile.shape, 1)
    own = (lane < HEAD_DIM) if kvh == 0 else (lane >= HEAD_DIM)
    return jnp.where(own, k_tile, feat_tile)


def _with_ones(vT, kvh):
    half = vT.shape[0] // 2
    ones = jnp.ones((half, vT.shape[1]), vT.dtype)
    return jnp.concatenate([vT[:half], ones] if kvh == 0 else [ones, vT[half:]], axis=0)


def _cmp_branch_t(q_aug, kc_aug, vcT, pos_row, tq):
    ncp = vcT.shape[1]
    n_idx = lax.broadcasted_iota(jnp.int32, (ncp, tq), 0)
    cvis = pos_row >= (n_idx * CMP_STRIDE + (CMP_BLOCK - 1))
    any_vis = pos_row >= (CMP_BLOCK - 1)
    s = [jnp.where(cvis, jnp.dot(k, q, preferred_element_type=f32), NEG_INF) for k, q in zip(kc_aug, q_aug)]
    e = [jnp.exp2(x - jnp.max(x, axis=0, keepdims=True)) for x in s]
    p = [jnp.where(any_vis, x / jnp.sum(x, axis=0, keepdims=True), 0.0) for x in e]
    outs = [jnp.dot(vcT, x.astype(bf16), preferred_element_type=f32) for x in p]
    return outs, p


def _select_blocks_t(p_sumT, pos_row, ns, tq):
    ncp = p_sumT.shape[0]
    cj = lax.broadcasted_iota(jnp.int32, (ns, ncp), 0)
    ci = lax.broadcasted_iota(jnp.int32, (ns, ncp), 1) * CMP_STRIDE
    c2sT = ((ci < (cj + 1) * SLC_BLOCK) & (ci + CMP_BLOCK > cj * SLC_BLOCK)).astype(f32).astype(bf16)
    hi, mid, lo = _split3(p_sumT)
    d = lambda x: jnp.dot(c2sT, x, preferred_element_type=f32)
    imp = d(hi) + d(mid) + d(lo)
    blk = lax.broadcasted_iota(jnp.int32, (ns, tq), 0)
    cur = pos_row // SLC_BLOCK
    forced = (blk == 0) | (blk == cur) | (blk == cur - 1)
    imp = jnp.where(forced, FORCED, jnp.where(blk <= cur, imp, NEG_INF))
    rank = jnp.zeros((ns, tq), jnp.int32)
    for jp in range(ns):
        row = imp[jp:jp + 1, :]
        beats = (row > imp) | ((row == imp) & (blk > jp))
        rank = rank + beats.astype(jnp.int32)
    return ((rank < SLC_TOPN) & (blk <= cur)).astype(f32)


def _nsa_prompt_kernel(qT_ref, gT_ref, sk_ref, svT_ref, wk_ref, wvT_ref, kc_ref, vcT_ref, kf_ref, cf_ref, o_ref,
                       m_s, acc_s, *, tq, kt, ns):
    qi = pl.program_id(1)
    p0 = qi * tq
    pos_row = p0 + lax.broadcasted_iota(jnp.int32, (1, tq), 1)
    gates = _sigmoid(gT_ref[...])
    qT = qT_ref[...]
    kc = kc_ref[0].astype(bf16)
    vcT = vcT_ref[0].astype(bf16)
    half = LANES // 2
    KV = range(NSA_KV_HEADS)

    zsel = jnp.zeros((ns, tq), f32)
    q_pos = [_aug_queries(qT, _query_features(pos_row, zsel, kvh, tq)[0], kvh, tq) for kvh in KV]
    kc_aug = [_aug_keys(kc, cf_ref[kvh], kvh) for kvh in KV]
    oc, pc = _cmp_branch_t([q for kvh in KV for q in q_pos[kvh]], [kc_aug[kvh] for kvh in KV for _ in range(NSA_GROUP)],
                           vcT, pos_row, tq)
    o_cmp = [oc[kvh * NSA_GROUP:(kvh + 1) * NSA_GROUP] for kvh in KV]
    q_sel = []
    for kvh in KV:
        p_sum = pc[kvh * NSA_GROUP]
        for x in pc[kvh * NSA_GROUP + 1:(kvh + 1) * NSA_GROUP]:
            p_sum = p_sum + x
        selT = _select_blocks_t(p_sum, pos_row, ns, tq)
        q_sel.append(_aug_queries(qT, _query_features(pos_row, selT, kvh, tq)[1], kvh, tq))

    m_s[...] = jnp.full(m_s.shape, NEG_INF, f32)
    acc_s[...] = jnp.zeros(acc_s.shape, f32)
    rel = pos_row - lax.broadcasted_iota(jnp.int32, (kt, tq), 0)

    def slc_tile(c, causal):
        k0 = pl.multiple_of(c * kt, kt)
        k_tile = sk_ref[pl.ds(k0, kt), :]
        v_tile = svT_ref[:, pl.ds(k0, kt)]
        ok = (rel - k0) >= 0
        for kvh in KV:
            ka = _aug_keys(k_tile, kf_ref[kvh, pl.ds(k0, kt), :], kvh)
            va = _with_ones(v_tile, kvh)
            for g in range(NSA_GROUP):
                cs = slice(g * tq, (g + 1) * tq)
                sg = jnp.dot(ka, q_sel[kvh][g], preferred_element_type=f32)
                if causal:
                    sg = jnp.where(ok, sg, NEG_INF)
                m_old = m_s[kvh, :, cs]
                m_new = jnp.maximum(m_old, jnp.max(sg, axis=0, keepdims=True))
                alpha = jnp.exp2(m_old - m_new)
                p = jnp.exp2(sg - m_new).astype(bf16)
                acc_s[kvh, :, cs] = alpha * acc_s[kvh, :, cs] + jnp.dot(va, p, preferred_element_type=f32)
                m_s[kvh, :, cs] = m_new

    n_kt = (p0 + tq + kt - 1) // kt

    def slc_body(c, _):
        slc_tile(c, False)
        return 0

    lax.fori_loop(0, n_kt - 1, slc_body, 0)
    slc_tile(n_kt - 1, True)

    wlen = WINDOW + tq
    w0 = pl.multiple_of(jnp.maximum(p0 - WINDOW, 0), tq)
    k_win = wk_ref[pl.ds(w0, wlen), :]
    v_win = wvT_ref[:, pl.ds(w0, wlen)]
    dist = pos_row - (w0 + lax.broadcasted_iota(jnp.int32, (wlen, tq), 0))
    valid = (dist >= 0) & (dist < WINDOW)
    ch = [(kvh, g) for kvh in KV for g in range(NSA_GROUP)]
    ka = [_aug_keys(k_win, kf_ref[kvh, pl.ds(w0, wlen), :], kvh) for kvh in KV]
    va = [_with_ones(v_win, kvh) for kvh in KV]
    sg = [jnp.where(valid, jnp.dot(ka[kvh], q_pos[kvh][g], preferred_element_type=f32), NEG_INF) for kvh, g in ch]
    p = [jnp.exp2(x - jnp.max(x, axis=0, keepdims=True)).astype(bf16) for x in sg]
    o_win = [jnp.dot(va[kvh], p[i], preferred_element_type=f32) for i, (kvh, g) in enumerate(ch)]
    heads = []
    for i, (kvh, g) in enumerate(ch):
        h = kvh * NSA_GROUP + g
        d0, l0 = (0, half) if kvh == 0 else (half, 0)
        o_slc = acc_s[kvh, :, g * tq:(g + 1) * tq]
        heads.append(gates[3 * h:3 * h + 1] * o_cmp[kvh][g][d0:d0 + half]
                     + gates[3 * h + 1:3 * h + 2] * (o_slc[d0:d0 + half] / o_slc[l0:l0 + 1])
                     + gates[3 * h + 2:3 * h + 3] * (o_win[i][d0:d0 + half] / o_win[i][l0:l0 + 1]))
    o_ref[0] = jnp.concatenate(heads, axis=0).T.astype(o_ref.dtype)


def nsa_prompt(qT, gT, sk, svT, wk, wvT, kc, vcT, B, tq=128, kt=512):
    N = sk.shape[0]
    T = N // B
    ns = T // SLC_BLOCK
    nq = T // tq
    ncp = kc.shape[1]
    assert T % kt == 0 and T % tq == 0 and kt % tq == 0 and T >= WINDOW + tq and T <= POS_SPLIT * 256
    assert F_SEL0 + ns <= HEAD_DIM
    kf, cf = feature_tables(T, ncp)
    keys = pl.BlockSpec((T, LANES), lambda b, i: (b, 0))
    vals = pl.BlockSpec((LANES, T), lambda b, i: (0, b))
    return pl.pallas_call(
        functools.partial(_nsa_prompt_kernel, tq=tq, kt=kt, ns=ns),
        grid=(B, nq),
        in_specs=[pl.BlockSpec((NSA_WIDTH, tq), lambda b, i: (0, b * nq + i)),
                  pl.BlockSpec((3 * NSA_HEADS, tq), lambda b, i: (0, b * nq + i)),
                  keys, vals, keys, vals,
                  pl.BlockSpec((1, ncp, LANES), lambda b, i: (b, 0, 0)),
                  pl.BlockSpec((1, LANES, ncp), lambda b, i: (b, 0, 0)),
                  pl.BlockSpec((NSA_KV_HEADS, T, LANES), lambda b, i: (0, 0, 0)),
                  pl.BlockSpec((NSA_KV_HEADS, ncp, LANES), lambda b, i: (0, 0, 0))],
        out_specs=pl.BlockSpec((1, tq, NSA_WIDTH), lambda b, i: (b, i, 0)),
        out_shape=jax.ShapeDtypeStruct((B, T, NSA_WIDTH), bf16),
        scratch_shapes=[pltpu.VMEM((NSA_KV_HEADS, 1, NSA_GROUP * tq), f32),
                        pltpu.VMEM((NSA_KV_HEADS, LANES, NSA_GROUP * tq), f32)],
        compiler_params=_cparams("parallel", "arbitrary"),
        name="nsa_prompt",
    )(qT, gT, sk, svT, wk, wvT, kc, vcT, kf, cf)


SF_SPLIT = 64
SF_BLK0 = 16
SF_ROWS = 256


def sample_key_features(past_len, nk, tq, ns):
    sf_new0 = SF_BLK0 + -(-ns // 8) * 8
    assert sf_new0 + tq + 1 <= SF_ROWS and nk <= SF_SPLIT * 256
    pos = np.arange(nk)[None, :]
    row = np.arange(SF_ROWS)[:, None]
    hi = ((pos // SF_SPLIT) * SF_SPLIT).astype(np.float32)
    lo = (pos % SF_SPLIT).astype(np.float32)
    f = np.where(row < 3, hi, np.where(row < 6, lo, np.where(row < 9, 1.0, 0.0)))
    blk = (row >= SF_BLK0) & (row < SF_BLK0 + ns) & (pos // SLC_BLOCK == row - SF_BLK0)
    new = (row >= sf_new0) & (row < sf_new0 + tq) & (pos - past_len == row - sf_new0)
    pad = (row == sf_new0 + tq) & (pos >= past_len + tq)
    return jnp.asarray(np.where(blk | new | pad, 1.0, f).astype(bf16)), sf_new0


def _sample_query_features(sel, pos_col, kvh, tq, ns, sf_new0):
    lane = lax.broadcasted_iota(jnp.int32, (tq, SF_ROWS), 1)
    t_idx = lax.broadcasted_iota(jnp.int32, (tq, SF_ROWS), 0)
    selbig = (pltpu.roll(sel[:, 0:SF_ROWS], SF_BLK0, 1) - 1.0) * MASK_BIG
    base = jnp.where((lane >= SF_BLK0) & (lane < SF_BLK0 + ns), selbig, 0.0)
    base = jnp.where((lane >= sf_new0) & (lane < sf_new0 + tq) & (lane - sf_new0 > t_idx), -MASK_BIG, base)
    base = jnp.where(lane == sf_new0 + tq, -MASK_BIG, base)
    posf = pos_col.astype(f32)
    out = []
    for g in range(NSA_GROUP):
        s = jnp.full((tq, 1), SLOPES[kvh][g], f32)
        cols = [p.astype(f32) for p in _split3(s)] * 2 + [-p.astype(f32) for p in _split3(s * posf)]
        qf = base
        for i, cv in enumerate(cols):
            qf = jnp.where(lane == i, cv, qf)
        out.append(qf)
    return jnp.concatenate(out, axis=0).astype(bf16)


def _new_rows_t(rows, width):
    tq = rows.shape[0]
    sq = jnp.concatenate([rows, jnp.zeros((LANES - tq, LANES), f32)], axis=0).T
    if width == LANES:
        return sq
    return jnp.concatenate([sq, jnp.zeros((LANES, width - LANES), f32)], axis=1)


def _compress_staged(xs, t, nch, pe_ref, w_ref, w2_ref):
    lhs = jnp.concatenate([jnp.concatenate([xs[2 * t + h, j] for j in range(CMP_STRIDE // 2)], axis=1)
                           for h in range(NSA_KV_HEADS)], axis=0)
    acc = jnp.dot(lhs, w_ref[...], preferred_element_type=f32)
    bc = lambda v: jnp.broadcast_to(v, (8, v.shape[1])).astype(bf16)
    c = (jnp.dot(bc(pe_ref[0:1, :]), w_ref[:, 0:CMP_HIDDEN], preferred_element_type=f32)
         + jnp.dot(bc(pe_ref[1:2, :]), w_ref[:, CMP_HIDDEN:], preferred_element_type=f32))[0:1, :]
    outs = []
    for h in range(NSA_KV_HEADS):
        r = slice(h * nch, (h + 1) * nch)
        hid = acc[r, 0:CMP_HIDDEN] + pltpu.roll(acc[r, CMP_HIDDEN:], nch - 1, 0) + c
        outs.append(jnp.dot(_silu(hid).astype(bf16), w2_ref[...], preferred_element_type=f32))
    return jnp.concatenate(outs, axis=1)


def _nsa_sample_kernel(pt_ref, *refs, pps, n_steps, tq, past_len, ns, nsp, sf_new0):
    page_refs = refs[:pps]
    (q_ref, g_ref, rn_ref, wn_ref, cw_ref, ft_ref, perm_ref, pek_ref, pev_ref, wk_ref, wv_ref, w2k_ref, w2v_ref,
     o_ref, xs_s, kT_s, vT_s) = refs[pps:]
    del pt_ref
    step = pl.program_id(1)
    cpp = PAGE_SIZE // CMP_STRIDE
    staged = []
    for i in range(pps):
        r0 = pl.multiple_of((step * pps + i) * PAGE_SIZE, PAGE_SIZE)
        pg = page_refs[i]
        slabs = []
        for th in range(4):
            x = pg[0, th]
            slabs += [x, pltpu.roll(x, PAGE_SIZE - 1, 1)]
        a = jnp.concatenate(slabs, axis=0).astype(bf16)
        staged.append(lax.dot_general(perm_ref[...], a, NT, preferred_element_type=f32))
        kT_s[:, pl.ds(r0, PAGE_SIZE)] = pg[0, 4:6].reshape(LANES, PAGE_SIZE).astype(bf16)
        vT_s[:, pl.ds(r0, PAGE_SIZE)] = pg[0, 6:8].reshape(LANES, PAGE_SIZE).astype(bf16)
    c0 = pl.multiple_of(step * (pps * cpp), pps * cpp)
    for j in range(CMP_STRIDE // 2):
        rows = slice(j * cpp, (j + 1) * cpp)
        both = jnp.concatenate([t[rows] for t in staged], axis=0).astype(bf16)
        for th in range(4):
            xs_s[th, j, pl.ds(c0, pps * cpp), :] = both[:, th * LANES:(th + 1) * LANES]

    @pl.when(step == n_steps - 1)
    def _():
        nk = kT_s.shape[1]
        rn = rn_ref[0]
        kT_s[:, past_len:] = _new_rows_t(rn[:, 2 * LANES:3 * LANES], nk - past_len).astype(bf16)
        vT_s[:, past_len:] = _new_rows_t(rn[:, 3 * LANES:4 * LANES], nk - past_len).astype(bf16)
        nwin = cw_ref.shape[2]
        wn = wn_ref[0]
        wkT = jnp.concatenate([cw_ref[0, 0:LANES, :], _new_rows_t(wn[:, 0:LANES], LANES)], axis=1).astype(bf16)
        wvT = jnp.concatenate([cw_ref[0, LANES:, :], _new_rows_t(wn[:, LANES:], LANES)], axis=1).astype(bf16)
        wlen = nwin + LANES

        nch = past_len // CMP_STRIDE
        kc = _compress_staged(xs_s, 0, nch, pek_ref, wk_ref, w2k_ref)

        pos_col = past_len + lax.broadcasted_iota(jnp.int32, (tq, 1), 0)
        gates = _sigmoid(g_ref[0])
        q_all = q_ref[0]
        wpos = (past_len - nwin) + lax.broadcasted_iota(jnp.int32, (tq, wlen), 1)
        wdist = pos_col - wpos
        wvalid = (wdist >= 0) & (wdist < WINDOW)
        wdistf = wdist.astype(f32)
        qms = [_padded_queries(q_all, kvh, tq) for kvh in range(NSA_KV_HEADS)]
        probs = [_cmp_probs(qms[kvh], kc, pos_col, kvh, tq) for kvh in range(NSA_KV_HEADS)]
        vc = _compress_staged(xs_s, 1, nch, pev_ref, wv_ref, w2v_ref).astype(bf16)
        qfs = [_sample_query_features(_select_blocks(probs[kvh][1], pos_col, ns, nsp, tq), pos_col, kvh, tq, ns, sf_new0)
               for kvh in range(NSA_KV_HEADS)]
        o_cmps = [jnp.dot(probs[kvh][0], vc, preferred_element_type=f32) for kvh in range(NSA_KV_HEADS)]
        qm2 = jnp.concatenate(qms, axis=0)
        s = (jnp.dot(qm2, kT_s[...], preferred_element_type=f32)
             + jnp.dot(jnp.concatenate(qfs, axis=0), ft_ref[...], preferred_element_type=f32))
        e = jnp.exp(s - jnp.max(s, axis=-1, keepdims=True))
        o_slc2 = (lax.dot_general(e.astype(bf16), vT_s[...], NT, preferred_element_type=f32)
                  / jnp.sum(e, axis=-1, keepdims=True))
        sw2 = jnp.dot(qm2, wkT, preferred_element_type=f32)
        pw = []
        for kvh in range(NSA_KV_HEADS):
            for g in range(NSA_GROUP):
                r = slice((kvh * NSA_GROUP + g) * tq, (kvh * NSA_GROUP + g + 1) * tq)
                sg = jnp.where(wvalid, sw2[r] - SLOPES[kvh][g] * wdistf, NEG_INF)
                ew = jnp.where(wvalid, jnp.exp(sg - jnp.max(sg, axis=-1, keepdims=True)), 0.0)
                pw.append(ew / jnp.sum(ew, axis=-1, keepdims=True))
        o_win2 = lax.dot_general(jnp.concatenate(pw, axis=0).astype(bf16), wvT, NT, preferred_element_type=f32)
        outs = []
        for kvh in range(NSA_KV_HEADS):
            r = slice(kvh * NSA_GROUP * tq, (kvh + 1) * NSA_GROUP * tq)
            outs.extend(_combine(gates, kvh, tq, o_cmps[kvh], o_slc2[r], o_win2[r]))
        o_ref[0] = jnp.concatenate(outs, axis=1).astype(o_ref.dtype)


def nsa_sample(page_table, cacheT, q, gates, rows_new, win_new, cache_winT, cw, pps=32):
    B, tq, _ = q.shape
    n_pages = page_table.shape[1]
    past_len = n_pages * PAGE_SIZE
    assert n_pages % pps == 0 and past_len % SLC_BLOCK == 0
    n_steps = n_pages // pps
    nk = past_len + LANES
    ns = past_len // SLC_BLOCK + 1
    nsp = -(-ns // LANES) * LANES
    nch = past_len // CMP_STRIDE
    nwin = cache_winT.shape[2]
    pek, wk, w2k, pev, wv, w2v = cw
    ft, sf_new0 = sample_key_features(past_len, nk, tq, ns)
    cpp = PAGE_SIZE // CMP_STRIDE
    ro = np.arange(PAGE_SIZE)
    ro2 = np.arange(PAGE_SIZE // 2)
    perm = jnp.asarray((ro[None, :] == (ro2[:, None] % cpp) * CMP_STRIDE + 2 * (ro2[:, None] // cpp))
                       .astype(bf16))
    per_b = lambda n: pl.BlockSpec((1, tq, n), lambda b, s, pt: (b, 0, 0))
    full = lambda a: pl.BlockSpec(a.shape, lambda b, s, pt: (0,) * a.ndim)
    page_specs = [pl.BlockSpec((1, 8, HEAD_DIM, PAGE_SIZE),
                               functools.partial(lambda b, s, pt, i: (pt[b, s * pps + i], 0, 0, 0), i=i))
                  for i in range(pps)]
    grid_spec = pltpu.PrefetchScalarGridSpec(
        num_scalar_prefetch=1,
        grid=(B, n_steps),
        in_specs=page_specs + [per_b(NSA_HEADS * LANES), per_b(3 * NSA_HEADS), per_b(4 * LANES), per_b(2 * LANES),
                               pl.BlockSpec((1, 2 * LANES, nwin), lambda b, s, pt: (b, 0, 0)),
                               full(ft), full(perm), full(pek), full(pev), full(wk), full(wv), full(w2k), full(w2v)],
        out_specs=per_b(NSA_WIDTH),
        scratch_shapes=[pltpu.VMEM((4, CMP_STRIDE // 2, nch, LANES), bf16),
                        pltpu.VMEM((LANES, nk), bf16), pltpu.VMEM((LANES, nk), bf16)])
    return pl.pallas_call(
        functools.partial(_nsa_sample_kernel, pps=pps, n_steps=n_steps, tq=tq, past_len=past_len, ns=ns, nsp=nsp,
                          sf_new0=sf_new0),
        grid_spec=grid_spec,
        out_shape=jax.ShapeDtypeStruct((B, tq, NSA_WIDTH), bf16),
        compiler_params=_cparams("parallel", "arbitrary"),
        name="nsa_sample",
    )(page_table, *([cacheT] * pps), q, gates, rows_new, win_new, cache_winT, ft, perm, pek, pev, wk, wv, w2k, w2v)


def _gdn_prep_kernel(x_ref, halo_ref, cb_ref, cw_ref, ba_ref, al_ref, dtb_ref,
                     u_ref, w_ref, qg_ref, kg_ref, qk_ref, eg_ref, xs_s, *, tb, C, t_valid, nb):
    i = pl.program_id(1)
    HK = GDN_HEADS * GDN_DK
    R = min(tb, LANES)
    cpb = R // C

    @pl.when(i == 0)
    def _():
        xs_s[:, 0:8, :] = cb_ref[...]

    @pl.when(i > 0)
    def _():
        xs_s[:, 0:8, :] = halo_ref[...]

    xs_s[:, 8:, :] = x_ref[...]
    ri = lax.broadcasted_iota(jnp.int32, (R, R), 0)
    ci = lax.broadcasted_iota(jnp.int32, (R, R), 1)
    same = (ri // C) == (ci // C)
    lower = same & (ri >= ci)
    strict = same & (ri > ci)
    eye = (ri == ci).astype(f32)
    rr = lax.broadcasted_iota(jnp.int32, (R, LANES), 0)
    row_t = i * tb + lax.broadcasted_iota(jnp.int32, (tb, 1), 0)
    live = row_t < t_valid

    def conv_act(bb, c0):
        acc = cw_ref[GDN_CONV - 1:GDN_CONV, c0:c0 + LANES] * xs_s[bb, pl.ds(8, tb), c0:c0 + LANES]
        for k in range(1, GDN_CONV):
            acc = acc + cw_ref[GDN_CONV - 1 - k:GDN_CONV - k, c0:c0 + LANES] * xs_s[bb, pl.ds(8 - k, tb), c0:c0 + LANES]
        return _silu(acc)

    chains = []
    for bb, h in [(bb, h) for bb in range(nb) for h in range(GDN_HEADS)]:
        ba = ba_ref[bb]
        q = conv_act(bb, h * GDN_DK)
        k = conv_act(bb, HK + h * GDN_DK)
        v = conv_act(bb, 2 * HK + h * GDN_DV)
        q = q * lax.rsqrt(jnp.sum(q * q, axis=-1, keepdims=True) + L2_EPS) * (GDN_DK ** -0.5)
        k = k * lax.rsqrt(jnp.sum(k * k, axis=-1, keepdims=True) + L2_EPS)
        beta = _sigmoid(ba[:, h:h + 1])
        ar = ba[:, GDN_HEADS + h:GDN_HEADS + h + 1] + dtb_ref[:, h:h + 1]
        softplus = jnp.maximum(ar, 0.0) + jnp.log(1.0 + jnp.exp(-jnp.abs(ar)))
        gcol = -jnp.exp(al_ref[:, h:h + 1]) * softplus
        if t_valid < 10 ** 9:
            beta = jnp.where(live, beta, 0.0)
            gcol = jnp.where(live, gcol, 0.0)
            k = jnp.where(live, k, 0.0)
            v = jnp.where(live, v, 0.0)
        for blk in range(tb // R):
            r = slice(blk * R, (blk + 1) * R)
            qc, kc, vc, bc = q[r], k[r], v[r], beta[r]
            gc = jnp.broadcast_to(gcol[r], (R, LANES))
            sh = 1
            while sh < C:
                gc = gc + jnp.where((rr % C) >= sh, pltpu.roll(gc, sh, 0), 0.0)
                sh *= 2
            gct = gc.T
            decay = jnp.exp(jnp.where(lower, gc[:, 0:R] - gct[0:R, :], NEG_INF))
            kb = kc * bc
            a = jnp.where(strict, _dot_hp(kb, kc, NT) * decay, 0.0)
            qk = jnp.where(lower, lax.dot_general(qc.astype(bf16), kc.astype(bf16), NT,
                                                  preferred_element_type=f32) * decay, 0.0)
            gl = gc[R - 1:R, :]
            for j in range(cpb - 2, -1, -1):
                gl = jnp.where(rr < (j + 1) * C, gc[(j + 1) * C - 1:(j + 1) * C, :], gl)
            chains.append(dict(bb=bb, h=h, r=r, blk=blk, qc=qc, kc=kc, gc=gc, gl=gl, qk=qk, n=-a, t=eye - a,
                               rhs=jnp.concatenate([vc * bc, kb * jnp.exp(gc)], axis=1)))

    for ch in chains:
        ch['n'] = _split2(ch['n'])
    sh = 2
    while sh < C:
        for ch in chains:
            ch['n'] = _split2(_dot_parts(ch['n'], ch['n']))
        for ch in chains:
            ch['t'] = ch['t'] + _dot_parts(_split2(ch['t']), ch['n'])
        sh *= 2

    for ch in chains:
        bb, h, r, gc, gl = ch['bb'], ch['h'], ch['r'], ch['gc'], ch['gl']
        sol = ch['rhs'] + _dot_hp(ch['t'] - eye, ch['rhs'])
        cs = slice(h * GDN_DV, (h + 1) * GDN_DV)
        u_ref[bb, r, cs] = sol[:, 0:GDN_DV]
        w_ref[bb, r, cs] = sol[:, GDN_DV:].astype(w_ref.dtype)
        qg_ref[bb, r, cs] = (ch['qc'] * jnp.exp(gc)).astype(qg_ref.dtype)
        kg_ref[bb, r, cs] = (ch['kc'] * jnp.exp(gl - gc)).astype(kg_ref.dtype)
        qk = ch['qk'][:, 0:C]
        for j in range(1, cpb):
            qk = jnp.where(ri[:, 0:C] >= j * C, ch['qk'][:, j * C:(j + 1) * C], qk)
        qk_ref[bb, r, h * C:(h + 1) * C] = qk.astype(qk_ref.dtype)
        for j in range(cpb):
            eg_ref[bb, ch['blk'] * cpb + j, h:h + 1, :] = jnp.exp(gc[(j + 1) * C - 1:(j + 1) * C, :])
    eg_ref[:, :, GDN_HEADS:, :] = jnp.zeros((nb, tb // C, 8 - GDN_HEADS, LANES), f32)


def gdn_prep(qkv, conv_buf8, conv_w, ba, a_log, dt_bias, C, tb, t_valid, nb=1):
    B, T, CH = qkv.shape
    nb = math.gcd(nb, B)
    hb = tb // 8
    per = lambda n, dt: (pl.BlockSpec((nb, tb, n), lambda b, i: (b, i, 0)), jax.ShapeDtypeStruct((B, T, n), dt))
    outs = [per(GDN_WIDTH, f32), per(GDN_WIDTH, bf16), per(GDN_WIDTH, bf16), per(GDN_WIDTH, bf16),
            per(GDN_HEADS * C, bf16),
            (pl.BlockSpec((nb, tb // C, 8, LANES), lambda b, i: (b, i, 0, 0)),
             jax.ShapeDtypeStruct((B, T // C, 8, LANES), f32))]
    return pl.pallas_call(
        functools.partial(_gdn_prep_kernel, tb=tb, C=C, t_valid=t_valid, nb=nb),
        grid=(B // nb, T // tb),
        in_specs=[pl.BlockSpec((nb, tb, CH), lambda b, i: (b, i, 0)),
                  pl.BlockSpec((nb, 8, CH), lambda b, i: (b, jnp.maximum(i * hb - 1, 0), 0)),
                  pl.BlockSpec((nb, 8, CH), lambda b, i: (b, 0, 0)),
                  pl.BlockSpec((GDN_CONV, CH), lambda b, i: (0, 0)),
                  pl.BlockSpec((nb, tb, 2 * GDN_HEADS), lambda b, i: (b, i, 0)),
                  pl.BlockSpec((1, GDN_HEADS), lambda b, i: (0, 0)),
                  pl.BlockSpec((1, GDN_HEADS), lambda b, i: (0, 0))],
        out_specs=[o[0] for o in outs],
        out_shape=[o[1] for o in outs],
        scratch_shapes=[pltpu.VMEM((nb, tb + 8, CH), f32)],
        compiler_params=_cparams("parallel", "arbitrary"),
        name="gdn_prep",
    )(qkv, qkv, conv_buf8, conv_w, ba, a_log.reshape(1, GDN_HEADS), dt_bias.reshape(1, GDN_HEADS))


def _gdn_scan_kernel(u_ref, w_ref, qg_ref, kg_ref, qk_ref, eg_ref, z_ref, ng_ref, s0_ref,
                     o_ref, sT_ref, s_s, *, C, nb):
    c = pl.program_id(1)

    @pl.when(c == 0)
    def _():
        s_s[...] = s0_ref[...]

    ch = [(b, h) for b in range(nb) for h in range(GDN_HEADS)]
    cs = [slice(h * GDN_DV, (h + 1) * GDN_DV) for h in range(GDN_HEADS)]
    S = [s_s[b, h] for b, h in ch]
    Sb = [x.astype(bf16) for x in S]
    v_new = [u_ref[b, :, cs[h]] - jnp.dot(w_ref[b, :, cs[h]], Sb[i], preferred_element_type=f32)
             for i, (b, h) in enumerate(ch)]
    vb = [x.astype(bf16) for x in v_new]
    for i, (b, h) in enumerate(ch):
        s_s[b, h] = S[i] * eg_ref[b, 0, h:h + 1, :] + lax.dot_general(kg_ref[b, :, cs[h]], vb[i], TN,
                                                                       preferred_element_type=f32)
    o = [jnp.dot(qg_ref[b, :, cs[h]], Sb[i], preferred_element_type=f32)
         + jnp.dot(qk_ref[b, :, h * C:(h + 1) * C], vb[i], preferred_element_type=f32) for i, (b, h) in enumerate(ch)]
    outs = [_rms(o[i], ng_ref[...]) * _silu(z_ref[b, :, cs[h]]) for i, (b, h) in enumerate(ch)]
    for b in range(nb):
        o_ref[b] = jnp.concatenate(outs[b * GDN_HEADS:(b + 1) * GDN_HEADS], axis=1).astype(o_ref.dtype)

    @pl.when(c == pl.num_programs(1) - 1)
    def _():
        sT_ref[...] = s_s[...]


def gdn_scan(u, w, qg, kg, qk, eg, z, norm_g, s0, C, nb=8):
    B, T, _ = u.shape
    nb = math.gcd(nb, B)
    per = lambda n: pl.BlockSpec((nb, C, n), lambda b, c: (b, c, 0))
    st = pl.BlockSpec((nb, GDN_HEADS, GDN_DK, GDN_DV), lambda b, c: (b, 0, 0, 0))
    return pl.pallas_call(
        functools.partial(_gdn_scan_kernel, C=C, nb=nb),
        grid=(B // nb, T // C),
        in_specs=[per(GDN_WIDTH), per(GDN_WIDTH), per(GDN_WIDTH), per(GDN_WIDTH), per(GDN_HEADS * C),
                  pl.BlockSpec((nb, 1, 8, LANES), lambda b, c: (b, c, 0, 0)),
                  per(GDN_WIDTH), pl.BlockSpec((1, GDN_DV), lambda b, c: (0, 0)), st],
        out_specs=[per(GDN_WIDTH), st],
        out_shape=[jax.ShapeDtypeStruct((B, T, GDN_WIDTH), bf16),
                   jax.ShapeDtypeStruct((B, GDN_HEADS, GDN_DK, GDN_DV), f32)],
        scratch_shapes=[pltpu.VMEM((nb, GDN_HEADS, GDN_DK, GDN_DV), f32)],
        compiler_params=_cparams("parallel", "arbitrary"),
        name="gdn_scan",
    )(u, w, qg, kg, qk, eg, z, norm_g.reshape(1, GDN_DV), s0)


def _nsa_in_groups(w, feature_major):
    c1 = NSA_WIDTH
    c2 = c1 + 4 * NSA_KV_COLS
    c3 = c2 + 2 * NSA_KV_COLS
    c4 = c3 + 3 * NSA_HEADS
    wT = w.T.astype(bf16)
    xa_scale = XA_DIM ** -0.5 * LOG2E
    if feature_major:
        k0 = c1 + 2 * NSA_KV_COLS
        return [(wT[c1:c2], f32, 1.0, FEATURE_MAJOR_PER_BATCH),
                (wT[c1:k0], f32, 1.0, TOKEN_MAJOR),
                (wT[c2:c3], f32, 1.0, FEATURE_MAJOR_PER_BATCH),
                (wT[c4:], bf16, xa_scale, FEATURE_MAJOR),
                (wT[:c1], bf16, HEAD_DIM ** -0.5 * LOG2E, FEATURE_MAJOR),
                (wT[c3:c4], f32, 1.0, FEATURE_MAJOR),
                (wT[k0:k0 + NSA_KV_COLS], bf16, 1.0, TOKEN_MAJOR),
                (wT[k0 + NSA_KV_COLS:c2], bf16, 1.0, FEATURE_MAJOR),
                (wT[c2:c2 + NSA_KV_COLS], bf16, 1.0, TOKEN_MAJOR),
                (wT[c2 + NSA_KV_COLS:c3], bf16, 1.0, FEATURE_MAJOR)]
    wq = wT[:c1].reshape(NSA_KV_HEADS, NSA_GROUP, HEAD_DIM, -1)
    zq = jnp.zeros_like(wq[0])
    wq = jnp.concatenate([jnp.concatenate([wq[0], zq], axis=1), jnp.concatenate([zq, wq[1]], axis=1)], axis=0)
    wq = wq.reshape(NSA_HEADS * LANES, -1)
    return [(wT[c1:c2], f32, 1.0, TOKEN_MAJOR), (wT[c2:c3], f32, 1.0, TOKEN_MAJOR),
            (wT[c4:], bf16, xa_scale, TOKEN_MAJOR), (wq, bf16, HEAD_DIM ** -0.5, TOKEN_MAJOR),
            (wT[c3:c4], f32, 1.0, TOKEN_MAJOR)]


def _gdn_in_groups(w, feature_major):
    c1 = GDN_CONV_CH
    c2 = c1 + GDN_WIDTH
    c3 = c2 + 2 * GDN_HEADS
    wT = w.T.astype(bf16)
    xq = (wT[c3:], bf16, XA_DIM ** -0.5 * LOG2E, FEATURE_MAJOR if feature_major else TOKEN_MAJOR)
    return [(wT[:c1], f32, 1.0, TOKEN_MAJOR), (wT[c1:c2], f32, 1.0, TOKEN_MAJOR),
            (wT[c2:c3], f32, 1.0, TOKEN_MAJOR), xq]


def _trunk(x, mem_kv, nsa_past, gdn_state, p, t_valid):
    B, T, D = x.shape
    N = B * T
    xf = x.reshape(N, D)

    prompt = nsa_past is None
    if prompt:
        kv_rows, rows_cmp, win_rows, xq, qT, gT, sk, svT, wk, wvT = norm_proj(
            xf, p['norm_mix_g'][0], p['nsa_in_prompt'], tm=512, rows_per_batch=T)
        kc, vcT = cmp_prompt(rows_cmp.reshape(B, T, -1), p['cmp'])
        o_mix = nsa_prompt(qT, gT, sk, svT, wk, wvT, kc, vcT, B)
    else:
        rows, win, xq, q, gates = norm_proj(xf, p['norm_mix_g'][0], p['nsa_in_sample'])
        kv_rows = rows.reshape(B, T, -1)
        win_rows = win.reshape(B, T, -1)
        xq = xq.reshape(B, T, -1)
        page_table, cache, cache_win = nsa_past
        o_mix = nsa_sample(page_table, cache, q.reshape(B, T, -1), gates.reshape(B, T, -1), kv_rows, win_rows,
                           cache_win, p['cmp_dense'])
    o_mem = cross_attend(xq, mem_kv, 0, B)
    xf = post_block(xf, o_mix.reshape(N, -1), o_mem.reshape(N, -1), p['w_out'][0], p['norm_ffn_g'][0],
                    p['w_up'][0], p['w_down'][0], p['final_norm_g'], final=False)

    qkv, z, ba, xq = norm_proj(xf, p['norm_mix_g'][1], p['gdn_in_prompt' if prompt else 'gdn_in_sample'], tm=512)
    if not prompt:
        xq = xq.reshape(B, T, -1)
    qkv3 = qkv.reshape(B, T, -1)
    S0, conv_buf8 = gdn_state
    C = min(64, T)
    tb = min(512, T)
    u, w, qg, kg, qk, eg = gdn_prep(qkv3, conv_buf8, p['gdn_conv_w'], ba.reshape(B, T, -1), p['gdn_a_log'],
                                    p['gdn_dt_bias'], C, tb, t_valid, nb=1 if prompt else 4)
    o_mix, S = gdn_scan(u, w, qg, kg, qk, eg, z.reshape(B, T, -1), p['gdn_norm_g'], S0, C)
    o_mem = cross_attend(xq, mem_kv, 1, B)
    y = post_block(xf, o_mix.reshape(N, -1), o_mem.reshape(N, -1), p['w_out'][1], p['norm_ffn_g'][1],
                   p['w_up'][1], p['w_down'][1], p['final_norm_g'], final=True)
    return y.reshape(B, T, D), kv_rows, win_rows, S, qkv3


def kernel(x_prompt, x_sample, mem_prompt, cache_nsa_kv, cache_nsa_win, state_gdn_s, state_gdn_conv,
           cache_mem_kv, page_table, norm_mix_g, norm_mem_g, w_mem_kv, nsa_w_in, cmp_pe_k, cmp_w1_k,
           cmp_w2_k, cmp_pe_v, cmp_w1_v, cmp_w2_v, gdn_w_in, gdn_conv_w, gdn_a_log, gdn_dt_bias,
           gdn_norm_g, w_out, norm_ffn_g, w_up, w_down, final_norm_g):
    B, T, D = x_prompt.shape
    DB, TS, _ = x_sample.shape
    M = mem_prompt.shape[1]
    p = dict(norm_mix_g=norm_mix_g, nsa_in_prompt=_nsa_in_groups(nsa_w_in[0], True),
             nsa_in_sample=_nsa_in_groups(nsa_w_in[0], False), gdn_in_prompt=_gdn_in_groups(gdn_w_in[0], True),
             gdn_in_sample=_gdn_in_groups(gdn_w_in[0], False),
             cmp=_cmp_weights(cmp_pe_k[0], cmp_w1_k[0], cmp_w2_k[0]) + _cmp_weights(cmp_pe_v[0], cmp_w1_v[0], cmp_w2_v[0]),
             cmp_dense=(_cmp_weights_dense(cmp_pe_k[0], cmp_w1_k[0], cmp_w2_k[0])
                        + _cmp_weights_dense(cmp_pe_v[0], cmp_w1_v[0], cmp_w2_v[0])),
             gdn_conv_w=gdn_conv_w[0], gdn_a_log=gdn_a_log[0], gdn_dt_bias=gdn_dt_bias[0], gdn_norm_g=gdn_norm_g[0],
             w_out=w_out.astype(bf16), norm_ffn_g=norm_ffn_g, w_up=w_up.astype(bf16), w_down=w_down.astype(bf16),
             final_norm_g=final_norm_g)

    mem_flat = mem_prompt.reshape(B * M, D)
    mem_kv = [norm_proj(mem_flat, norm_mem_g[i], [(w_mem_kv[i].astype(bf16).T, f32, 1.0, FEATURE_MAJOR_PER_BATCH)],
                        tm=M, rows_per_batch=M)[0] for i in range(DEPTH)]
    mem_kv_p = jnp.stack(mem_kv, axis=1)
    zero_state = (jnp.zeros((B, GDN_HEADS, GDN_DK, GDN_DV), f32), jnp.zeros((B, 8, GDN_CONV_CH), f32))
    y_p, rows_p, win_p, S_p, qkv_p = _trunk(x_prompt, mem_kv_p, None, zero_state, p, 10 ** 9)

    TP = 8
    x_s = jnp.pad(x_sample, ((0, 0), (0, TP - TS), (0, 0)))
    cache = cache_nsa_kv.transpose(0, 2, 3, 4, 5, 1).reshape(cache_nsa_kv.shape[0], -1, HEAD_DIM, PAGE_SIZE)
    nwin = cache_nsa_win.shape[2]
    cache_win = cache_nsa_win.transpose(0, 1, 3, 4, 5, 2).reshape(DB, -1, nwin)
    conv8 = jnp.pad(state_gdn_conv[:, 0], ((0, 0), (8 - (GDN_CONV - 1), 0), (0, 0)))
    mem_kv_s = cache_mem_kv.transpose(0, 1, 3, 4, 5, 2).reshape(DB, DEPTH, 2 * XA_WIDTH, M)
    y_s, rows_s, win_s, S_s, qkv_s = _trunk(x_s, mem_kv_s, (page_table, cache, cache_win),
                                            (state_gdn_s[:, 0], conv8), p, TS)

    kvshape = (4, NSA_KV_HEADS, HEAD_DIM)
    nsa_kv_prompt = rows_p.reshape(B, 1, *kvshape, T).transpose(0, 5, 1, 2, 3, 4)
    nsa_kv_sample = rows_s[:, :TS].reshape(DB, TS, 1, *kvshape)
    wshape = (2, NSA_KV_HEADS, HEAD_DIM)
    wkeep = min(WINDOW, T)
    nsa_win_prompt = win_p[:, :, T - wkeep:].reshape(B, 1, *wshape, wkeep).transpose(0, 1, 5, 2, 3, 4)
    win_cat = jnp.concatenate([cache_win, win_s[:, :TS].transpose(0, 2, 1)], axis=2)[:, :, -WINDOW:]
    nsa_win_sample = win_cat.reshape(DB, 1, *wshape, -1).transpose(0, 1, 5, 2, 3, 4)
    gdn_conv_prompt = qkv_p[:, None, T - (GDN_CONV - 1):]
    conv_cat = jnp.concatenate([state_gdn_conv[:, 0], qkv_s[:, :TS]], axis=1)
    gdn_conv_sample = conv_cat[:, None, -(GDN_CONV - 1):]
    mem_kv_prompt = mem_kv_p.reshape(B, DEPTH, 2, XA_HEADS, XA_DIM, M).transpose(0, 1, 5, 2, 3, 4)
    return (y_p, y_s[:, :TS], nsa_kv_prompt, nsa_kv_sample, nsa_win_prompt, nsa_win_sample,
            S_p[:, None], S_s[:, None], gdn_conv_prompt, gdn_conv_sample, mem_kv_prompt)
```

```python
import functools
import math

import jax
import jax.numpy as jnp
import numpy as np
from jax import lax
from jax.experimental import pallas as pl
from jax.experimental.pallas import tpu as pltpu

f32 = jnp.float32
bf16 = jnp.bfloat16

DEPTH = 2
PAGE_SIZE = 128
HEAD_DIM = 64
NSA_HEADS = 12
NSA_KV_HEADS = 2
NSA_GROUP = NSA_HEADS // NSA_KV_HEADS
NSA_WIDTH = NSA_HEADS * HEAD_DIM
NSA_KV_COLS = NSA_KV_HEADS * HEAD_DIM
CMP_STRIDE = 16
CMP_BLOCK = 2 * CMP_STRIDE
CMP_HIDDEN = 128
SLC_BLOCK = 64
SLC_TOPN = 16
WINDOW = 512
GDN_HEADS = 6
GDN_DK = 128
GDN_DV = 128
GDN_WIDTH = GDN_HEADS * GDN_DV
GDN_CONV = 4
GDN_CONV_CH = 2 * GDN_HEADS * GDN_DK + GDN_HEADS * GDN_DV
XA_HEADS = 4
XA_DIM = 64
XA_WIDTH = XA_HEADS * XA_DIM
NORM_EPS = 1e-6
L2_EPS = 1e-6
NEG_INF = -1e30
FORCED = 1e9

VMEM_LIMIT = 52 * 1024 * 1024
LANES = 128

SLOPES = [[2.0 ** (-8.0 * (k * NSA_GROUP + g + 1) / NSA_HEADS) for g in range(NSA_GROUP)]
          for k in range(NSA_KV_HEADS)]

NT = (((1,), (1,)), ((), ()))
TN = (((0,), (0,)), ((), ()))


def _cparams(*sem):
    return pltpu.CompilerParams(dimension_semantics=sem, vmem_limit_bytes=VMEM_LIMIT)


def _rms(x, g):
    r = lax.rsqrt(jnp.mean(x * x, axis=-1, keepdims=True) + NORM_EPS)
    return (x * r) * g


def _sigmoid(x):
    return 1.0 / (1.0 + jnp.exp(-x))


def _silu(x):
    return x * _sigmoid(x)


def _split3(a):
    hi = a.astype(bf16)
    r1 = a - hi.astype(f32)
    mid = r1.astype(bf16)
    lo = (r1 - mid.astype(f32)).astype(bf16)
    return hi, mid, lo


def _dot_exact_rhs(a, b_bf16):
    hi, mid, lo = _split3(a)
    d = lambda x: jnp.dot(x, b_bf16, preferred_element_type=f32)
    return d(hi) + d(mid) + d(lo)


def _split2(a):
    hi = a.astype(bf16)
    return hi, (a - hi.astype(f32)).astype(bf16)


def _dot_parts(a, b, dims=None):
    if dims is None:
        d = lambda x, y: jnp.dot(x, y, preferred_element_type=f32)
    else:
        d = lambda x, y: lax.dot_general(x, y, dims, preferred_element_type=f32)
    return d(a[0], b[0]) + (d(a[0], b[1]) + d(a[1], b[0]))


def _dot_hp(a, b, dims=None):
    return _dot_parts(_split2(a), _split2(b), dims)


TOKEN_MAJOR, FEATURE_MAJOR, FEATURE_MAJOR_PER_BATCH = 'nt', 'fm', 'fmb'


def _norm_proj_kernel(x_ref, g_ref, *refs, specs):
    n = len(specs)
    w_refs, o_refs = refs[:n], refs[n:]
    h = _rms(x_ref[...], g_ref[...]).astype(bf16)
    for (scale, kind), w_ref, o_ref in zip(specs, w_refs, o_refs):
        if kind == TOKEN_MAJOR:
            y = lax.dot_general(h, w_ref[...], NT, preferred_element_type=f32)
        else:
            y = lax.dot_general(w_ref[...], h, NT, preferred_element_type=f32)
        if scale != 1.0:
            y = y * scale
        o_ref[...] = y.astype(o_ref.dtype).reshape(o_ref.shape)


def norm_proj(x, g, groups, tm=256, rows_per_batch=None):
    N, D = x.shape
    tm = min(tm, N)
    assert N % tm == 0
    in_specs = [pl.BlockSpec((tm, D), lambda i: (i, 0)), pl.BlockSpec((1, D), lambda i: (0, 0))]
    out_specs, out_shape = [], []
    for w, dt, _, kind in groups:
        in_specs.append(pl.BlockSpec(w.shape, lambda i: (0, 0)))
        n = w.shape[0]
        if kind == TOKEN_MAJOR:
            out_specs.append(pl.BlockSpec((tm, n), lambda i: (i, 0)))
            out_shape.append(jax.ShapeDtypeStruct((N, n), dt))
        elif kind == FEATURE_MAJOR:
            out_specs.append(pl.BlockSpec((n, tm), lambda i: (0, i)))
            out_shape.append(jax.ShapeDtypeStruct((n, N), dt))
        else:
            per = rows_per_batch // tm
            assert rows_per_batch % tm == 0
            out_specs.append(pl.BlockSpec((1, n, tm), lambda i: (i // per, 0, i % per)))
            out_shape.append(jax.ShapeDtypeStruct((N // rows_per_batch, n, rows_per_batch), dt))
    return pl.pallas_call(
        functools.partial(_norm_proj_kernel, specs=tuple((s, t) for _, _, s, t in groups)),
        grid=(N // tm,),
        in_specs=in_specs,
        out_specs=out_specs,
        out_shape=out_shape,
        compiler_params=_cparams("parallel"),
        name="norm_proj",
    )(x, g.reshape(1, D), *[w for w, _, _, _ in groups])


def _xattn_kernel(q_ref, kv_ref, o_ref, *, q_feature_major, nb):
    ch = [(b, h) for b in range(nb) for h in range(XA_HEADS)]
    hs = lambda h, off=0: slice(off + h * XA_DIM, off + (h + 1) * XA_DIM)
    qT = [q_ref[...] if q_feature_major else q_ref[b].astype(f32).T.astype(bf16) for b in range(nb)]
    s = [lax.dot_general(kv_ref[b, 0, hs(h), :].astype(bf16), qT[b][hs(h)], TN, preferred_element_type=f32)
         for b, h in ch]
    e = [jnp.exp2(x - jnp.max(x, axis=0, keepdims=True)) for x in s]
    o = [jnp.dot(kv_ref[b, 0, hs(h, XA_WIDTH), :].astype(bf16), e[i].astype(bf16), preferred_element_type=f32)
         / jnp.sum(e[i], axis=0, keepdims=True) for i, (b, h) in enumerate(ch)]
    for b in range(nb):
        o_ref[b] = jnp.concatenate(o[b * XA_HEADS:(b + 1) * XA_HEADS], axis=0).T.astype(o_ref.dtype)


def cross_attend(xq, kvT, layer, B, tq=2048, nb=8):
    q_feature_major = xq.ndim == 2
    T = xq.shape[1] // B if q_feature_major else xq.shape[1]
    M = kvT.shape[3]
    tq = min(tq, T)
    nq = T // tq
    if q_feature_major:
        nb = 1
        q_spec = pl.BlockSpec((XA_WIDTH, tq), lambda b, i: (0, b * nq + i))
    else:
        nb = math.gcd(nb, B)
        q_spec = pl.BlockSpec((nb, tq, XA_WIDTH), lambda b, i: (b, i, 0))
    return pl.pallas_call(
        functools.partial(_xattn_kernel, q_feature_major=q_feature_major, nb=nb),
        grid=(B // nb, nq),
        in_specs=[q_spec, pl.BlockSpec((nb, 1, 2 * XA_WIDTH, M), lambda b, i: (b, layer, 0, 0))],
        out_specs=pl.BlockSpec((nb, tq, XA_WIDTH), lambda b, i: (b, i, 0)),
        out_shape=jax.ShapeDtypeStruct((B, T, XA_WIDTH), bf16),
        compiler_params=_cparams("parallel", "parallel"),
        name="cross_attend",
    )(xq, kvT)


def _post_kernel(x_ref, om_ref, oc_ref, wo_ref, gf_ref, wu_ref, wd_ref, gl_ref, o_ref,
                 x1_s, h_s, acc_s, *, mix_w, final):
    j = pl.program_id(1)

    @pl.when(j == 0)
    def _():
        x1 = x_ref[...] + (jnp.dot(om_ref[...], wo_ref[0:mix_w, :], preferred_element_type=f32)
                           + jnp.dot(oc_ref[...], wo_ref[mix_w:, :], preferred_element_type=f32))
        x1_s[...] = x1
        h_s[...] = _rms(x1, gf_ref[...]).astype(bf16)
        acc_s[...] = jnp.zeros_like(acc_s)

    u = jnp.dot(h_s[...], wu_ref[...], preferred_element_type=f32)
    u = jnp.square(jnp.maximum(u, 0.0)).astype(bf16)
    acc_s[...] += jnp.dot(u, wd_ref[...], preferred_element_type=f32)

    @pl.when(j == pl.num_programs(1) - 1)
    def _():
        x2 = x1_s[...] + acc_s[...]
        if final:
            x2 = _rms(x2, gl_ref[...])
        o_ref[...] = x2


def post_block(x, o_mix, o_mem, w_out, g_ffn, w_up, w_down, g_final, final, tm=1024, tf=512):
    N, D = x.shape
    mix_w = o_mix.shape[1]
    F = w_up.shape[1]
    tm = min(tm, N)
    return pl.pallas_call(
        functools.partial(_post_kernel, mix_w=mix_w, final=final),
        grid=(N // tm, F // tf),
        in_specs=[pl.BlockSpec((tm, D), lambda i, j: (i, 0)),
                  pl.BlockSpec((tm, mix_w), lambda i, j: (i, 0)),
                  pl.BlockSpec((tm, XA_WIDTH), lambda i, j: (i, 0)),
                  pl.BlockSpec((mix_w + XA_WIDTH, D), lambda i, j: (0, 0)),
                  pl.BlockSpec((1, D), lambda i, j: (0, 0)),
                  pl.BlockSpec((D, tf), lambda i, j: (0, j)),
                  pl.BlockSpec((tf, D), lambda i, j: (j, 0)),
                  pl.BlockSpec((1, D), lambda i, j: (0, 0))],
        out_specs=pl.BlockSpec((tm, D), lambda i, j: (i, 0)),
        out_shape=jax.ShapeDtypeStruct((N, D), f32),
        scratch_shapes=[pltpu.VMEM((tm, D), f32), pltpu.VMEM((tm, D), bf16), pltpu.VMEM((tm, D), f32)],
        compiler_params=_cparams("parallel", "arbitrary"),
        name="post_block",
    )(x, o_mix, o_mem, w_out, g_ffn.reshape(1, D), w_up, w_down, g_final.reshape(1, D))


def _compress(load_chunk_rows, nch, pe_ref, w_ref, w2_ref):
    acc_p = jnp.zeros((nch, 2 * CMP_HIDDEN), f32)
    acc_q = jnp.zeros((nch, 2 * CMP_HIDDEN), f32)
    for l in range(CMP_STRIDE):
        x = load_chunk_rows(l)
        xp = (x + pe_ref[l:l + 1, :]).astype(bf16)
        xq = (x + pe_ref[CMP_STRIDE + l:CMP_STRIDE + l + 1, :]).astype(bf16)
        acc_p = acc_p + jnp.dot(xp, w_ref[l, :, 0:2 * CMP_HIDDEN], preferred_element_type=f32)
        acc_q = acc_q + jnp.dot(xq, w_ref[l, :, 2 * CMP_HIDDEN:], preferred_element_type=f32)
    hid = _silu(acc_p + pltpu.roll(acc_q, nch - 1, 0)).astype(bf16)
    outs = [jnp.dot(hid[:, h * CMP_HIDDEN:(h + 1) * CMP_HIDDEN], w2_ref[...], preferred_element_type=f32)
            for h in range(NSA_KV_HEADS)]
    return jnp.concatenate(outs, axis=1)


def _cmp_prompt_kernel(xk_ref, xv_ref, pek_ref, pev_ref, wk_ref, wv_ref, w2k_ref, w2v_ref, kc_ref, vc_ref, *, nch):
    kc_ref[0] = _compress(lambda l: xk_ref[0, pl.ds(l, nch, stride=CMP_STRIDE), :], nch, pek_ref, wk_ref, w2k_ref)
    vc_ref[0] = _compress(lambda l: xv_ref[0, pl.ds(l, nch, stride=CMP_STRIDE), :], nch, pev_ref, wv_ref, w2v_ref).T


def _cmp_weights(pe, w1, w2):
    pe2 = jnp.tile(pe, (1, NSA_KV_HEADS))
    z = jnp.zeros((CMP_STRIDE, HEAD_DIM, CMP_HIDDEN), w1.dtype)
    a, b = w1[:CMP_STRIDE], w1[CMP_STRIDE:]
    top = jnp.concatenate([a, z, b, z], axis=2)
    bot = jnp.concatenate([z, a, z, b], axis=2)
    wbd = jnp.concatenate([top, bot], axis=1).astype(bf16)
    return pe2, wbd, w2.astype(bf16)


def _cmp_weights_dense(pe, w1, w2):
    half = CMP_STRIDE * HEAD_DIM
    w = jnp.concatenate([w1[:CMP_STRIDE].reshape(half, CMP_HIDDEN), w1[CMP_STRIDE:].reshape(half, CMP_HIDDEN)], axis=1)
    return pe.reshape(2, half), w.astype(bf16), w2.astype(bf16)


def cmp_prompt(rows, cw):
    B, T, _ = rows.shape
    nch = T // CMP_STRIDE
    pek, wk, w2k, pev, wv, w2v = cw
    full = lambda a: pl.BlockSpec(a.shape, lambda b: (0,) * a.ndim)
    return pl.pallas_call(
        functools.partial(_cmp_prompt_kernel, nch=nch),
        grid=(B,),
        in_specs=[pl.BlockSpec((1, T, LANES), lambda b: (b, 0, 0)),
                  pl.BlockSpec((1, T, LANES), lambda b: (b, 0, 1)),
                  full(pek), full(pev), full(wk), full(wv), full(w2k), full(w2v)],
        out_specs=[pl.BlockSpec((1, nch, LANES), lambda b: (b, 0, 0)), pl.BlockSpec((1, LANES, nch), lambda b: (b, 0, 0))],
        out_shape=[jax.ShapeDtypeStruct((B, nch, LANES), f32), jax.ShapeDtypeStruct((B, LANES, nch), f32)],
        compiler_params=_cparams("parallel"),
        name="cmp_prompt",
    )(rows, rows, pek, pev, wk, wv, w2k, w2v)


def _padded_queries(q_all, kvh, tq):
    del tq
    lo = kvh * NSA_GROUP
    return jnp.concatenate([q_all[:, (lo + g) * LANES:(lo + g + 1) * LANES] for g in range(NSA_GROUP)], axis=0)


def _cmp_probs(qm, kc, pos_col, kvh, tq):
    ncp = kc.shape[0]
    s = lax.dot_general(qm, kc.astype(bf16), NT, preferred_element_type=f32)
    n_idx = lax.broadcasted_iota(jnp.int32, (tq, ncp), 1)
    cdist = pos_col - (n_idx * CMP_STRIDE + (CMP_BLOCK - 1))
    cvis = cdist >= 0
    any_vis = pos_col >= (CMP_BLOCK - 1)
    cdf = cdist.astype(f32)
    p_sum = jnp.zeros((tq, ncp), f32)
    ps = []
    for g in range(NSA_GROUP):
        sg = jnp.where(cvis, s[g * tq:(g + 1) * tq] - SLOPES[kvh][g] * cdf, NEG_INF)
        m = jnp.max(sg, axis=-1, keepdims=True)
        e = jnp.exp(sg - m)
        p = e / jnp.sum(e, axis=-1, keepdims=True)
        p = jnp.where(any_vis, p, 0.0)
        p_sum = p_sum + p
        ps.append(p)
    return jnp.concatenate(ps, axis=0).astype(bf16), p_sum


def _select_blocks(p_sum, pos_col, ns, nsp, tq):
    ncp = p_sum.shape[1]
    ci = lax.broadcasted_iota(jnp.int32, (ncp, nsp), 0) * CMP_STRIDE
    cj = lax.broadcasted_iota(jnp.int32, (ncp, nsp), 1)
    c2s = ((ci < (cj + 1) * SLC_BLOCK) & (ci + CMP_BLOCK > cj * SLC_BLOCK)).astype(f32).astype(bf16)
    imp = _dot_exact_rhs(p_sum, c2s)
    blk = lax.broadcasted_iota(jnp.int32, (tq, nsp), 1)
    cur = pos_col // SLC_BLOCK
    forced = (blk == 0) | (blk == cur) | (blk == cur - 1)
    imp = jnp.where(forced, FORCED, jnp.where(blk <= cur, imp, NEG_INF))
    rank = jnp.zeros((tq, nsp), jnp.int32)
    for jp in range(ns):
        col = imp[:, jp:jp + 1]
        beats = (col > imp) | ((col == imp) & (blk > jp))
        rank = rank + beats.astype(jnp.int32)
    return ((rank < SLC_TOPN) & (blk <= cur)).astype(f32)


def _combine(gates, kvh, tq, o_cmp, o_slc, o_win):
    lo = kvh * HEAD_DIM
    outs = []
    for g in range(NSA_GROUP):
        h = kvh * NSA_GROUP + g
        r = slice(g * tq, (g + 1) * tq)
        o = (gates[:, 3 * h:3 * h + 1] * o_cmp[r] + gates[:, 3 * h + 1:3 * h + 2] * o_slc[r]
             + gates[:, 3 * h + 2:3 * h + 3] * o_win[r])
        outs.append(o[:, lo:lo + HEAD_DIM])
    return outs


LOG2E = 1.4426950408889634
SLOPES2 = [[v * LOG2E for v in row] for row in SLOPES]

POS_SPLIT = 8
F_SEL0 = 16
MASK_BIG = 1e30


def position_features(pos, blk, ns):
    n = pos.shape[0]
    lane = np.arange(HEAD_DIM)[None, :]
    hi = ((pos // POS_SPLIT) * POS_SPLIT).astype(np.float32)[:, None]
    lo = (pos % POS_SPLIT).astype(np.float32)[:, None]
    f = np.where(lane < 3, hi, np.where(lane < 6, lo, np.where(lane < 9, 1.0, 0.0)))
    if ns:
        f = np.where((lane >= F_SEL0) & (lane < F_SEL0 + ns), (blk[:, None] == lane - F_SEL0).astype(np.float32), f)
    return np.broadcast_to(f, (n, HEAD_DIM)).astype(np.float32)


def feature_tables(T, nch):
    pos = np.arange(T)
    kf = position_features(pos, pos // SLC_BLOCK, T // SLC_BLOCK)
    cpos = np.arange(nch) * CMP_STRIDE + (CMP_BLOCK - 1)
    cf = position_features(cpos, cpos, 0)
    z = np.zeros_like
    two = lambda a: jnp.asarray(np.stack([np.concatenate([z(a), a], axis=1),
                                          np.concatenate([a, z(a)], axis=1)]).astype(bf16))
    return two(kf), two(cf)


def _query_features(pos_row, selT, kvh, tq):
    posf = pos_row.astype(f32)
    r16 = lax.broadcasted_iota(jnp.int32, (16, tq), 0)
    zeros16 = jnp.zeros((16, tq), f32)
    selbig = (selT - 1.0) * MASK_BIG
    ns = selT.shape[0]
    pad = jnp.zeros((HEAD_DIM - F_SEL0 - ns, tq), f32)
    plain, full = [], []
    for g in range(NSA_GROUP):
        s = jnp.full((1, tq), SLOPES2[kvh][g], f32)
        s_parts = _split3(s)
        a_parts = _split3(s * posf)
        rows = [p.astype(f32) for p in s_parts] * 2 + [-p.astype(f32) for p in a_parts]
        blk0 = zeros16
        for i, rv in enumerate(rows):
            blk0 = jnp.where(r16 == i, rv, blk0)
        plain.append(jnp.concatenate([blk0, jnp.zeros((HEAD_DIM - 16, tq), f32)], axis=0).astype(bf16))
        full.append(jnp.concatenate([blk0, selbig, pad], axis=0).astype(bf16))
    return plain, full


def _aug_queries(qT, feats, kvh, tq):
    out = []
    for g in range(NSA_GROUP):
        h = kvh * NSA_GROUP + g
        qh = qT[h * HEAD_DIM:(h + 1) * HEAD_DIM, :]
        out.append(jnp.concatenate([qh, feats[g]] if kvh == 0 else [feats[g], qh], axis=0))
    return out


def _aug_keys(k_tile, feat_tile, kvh):
    lane = lax.broadcasted_iota(jnp.int32, k_tile.shape, 1)
    own = (lane < HEAD_DIM) if kvh == 0 else (lane >= HEAD_DIM)
    return jnp.where(own, k_tile, feat_tile)


def _with_ones(vT, kvh):
    half = vT.shape[0] // 2
    ones = jnp.ones((half, vT.shape[1]), vT.dtype)
    return jnp.concatenate([vT[:half], ones] if kvh == 0 else [ones, vT[half:]], axis=0)


def _cmp_branch_t(q_aug, kc_aug, vcT, pos_row, tq):
    ncp = vcT.shape[1]
    n_idx = lax.broadcasted_iota(jnp.int32, (ncp, tq), 0)
    cvis = pos_row >= (n_idx * CMP_STRIDE + (CMP_BLOCK - 1))
    any_vis = pos_row >= (CMP_BLOCK - 1)
    s = [jnp.where(cvis, jnp.dot(k, q, preferred_element_type=f32), NEG_INF) for k, q in zip(kc_aug, q_aug)]
    e = [jnp.exp2(x - jnp.max(x, axis=0, keepdims=True)) for x in s]
    p = [jnp.where(any_vis, x / jnp.sum(x, axis=0, keepdims=True), 0.0) for x in e]
    outs = [jnp.dot(vcT, x.astype(bf16), preferred_element_type=f32) for x in p]
    return outs, p


def _select_blocks_t(p_sumT, pos_row, ns, tq):
    ncp = p_sumT.shape[0]
    cj = lax.broadcasted_iota(jnp.int32, (ns, ncp), 0)
    ci = lax.broadcasted_iota(jnp.int32, (ns, ncp), 1) * CMP_STRIDE
    c2sT = ((ci < (cj + 1) * SLC_BLOCK) & (ci + CMP_BLOCK > cj * SLC_BLOCK)).astype(f32).astype(bf16)
    hi, mid, lo = _split3(p_sumT)
    d = lambda x: jnp.dot(c2sT, x, preferred_element_type=f32)
    imp = d(hi) + d(mid) + d(lo)
    blk = lax.broadcasted_iota(jnp.int32, (ns, tq), 0)
    cur = pos_row // SLC_BLOCK
    forced = (blk == 0) | (blk == cur) | (blk == cur - 1)
    imp = jnp.where(forced, FORCED, jnp.where(blk <= cur, imp, NEG_INF))
    rank = jnp.zeros((ns, tq), jnp.int32)
    for jp in range(ns):
        row = imp[jp:jp + 1, :]
        beats = (row > imp) | ((row == imp) & (blk > jp))
        rank = rank + beats.astype(jnp.int32)
    return ((rank < SLC_TOPN) & (blk <= cur)).astype(f32)


def _nsa_prompt_kernel(qT_ref, gT_ref, sk_ref, svT_ref, wk_ref, wvT_ref, kc_ref, vcT_ref, kf_ref, cf_ref, o_ref,
                       m_s, acc_s, *, tq, kt, ns):
    qi = pl.program_id(1)
    p0 = qi * tq
    pos_row = p0 + lax.broadcasted_iota(jnp.int32, (1, tq), 1)
    gates = _sigmoid(gT_ref[...])
    qT = qT_ref[...]
    kc = kc_ref[0].astype(bf16)
    vcT = vcT_ref[0].astype(bf16)
    half = LANES // 2
    KV = range(NSA_KV_HEADS)

    zsel = jnp.zeros((ns, tq), f32)
    q_pos = [_aug_queries(qT, _query_features(pos_row, zsel, kvh, tq)[0], kvh, tq) for kvh in KV]
    kc_aug = [_aug_keys(kc, cf_ref[kvh], kvh) for kvh in KV]
    oc, pc = _cmp_branch_t([q for kvh in KV for q in q_pos[kvh]], [kc_aug[kvh] for kvh in KV for _ in range(NSA_GROUP)],
                           vcT, pos_row, tq)
    o_cmp = [oc[kvh * NSA_GROUP:(kvh + 1) * NSA_GROUP] for kvh in KV]
    q_sel = []
    for kvh in KV:
        p_sum = pc[kvh * NSA_GROUP]
        for x in pc[kvh * NSA_GROUP + 1:(kvh + 1) * NSA_GROUP]:
            p_sum = p_sum + x
        selT = _select_blocks_t(p_sum, pos_row, ns, tq)
        q_sel.append(_aug_queries(qT, _query_features(pos_row, selT, kvh, tq)[1], kvh, tq))

    m_s[...] = jnp.full(m_s.shape, NEG_INF, f32)
    acc_s[...] = jnp.zeros(acc_s.shape, f32)
    rel = pos_row - lax.broadcasted_iota(jnp.int32, (kt, tq), 0)

    def slc_tile(c, causal):
        k0 = pl.multiple_of(c * kt, kt)
        k_tile = sk_ref[pl.ds(k0, kt), :]
        v_tile = svT_ref[:, pl.ds(k0, kt)]
        ok = (rel - k0) >= 0
        for kvh in KV:
            ka = _aug_keys(k_tile, kf_ref[kvh, pl.ds(k0, kt), :], kvh)
            va = _with_ones(v_tile, kvh)
            for g in range(NSA_GROUP):
                cs = slice(g * tq, (g + 1) * tq)
                sg = jnp.dot(ka, q_sel[kvh][g], preferred_element_type=f32)
                if causal:
                    sg = jnp.where(ok, sg, NEG_INF)
                m_old = m_s[kvh, :, cs]
                m_new = jnp.maximum(m_old, jnp.max(sg, axis=0, keepdims=True))
                alpha = jnp.exp2(m_old - m_new)
                p = jnp.exp2(sg - m_new).astype(bf16)
                acc_s[kvh, :, cs] = alpha * acc_s[kvh, :, cs] + jnp.dot(va, p, preferred_element_type=f32)
                m_s[kvh, :, cs] = m_new

    n_kt = (p0 + tq + kt - 1) // kt

    def slc_body(c, _):
        slc_tile(c, False)
        return 0

    lax.fori_loop(0, n_kt - 1, slc_body, 0)
    slc_tile(n_kt - 1, True)

    wlen = WINDOW + tq
    w0 = pl.multiple_of(jnp.maximum(p0 - WINDOW, 0), tq)
    k_win = wk_ref[pl.ds(w0, wlen), :]
    v_win = wvT_ref[:, pl.ds(w0, wlen)]
    dist = pos_row - (w0 + lax.broadcasted_iota(jnp.int32, (wlen, tq), 0))
    valid = (dist >= 0) & (dist < WINDOW)
    ch = [(kvh, g) for kvh in KV for g in range(NSA_GROUP)]
    ka = [_aug_keys(k_win, kf_ref[kvh, pl.ds(w0, wlen), :], kvh) for kvh in KV]
    va = [_with_ones(v_win, kvh) for kvh in KV]
    sg = [jnp.where(valid, jnp.dot(ka[kvh], q_pos[kvh][g], preferred_element_type=f32), NEG_INF) for kvh, g in ch]
    p = [jnp.exp2(x - jnp.max(x, axis=0, keepdims=True)).astype(bf16) for x in sg]
    o_win = [jnp.dot(va[kvh], p[i], preferred_element_type=f32) for i, (kvh, g) in enumerate(ch)]
    heads = []
    for i, (kvh, g) in enumerate(ch):
        h = kvh * NSA_GROUP + g
        d0, l0 = (0, half) if kvh == 0 else (half, 0)
        o_slc = acc_s[kvh, :, g * tq:(g + 1) * tq]
        heads.append(gates[3 * h:3 * h + 1] * o_cmp[kvh][g][d0:d0 + half]
                     + gates[3 * h + 1:3 * h + 2] * (o_slc[d0:d0 + half] / o_slc[l0:l0 + 1])
                     + gates[3 * h + 2:3 * h + 3] * (o_win[i][d0:d0 + half] / o_win[i][l0:l0 + 1]))
    o_ref[0] = jnp.concatenate(heads, axis=0).T.astype(o_ref.dtype)


def nsa_prompt(qT, gT, sk, svT, wk, wvT, kc, vcT, B, tq=128, kt=512):
    N = sk.shape[0]
    T = N // B
    ns = T // SLC_BLOCK
    nq = T // tq
    ncp = kc.shape[1]
    assert T % kt == 0 and T % tq == 0 and kt % tq == 0 and T >= WINDOW + tq and T <= POS_SPLIT * 256
    assert F_SEL0 + ns <= HEAD_DIM
    kf, cf = feature_tables(T, ncp)
    keys = pl.BlockSpec((T, LANES), lambda b, i: (b, 0))
    vals = pl.BlockSpec((LANES, T), lambda b, i: (0, b))
    return pl.pallas_call(
        functools.partial(_nsa_prompt_kernel, tq=tq, kt=kt, ns=ns),
        grid=(B, nq),
        in_specs=[pl.BlockSpec((NSA_WIDTH, tq), lambda b, i: (0, b * nq + i)),
                  pl.BlockSpec((3 * NSA_HEADS, tq), lambda b, i: (0, b * nq + i)),
                  keys, vals, keys, vals,
                  pl.BlockSpec((1, ncp, LANES), lambda b, i: (b, 0, 0)),
                  pl.BlockSpec((1, LANES, ncp), lambda b, i: (b, 0, 0)),
                  pl.BlockSpec((NSA_KV_HEADS, T, LANES), lambda b, i: (0, 0, 0)),
                  pl.BlockSpec((NSA_KV_HEADS, ncp, LANES), lambda b, i: (0, 0, 0))],
        out_specs=pl.BlockSpec((1, tq, NSA_WIDTH), lambda b, i: (b, i, 0)),
        out_shape=jax.ShapeDtypeStruct((B, T, NSA_WIDTH), bf16),
        scratch_shapes=[pltpu.VMEM((NSA_KV_HEADS, 1, NSA_GROUP * tq), f32),
                        pltpu.VMEM((NSA_KV_HEADS, LANES, NSA_GROUP * tq), f32)],
        compiler_params=_cparams("parallel", "arbitrary"),
        name="nsa_prompt",
    )(qT, gT, sk, svT, wk, wvT, kc, vcT, kf, cf)


SF_SPLIT = 64
SF_BLK0 = 16
SF_ROWS = 256


def sample_key_features(past_len, nk, tq, ns):
    sf_new0 = SF_BLK0 + -(-ns // 8) * 8
    assert sf_new0 + tq + 1 <= SF_ROWS and nk <= SF_SPLIT * 256
    pos = np.arange(nk)[None, :]
    row = np.arange(SF_ROWS)[:, None]
    hi = ((pos // SF_SPLIT) * SF_SPLIT).astype(np.float32)
    lo = (pos % SF_SPLIT).astype(np.float32)
    f = np.where(row < 3, hi, np.where(row < 6, lo, np.where(row < 9, 1.0, 0.0)))
    blk = (row >= SF_BLK0) & (row < SF_BLK0 + ns) & (pos // SLC_BLOCK == row - SF_BLK0)
    new = (row >= sf_new0) & (row < sf_new0 + tq) & (pos - past_len == row - sf_new0)
    pad = (row == sf_new0 + tq) & (pos >= past_len + tq)
    return jnp.asarray(np.where(blk | new | pad, 1.0, f).astype(bf16)), sf_new0


def _sample_query_features(sel, pos_col, kvh, tq, ns, sf_new0):
    lane = lax.broadcasted_iota(jnp.int32, (tq, SF_ROWS), 1)
    t_idx = lax.broadcasted_iota(jnp.int32, (tq, SF_ROWS), 0)
    selbig = (pltpu.roll(sel[:, 0:SF_ROWS], SF_BLK0, 1) - 1.0) * MASK_BIG
    base = jnp.where((lane >= SF_BLK0) & (lane < SF_BLK0 + ns), selbig, 0.0)
    base = jnp.where((lane >= sf_new0) & (lane < sf_new0 + tq) & (lane - sf_new0 > t_idx), -MASK_BIG, base)
    base = jnp.where(lane == sf_new0 + tq, -MASK_BIG, base)
    posf = pos_col.astype(f32)
    out = []
    for g in range(NSA_GROUP):
        s = jnp.full((tq, 1), SLOPES[kvh][g], f32)
        cols = [p.astype(f32) for p in _split3(s)] * 2 + [-p.astype(f32) for p in _split3(s * posf)]
        qf = base
        for i, cv in enumerate(cols):
            qf = jnp.where(lane == i, cv, qf)
        out.append(qf)
    return jnp.concatenate(out, axis=0).astype(bf16)


def _new_rows_t(rows, width):
    tq = rows.shape[0]
    sq = jnp.concatenate([rows, jnp.zeros((LANES - tq, LANES), f32)], axis=0).T
    if width == LANES:
        return sq
    return jnp.concatenate([sq, jnp.zeros((LANES, width - LANES), f32)], axis=1)


def _compress_staged(xs, t, nch, pe_ref, w_ref, w2_ref):
    lhs = jnp.concatenate([jnp.concatenate([xs[2 * t + h, j] for j in range(CMP_STRIDE // 2)], axis=1)
                           for h in range(NSA_KV_HEADS)], axis=0)
    acc = jnp.dot(lhs, w_ref[...], preferred_element_type=f32)
    bc = lambda v: jnp.broadcast_to(v, (8, v.shape[1])).astype(bf16)
    c = (jnp.dot(bc(pe_ref[0:1, :]), w_ref[:, 0:CMP_HIDDEN], preferred_element_type=f32)
         + jnp.dot(bc(pe_ref[1:2, :]), w_ref[:, CMP_HIDDEN:], preferred_element_type=f32))[0:1, :]
    outs = []
    for h in range(NSA_KV_HEADS):
        r = slice(h * nch, (h + 1) * nch)
        hid = acc[r, 0:CMP_HIDDEN] + pltpu.roll(acc[r, CMP_HIDDEN:], nch - 1, 0) + c
        outs.append(jnp.dot(_silu(hid).astype(bf16), w2_ref[...], preferred_element_type=f32))
    return jnp.concatenate(outs, axis=1)


def _nsa_sample_kernel(pt_ref, *refs, pps, n_steps, tq, past_len, ns, nsp, sf_new0):
    page_refs = refs[:pps]
    (q_ref, g_ref, rn_ref, wn_ref, cw_ref, ft_ref, perm_ref, pek_ref, pev_ref, wk_ref, wv_ref, w2k_ref, w2v_ref,
     o_ref, xs_s, kT_s, vT_s) = refs[pps:]
    del pt_ref
    step = pl.program_id(1)
    cpp = PAGE_SIZE // CMP_STRIDE
    staged = []
    for i in range(pps):
        r0 = pl.multiple_of((step * pps + i) * PAGE_SIZE, PAGE_SIZE)
        pg = page_refs[i]
        slabs = []
        for th in range(4):
            x = pg[0, th]
            slabs += [x, pltpu.roll(x, PAGE_SIZE - 1, 1)]
        a = jnp.concatenate(slabs, axis=0).astype(bf16)
        staged.append(lax.dot_general(perm_ref[...], a, NT, preferred_element_type=f32))
        kT_s[:, pl.ds(r0, PAGE_SIZE)] = pg[0, 4:6].reshape(LANES, PAGE_SIZE).astype(bf16)
        vT_s[:, pl.ds(r0, PAGE_SIZE)] = pg[0, 6:8].reshape(LANES, PAGE_SIZE).astype(bf16)
    c0 = pl.multiple_of(step * (pps * cpp), pps * cpp)
    for j in range(CMP_STRIDE // 2):
        rows = slice(j * cpp, (j + 1) * cpp)
        both = jnp.concatenate([t[rows] for t in staged], axis=0).astype(bf16)
        for th in range(4):
            xs_s[th, j, pl.ds(c0, pps * cpp), :] = both[:, th * LANES:(th + 1) * LANES]

    @pl.when(step == n_steps - 1)
    def _():
        nk = kT_s.shape[1]
        rn = rn_ref[0]
        kT_s[:, past_len:] = _new_rows_t(rn[:, 2 * LANES:3 * LANES], nk - past_len).astype(bf16)
        vT_s[:, past_len:] = _new_rows_t(rn[:, 3 * LANES:4 * LANES], nk - past_len).astype(bf16)
        nwin = cw_ref.shape[2]
        wn = wn_ref[0]
        wkT = jnp.concatenate([cw_ref[0, 0:LANES, :], _new_rows_t(wn[:, 0:LANES], LANES)], axis=1).astype(bf16)
        wvT = jnp.concatenate([cw_ref[0, LANES:, :], _new_rows_t(wn[:, LANES:], LANES)], axis=1).astype(bf16)
        wlen = nwin + LANES

        nch = past_len // CMP_STRIDE
        kc = _compress_staged(xs_s, 0, nch, pek_ref, wk_ref, w2k_ref)

        pos_col = past_len + lax.broadcasted_iota(jnp.int32, (tq, 1), 0)
        gates = _sigmoid(g_ref[0])
        q_all = q_ref[0]
        wpos = (past_len - nwin) + lax.broadcasted_iota(jnp.int32, (tq, wlen), 1)
        wdist = pos_col - wpos
        wvalid = (wdist >= 0) & (wdist < WINDOW)
        wdistf = wdist.astype(f32)
        qms = [_padded_queries(q_all, kvh, tq) for kvh in range(NSA_KV_HEADS)]
        probs = [_cmp_probs(qms[kvh], kc, pos_col, kvh, tq) for kvh in range(NSA_KV_HEADS)]
        vc = _compress_staged(xs_s, 1, nch, pev_ref, wv_ref, w2v_ref).astype(bf16)
        qfs = [_sample_query_features(_select_blocks(probs[kvh][1], pos_col, ns, nsp, tq), pos_col, kvh, tq, ns, sf_new0)
               for kvh in range(NSA_KV_HEADS)]
        o_cmps = [jnp.dot(probs[kvh][0], vc, preferred_element_type=f32) for kvh in range(NSA_KV_HEADS)]
        qm2 = jnp.concatenate(qms, axis=0)
        s = (jnp.dot(qm2, kT_s[...], preferred_element_type=f32)
             + jnp.dot(jnp.concatenate(qfs, axis=0), ft_ref[...], preferred_element_type=f32))
        e = jnp.exp(s - jnp.max(s, axis=-1, keepdims=True))
        o_slc2 = (lax.dot_general(e.astype(bf16), vT_s[...], NT, preferred_element_type=f32)
                  / jnp.sum(e, axis=-1, keepdims=True))
        sw2 = jnp.dot(qm2, wkT, preferred_element_type=f32)
        pw = []
        for kvh in range(NSA_KV_HEADS):
            for g in range(NSA_GROUP):
                r = slice((kvh * NSA_GROUP + g) * tq, (kvh * NSA_GROUP + g + 1) * tq)
                sg = jnp.where(wvalid, sw2[r] - SLOPES[kvh][g] * wdistf, NEG_INF)
                ew = jnp.where(wvalid, jnp.exp(sg - jnp.max(sg, axis=-1, keepdims=True)), 0.0)
                pw.append(ew / jnp.sum(ew, axis=-1, keepdims=True))
        o_win2 = lax.dot_general(jnp.concatenate(pw, axis=0).astype(bf16), wvT, NT, preferred_element_type=f32)
        outs = []
        for kvh in range(NSA_KV_HEADS):
            r = slice(kvh * NSA_GROUP * tq, (kvh + 1) * NSA_GROUP * tq)
            outs.extend(_combine(gates, kvh, tq, o_cmps[kvh], o_slc2[r], o_win2[r]))
        o_ref[0] = jnp.concatenate(outs, axis=1).astype(o_ref.dtype)


def nsa_sample(page_table, cacheT, q, gates, rows_new, win_new, cache_winT, cw, pps=32):
    B, tq, _ = q.shape
    n_pages = page_table.shape[1]
    past_len = n_pages * PAGE_SIZE
    assert n_pages % pps == 0 and past_len % SLC_BLOCK == 0
    n_steps = n_pages // pps
    nk = past_len + LANES
    ns = past_len // SLC_BLOCK + 1
    nsp = -(-ns // LANES) * LANES
    nch = past_len // CMP_STRIDE
    nwin = cache_winT.shape[2]
    pek, wk, w2k, pev, wv, w2v = cw
    ft, sf_new0 = sample_key_features(past_len, nk, tq, ns)
    cpp = PAGE_SIZE // CMP_STRIDE
    ro = np.arange(PAGE_SIZE)
    ro2 = np.arange(PAGE_SIZE // 2)
    perm = jnp.asarray((ro[None, :] == (ro2[:, None] % cpp) * CMP_STRIDE + 2 * (ro2[:, None] // cpp))
                       .astype(bf16))
    per_b = lambda n: pl.BlockSpec((1, tq, n), lambda b, s, pt: (b, 0, 0))
    full = lambda a: pl.BlockSpec(a.shape, lambda b, s, pt: (0,) * a.ndim)
    page_specs = [pl.BlockSpec((1, 8, HEAD_DIM, PAGE_SIZE),
                               functools.partial(lambda b, s, pt, i: (pt[b, s * pps + i], 0, 0, 0), i=i))
                  for i in range(pps)]
    grid_spec = pltpu.PrefetchScalarGridSpec(
        num_scalar_prefetch=1,
        grid=(B, n_steps),
        in_specs=page_specs + [per_b(NSA_HEADS * LANES), per_b(3 * NSA_HEADS), per_b(4 * LANES), per_b(2 * LANES),
                               pl.BlockSpec((1, 2 * LANES, nwin), lambda b, s, pt: (b, 0, 0)),
                               full(ft), full(perm), full(pek), full(pev), full(wk), full(wv), full(w2k), full(w2v)],
        out_specs=per_b(NSA_WIDTH),
        scratch_shapes=[pltpu.VMEM((4, CMP_STRIDE // 2, nch, LANES), bf16),
                        pltpu.VMEM((LANES, nk), bf16), pltpu.VMEM((LANES, nk), bf16)])
    return pl.pallas_call(
        functools.partial(_nsa_sample_kernel, pps=pps, n_steps=n_steps, tq=tq, past_len=past_len, ns=ns, nsp=nsp,
                          sf_new0=sf_new0),
        grid_spec=grid_spec,
        out_shape=jax.ShapeDtypeStruct((B, tq, NSA_WIDTH), bf16),
        compiler_params=_cparams("parallel", "arbitrary"),
        name="nsa_sample",
    )(page_table, *([cacheT] * pps), q, gates, rows_new, win_new, cache_winT, ft, perm, pek, pev, wk, wv, w2k, w2v)


def _gdn_prep_kernel(x_ref, halo_ref, cb_ref, cw_ref, ba_ref, al_ref, dtb_ref,
                     u_ref, w_ref, qg_ref, kg_ref, qk_ref, eg_ref, xs_s, *, tb, C, t_valid, nb):
    i = pl.program_id(1)
    HK = GDN_HEADS * GDN_DK
    R = min(tb, LANES)
    cpb = R // C

    @pl.when(i == 0)
    def _():
        xs_s[:, 0:8, :] = cb_ref[...]

    @pl.when(i > 0)
    def _():
        xs_s[:, 0:8, :] = halo_ref[...]

    xs_s[:, 8:, :] = x_ref[...]
    ri = lax.broadcasted_iota(jnp.int32, (R, R), 0)
    ci = lax.broadcasted_iota(jnp.int32, (R, R), 1)
    same = (ri // C) == (ci // C)
    lower = same & (ri >= ci)
    strict = same & (ri > ci)
    eye = (ri == ci).astype(f32)
    rr = lax.broadcasted_iota(jnp.int32, (R, LANES), 0)
    row_t = i * tb + lax.broadcasted_iota(jnp.int32, (tb, 1), 0)
    live = row_t < t_valid

    def conv_act(bb, c0):
        acc = cw_ref[GDN_CONV - 1:GDN_CONV, c0:c0 + LANES] * xs_s[bb, pl.ds(8, tb), c0:c0 + LANES]
        for k in range(1, GDN_CONV):
            acc = acc + cw_ref[GDN_CONV - 1 - k:GDN_CONV - k, c0:c0 + LANES] * xs_s[bb, pl.ds(8 - k, tb), c0:c0 + LANES]
        return _silu(acc)

    chains = []
    for bb, h in [(bb, h) for bb in range(nb) for h in range(GDN_HEADS)]:
        ba = ba_ref[bb]
        q = conv_act(bb, h * GDN_DK)
        k = conv_act(bb, HK + h * GDN_DK)
        v = conv_act(bb, 2 * HK + h * GDN_DV)
        q = q * lax.rsqrt(jnp.sum(q * q, axis=-1, keepdims=True) + L2_EPS) * (GDN_DK ** -0.5)
        k = k * lax.rsqrt(jnp.sum(k * k, axis=-1, keepdims=True) + L2_EPS)
        beta = _sigmoid(ba[:, h:h + 1])
        ar = ba[:, GDN_HEADS + h:GDN_HEADS + h + 1] + dtb_ref[:, h:h + 1]
        softplus = jnp.maximum(ar, 0.0) + jnp.log(1.0 + jnp.exp(-jnp.abs(ar)))
        gcol = -jnp.exp(al_ref[:, h:h + 1]) * softplus
        if t_valid < 10 ** 9:
            beta = jnp.where(live, beta, 0.0)
            gcol = jnp.where(live, gcol, 0.0)
            k = jnp.where(live, k, 0.0)
            v = jnp.where(live, v, 0.0)
        for blk in range(tb // R):
            r = slice(blk * R, (blk + 1) * R)
            qc, kc, vc, bc = q[r], k[r], v[r], beta[r]
            gc = jnp.broadcast_to(gcol[r], (R, LANES))
            sh = 1
            while sh < C:
                gc = gc + jnp.where((rr % C) >= sh, pltpu.roll(gc, sh, 0), 0.0)
                sh *= 2
            gct = gc.T
            decay = jnp.exp(jnp.where(lower, gc[:, 0:R] - gct[0:R, :], NEG_INF))
            kb = kc * bc
            a = jnp.where(strict, _dot_hp(kb, kc, NT) * decay, 0.0)
            qk = jnp.where(lower, lax.dot_general(qc.astype(bf16), kc.astype(bf16), NT,
                                                  preferred_element_type=f32) * decay, 0.0)
            gl = gc[R - 1:R, :]
            for j in range(cpb - 2, -1, -1):
                gl = jnp.where(rr < (j + 1) * C, gc[(j + 1) * C - 1:(j + 1) * C, :], gl)
            chains.append(dict(bb=bb, h=h, r=r, blk=blk, qc=qc, kc=kc, gc=gc, gl=gl, qk=qk, n=-a, t=eye - a,
                               rhs=jnp.concatenate([vc * bc, kb * jnp.exp(gc)], axis=1)))

    for ch in chains:
        ch['n'] = _split2(ch['n'])
    sh = 2
    while sh < C:
        for ch in chains:
            ch['n'] = _split2(_dot_parts(ch['n'], ch['n']))
        for ch in chains:
            ch['t'] = ch['t'] + _dot_parts(_split2(ch['t']), ch['n'])
        sh *= 2

    for ch in chains:
        bb, h, r, gc, gl = ch['bb'], ch['h'], ch['r'], ch['gc'], ch['gl']
        sol = ch['rhs'] + _dot_hp(ch['t'] - eye, ch['rhs'])
        cs = slice(h * GDN_DV, (h + 1) * GDN_DV)
        u_ref[bb, r, cs] = sol[:, 0:GDN_DV]
        w_ref[bb, r, cs] = sol[:, GDN_DV:].astype(w_ref.dtype)
        qg_ref[bb, r, cs] = (ch['qc'] * jnp.exp(gc)).astype(qg_ref.dtype)
        kg_ref[bb, r, cs] = (ch['kc'] * jnp.exp(gl - gc)).astype(kg_ref.dtype)
        qk = ch['qk'][:, 0:C]
        for j in range(1, cpb):
            qk = jnp.where(ri[:, 0:C] >= j * C, ch['qk'][:, j * C:(j + 1) * C], qk)
        qk_ref[bb, r, h * C:(h + 1) * C] = qk.astype(qk_ref.dtype)
        for j in range(cpb):
            eg_ref[bb, ch['blk'] * cpb + j, h:h + 1, :] = jnp.exp(gc[(j + 1) * C - 1:(j + 1) * C, :])
    eg_ref[:, :, GDN_HEADS:, :] = jnp.zeros((nb, tb // C, 8 - GDN_HEADS, LANES), f32)


def gdn_prep(qkv, conv_buf8, conv_w, ba, a_log, dt_bias, C, tb, t_valid, nb=1):
    B, T, CH = qkv.shape
    nb = math.gcd(nb, B)
    hb = tb // 8
    per = lambda n, dt: (pl.BlockSpec((nb, tb, n), lambda b, i: (b, i, 0)), jax.ShapeDtypeStruct((B, T, n), dt))
    outs = [per(GDN_WIDTH, f32), per(GDN_WIDTH, bf16), per(GDN_WIDTH, bf16), per(GDN_WIDTH, bf16),
            per(GDN_HEADS * C, bf16),
            (pl.BlockSpec((nb, tb // C, 8, LANES), lambda b, i: (b, i, 0, 0)),
             jax.ShapeDtypeStruct((B, T // C, 8, LANES), f32))]
    return pl.pallas_call(
        functools.partial(_gdn_prep_kernel, tb=tb, C=C, t_valid=t_valid, nb=nb),
        grid=(B // nb, T // tb),
        in_specs=[pl.BlockSpec((nb, tb, CH), lambda b, i: (b, i, 0)),
                  pl.BlockSpec((nb, 8, CH), lambda b, i: (b, jnp.maximum(i * hb - 1, 0), 0)),
                  pl.BlockSpec((nb, 8, CH), lambda b, i: (b, 0, 0)),
                  pl.BlockSpec((GDN_CONV, CH), lambda b, i: (0, 0)),
                  pl.BlockSpec((nb, tb, 2 * GDN_HEADS), lambda b, i: (b, i, 0)),
                  pl.BlockSpec((1, GDN_HEADS), lambda b, i: (0, 0)),
                  pl.BlockSpec((1, GDN_HEADS), lambda b, i: (0, 0))],
        out_specs=[o[0] for o in outs],
        out_shape=[o[1] for o in outs],
        scratch_shapes=[pltpu.VMEM((nb, tb + 8, CH), f32)],
        compiler_params=_cparams("parallel", "arbitrary"),
        name="gdn_prep",
    )(qkv, qkv, conv_buf8, conv_w, ba, a_log.reshape(1, GDN_HEADS), dt_bias.reshape(1, GDN_HEADS))


def _gdn_scan_kernel(u_ref, w_ref, qg_ref, kg_ref, qk_ref, eg_ref, z_ref, ng_ref, s0_ref,
                     o_ref, sT_ref, s_s, *, C, nb):
    c = pl.program_id(1)

    @pl.when(c == 0)
    def _():
        s_s[...] = s0_ref[...]

    ch = [(b, h) for b in range(nb) for h in range(GDN_HEADS)]
    cs = [slice(h * GDN_DV, (h + 1) * GDN_DV) for h in range(GDN_HEADS)]
    S = [s_s[b, h] for b, h in ch]
    Sb = [x.astype(bf16) for x in S]
    v_new = [u_ref[b, :, cs[h]] - jnp.dot(w_ref[b, :, cs[h]], Sb[i], preferred_element_type=f32)
             for i, (b, h) in enumerate(ch)]
    vb = [x.astype(bf16) for x in v_new]
    for i, (b, h) in enumerate(ch):
        s_s[b, h] = S[i] * eg_ref[b, 0, h:h + 1, :] + lax.dot_general(kg_ref[b, :, cs[h]], vb[i], TN,
                                                                       preferred_element_type=f32)
    o = [jnp.dot(qg_ref[b, :, cs[h]], Sb[i], preferred_element_type=f32)
         + jnp.dot(qk_ref[b, :, h * C:(h + 1) * C], vb[i], preferred_element_type=f32) for i, (b, h) in enumerate(ch)]
    outs = [_rms(o[i], ng_ref[...]) * _silu(z_ref[b, :, cs[h]]) for i, (b, h) in enumerate(ch)]
    for b in range(nb):
        o_ref[b] = jnp.concatenate(outs[b * GDN_HEADS:(b + 1) * GDN_HEADS], axis=1).astype(o_ref.dtype)

    @pl.when(c == pl.num_programs(1) - 1)
    def _():
        sT_ref[...] = s_s[...]


def gdn_scan(u, w, qg, kg, qk, eg, z, norm_g, s0, C, nb=8):
    B, T, _ = u.shape
    nb = math.gcd(nb, B)
    per = lambda n: pl.BlockSpec((nb, C, n), lambda b, c: (b, c, 0))
    st = pl.BlockSpec((nb, GDN_HEADS, GDN_DK, GDN_DV), lambda b, c: (b, 0, 0, 0))
    return pl.pallas_call(
        functools.partial(_gdn_scan_kernel, C=C, nb=nb),
        grid=(B // nb, T // C),
        in_specs=[per(GDN_WIDTH), per(GDN_WIDTH), per(GDN_WIDTH), per(GDN_WIDTH), per(GDN_HEADS * C),
                  pl.BlockSpec((nb, 1, 8, LANES), lambda b, c: (b, c, 0, 0)),
                  per(GDN_WIDTH), pl.BlockSpec((1, GDN_DV), lambda b, c: (0, 0)), st],
        out_specs=[per(GDN_WIDTH), st],
        out_shape=[jax.ShapeDtypeStruct((B, T, GDN_WIDTH), bf16),
                   jax.ShapeDtypeStruct((B, GDN_HEADS, GDN_DK, GDN_DV), f32)],
        scratch_shapes=[pltpu.VMEM((nb, GDN_HEADS, GDN_DK, GDN_DV), f32)],
        compiler_params=_cparams("parallel", "arbitrary"),
        name="gdn_scan",
    )(u, w, qg, kg, qk, eg, z, norm_g.reshape(1, GDN_DV), s0)


def _nsa_in_groups(w, feature_major):
    c1 = NSA_WIDTH
    c2 = c1 + 4 * NSA_KV_COLS
    c3 = c2 + 2 * NSA_KV_COLS
    c4 = c3 + 3 * NSA_HEADS
    wT = w.T.astype(bf16)
    xa_scale = XA_DIM ** -0.5 * LOG2E
    if feature_major:
        k0 = c1 + 2 * NSA_KV_COLS
        return [(wT[c1:c2], f32, 1.0, FEATURE_MAJOR_PER_BATCH),
                (wT[c1:k0], f32, 1.0, TOKEN_MAJOR),
                (wT[c2:c3], f32, 1.0, FEATURE_MAJOR_PER_BATCH),
                (wT[c4:], bf16, xa_scale, FEATURE_MAJOR),
                (wT[:c1], bf16, HEAD_DIM ** -0.5 * LOG2E, FEATURE_MAJOR),
                (wT[c3:c4], f32, 1.0, FEATURE_MAJOR),
                (wT[k0:k0 + NSA_KV_COLS], bf16, 1.0, TOKEN_MAJOR),
                (wT[k0 + NSA_KV_COLS:c2], bf16, 1.0, FEATURE_MAJOR),
                (wT[c2:c2 + NSA_KV_COLS], bf16, 1.0, TOKEN_MAJOR),
                (wT[c2 + NSA_KV_COLS:c3], bf16, 1.0, FEATURE_MAJOR)]
    wq = wT[:c1].reshape(NSA_KV_HEADS, NSA_GROUP, HEAD_DIM, -1)
    zq = jnp.zeros_like(wq[0])
    wq = jnp.concatenate([jnp.concatenate([wq[0], zq], axis=1), jnp.concatenate([zq, wq[1]], axis=1)], axis=0)
    wq = wq.reshape(NSA_HEADS * LANES, -1)
    return [(wT[c1:c2], f32, 1.0, TOKEN_MAJOR), (wT[c2:c3], f32, 1.0, TOKEN_MAJOR),
            (wT[c4:], bf16, xa_scale, TOKEN_MAJOR), (wq, bf16, HEAD_DIM ** -0.5, TOKEN_MAJOR),
            (wT[c3:c4], f32, 1.0, TOKEN_MAJOR)]


def _gdn_in_groups(w, feature_major):
    c1 = GDN_CONV_CH
    c2 = c1 + GDN_WIDTH
    c3 = c2 + 2 * GDN_HEADS
    wT = w.T.astype(bf16)
    xq = (wT[c3:], bf16, XA_DIM ** -0.5 * LOG2E, FEATURE_MAJOR if feature_major else TOKEN_MAJOR)
    return [(wT[:c1], f32, 1.0, TOKEN_MAJOR), (wT[c1:c2], f32, 1.0, TOKEN_MAJOR),
            (wT[c2:c3], f32, 1.0, TOKEN_MAJOR), xq]


def _trunk(x, mem_kv, nsa_past, gdn_state, p, t_valid):
    B, T, D = x.shape
    N = B * T
    xf = x.reshape(N, D)

    prompt = nsa_past is None
    if prompt:
        kv_rows, rows_cmp, win_rows, xq, qT, gT, sk, svT, wk, wvT = norm_proj(
            xf, p['norm_mix_g'][0], p['nsa_in_prompt'], tm=512, rows_per_batch=T)
        kc, vcT = cmp_prompt(rows_cmp.reshape(B, T, -1), p['cmp'])
        o_mix = nsa_prompt(qT, gT, sk, svT, wk, wvT, kc, vcT, B)
    else:
        rows, win, xq, q, gates = norm_proj(xf, p['norm_mix_g'][0], p['nsa_in_sample'])
        kv_rows = rows.reshape(B, T, -1)
        win_rows = win.reshape(B, T, -1)
        xq = xq.reshape(B, T, -1)
        page_table, cache, cache_win = nsa_past
        o_mix = nsa_sample(page_table, cache, q.reshape(B, T, -1), gates.reshape(B, T, -1), kv_rows, win_rows,
                           cache_win, p['cmp_dense'])
    o_mem = cross_attend(xq, mem_kv, 0, B)
    xf = post_block(xf, o_mix.reshape(N, -1), o_mem.reshape(N, -1), p['w_out'][0], p['norm_ffn_g'][0],
                    p['w_up'][0], p['w_down'][0], p['final_norm_g'], final=False)

    qkv, z, ba, xq = norm_proj(xf, p['norm_mix_g'][1], p['gdn_in_prompt' if prompt else 'gdn_in_sample'], tm=512)
    if not prompt:
        xq = xq.reshape(B, T, -1)
    qkv3 = qkv.reshape(B, T, -1)
    S0, conv_buf8 = gdn_state
    C = min(64, T)
    tb = min(512, T)
    u, w, qg, kg, qk, eg = gdn_prep(qkv3, conv_buf8, p['gdn_conv_w'], ba.reshape(B, T, -1), p['gdn_a_log'],
                                    p['gdn_dt_bias'], C, tb, t_valid, nb=1 if prompt else 8)
    o_mix, S = gdn_scan(u, w, qg, kg, qk, eg, z.reshape(B, T, -1), p['gdn_norm_g'], S0, C)
    o_mem = cross_attend(xq, mem_kv, 1, B)
    y = post_block(xf, o_mix.reshape(N, -1), o_mem.reshape(N, -1), p['w_out'][1], p['norm_ffn_g'][1],
                   p['w_up'][1], p['w_down'][1], p['final_norm_g'], final=True)
    return y.reshape(B, T, D), kv_rows, win_rows, S, qkv3


def kernel(x_prompt, x_sample, mem_prompt, cache_nsa_kv, cache_nsa_win, state_gdn_s, state_gdn_conv,
           cache_mem_kv, page_table, norm_mix_g, norm_mem_g, w_mem_kv, nsa_w_in, cmp_pe_k, cmp_w1_k,
           cmp_w2_k, cmp_pe_v, cmp_w1_v, cmp_w2_v, gdn_w_in, gdn_conv_w, gdn_a_log, gdn_dt_bias,
           gdn_norm_g, w_out, norm_ffn_g, w_up, w_down, final_norm_g):
    B, T, D = x_prompt.shape
    DB, TS, _ = x_sample.shape
    M = mem_prompt.shape[1]
    p = dict(norm_mix_g=norm_mix_g, nsa_in_prompt=_nsa_in_groups(nsa_w_in[0], True),
             nsa_in_sample=_nsa_in_groups(nsa_w_in[0], False), gdn_in_prompt=_gdn_in_groups(gdn_w_in[0], True),
             gdn_in_sample=_gdn_in_groups(gdn_w_in[0], False),
             cmp=_cmp_weights(cmp_pe_k[0], cmp_w1_k[0], cmp_w2_k[0]) + _cmp_weights(cmp_pe_v[0], cmp_w1_v[0], cmp_w2_v[0]),
             cmp_dense=(_cmp_weights_dense(cmp_pe_k[0], cmp_w1_k[0], cmp_w2_k[0])
                        + _cmp_weights_dense(cmp_pe_v[0], cmp_w1_v[0], cmp_w2_v[0])),
             gdn_conv_w=gdn_conv_w[0], gdn_a_log=gdn_a_log[0], gdn_dt_bias=gdn_dt_bias[0], gdn_norm_g=gdn_norm_g[0],
             w_out=w_out.astype(bf16), norm_ffn_g=norm_ffn_g, w_up=w_up.astype(bf16), w_down=w_down.astype(bf16),
             final_norm_g=final_norm_g)

    mem_flat = mem_prompt.reshape(B * M, D)
    mem_kv = [norm_proj(mem_flat, norm_mem_g[i], [(w_mem_kv[i].astype(bf16).T, f32, 1.0, FEATURE_MAJOR_PER_BATCH)],
                        tm=M, rows_per_batch=M)[0] for i in range(DEPTH)]
    mem_kv_p = jnp.stack(mem_kv, axis=1)
    zero_state = (jnp.zeros((B, GDN_HEADS, GDN_DK, GDN_DV), f32), jnp.zeros((B, 8, GDN_CONV_CH), f32))
    y_p, rows_p, win_p, S_p, qkv_p = _trunk(x_prompt, mem_kv_p, None, zero_state, p, 10 ** 9)

    TP = 8
    x_s = jnp.pad(x_sample, ((0, 0), (0, TP - TS), (0, 0)))
    cache = cache_nsa_kv.transpose(0, 2, 3, 4, 5, 1).reshape(cache_nsa_kv.shape[0], -1, HEAD_DIM, PAGE_SIZE)
    nwin = cache_nsa_win.shape[2]
    cache_win = cache_nsa_win.transpose(0, 1, 3, 4, 5, 2).reshape(DB, -1, nwin)
    conv8 = jnp.pad(state_gdn_conv[:, 0], ((0, 0), (8 - (GDN_CONV - 1), 0), (0, 0)))
    mem_kv_s = cache_mem_kv.transpose(0, 1, 3, 4, 5, 2).reshape(DB, DEPTH, 2 * XA_WIDTH, M)
    y_s, rows_s, win_s, S_s, qkv_s = _trunk(x_s, mem_kv_s, (page_table, cache, cache_win),
                                            (state_gdn_s[:, 0], conv8), p, TS)

    kvshape = (4, NSA_KV_HEADS, HEAD_DIM)
    nsa_kv_prompt = rows_p.reshape(B, 1, *kvshape, T).transpose(0, 5, 1, 2, 3, 4)
    nsa_kv_sample = rows_s[:, :TS].reshape(DB, TS, 1, *kvshape)
    wshape = (2, NSA_KV_HEADS, HEAD_DIM)
    wkeep = min(WINDOW, T)
    nsa_win_prompt = win_p[:, :, T - wkeep:].reshape(B, 1, *wshape, wkeep).transpose(0, 1, 5, 2, 3, 4)
    win_cat = jnp.concatenate([cache_win, win_s[:, :TS].transpose(0, 2, 1)], axis=2)[:, :, -WINDOW:]
    nsa_win_sample = win_cat.reshape(DB, 1, *wshape, -1).transpose(0, 1, 5, 2, 3, 4)
    gdn_conv_prompt = qkv_p[:, None, T - (GDN_CONV - 1):]
    conv_cat = jnp.concatenate([state_gdn_conv[:, 0], qkv_s[:, :TS]], axis=1)
    gdn_conv_sample = conv_cat[:, None, -(GDN_CONV - 1):]
    mem_kv_prompt = mem_kv_p.reshape(B, DEPTH, 2, XA_HEADS, XA_DIM, M).transpose(0, 1, 5, 2, 3, 4)
    return (y_p, y_s[:, :TS], nsa_kv_prompt, nsa_kv_sample, nsa_win_prompt, nsa_win_sample,
            S_p[:, None], S_s[:, None], gdn_conv_prompt, gdn_conv_sample, mem_kv_prompt)
```

```python
import functools
import math

import jax
import jax.numpy as jnp
import numpy as np
from jax import lax
from jax.experimental import pallas as pl
from jax.experimental.pallas import tpu as pltpu

f32 = jnp.float32
bf16 = jnp.bfloat16

DEPTH = 2
PAGE_SIZE = 128
HEAD_DIM = 64
NSA_HEADS = 12
NSA_KV_HEADS = 2
NSA_GROUP = NSA_HEADS // NSA_KV_HEADS
NSA_WIDTH = NSA_HEADS * HEAD_DIM
NSA_KV_COLS = NSA_KV_HEADS * HEAD_DIM
CMP_STRIDE = 16
CMP_BLOCK = 2 * CMP_STRIDE
CMP_HIDDEN = 128
SLC_BLOCK = 64
SLC_TOPN = 16
WINDOW = 512
GDN_HEADS = 6
GDN_DK = 128
GDN_DV = 128
GDN_WIDTH = GDN_HEADS * GDN_DV
GDN_CONV = 4
GDN_CONV_CH = 2 * GDN_HEADS * GDN_DK + GDN_HEADS * GDN_DV
XA_HEADS = 4
XA_DIM = 64
XA_WIDTH = XA_HEADS * XA_DIM
NORM_EPS = 1e-6
L2_EPS = 1e-6
NEG_INF = -1e30
FORCED = 1e9

VMEM_LIMIT = 52 * 1024 * 1024
LANES = 128

SLOPES = [[2.0 ** (-8.0 * (k * NSA_GROUP + g + 1) / NSA_HEADS) for g in range(NSA_GROUP)]
          for k in range(NSA_KV_HEADS)]

NT = (((1,), (1,)), ((), ()))
TN = (((0,), (0,)), ((), ()))


def _cparams(*sem):
    return pltpu.CompilerParams(dimension_semantics=sem, vmem_limit_bytes=VMEM_LIMIT)


def _rms(x, g):
    r = lax.rsqrt(jnp.mean(x * x, axis=-1, keepdims=True) + NORM_EPS)
    return (x * r) * g


def _sigmoid(x):
    return 1.0 / (1.0 + jnp.exp(-x))


def _silu(x):
    return x * _sigmoid(x)


def _split3(a):
    hi = a.astype(bf16)
    r1 = a - hi.astype(f32)
    mid = r1.astype(bf16)
    lo = (r1 - mid.astype(f32)).astype(bf16)
    return hi, mid, lo


def _dot_exact_rhs(a, b_bf16):
    hi, mid, lo = _split3(a)
    d = lambda x: jnp.dot(x, b_bf16, preferred_element_type=f32)
    return d(hi) + d(mid) + d(lo)


def _split2(a):
    hi = a.astype(bf16)
    return hi, (a - hi.astype(f32)).astype(bf16)


def _dot_parts(a, b, dims=None):
    if dims is None:
        d = lambda x, y: jnp.dot(x, y, preferred_element_type=f32)
    else:
        d = lambda x, y: lax.dot_general(x, y, dims, preferred_element_type=f32)
    return d(a[0], b[0]) + (d(a[0], b[1]) + d(a[1], b[0]))


def _dot_hp(a, b, dims=None):
    return _dot_parts(_split2(a), _split2(b), dims)


TOKEN_MAJOR, FEATURE_MAJOR, FEATURE_MAJOR_PER_BATCH = 'nt', 'fm', 'fmb'


def _norm_proj_kernel(x_ref, g_ref, *refs, specs):
    n = len(specs)
    w_refs, o_refs = refs[:n], refs[n:]
    h = _rms(x_ref[...], g_ref[...]).astype(bf16)
    for (scale, kind), w_ref, o_ref in zip(specs, w_refs, o_refs):
        if kind == TOKEN_MAJOR:
            y = lax.dot_general(h, w_ref[...], NT, preferred_element_type=f32)
        else:
            y = lax.dot_general(w_ref[...], h, NT, preferred_element_type=f32)
        if scale != 1.0:
            y = y * scale
        o_ref[...] = y.astype(o_ref.dtype).reshape(o_ref.shape)


def norm_proj(x, g, groups, tm=256, rows_per_batch=None):
    N, D = x.shape
    tm = min(tm, N)
    assert N % tm == 0
    in_specs = [pl.BlockSpec((tm, D), lambda i: (i, 0)), pl.BlockSpec((1, D), lambda i: (0, 0))]
    out_specs, out_shape = [], []
    for w, dt, _, kind in groups:
        in_specs.append(pl.BlockSpec(w.shape, lambda i: (0, 0)))
        n = w.shape[0]
        if kind == TOKEN_MAJOR:
            out_specs.append(pl.BlockSpec((tm, n), lambda i: (i, 0)))
            out_shape.append(jax.ShapeDtypeStruct((N, n), dt))
        elif kind == FEATURE_MAJOR:
            out_specs.append(pl.BlockSpec((n, tm), lambda i: (0, i)))
            out_shape.append(jax.ShapeDtypeStruct((n, N), dt))
        else:
            per = rows_per_batch // tm
            assert rows_per_batch % tm == 0
            out_specs.append(pl.BlockSpec((1, n, tm), lambda i: (i // per, 0, i % per)))
            out_shape.append(jax.ShapeDtypeStruct((N // rows_per_batch, n, rows_per_batch), dt))
    return pl.pallas_call(
        functools.partial(_norm_proj_kernel, specs=tuple((s, t) for _, _, s, t in groups)),
        grid=(N // tm,),
        in_specs=in_specs,
        out_specs=out_specs,
        out_shape=out_shape,
        compiler_params=_cparams("parallel"),
        name="norm_proj",
    )(x, g.reshape(1, D), *[w for w, _, _, _ in groups])


def _xattn_kernel(q_ref, kv_ref, o_ref, *, q_feature_major, nb):
    ch = [(b, h) for b in range(nb) for h in range(XA_HEADS)]
    hs = lambda h, off=0: slice(off + h * XA_DIM, off + (h + 1) * XA_DIM)
    qT = [q_ref[...] if q_feature_major else q_ref[b].astype(f32).T.astype(bf16) for b in range(nb)]
    s = [lax.dot_general(kv_ref[b, 0, hs(h), :].astype(bf16), qT[b][hs(h)], TN, preferred_element_type=f32)
         for b, h in ch]
    e = [jnp.exp2(x - jnp.max(x, axis=0, keepdims=True)) for x in s]
    o = [jnp.dot(kv_ref[b, 0, hs(h, XA_WIDTH), :].astype(bf16), e[i].astype(bf16), preferred_element_type=f32)
         / jnp.sum(e[i], axis=0, keepdims=True) for i, (b, h) in enumerate(ch)]
    for b in range(nb):
        o_ref[b] = jnp.concatenate(o[b * XA_HEADS:(b + 1) * XA_HEADS], axis=0).T.astype(o_ref.dtype)


def cross_attend(xq, kvT, layer, B, tq=2048, nb=8):
    q_feature_major = xq.ndim == 2
    T = xq.shape[1] // B if q_feature_major else xq.shape[1]
    M = kvT.shape[3]
    tq = min(tq, T)
    nq = T // tq
    if q_feature_major:
        nb = 1
        q_spec = pl.BlockSpec((XA_WIDTH, tq), lambda b, i: (0, b * nq + i))
    else:
        nb = math.gcd(nb, B)
        q_spec = pl.BlockSpec((nb, tq, XA_WIDTH), lambda b, i: (b, i, 0))
    return pl.pallas_call(
        functools.partial(_xattn_kernel, q_feature_major=q_feature_major, nb=nb),
        grid=(B // nb, nq),
        in_specs=[q_spec, pl.BlockSpec((nb, 1, 2 * XA_WIDTH, M), lambda b, i: (b, layer, 0, 0))],
        out_specs=pl.BlockSpec((nb, tq, XA_WIDTH), lambda b, i: (b, i, 0)),
        out_shape=jax.ShapeDtypeStruct((B, T, XA_WIDTH), bf16),
        compiler_params=_cparams("parallel", "parallel"),
        name="cross_attend",
    )(xq, kvT)


def _post_kernel(x_ref, om_ref, oc_ref, wo_ref, gf_ref, wu_ref, wd_ref, gl_ref, o_ref,
                 x1_s, h_s, acc_s, *, mix_w, final):
    j = pl.program_id(1)

    @pl.when(j == 0)
    def _():
        x1 = x_ref[...] + (jnp.dot(om_ref[...], wo_ref[0:mix_w, :], preferred_element_type=f32)
                           + jnp.dot(oc_ref[...], wo_ref[mix_w:, :], preferred_element_type=f32))
        x1_s[...] = x1
        h_s[...] = _rms(x1, gf_ref[...]).astype(bf16)
        acc_s[...] = jnp.zeros_like(acc_s)

    u = jnp.dot(h_s[...], wu_ref[...], preferred_element_type=f32)
    u = jnp.square(jnp.maximum(u, 0.0)).astype(bf16)
    acc_s[...] += jnp.dot(u, wd_ref[...], preferred_element_type=f32)

    @pl.when(j == pl.num_programs(1) - 1)
    def _():
        x2 = x1_s[...] + acc_s[...]
        if final:
            x2 = _rms(x2, gl_ref[...])
        o_ref[...] = x2


def post_block(x, o_mix, o_mem, w_out, g_ffn, w_up, w_down, g_final, final, tm=1024, tf=512):
    N, D = x.shape
    mix_w = o_mix.shape[1]
    F = w_up.shape[1]
    tm = min(tm, N)
    return pl.pallas_call(
        functools.partial(_post_kernel, mix_w=mix_w, final=final),
        grid=(N // tm, F // tf),
        in_specs=[pl.BlockSpec((tm, D), lambda i, j: (i, 0)),
                  pl.BlockSpec((tm, mix_w), lambda i, j: (i, 0)),
                  pl.BlockSpec((tm, XA_WIDTH), lambda i, j: (i, 0)),
                  pl.BlockSpec((mix_w + XA_WIDTH, D), lambda i, j: (0, 0)),
                  pl.BlockSpec((1, D), lambda i, j: (0, 0)),
                  pl.BlockSpec((D, tf), lambda i, j: (0, j)),
                  pl.BlockSpec((tf, D), lambda i, j: (j, 0)),
                  pl.BlockSpec((1, D), lambda i, j: (0, 0))],
        out_specs=pl.BlockSpec((tm, D), lambda i, j: (i, 0)),
        out_shape=jax.ShapeDtypeStruct((N, D), f32),
        scratch_shapes=[pltpu.VMEM((tm, D), f32), pltpu.VMEM((tm, D), bf16), pltpu.VMEM((tm, D), f32)],
        compiler_params=_cparams("parallel", "arbitrary"),
        name="post_block",
    )(x, o_mix, o_mem, w_out, g_ffn.reshape(1, D), w_up, w_down, g_final.reshape(1, D))


def _compress(load_chunk_rows, nch, pe_ref, w_ref, w2_ref):
    acc_p = jnp.zeros((nch, 2 * CMP_HIDDEN), f32)
    acc_q = jnp.zeros((nch, 2 * CMP_HIDDEN), f32)
    for l in range(CMP_STRIDE):
        x = load_chunk_rows(l)
        xp = (x + pe_ref[l:l + 1, :]).astype(bf16)
        xq = (x + pe_ref[CMP_STRIDE + l:CMP_STRIDE + l + 1, :]).astype(bf16)
        acc_p = acc_p + jnp.dot(xp, w_ref[l, :, 0:2 * CMP_HIDDEN], preferred_element_type=f32)
        acc_q = acc_q + jnp.dot(xq, w_ref[l, :, 2 * CMP_HIDDEN:], preferred_element_type=f32)
    hid = _silu(acc_p + pltpu.roll(acc_q, nch - 1, 0)).astype(bf16)
    outs = [jnp.dot(hid[:, h * CMP_HIDDEN:(h + 1) * CMP_HIDDEN], w2_ref[...], preferred_element_type=f32)
            for h in range(NSA_KV_HEADS)]
    return jnp.concatenate(outs, axis=1)


def _cmp_prompt_kernel(xk_ref, xv_ref, pek_ref, pev_ref, wk_ref, wv_ref, w2k_ref, w2v_ref, kc_ref, vc_ref, *, nch):
    kc_ref[0] = _compress(lambda l: xk_ref[0, pl.ds(l, nch, stride=CMP_STRIDE), :], nch, pek_ref, wk_ref, w2k_ref)
    vc_ref[0] = _compress(lambda l: xv_ref[0, pl.ds(l, nch, stride=CMP_STRIDE), :], nch, pev_ref, wv_ref, w2v_ref).T


def _cmp_weights(pe, w1, w2):
    pe2 = jnp.tile(pe, (1, NSA_KV_HEADS))
    z = jnp.zeros((CMP_STRIDE, HEAD_DIM, CMP_HIDDEN), w1.dtype)
    a, b = w1[:CMP_STRIDE], w1[CMP_STRIDE:]
    top = jnp.concatenate([a, z, b, z], axis=2)
    bot = jnp.concatenate([z, a, z, b], axis=2)
    wbd = jnp.concatenate([top, bot], axis=1).astype(bf16)
    return pe2, wbd, w2.astype(bf16)


def _cmp_weights_dense(pe, w1, w2):
    half = CMP_STRIDE * HEAD_DIM
    w = jnp.concatenate([w1[:CMP_STRIDE].reshape(half, CMP_HIDDEN), w1[CMP_STRIDE:].reshape(half, CMP_HIDDEN)], axis=1)
    return pe.reshape(2, half), w.astype(bf16), w2.astype(bf16)


def cmp_prompt(rows, cw):
    B, T, _ = rows.shape
    nch = T // CMP_STRIDE
    pek, wk, w2k, pev, wv, w2v = cw
    full = lambda a: pl.BlockSpec(a.shape, lambda b: (0,) * a.ndim)
    return pl.pallas_call(
        functools.partial(_cmp_prompt_kernel, nch=nch),
        grid=(B,),
        in_specs=[pl.BlockSpec((1, T, LANES), lambda b: (b, 0, 0)),
                  pl.BlockSpec((1, T, LANES), lambda b: (b, 0, 1)),
                  full(pek), full(pev), full(wk), full(wv), full(w2k), full(w2v)],
        out_specs=[pl.BlockSpec((1, nch, LANES), lambda b: (b, 0, 0)), pl.BlockSpec((1, LANES, nch), lambda b: (b, 0, 0))],
        out_shape=[jax.ShapeDtypeStruct((B, nch, LANES), f32), jax.ShapeDtypeStruct((B, LANES, nch), f32)],
        compiler_params=_cparams("parallel"),
        name="cmp_prompt",
    )(rows, rows, pek, pev, wk, wv, w2k, w2v)


def _padded_queries(q_all, kvh, tq):
    del tq
    lo = kvh * NSA_GROUP
    return jnp.concatenate([q_all[:, (lo + g) * LANES:(lo + g + 1) * LANES] for g in range(NSA_GROUP)], axis=0)


def _cmp_probs(qm, kc, pos_col, kvh, tq):
    ncp = kc.shape[0]
    s = lax.dot_general(qm, kc.astype(bf16), NT, preferred_element_type=f32)
    n_idx = lax.broadcasted_iota(jnp.int32, (tq, ncp), 1)
    cdist = pos_col - (n_idx * CMP_STRIDE + (CMP_BLOCK - 1))
    cvis = cdist >= 0
    any_vis = pos_col >= (CMP_BLOCK - 1)
    cdf = cdist.astype(f32)
    p_sum = jnp.zeros((tq, ncp), f32)
    ps = []
    for g in range(NSA_GROUP):
        sg = jnp.where(cvis, s[g * tq:(g + 1) * tq] - SLOPES[kvh][g] * cdf, NEG_INF)
        m = jnp.max(sg, axis=-1, keepdims=True)
        e = jnp.exp(sg - m)
        p = e / jnp.sum(e, axis=-1, keepdims=True)
        p = jnp.where(any_vis, p, 0.0)
        p_sum = p_sum + p
        ps.append(p)
    return jnp.concatenate(ps, axis=0).astype(bf16), p_sum


def _select_blocks(p_sum, pos_col, ns, nsp, tq):
    ncp = p_sum.shape[1]
    ci = lax.broadcasted_iota(jnp.int32, (ncp, nsp), 0) * CMP_STRIDE
    cj = lax.broadcasted_iota(jnp.int32, (ncp, nsp), 1)
    c2s = ((ci < (cj + 1) * SLC_BLOCK) & (ci + CMP_BLOCK > cj * SLC_BLOCK)).astype(f32).astype(bf16)
    imp = _dot_exact_rhs(p_sum, c2s)
    blk = lax.broadcasted_iota(jnp.int32, (tq, nsp), 1)
    cur = pos_col // SLC_BLOCK
    forced = (blk == 0) | (blk == cur) | (blk == cur - 1)
    imp = jnp.where(forced, FORCED, jnp.where(blk <= cur, imp, NEG_INF))
    rank = jnp.zeros((tq, nsp), jnp.int32)
    for jp in range(ns):
        col = imp[:, jp:jp + 1]
        beats = (col > imp) | ((col == imp) & (blk > jp))
        rank = rank + beats.astype(jnp.int32)
    return ((rank < SLC_TOPN) & (blk <= cur)).astype(f32)


def _combine(gates, kvh, tq, o_cmp, o_slc, o_win):
    lo = kvh * HEAD_DIM
    outs = []
    for g in range(NSA_GROUP):
        h = kvh * NSA_GROUP + g
        r = slice(g * tq, (g + 1) * tq)
        o = (gates[:, 3 * h:3 * h + 1] * o_cmp[r] + gates[:, 3 * h + 1:3 * h + 2] * o_slc[r]
             + gates[:, 3 * h + 2:3 * h + 3] * o_win[r])
        outs.append(o[:, lo:lo + HEAD_DIM])
    return outs


LOG2E = 1.4426950408889634
SLOPES2 = [[v * LOG2E for v in row] for row in SLOPES]

POS_SPLIT = 8
F_SEL0 = 16
MASK_BIG = 1e30


def position_features(pos, blk, ns):
    n = pos.shape[0]
    lane = np.arange(HEAD_DIM)[None, :]
    hi = ((pos // POS_SPLIT) * POS_SPLIT).astype(np.float32)[:, None]
    lo = (pos % POS_SPLIT).astype(np.float32)[:, None]
    f = np.where(lane < 3, hi, np.where(lane < 6, lo, np.where(lane < 9, 1.0, 0.0)))
    if ns:
        f = np.where((lane >= F_SEL0) & (lane < F_SEL0 + ns), (blk[:, None] == lane - F_SEL0).astype(np.float32), f)
    return np.broadcast_to(f, (n, HEAD_DIM)).astype(np.float32)


def feature_tables(T, nch):
    pos = np.arange(T)
    kf = position_features(pos, pos // SLC_BLOCK, T // SLC_BLOCK)
    cpos = np.arange(nch) * CMP_STRIDE + (CMP_BLOCK - 1)
    cf = position_features(cpos, cpos, 0)
    z = np.zeros_like
    two = lambda a: jnp.asarray(np.stack([np.concatenate([z(a), a], axis=1),
                                          np.concatenate([a, z(a)], axis=1)]).astype(bf16))
    return two(kf), two(cf)


def _query_features(pos_row, selT, kvh, tq):
    posf = pos_row.astype(f32)
    r16 = lax.broadcasted_iota(jnp.int32, (16, tq), 0)
    zeros16 = jnp.zeros((16, tq), f32)
    selbig = (selT - 1.0) * MASK_BIG
    ns = selT.shape[0]
    pad = jnp.zeros((HEAD_DIM - F_SEL0 - ns, tq), f32)
    plain, full = [], []
    for g in range(NSA_GROUP):
        s = jnp.full((1, tq), SLOPES2[kvh][g], f32)
        s_parts = _split3(s)
        a_parts = _split3(s * posf)
        rows = [p.astype(f32) for p in s_parts] * 2 + [-p.astype(f32) for p in a_parts]
        blk0 = zeros16
        for i, rv in enumerate(rows):
            blk0 = jnp.where(r16 == i, rv, blk0)
        plain.append(jnp.concatenate([blk0, jnp.zeros((HEAD_DIM - 16, tq), f32)], axis=0).astype(bf16))
        full.append(jnp.concatenate([blk0, selbig, pad], axis=0).astype(bf16))
    return plain, full


def _aug_queries(qT, feats, kvh, tq):
    out = []
    for g in range(NSA_GROUP):
        h = kvh * NSA_GROUP + g
        qh = qT[h * HEAD_DIM:(h + 1) * HEAD_DIM, :]
        out.append(jnp.concatenate([qh, feats[g]] if kvh == 0 else [feats[g], qh], axis=0))
    return out


def _aug_keys(k_tile, feat_tile, kvh):
    lane = lax.broadcasted_iota(jnp.int32, k_tile.shape, 1)
    own = (lane < HEAD_DIM) if kvh == 0 else (lane >= HEAD_DIM)
    return jnp.where(own, k_tile, feat_tile)


def _with_ones(vT, kvh):
    half = vT.shape[0] // 2
    ones = jnp.ones((half, vT.shape[1]), vT.dtype)
    return jnp.concatenate([vT[:half], ones] if kvh == 0 else [ones, vT[half:]], axis=0)


def _cmp_branch_t(q_aug, kc_aug, vcT, pos_row, tq):
    ncp = vcT.shape[1]
    n_idx = lax.broadcasted_iota(jnp.int32, (ncp, tq), 0)
    cvis = pos_row >= (n_idx * CMP_STRIDE + (CMP_BLOCK - 1))
    any_vis = pos_row >= (CMP_BLOCK - 1)
    s = [jnp.where(cvis, jnp.dot(k, q, preferred_element_type=f32), NEG_INF) for k, q in zip(kc_aug, q_aug)]
    e = [jnp.exp2(x - jnp.max(x, axis=0, keepdims=True)) for x in s]
    p = [jnp.where(any_vis, x / jnp.sum(x, axis=0, keepdims=True), 0.0) for x in e]
    outs = [jnp.dot(vcT, x.astype(bf16), preferred_element_type=f32) for x in p]
    return outs, p


def _select_blocks_t(p_sumT, pos_row, ns, tq):
    ncp = p_sumT.shape[0]
    cj = lax.broadcasted_iota(jnp.int32, (ns, ncp), 0)
    ci = lax.broadcasted_iota(jnp.int32, (ns, ncp), 1) * CMP_STRIDE
    c2sT = ((ci < (cj + 1) * SLC_BLOCK) & (ci + CMP_BLOCK > cj * SLC_BLOCK)).astype(f32).astype(bf16)
    hi, mid, lo = _split3(p_sumT)
    d = lambda x: jnp.dot(c2sT, x, preferred_element_type=f32)
    imp = d(hi) + d(mid) + d(lo)
    blk = lax.broadcasted_iota(jnp.int32, (ns, tq), 0)
    cur = pos_row // SLC_BLOCK
    forced = (blk == 0) | (blk == cur) | (blk == cur - 1)
    imp = jnp.where(forced, FORCED, jnp.where(blk <= cur, imp, NEG_INF))
    rank = jnp.zeros((ns, tq), jnp.int32)
    for jp in range(ns):
        row = imp[jp:jp + 1, :]
        beats = (row > imp) | ((row == imp) & (blk > jp))
        rank = rank + beats.astype(jnp.int32)
    return ((rank < SLC_TOPN) & (blk <= cur)).astype(f32)


def _nsa_prompt_kernel(qT_ref, gT_ref, sk_ref, svT_ref, wk_ref, wvT_ref, kc_ref, vcT_ref, kf_ref, cf_ref, o_ref,
                       m_s, acc_s, *, tq, kt, ns):
    qi = pl.program_id(1)
    p0 = qi * tq
    pos_row = p0 + lax.broadcasted_iota(jnp.int32, (1, tq), 1)
    gates = _sigmoid(gT_ref[...])
    qT = qT_ref[...]
    kc = kc_ref[0].astype(bf16)
    vcT = vcT_ref[0].astype(bf16)
    half = LANES // 2
    KV = range(NSA_KV_HEADS)

    zsel = jnp.zeros((ns, tq), f32)
    q_pos = [_aug_queries(qT, _query_features(pos_row, zsel, kvh, tq)[0], kvh, tq) for kvh in KV]
    kc_aug = [_aug_keys(kc, cf_ref[kvh], kvh) for kvh in KV]
    oc, pc = _cmp_branch_t([q for kvh in KV for q in q_pos[kvh]], [kc_aug[kvh] for kvh in KV for _ in range(NSA_GROUP)],
                           vcT, pos_row, tq)
    o_cmp = [oc[kvh * NSA_GROUP:(kvh + 1) * NSA_GROUP] for kvh in KV]
    q_sel = []
    for kvh in KV:
        p_sum = pc[kvh * NSA_GROUP]
        for x in pc[kvh * NSA_GROUP + 1:(kvh + 1) * NSA_GROUP]:
            p_sum = p_sum + x
        selT = _select_blocks_t(p_sum, pos_row, ns, tq)
        q_sel.append(_aug_queries(qT, _query_features(pos_row, selT, kvh, tq)[1], kvh, tq))

    m_s[...] = jnp.full(m_s.shape, NEG_INF, f32)
    acc_s[...] = jnp.zeros(acc_s.shape, f32)
    rel = pos_row - lax.broadcasted_iota(jnp.int32, (kt, tq), 0)

    def slc_tile(c, causal):
        k0 = pl.multiple_of(c * kt, kt)
        k_tile = sk_ref[pl.ds(k0, kt), :]
        v_tile = svT_ref[0, :, pl.ds(k0, kt)].astype(bf16)
        ok = (rel - k0) >= 0
        for kvh in KV:
            ka = _aug_keys(k_tile, kf_ref[kvh, pl.ds(k0, kt), :], kvh)
            va = _with_ones(v_tile, kvh)
            for g in range(NSA_GROUP):
                cs = slice(g * tq, (g + 1) * tq)
                sg = jnp.dot(ka, q_sel[kvh][g], preferred_element_type=f32)
                if causal:
                    sg = jnp.where(ok, sg, NEG_INF)
                m_old = m_s[kvh, :, cs]
                m_new = jnp.maximum(m_old, jnp.max(sg, axis=0, keepdims=True))
                alpha = jnp.exp2(m_old - m_new)
                p = jnp.exp2(sg - m_new).astype(bf16)
                acc_s[kvh, :, cs] = alpha * acc_s[kvh, :, cs] + jnp.dot(va, p, preferred_element_type=f32)
                m_s[kvh, :, cs] = m_new

    n_kt = (p0 + tq + kt - 1) // kt

    def slc_body(c, _):
        slc_tile(c, False)
        return 0

    lax.fori_loop(0, n_kt - 1, slc_body, 0)
    slc_tile(n_kt - 1, True)

    wlen = WINDOW + tq
    w0 = pl.multiple_of(jnp.maximum(p0 - WINDOW, 0), tq)
    k_win = wk_ref[pl.ds(w0, wlen), :]
    v_win = wvT_ref[0, :, pl.ds(w0, wlen)].astype(bf16)
    dist = pos_row - (w0 + lax.broadcasted_iota(jnp.int32, (wlen, tq), 0))
    valid = (dist >= 0) & (dist < WINDOW)
    ch = [(kvh, g) for kvh in KV for g in range(NSA_GROUP)]
    ka = [_aug_keys(k_win, kf_ref[kvh, pl.ds(w0, wlen), :], kvh) for kvh in KV]
    va = [_with_ones(v_win, kvh) for kvh in KV]
    sg = [jnp.where(valid, jnp.dot(ka[kvh], q_pos[kvh][g], preferred_element_type=f32), NEG_INF) for kvh, g in ch]
    p = [jnp.exp2(x - jnp.max(x, axis=0, keepdims=True)).astype(bf16) for x in sg]
    o_win = [jnp.dot(va[kvh], p[i], preferred_element_type=f32) for i, (kvh, g) in enumerate(ch)]
    heads = []
    for i, (kvh, g) in enumerate(ch):
        h = kvh * NSA_GROUP + g
        d0, l0 = (0, half) if kvh == 0 else (half, 0)
        o_slc = acc_s[kvh, :, g * tq:(g + 1) * tq]
        heads.append(gates[3 * h:3 * h + 1] * o_cmp[kvh][g][d0:d0 + half]
                     + gates[3 * h + 1:3 * h + 2] * (o_slc[d0:d0 + half] / o_slc[l0:l0 + 1])
                     + gates[3 * h + 2:3 * h + 3] * (o_win[i][d0:d0 + half] / o_win[i][l0:l0 + 1]))
    o_ref[0] = jnp.concatenate(heads, axis=0).T.astype(o_ref.dtype)


def nsa_prompt(qT, gT, sk, kv_rows, wk, win_rows, kc, vcT, B, tq=128, kt=512):
    N = sk.shape[0]
    T = N // B
    ns = T // SLC_BLOCK
    nq = T // tq
    ncp = kc.shape[1]
    assert T % kt == 0 and T % tq == 0 and kt % tq == 0 and T >= WINDOW + tq and T <= POS_SPLIT * 256
    assert F_SEL0 + ns <= HEAD_DIM
    kf, cf = feature_tables(T, ncp)
    keys = pl.BlockSpec((T, LANES), lambda b, i: (b, 0))
    vals = lambda arr: pl.BlockSpec((1, LANES, T), lambda b, i: (b, arr.shape[1] // LANES - 1, 0))
    return pl.pallas_call(
        functools.partial(_nsa_prompt_kernel, tq=tq, kt=kt, ns=ns),
        grid=(B, nq),
        in_specs=[pl.BlockSpec((NSA_WIDTH, tq), lambda b, i: (0, b * nq + i)),
                  pl.BlockSpec((3 * NSA_HEADS, tq), lambda b, i: (0, b * nq + i)),
                  keys, vals(kv_rows), keys, vals(win_rows),
                  pl.BlockSpec((1, ncp, LANES), lambda b, i: (b, 0, 0)),
                  pl.BlockSpec((1, LANES, ncp), lambda b, i: (b, 0, 0)),
                  pl.BlockSpec((NSA_KV_HEADS, T, LANES), lambda b, i: (0, 0, 0)),
                  pl.BlockSpec((NSA_KV_HEADS, ncp, LANES), lambda b, i: (0, 0, 0))],
        out_specs=pl.BlockSpec((1, tq, NSA_WIDTH), lambda b, i: (b, i, 0)),
        out_shape=jax.ShapeDtypeStruct((B, T, NSA_WIDTH), bf16),
        scratch_shapes=[pltpu.VMEM((NSA_KV_HEADS, 1, NSA_GROUP * tq), f32),
                        pltpu.VMEM((NSA_KV_HEADS, LANES, NSA_GROUP * tq), f32)],
        compiler_params=_cparams("parallel", "arbitrary"),
        name="nsa_prompt",
    )(qT, gT, sk, kv_rows, wk, win_rows, kc, vcT, kf, cf)


SF_SPLIT = 64
SF_BLK0 = 16
SF_ROWS = 256


def sample_key_features(past_len, nk, tq, ns):
    sf_new0 = SF_BLK0 + -(-ns // 8) * 8
    assert sf_new0 + tq + 1 <= SF_ROWS and nk <= SF_SPLIT * 256
    pos = np.arange(nk)[None, :]
    row = np.arange(SF_ROWS)[:, None]
    hi = ((pos // SF_SPLIT) * SF_SPLIT).astype(np.float32)
    lo = (pos % SF_SPLIT).astype(np.float32)
    f = np.where(row < 3, hi, np.where(row < 6, lo, np.where(row < 9, 1.0, 0.0)))
    blk = (row >= SF_BLK0) & (row < SF_BLK0 + ns) & (pos // SLC_BLOCK == row - SF_BLK0)
    new = (row >= sf_new0) & (row < sf_new0 + tq) & (pos - past_len == row - sf_new0)
    pad = (row == sf_new0 + tq) & (pos >= past_len + tq)
    return jnp.asarray(np.where(blk | new | pad, 1.0, f).astype(bf16)), sf_new0


def _sample_query_features(sel, pos_col, kvh, tq, ns, sf_new0):
    lane = lax.broadcasted_iota(jnp.int32, (tq, SF_ROWS), 1)
    t_idx = lax.broadcasted_iota(jnp.int32, (tq, SF_ROWS), 0)
    selbig = (pltpu.roll(sel[:, 0:SF_ROWS], SF_BLK0, 1) - 1.0) * MASK_BIG
    base = jnp.where((lane >= SF_BLK0) & (lane < SF_BLK0 + ns), selbig, 0.0)
    base = jnp.where((lane >= sf_new0) & (lane < sf_new0 + tq) & (lane - sf_new0 > t_idx), -MASK_BIG, base)
    base = jnp.where(lane == sf_new0 + tq, -MASK_BIG, base)
    posf = pos_col.astype(f32)
    out = []
    for g in range(NSA_GROUP):
        s = jnp.full((tq, 1), SLOPES[kvh][g], f32)
        cols = [p.astype(f32) for p in _split3(s)] * 2 + [-p.astype(f32) for p in _split3(s * posf)]
        qf = base
        for i, cv in enumerate(cols):
            qf = jnp.where(lane == i, cv, qf)
        out.append(qf)
    return jnp.concatenate(out, axis=0).astype(bf16)


def _new_rows_t(rows, width):
    tq = rows.shape[0]
    sq = jnp.concatenate([rows, jnp.zeros((LANES - tq, LANES), f32)], axis=0).T
    if width == LANES:
        return sq
    return jnp.concatenate([sq, jnp.zeros((LANES, width - LANES), f32)], axis=1)


def _compress_staged(xs, t, nch, pe_ref, w_ref, w2_ref):
    lhs = jnp.concatenate([jnp.concatenate([xs[2 * t + h, j] for j in range(CMP_STRIDE // 2)], axis=1)
                           for h in range(NSA_KV_HEADS)], axis=0)
    acc = jnp.dot(lhs, w_ref[...], preferred_element_type=f32)
    bc = lambda v: jnp.broadcast_to(v, (8, v.shape[1])).astype(bf16)
    c = (jnp.dot(bc(pe_ref[0:1, :]), w_ref[:, 0:CMP_HIDDEN], preferred_element_type=f32)
         + jnp.dot(bc(pe_ref[1:2, :]), w_ref[:, CMP_HIDDEN:], preferred_element_type=f32))[0:1, :]
    outs = []
    for h in range(NSA_KV_HEADS):
        r = slice(h * nch, (h + 1) * nch)
        hid = acc[r, 0:CMP_HIDDEN] + pltpu.roll(acc[r, CMP_HIDDEN:], nch - 1, 0) + c
        outs.append(jnp.dot(_silu(hid).astype(bf16), w2_ref[...], preferred_element_type=f32))
    return jnp.concatenate(outs, axis=1)


def _nsa_sample_kernel(pt_ref, *refs, pps, n_steps, tq, past_len, ns, nsp, sf_new0):
    page_refs = refs[:pps]
    (q_ref, g_ref, rn_ref, wn_ref, cw_ref, ft_ref, perm_ref, pek_ref, pev_ref, wk_ref, wv_ref, w2k_ref, w2v_ref,
     o_ref, xs_s, kT_s, vT_s) = refs[pps:]
    del pt_ref
    step = pl.program_id(1)
    cpp = PAGE_SIZE // CMP_STRIDE
    staged = []
    for i in range(pps):
        r0 = pl.multiple_of((step * pps + i) * PAGE_SIZE, PAGE_SIZE)
        pg = page_refs[i]
        slabs = []
        for th in range(4):
            x = pg[0, th]
            slabs += [x, pltpu.roll(x, PAGE_SIZE - 1, 1)]
        a = jnp.concatenate(slabs, axis=0).astype(bf16)
        staged.append(lax.dot_general(perm_ref[...], a, NT, preferred_element_type=f32))
        kT_s[:, pl.ds(r0, PAGE_SIZE)] = pg[0, 4:6].reshape(LANES, PAGE_SIZE).astype(bf16)
        vT_s[:, pl.ds(r0, PAGE_SIZE)] = pg[0, 6:8].reshape(LANES, PAGE_SIZE).astype(bf16)
    c0 = pl.multiple_of(step * (pps * cpp), pps * cpp)
    for j in range(CMP_STRIDE // 2):
        rows = slice(j * cpp, (j + 1) * cpp)
        both = jnp.concatenate([t[rows] for t in staged], axis=0).astype(bf16)
        for th in range(4):
            xs_s[th, j, pl.ds(c0, pps * cpp), :] = both[:, th * LANES:(th + 1) * LANES]

    @pl.when(step == n_steps - 1)
    def _():
        nk = kT_s.shape[1]
        rn = rn_ref[0]
        kT_s[:, past_len:] = _new_rows_t(rn[:, 2 * LANES:3 * LANES], nk - past_len).astype(bf16)
        vT_s[:, past_len:] = _new_rows_t(rn[:, 3 * LANES:4 * LANES], nk - past_len).astype(bf16)
        nwin = cw_ref.shape[2]
        wn = wn_ref[0]
        wkT = jnp.concatenate([cw_ref[0, 0:LANES, :], _new_rows_t(wn[:, 0:LANES], LANES)], axis=1).astype(bf16)
        wvT = jnp.concatenate([cw_ref[0, LANES:, :], _new_rows_t(wn[:, LANES:], LANES)], axis=1).astype(bf16)
        wlen = nwin + LANES

        nch = past_len // CMP_STRIDE
        kc = _compress_staged(xs_s, 0, nch, pek_ref, wk_ref, w2k_ref)

        pos_col = past_len + lax.broadcasted_iota(jnp.int32, (tq, 1), 0)
        gates = _sigmoid(g_ref[0])
        q_all = q_ref[0]
        wpos = (past_len - nwin) + lax.broadcasted_iota(jnp.int32, (tq, wlen), 1)
        wdist = pos_col - wpos
        wvalid = (wdist >= 0) & (wdist < WINDOW)
        wdistf = wdist.astype(f32)
        qms = [_padded_queries(q_all, kvh, tq) for kvh in range(NSA_KV_HEADS)]
        probs = [_cmp_probs(qms[kvh], kc, pos_col, kvh, tq) for kvh in range(NSA_KV_HEADS)]
        vc = _compress_staged(xs_s, 1, nch, pev_ref, wv_ref, w2v_ref).astype(bf16)
        qfs = [_sample_query_features(_select_blocks(probs[kvh][1], pos_col, ns, nsp, tq), pos_col, kvh, tq, ns, sf_new0)
               for kvh in range(NSA_KV_HEADS)]
        o_cmps = [jnp.dot(probs[kvh][0], vc, preferred_element_type=f32) for kvh in range(NSA_KV_HEADS)]
        qm2 = jnp.concatenate(qms, axis=0)
        s = (jnp.dot(qm2, kT_s[...], preferred_element_type=f32)
             + jnp.dot(jnp.concatenate(qfs, axis=0), ft_ref[...], preferred_element_type=f32))
        e = jnp.exp(s - jnp.max(s, axis=-1, keepdims=True))
        o_slc2 = (lax.dot_general(e.astype(bf16), vT_s[...], NT, preferred_element_type=f32)
                  / jnp.sum(e, axis=-1, keepdims=True))
        sw2 = jnp.dot(qm2, wkT, preferred_element_type=f32)
        pw = []
        for kvh in range(NSA_KV_HEADS):
            for g in range(NSA_GROUP):
                r = slice((kvh * NSA_GROUP + g) * tq, (kvh * NSA_GROUP + g + 1) * tq)
                sg = jnp.where(wvalid, sw2[r] - SLOPES[kvh][g] * wdistf, NEG_INF)
                ew = jnp.where(wvalid, jnp.exp(sg - jnp.max(sg, axis=-1, keepdims=True)), 0.0)
                pw.append(ew / jnp.sum(ew, axis=-1, keepdims=True))
        o_win2 = lax.dot_general(jnp.concatenate(pw, axis=0).astype(bf16), wvT, NT, preferred_element_type=f32)
        outs = []
        for kvh in range(NSA_KV_HEADS):
            r = slice(kvh * NSA_GROUP * tq, (kvh + 1) * NSA_GROUP * tq)
            outs.extend(_combine(gates, kvh, tq, o_cmps[kvh], o_slc2[r], o_win2[r]))
        o_ref[0] = jnp.concatenate(outs, axis=1).astype(o_ref.dtype)


def nsa_sample(page_table, cacheT, q, gates, rows_new, win_new, cache_winT, cw, pps=32):
    B, tq, _ = q.shape
    n_pages = page_table.shape[1]
    past_len = n_pages * PAGE_SIZE
    assert n_pages % pps == 0 and past_len % SLC_BLOCK == 0
    n_steps = n_pages // pps
    nk = past_len + LANES
    ns = past_len // SLC_BLOCK + 1
    nsp = -(-ns // LANES) * LANES
    nch = past_len // CMP_STRIDE
    nwin = cache_winT.shape[2]
    pek, wk, w2k, pev, wv, w2v = cw
    ft, sf_new0 = sample_key_features(past_len, nk, tq, ns)
    cpp = PAGE_SIZE // CMP_STRIDE
    ro = np.arange(PAGE_SIZE)
    ro2 = np.arange(PAGE_SIZE // 2)
    perm = jnp.asarray((ro[None, :] == (ro2[:, None] % cpp) * CMP_STRIDE + 2 * (ro2[:, None] // cpp))
                       .astype(bf16))
    per_b = lambda n: pl.BlockSpec((1, tq, n), lambda b, s, pt: (b, 0, 0))
    full = lambda a: pl.BlockSpec(a.shape, lambda b, s, pt: (0,) * a.ndim)
    page_specs = [pl.BlockSpec((1, 8, HEAD_DIM, PAGE_SIZE),
                               functools.partial(lambda b, s, pt, i: (pt[b, s * pps + i], 0, 0, 0), i=i))
                  for i in range(pps)]
    grid_spec = pltpu.PrefetchScalarGridSpec(
        num_scalar_prefetch=1,
        grid=(B, n_steps),
        in_specs=page_specs + [per_b(NSA_HEADS * LANES), per_b(3 * NSA_HEADS), per_b(4 * LANES), per_b(2 * LANES),
                               pl.BlockSpec((1, 2 * LANES, nwin), lambda b, s, pt: (b, 0, 0)),
                               full(ft), full(perm), full(pek), full(pev), full(wk), full(wv), full(w2k), full(w2v)],
        out_specs=per_b(NSA_WIDTH),
        scratch_shapes=[pltpu.VMEM((4, CMP_STRIDE // 2, nch, LANES), bf16),
                        pltpu.VMEM((LANES, nk), bf16), pltpu.VMEM((LANES, nk), bf16)])
    return pl.pallas_call(
        functools.partial(_nsa_sample_kernel, pps=pps, n_steps=n_steps, tq=tq, past_len=past_len, ns=ns, nsp=nsp,
                          sf_new0=sf_new0),
        grid_spec=grid_spec,
        out_shape=jax.ShapeDtypeStruct((B, tq, NSA_WIDTH), bf16),
        compiler_params=_cparams("parallel", "arbitrary"),
        name="nsa_sample",
    )(page_table, *([cacheT] * pps), q, gates, rows_new, win_new, cache_winT, ft, perm, pek, pev, wk, wv, w2k, w2v)


def _gdn_prep_kernel(x_ref, halo_ref, cb_ref, cw_ref, ba_ref, al_ref, dtb_ref,
                     u_ref, w_ref, qg_ref, kg_ref, qk_ref, eg_ref, xs_s, *, tb, C, t_valid, nb):
    i = pl.program_id(1)
    HK = GDN_HEADS * GDN_DK
    R = min(tb, LANES)
    cpb = R // C

    @pl.when(i == 0)
    def _():
        xs_s[:, 0:8, :] = cb_ref[...]

    @pl.when(i > 0)
    def _():
        xs_s[:, 0:8, :] = halo_ref[...]

    xs_s[:, 8:, :] = x_ref[...]
    ri = lax.broadcasted_iota(jnp.int32, (R, R), 0)
    ci = lax.broadcasted_iota(jnp.int32, (R, R), 1)
    same = (ri // C) == (ci // C)
    lower = same & (ri >= ci)
    strict = same & (ri > ci)
    eye = (ri == ci).astype(f32)
    rr = lax.broadcasted_iota(jnp.int32, (R, LANES), 0)
    row_t = i * tb + lax.broadcasted_iota(jnp.int32, (tb, 1), 0)
    live = row_t < t_valid

    def conv_act(bb, c0):
        acc = cw_ref[GDN_CONV - 1:GDN_CONV, c0:c0 + LANES] * xs_s[bb, pl.ds(8, tb), c0:c0 + LANES]
        for k in range(1, GDN_CONV):
            acc = acc + cw_ref[GDN_CONV - 1 - k:GDN_CONV - k, c0:c0 + LANES] * xs_s[bb, pl.ds(8 - k, tb), c0:c0 + LANES]
        return _silu(acc)

    chains = []
    for bb, h in [(bb, h) for bb in range(nb) for h in range(GDN_HEADS)]:
        ba = ba_ref[bb]
        q = conv_act(bb, h * GDN_DK)
        k = conv_act(bb, HK + h * GDN_DK)
        v = conv_act(bb, 2 * HK + h * GDN_DV)
        q = q * lax.rsqrt(jnp.sum(q * q, axis=-1, keepdims=True) + L2_EPS) * (GDN_DK ** -0.5)
        k = k * lax.rsqrt(jnp.sum(k * k, axis=-1, keepdims=True) + L2_EPS)
        beta = _sigmoid(ba[:, h:h + 1])
        ar = ba[:, GDN_HEADS + h:GDN_HEADS + h + 1] + dtb_ref[:, h:h + 1]
        softplus = jnp.maximum(ar, 0.0) + jnp.log(1.0 + jnp.exp(-jnp.abs(ar)))
        gcol = -jnp.exp(al_ref[:, h:h + 1]) * softplus
        if t_valid < 10 ** 9:
            beta = jnp.where(live, beta, 0.0)
            gcol = jnp.where(live, gcol, 0.0)
            k = jnp.where(live, k, 0.0)
            v = jnp.where(live, v, 0.0)
        for blk in range(tb // R):
            r = slice(blk * R, (blk + 1) * R)
            qc, kc, vc, bc = q[r], k[r], v[r], beta[r]
            gc = jnp.broadcast_to(gcol[r], (R, LANES))
            sh = 1
            while sh < C:
                gc = gc + jnp.where((rr % C) >= sh, pltpu.roll(gc, sh, 0), 0.0)
                sh *= 2
            gct = gc.T
            decay = jnp.exp(jnp.where(lower, gc[:, 0:R] - gct[0:R, :], NEG_INF))
            kb = kc * bc
            a = jnp.where(strict, _dot_hp(kb, kc, NT) * decay, 0.0)
            qk = jnp.where(lower, lax.dot_general(qc.astype(bf16), kc.astype(bf16), NT,
                                                  preferred_element_type=f32) * decay, 0.0)
            gl = gc[R - 1:R, :]
            for j in range(cpb - 2, -1, -1):
                gl = jnp.where(rr < (j + 1) * C, gc[(j + 1) * C - 1:(j + 1) * C, :], gl)
            chains.append(dict(bb=bb, h=h, r=r, blk=blk, qc=qc, kc=kc, gc=gc, gl=gl, qk=qk, n=-a, t=eye - a,
                               rhs=jnp.concatenate([vc * bc, kb * jnp.exp(gc)], axis=1)))

    for ch in chains:
        ch['n'] = _split2(ch['n'])
    sh = 2
    while sh < C:
        for ch in chains:
            ch['n'] = _split2(_dot_parts(ch['n'], ch['n']))
        for ch in chains:
            ch['t'] = ch['t'] + _dot_parts(_split2(ch['t']), ch['n'])
        sh *= 2

    for ch in chains:
        bb, h, r, gc, gl = ch['bb'], ch['h'], ch['r'], ch['gc'], ch['gl']
        sol = ch['rhs'] + _dot_hp(ch['t'] - eye, ch['rhs'])
        cs = slice(h * GDN_DV, (h + 1) * GDN_DV)
        u_ref[bb, r, cs] = sol[:, 0:GDN_DV]
        w_ref[bb, r, cs] = sol[:, GDN_DV:].astype(w_ref.dtype)
        qg_ref[bb, r, cs] = (ch['qc'] * jnp.exp(gc)).astype(qg_ref.dtype)
        kg_ref[bb, r, cs] = (ch['kc'] * jnp.exp(gl - gc)).astype(kg_ref.dtype)
        qk = ch['qk'][:, 0:C]
        for j in range(1, cpb):
            qk = jnp.where(ri[:, 0:C] >= j * C, ch['qk'][:, j * C:(j + 1) * C], qk)
        qk_ref[bb, r, h * C:(h + 1) * C] = qk.astype(qk_ref.dtype)
        for j in range(cpb):
            eg_ref[bb, ch['blk'] * cpb + j, h:h + 1, :] = jnp.exp(gc[(j + 1) * C - 1:(j + 1) * C, :])
    eg_ref[:, :, GDN_HEADS:, :] = jnp.zeros((nb, tb // C, 8 - GDN_HEADS, LANES), f32)


def gdn_prep(qkv, conv_buf8, conv_w, ba, a_log, dt_bias, C, tb, t_valid, nb=1):
    B, T, CH = qkv.shape
    nb = math.gcd(nb, B)
    hb = tb // 8
    per = lambda n, dt: (pl.BlockSpec((nb, tb, n), lambda b, i: (b, i, 0)), jax.ShapeDtypeStruct((B, T, n), dt))
    outs = [per(GDN_WIDTH, f32), per(GDN_WIDTH, bf16), per(GDN_WIDTH, bf16), per(GDN_WIDTH, bf16),
            per(GDN_HEADS * C, bf16),
            (pl.BlockSpec((nb, tb // C, 8, LANES), lambda b, i: (b, i, 0, 0)),
             jax.ShapeDtypeStruct((B, T // C, 8, LANES), f32))]
    return pl.pallas_call(
        functools.partial(_gdn_prep_kernel, tb=tb, C=C, t_valid=t_valid, nb=nb),
        grid=(B // nb, T // tb),
        in_specs=[pl.BlockSpec((nb, tb, CH), lambda b, i: (b, i, 0)),
                  pl.BlockSpec((nb, 8, CH), lambda b, i: (b, jnp.maximum(i * hb - 1, 0), 0)),
                  pl.BlockSpec((nb, 8, CH), lambda b, i: (b, 0, 0)),
                  pl.BlockSpec((GDN_CONV, CH), lambda b, i: (0, 0)),
                  pl.BlockSpec((nb, tb, 2 * GDN_HEADS), lambda b, i: (b, i, 0)),
                  pl.BlockSpec((1, GDN_HEADS), lambda b, i: (0, 0)),
                  pl.BlockSpec((1, GDN_HEADS), lambda b, i: (0, 0))],
        out_specs=[o[0] for o in outs],
        out_shape=[o[1] for o in outs],
        scratch_shapes=[pltpu.VMEM((nb, tb + 8, CH), f32)],
        compiler_params=_cparams("parallel", "arbitrary"),
        name="gdn_prep",
    )(qkv, qkv, conv_buf8, conv_w, ba, a_log.reshape(1, GDN_HEADS), dt_bias.reshape(1, GDN_HEADS))


def _gdn_scan_kernel(u_ref, w_ref, qg_ref, kg_ref, qk_ref, eg_ref, z_ref, ng_ref, s0_ref,
                     o_ref, sT_ref, s_s, *, C, nb):
    c = pl.program_id(1)

    @pl.when(c == 0)
    def _():
        s_s[...] = s0_ref[...]

    ch = [(b, h) for b in range(nb) for h in range(GDN_HEADS)]
    cs = [slice(h * GDN_DV, (h + 1) * GDN_DV) for h in range(GDN_HEADS)]
    S = [s_s[b, h] for b, h in ch]
    Sb = [x.astype(bf16) for x in S]
    v_new = [u_ref[b, :, cs[h]] - jnp.dot(w_ref[b, :, cs[h]], Sb[i], preferred_element_type=f32)
             for i, (b, h) in enumerate(ch)]
    vb = [x.astype(bf16) for x in v_new]
    for i, (b, h) in enumerate(ch):
        s_s[b, h] = S[i] * eg_ref[b, 0, h:h + 1, :] + lax.dot_general(kg_ref[b, :, cs[h]], vb[i], TN,
                                                                       preferred_element_type=f32)
    o = [jnp.dot(qg_ref[b, :, cs[h]], Sb[i], preferred_element_type=f32)
         + jnp.dot(qk_ref[b, :, h * C:(h + 1) * C], vb[i], preferred_element_type=f32) for i, (b, h) in enumerate(ch)]
    outs = [_rms(o[i], ng_ref[...]) * _silu(z_ref[b, :, cs[h]]) for i, (b, h) in enumerate(ch)]
    for b in range(nb):
        o_ref[b] = jnp.concatenate(outs[b * GDN_HEADS:(b + 1) * GDN_HEADS], axis=1).astype(o_ref.dtype)

    @pl.when(c == pl.num_programs(1) - 1)
    def _():
        sT_ref[...] = s_s[...]


def gdn_scan(u, w, qg, kg, qk, eg, z, norm_g, s0, C, nb=8):
    B, T, _ = u.shape
    nb = math.gcd(nb, B)
    per = lambda n: pl.BlockSpec((nb, C, n), lambda b, c: (b, c, 0))
    st = pl.BlockSpec((nb, GDN_HEADS, GDN_DK, GDN_DV), lambda b, c: (b, 0, 0, 0))
    return pl.pallas_call(
        functools.partial(_gdn_scan_kernel, C=C, nb=nb),
        grid=(B // nb, T // C),
        in_specs=[per(GDN_WIDTH), per(GDN_WIDTH), per(GDN_WIDTH), per(GDN_WIDTH), per(GDN_HEADS * C),
                  pl.BlockSpec((nb, 1, 8, LANES), lambda b, c: (b, c, 0, 0)),
                  per(GDN_WIDTH), pl.BlockSpec((1, GDN_DV), lambda b, c: (0, 0)), st],
        out_specs=[per(GDN_WIDTH), st],
        out_shape=[jax.ShapeDtypeStruct((B, T, GDN_WIDTH), bf16),
                   jax.ShapeDtypeStruct((B, GDN_HEADS, GDN_DK, GDN_DV), f32)],
        scratch_shapes=[pltpu.VMEM((nb, GDN_HEADS, GDN_DK, GDN_DV), f32)],
        compiler_params=_cparams("parallel", "arbitrary"),
        name="gdn_scan",
    )(u, w, qg, kg, qk, eg, z, norm_g.reshape(1, GDN_DV), s0)


def _nsa_in_groups(w, feature_major):
    c1 = NSA_WIDTH
    c2 = c1 + 4 * NSA_KV_COLS
    c3 = c2 + 2 * NSA_KV_COLS
    c4 = c3 + 3 * NSA_HEADS
    wT = w.T.astype(bf16)
    xa_scale = XA_DIM ** -0.5 * LOG2E
    if feature_major:
        k0 = c1 + 2 * NSA_KV_COLS
        return [(wT[c1:c2], f32, 1.0, FEATURE_MAJOR_PER_BATCH),
                (wT[c1:k0], f32, 1.0, TOKEN_MAJOR),
                (wT[c2:c3], f32, 1.0, FEATURE_MAJOR_PER_BATCH),
                (wT[c4:], bf16, xa_scale, FEATURE_MAJOR),
                (wT[:c1], bf16, HEAD_DIM ** -0.5 * LOG2E, FEATURE_MAJOR),
                (wT[c3:c4], f32, 1.0, FEATURE_MAJOR),
                (wT[k0:k0 + NSA_KV_COLS], bf16, 1.0, TOKEN_MAJOR),
                (wT[c2:c2 + NSA_KV_COLS], bf16, 1.0, TOKEN_MAJOR)]
    wq = wT[:c1].reshape(NSA_KV_HEADS, NSA_GROUP, HEAD_DIM, -1)
    zq = jnp.zeros_like(wq[0])
    wq = jnp.concatenate([jnp.concatenate([wq[0], zq], axis=1), jnp.concatenate([zq, wq[1]], axis=1)], axis=0)
    wq = wq.reshape(NSA_HEADS * LANES, -1)
    return [(wT[c1:c2], f32, 1.0, TOKEN_MAJOR), (wT[c2:c3], f32, 1.0, TOKEN_MAJOR),
            (wT[c4:], bf16, xa_scale, TOKEN_MAJOR), (wq, bf16, HEAD_DIM ** -0.5, TOKEN_MAJOR),
            (wT[c3:c4], f32, 1.0, TOKEN_MAJOR)]


def _gdn_in_groups(w, feature_major):
    c1 = GDN_CONV_CH
    c2 = c1 + GDN_WIDTH
    c3 = c2 + 2 * GDN_HEADS
    wT = w.T.astype(bf16)
    xq = (wT[c3:], bf16, XA_DIM ** -0.5 * LOG2E, FEATURE_MAJOR if feature_major else TOKEN_MAJOR)
    return [(wT[:c1], f32, 1.0, TOKEN_MAJOR), (wT[c1:c2], f32, 1.0, TOKEN_MAJOR),
            (wT[c2:c3], f32, 1.0, TOKEN_MAJOR), xq]


def _trunk(x, mem_kv, nsa_past, gdn_state, p, t_valid):
    B, T, D = x.shape
    N = B * T
    xf = x.reshape(N, D)

    prompt = nsa_past is None
    if prompt:
        kv_rows, rows_cmp, win_rows, xq, qT, gT, sk, wk = norm_proj(
            xf, p['norm_mix_g'][0], p['nsa_in_prompt'], tm=512, rows_per_batch=T)
        kc, vcT = cmp_prompt(rows_cmp.reshape(B, T, -1), p['cmp'])
        o_mix = nsa_prompt(qT, gT, sk, kv_rows, wk, win_rows, kc, vcT, B)
    else:
        rows, win, xq, q, gates = norm_proj(xf, p['norm_mix_g'][0], p['nsa_in_sample'])
        kv_rows = rows.reshape(B, T, -1)
        win_rows = win.reshape(B, T, -1)
        xq = xq.reshape(B, T, -1)
        page_table, cache, cache_win = nsa_past
        o_mix = nsa_sample(page_table, cache, q.reshape(B, T, -1), gates.reshape(B, T, -1), kv_rows, win_rows,
                           cache_win, p['cmp_dense'])
    o_mem = cross_attend(xq, mem_kv, 0, B)
    xf = post_block(xf, o_mix.reshape(N, -1), o_mem.reshape(N, -1), p['w_out'][0], p['norm_ffn_g'][0],
                    p['w_up'][0], p['w_down'][0], p['final_norm_g'], final=False)

    qkv, z, ba, xq = norm_proj(xf, p['norm_mix_g'][1], p['gdn_in_prompt' if prompt else 'gdn_in_sample'], tm=512)
    if not prompt:
        xq = xq.reshape(B, T, -1)
    qkv3 = qkv.reshape(B, T, -1)
    S0, conv_buf8 = gdn_state
    C = min(64, T)
    tb = min(512, T)
    u, w, qg, kg, qk, eg = gdn_prep(qkv3, conv_buf8, p['gdn_conv_w'], ba.reshape(B, T, -1), p['gdn_a_log'],
                                    p['gdn_dt_bias'], C, tb, t_valid, nb=1 if prompt else 8)
    o_mix, S = gdn_scan(u, w, qg, kg, qk, eg, z.reshape(B, T, -1), p['gdn_norm_g'], S0, C)
    o_mem = cross_attend(xq, mem_kv, 1, B)
    y = post_block(xf, o_mix.reshape(N, -1), o_mem.reshape(N, -1), p['w_out'][1], p['norm_ffn_g'][1],
                   p['w_up'][1], p['w_down'][1], p['final_norm_g'], final=True)
    return y.reshape(B, T, D), kv_rows, win_rows, S, qkv3


def kernel(x_prompt, x_sample, mem_prompt, cache_nsa_kv, cache_nsa_win, state_gdn_s, state_gdn_conv,
           cache_mem_kv, page_table, norm_mix_g, norm_mem_g, w_mem_kv, nsa_w_in, cmp_pe_k, cmp_w1_k,
           cmp_w2_k, cmp_pe_v, cmp_w1_v, cmp_w2_v, gdn_w_in, gdn_conv_w, gdn_a_log, gdn_dt_bias,
           gdn_norm_g, w_out, norm_ffn_g, w_up, w_down, final_norm_g):
    B, T, D = x_prompt.shape
    DB, TS, _ = x_sample.shape
    M = mem_prompt.shape[1]
    p = dict(norm_mix_g=norm_mix_g, nsa_in_prompt=_nsa_in_groups(nsa_w_in[0], True),
             nsa_in_sample=_nsa_in_groups(nsa_w_in[0], False), gdn_in_prompt=_gdn_in_groups(gdn_w_in[0], True),
             gdn_in_sample=_gdn_in_groups(gdn_w_in[0], False),
             cmp=_cmp_weights(cmp_pe_k[0], cmp_w1_k[0], cmp_w2_k[0]) + _cmp_weights(cmp_pe_v[0], cmp_w1_v[0], cmp_w2_v[0]),
             cmp_dense=(_cmp_weights_dense(cmp_pe_k[0], cmp_w1_k[0], cmp_w2_k[0])
                        + _cmp_weights_dense(cmp_pe_v[0], cmp_w1_v[0], cmp_w2_v[0])),
             gdn_conv_w=gdn_conv_w[0], gdn_a_log=gdn_a_log[0], gdn_dt_bias=gdn_dt_bias[0], gdn_norm_g=gdn_norm_g[0],
             w_out=w_out.astype(bf16), norm_ffn_g=norm_ffn_g, w_up=w_up.astype(bf16), w_down=w_down.astype(bf16),
             final_norm_g=final_norm_g)

    mem_flat = mem_prompt.reshape(B * M, D)
    mem_kv = [norm_proj(mem_flat, norm_mem_g[i], [(w_mem_kv[i].astype(bf16).T, f32, 1.0, FEATURE_MAJOR_PER_BATCH)],
                        tm=M, rows_per_batch=M)[0] for i in range(DEPTH)]
    mem_kv_p = jnp.stack(mem_kv, axis=1)
    zero_state = (jnp.zeros((B, GDN_HEADS, GDN_DK, GDN_DV), f32), jnp.zeros((B, 8, GDN_CONV_CH), f32))
    y_p, rows_p, win_p, S_p, qkv_p = _trunk(x_prompt, mem_kv_p, None, zero_state, p, 10 ** 9)

    TP = 8
    x_s = jnp.pad(x_sample, ((0, 0), (0, TP - TS), (0, 0)))
    cache = cache_nsa_kv.transpose(0, 2, 3, 4, 5, 1).reshape(cache_nsa_kv.shape[0], -1, HEAD_DIM, PAGE_SIZE)
    nwin = cache_nsa_win.shape[2]
    cache_win = cache_nsa_win.transpose(0, 1, 3, 4, 5, 2).reshape(DB, -1, nwin)
    conv8 = jnp.pad(state_gdn_conv[:, 0], ((0, 0), (8 - (GDN_CONV - 1), 0), (0, 0)))
    mem_kv_s = cache_mem_kv.transpose(0, 1, 3, 4, 5, 2).reshape(DB, DEPTH, 2 * XA_WIDTH, M)
    y_s, rows_s, win_s, S_s, qkv_s = _trunk(x_s, mem_kv_s, (page_table, cache, cache_win),
                                            (state_gdn_s[:, 0], conv8), p, TS)

    kvshape = (4, NSA_KV_HEADS, HEAD_DIM)
    nsa_kv_prompt = rows_p.reshape(B, 1, *kvshape, T).transpose(0, 5, 1, 2, 3, 4)
    nsa_kv_sample = rows_s[:, :TS].reshape(DB, TS, 1, *kvshape)
    wshape = (2, NSA_KV_HEADS, HEAD_DIM)
    wkeep = min(WINDOW, T)
    nsa_win_prompt = win_p[:, :, T - wkeep:].reshape(B, 1, *wshape, wkeep).transpose(0, 1, 5, 2, 3, 4)
    win_cat = jnp.concatenate([cache_win, win_s[:, :TS].transpose(0, 2, 1)], axis=2)[:, :, -WINDOW:]
    nsa_win_sample = win_cat.reshape(DB, 1, *wshape, -1).transpose(0, 1, 5, 2, 3, 4)
    gdn_conv_prompt = qkv_p[:, None, T - (GDN_CONV - 1):]
    conv_cat = jnp.concatenate([state_gdn_conv[:, 0], qkv_s[:, :TS]], axis=1)
    gdn_conv_sample = conv_cat[:, None, -(GDN_CONV - 1):]
    mem_kv_prompt = mem_kv_p.reshape(B, DEPTH, 2, XA_HEADS, XA_DIM, M).transpose(0, 1, 5, 2, 3, 4)
    return (y_p, y_s[:, :TS], nsa_kv_prompt, nsa_kv_sample, nsa_win_prompt, nsa_win_sample,
            S_p[:, None], S_s[:, None], gdn_conv_prompt, gdn_conv_sample, mem_kv_prompt)
```
